```python
import jax
import jax.numpy as jnp
from jax import lax
import numpy as np

D_MODEL = 1024
BATCH = 32
SEQ = 256
DEPTH = 4
DEC_BATCH = 2
DEC_SEQ = 2048
PAST_LEN = 256

GRID_W = 64
N_MIXERS = 3
N_MLA = (DEPTH + 2) // 3
N_HGRN = (DEPTH + 1) // 3
N_SWA = DEPTH // 3

MLA_HEADS = 8
MLA_Q_LORA = 512
MLA_KV_LORA = 256
MLA_NOPE_DIM = 128
MLA_ROPE_DIM = 64
MLA_V_DIM = 128
MLA_SCALE = (MLA_NOPE_DIM + MLA_ROPE_DIM) ** -0.5

HG_HEADS = 8
HG_DK = D_MODEL // HG_HEADS
HG_DV = D_MODEL // HG_HEADS
HG_CHUNK = 32

SWA_HEADS = 16
SWA_KV_HEADS = 4
SWA_GROUP = SWA_HEADS // SWA_KV_HEADS
SWA_HEAD_DIM = 64
SWA_WINDOW = 128
SWA_BLOCK = 128
SWA_SCALE = SWA_HEAD_DIM ** -0.5

D_FF = -(-8 * D_MODEL // (3 * 256)) * 256
Q_BLOCK = 128
ROPE_BASE = 10000.0
NORM_EPS = 1e-6
NEG_INF = -1e30

kernel_name = 'hybrid_dit_mla_hgrn2_swa_step'

F32 = jnp.float32


def _rmsnorm(x, g):
    xf = x.astype(F32)
    y = xf * lax.rsqrt(jnp.mean(xf * xf, axis=-1, keepdims=True) + NORM_EPS)
    return (y * g.astype(F32)).astype(x.dtype)


def _adaln(cond, w, b):
    m = jax.nn.silu(cond) @ w + b
    return [t[:, None, :] for t in jnp.split(m, 6, axis=-1)]


def _modulate(x, g, shift, scale):
    return _rmsnorm(x, g) * (1.0 + scale) + shift


def _axial_rope(x):
    S, R = x.shape[1], x.shape[-1]
    rows = S // GRID_W
    row = jnp.repeat(jnp.arange(rows), GRID_W).astype(F32)
    col = jnp.tile(jnp.arange(GRID_W), rows).astype(F32)
    n_freq = R // 4
    inv_freq = ROPE_BASE ** (-jnp.arange(n_freq, dtype=F32) / n_freq)
    bshape = (S,) + (1,) * (x.ndim - 3) + (n_freq,)

    def rot(xa, pos):
        ang = (pos[:, None] * inv_freq[None, :]).reshape(bshape)
        cos, sin = jnp.cos(ang), jnp.sin(ang)
        x1, x2 = xa[..., :n_freq].astype(F32), xa[..., n_freq:].astype(F32)
        return jnp.concatenate([x1 * cos - x2 * sin, x1 * sin + x2 * cos], axis=-1)

    half = R // 2
    out = jnp.concatenate([rot(x[..., :half], row), rot(x[..., half:], col)], axis=-1)
    return out.astype(x.dtype)


def _dense_attention(q, k, v, scale, sink=None):
    B, Sq, Hkv, G, Dk = q.shape
    nb = Sq // Q_BLOCK
    qb = jnp.moveaxis(q.reshape(B, nb, Q_BLOCK, Hkv, G, Dk), 1, 0)

    def block(qblk):
        s = jnp.einsum('bqhgd,bkhd->bhgqk', qblk, k).astype(F32) * scale
        if sink is not None:
            z = jnp.broadcast_to(sink.astype(F32)[None, :, :, None, None], s.shape[:-1] + (1,))
            p = jax.nn.softmax(jnp.concatenate([s, z], axis=-1), axis=-1)[..., :-1]
        else:
            p = jax.nn.softmax(s, axis=-1)
        return jnp.einsum('bhgqk,bkhd->bqhgd', p.astype(v.dtype), v)

    o = lax.map(block, qb)
    return jnp.moveaxis(o, 0, 1).reshape(B, Sq, Hkv, G, v.shape[-1])


def _band_attention_with_ctx(q, k, v, k_ctx, v_ctx, sink, scale):
    B, S, Hkv, G, D = q.shape
    nb = S // SWA_BLOCK
    qb = q.reshape(B, nb, SWA_BLOCK, Hkv, G, D)

    def band(t):
        tp = jnp.pad(t, ((0, 0), (SWA_BLOCK, SWA_BLOCK), (0, 0), (0, 0)))
        tp = tp.reshape(B, nb + 2, SWA_BLOCK, Hkv, D)
        return jnp.concatenate([tp[:, :-2], tp[:, 1:-1], tp[:, 2:]], axis=2)

    kb, vb = band(k), band(v)
    s_loc = jnp.einsum('bnqhgd,bnkhd->bnhgqk', qb, kb).astype(F32) * scale
    blk = jnp.arange(nb)[:, None, None]
    qpos = blk * SWA_BLOCK + jnp.arange(SWA_BLOCK)[None, :, None]
    kpos = (blk - 1) * SWA_BLOCK + jnp.arange(3 * SWA_BLOCK)[None, None, :]
    valid = (jnp.abs(qpos - kpos) <= SWA_WINDOW) & (kpos >= 0) & (kpos < S)
    s_loc = jnp.where(valid[None, :, None, None], s_loc, NEG_INF)
    s_ctx = jnp.einsum('bnqhgd,blhd->bnhgql', qb, k_ctx).astype(F32) * scale
    z = jnp.broadcast_to(sink.astype(F32)[None, None, :, :, None, None], s_ctx.shape[:-1] + (1,))
    p = jax.nn.softmax(jnp.concatenate([s_ctx, s_loc, z], axis=-1), axis=-1)
    L = k_ctx.shape[1]
    p_ctx = p[..., :L].astype(v.dtype)
    p_loc = p[..., L:L + 3 * SWA_BLOCK].astype(v.dtype)
    o = (jnp.einsum('bnhgql,blhd->bnqhgd', p_ctx, v_ctx)
         + jnp.einsum('bnhgqk,bnkhd->bnqhgd', p_loc, vb))
    return o.reshape(B, S, Hkv, G, D)


def _mla_q(xn, w_dq, q_norm, w_uq):
    B, S, _ = xn.shape
    q = (_rmsnorm(xn @ w_dq, q_norm) @ w_uq).reshape(B, S, MLA_HEADS, MLA_NOPE_DIM + MLA_ROPE_DIM)
    return q[..., :MLA_NOPE_DIM], q[..., MLA_NOPE_DIM:]


def _mla_compress(xn, w_dkv, kv_norm):
    kv = xn @ w_dkv
    return _rmsnorm(kv[..., :MLA_KV_LORA], kv_norm), kv[..., MLA_KV_LORA:]


def _mla_expand(c_kv, k_rope, w_uk, w_uv):
    B, S, _ = c_kv.shape
    k_nope = (c_kv @ w_uk).reshape(B, S, MLA_HEADS, MLA_NOPE_DIM)
    k_r = jnp.broadcast_to(k_rope[:, :, None, :], (B, S, MLA_HEADS, MLA_ROPE_DIM)).astype(k_nope.dtype)
    v = (c_kv @ w_uv).reshape(B, S, MLA_HEADS, MLA_V_DIM)
    return jnp.concatenate([k_nope, k_r], axis=-1), v


def _mla_context(xn, w_dq, q_norm, w_uq, w_dkv, kv_norm, w_uk, w_uv, w_o):
    B, S, _ = xn.shape
    q_nope, q_rope = _mla_q(xn, w_dq, q_norm, w_uq)
    c_kv, k_rope = _mla_compress(xn, w_dkv, kv_norm)
    k, v = _mla_expand(c_kv, k_rope, w_uk, w_uv)
    q = jnp.concatenate([q_nope, q_rope], axis=-1)[:, :, :, None, :]
    o = _dense_attention(q, k, v, MLA_SCALE)
    return o.reshape(B, S, MLA_HEADS * MLA_V_DIM) @ w_o, c_kv, k_rope


def _mla_latent(xn, ckv_ctx, krope_ctx, w_dq, q_norm, w_uq, w_dkv, kv_norm, w_uk, w_uv, w_o):
    B, S, _ = xn.shape
    q_nope, q_rope = _mla_q(xn, w_dq, q_norm, w_uq)
    q = jnp.concatenate([q_nope, _axial_rope(q_rope)], axis=-1)[:, :, :, None, :]
    c_kv, k_rope = _mla_compress(xn, w_dkv, kv_norm)
    k_lat, v_lat = _mla_expand(c_kv, _axial_rope(k_rope), w_uk, w_uv)
    k_ctx, v_ctx = _mla_expand(ckv_ctx.astype(xn.dtype), krope_ctx.astype(xn.dtype), w_uk, w_uv)
    k = jnp.concatenate([k_ctx, k_lat], axis=1)
    v = jnp.concatenate([v_ctx, v_lat], axis=1)
    o = _dense_attention(q, k, v, MLA_SCALE)
    return o.reshape(B, S, MLA_HEADS * MLA_V_DIM) @ w_o


def _hgrn_lower_bound(lb_logits, layer):
    s = jax.nn.softmax(lb_logits.astype(F32), axis=0)
    return jnp.cumsum(s, axis=0)[layer] - s[0]


def _gla_chunk_scan(q, k, v, logf, s0):
    B, H, S, K = q.shape
    V = v.shape[-1]
    n = S // HG_CHUNK

    def to_chunks(a):
        return a.reshape(B, H, n, HG_CHUNK, a.shape[-1]).transpose(2, 0, 1, 3, 4)

    tri = jnp.tril(jnp.ones((HG_CHUNK, HG_CHUNK), dtype=bool))

    def step(state, inp):
        qc, kc, vc, fc = inp
        b = jnp.cumsum(fc, axis=2)
        o_inter = jnp.einsum('bhtk,bhkv->bhtv', qc * jnp.exp(b), state)
        diff = b[:, :, :, None, :] - b[:, :, None, :, :]
        decay = jnp.exp(jnp.where(tri[None, None, :, :, None], diff, -jnp.inf))
        att = jnp.einsum('bhtk,bhsk,bhtsk->bhts', qc, kc, decay)
        o = o_inter + jnp.einsum('bhts,bhsv->bhtv', att, vc)
        b_last = b[:, :, -1:, :]
        k_dec = kc * jnp.exp(b_last - b)
        new_state = jnp.exp(b_last[:, :, 0, :])[..., None] * state + jnp.einsum('bhsk,bhsv->bhkv', k_dec, vc)
        return new_state, o

    s_fin, o = lax.scan(step, s0.astype(F32), (to_chunks(q), to_chunks(k), to_chunks(v), to_chunks(logf)))
    return o.transpose(1, 2, 0, 3, 4).reshape(B, H, S, V), s_fin


def _hgrn_mix(xn, s0_fwd, s0_bwd, lb_fwd, lb_bwd, w_q, w_f, w_i, w_g, o_norm, w_o):
    B, S, _ = xn.shape

    def heads(t, d):
        return t.reshape(B, S, HG_HEADS, d).transpose(0, 2, 1, 3).astype(F32)

    q = heads(jax.nn.silu(xn @ w_q), HG_DK)
    v = heads(xn @ w_i, HG_DV)
    f_fwd = lb_fwd + (1.0 - lb_fwd) * jax.nn.sigmoid((xn @ w_f[0]).astype(F32))
    f_bwd = lb_bwd + (1.0 - lb_bwd) * jax.nn.sigmoid((xn @ w_f[1]).astype(F32))
    flip = lambda t: jnp.flip(t, axis=2)
    o_f, s_f = _gla_chunk_scan(q, heads(1.0 - f_fwd, HG_DK), v, heads(jnp.log(f_fwd), HG_DK), s0_fwd)
    o_b, s_b = _gla_chunk_scan(flip(q), flip(heads(1.0 - f_bwd, HG_DK)), flip(v),
                               flip(heads(jnp.log(f_bwd), HG_DK)), s0_bwd)
    o = (o_f + flip(o_b)).transpose(0, 2, 1, 3)
    g = jax.nn.silu((xn @ w_g).reshape(B, S, HG_HEADS, HG_DV).astype(F32))
    o = (_rmsnorm(o, o_norm) * g).reshape(B, S, HG_HEADS * HG_DV).astype(xn.dtype)
    return o @ w_o, s_f, s_b


def _swa_qkv(xn, w_q, w_k, w_v):
    B, S, _ = xn.shape
    q = (xn @ w_q).reshape(B, S, SWA_KV_HEADS, SWA_GROUP, SWA_HEAD_DIM)
    k = (xn @ w_k).reshape(B, S, SWA_KV_HEADS, SWA_HEAD_DIM)
    v = (xn @ w_v).reshape(B, S, SWA_KV_HEADS, SWA_HEAD_DIM)
    return q, k, v


def _swa_context(xn, sink, w_q, w_k, w_v, w_o):
    B, S, _ = xn.shape
    q, k, v = _swa_qkv(xn, w_q, w_k, w_v)
    o = _dense_attention(q, k, v, SWA_SCALE, sink)
    return o.reshape(B, S, SWA_HEADS * SWA_HEAD_DIM) @ w_o, k, v


def _swa_latent(xn, k_ctx, v_ctx, sink, w_q, w_k, w_v, w_o):
    B, S, _ = xn.shape
    q, k, v = _swa_qkv(xn, w_q, w_k, w_v)
    o = _band_attention_with_ctx(_axial_rope(q), _axial_rope(k), v, k_ctx.astype(xn.dtype),
                                 v_ctx.astype(xn.dtype), sink, SWA_SCALE)
    return o.reshape(B, S, SWA_HEADS * SWA_HEAD_DIM) @ w_o


def _swiglu(h, w_gate, w_up, w_down):
    return (jax.nn.silu(h @ w_gate) * (h @ w_up)) @ w_down


def setup_inputs(seed: int = 0) -> dict:
    key = jax.random.key(seed)
    ks = iter(jax.random.split(key, 64))
    D = D_MODEL

    def nrm(shape, scale=1.0):
        return jax.random.normal(next(ks), shape, F32) * scale

    def w(shape, fan_in, scale=1.0):
        return nrm(shape, scale * fan_in ** -0.5)

    def gain(shape):
        return 1.0 + nrm(shape, 0.02)

    return {
        'x_prompt': nrm((BATCH, SEQ, D)),
        'x_sample': nrm((DEC_BATCH, DEC_SEQ, D)),
        'cache_mla_ckv': nrm((DEC_BATCH, N_MLA, PAST_LEN, MLA_KV_LORA)),
        'cache_mla_krope': nrm((DEC_BATCH, N_MLA, PAST_LEN, MLA_ROPE_DIM)),
        'state_hgrn': nrm((DEC_BATCH, N_HGRN, 2, HG_HEADS, HG_DK, HG_DV), 0.5),
        'cache_swa_k': nrm((DEC_BATCH, N_SWA, PAST_LEN, SWA_KV_HEADS, SWA_HEAD_DIM)),
        'cache_swa_v': nrm((DEC_BATCH, N_SWA, PAST_LEN, SWA_KV_HEADS, SWA_HEAD_DIM)),
        'c': nrm((DEC_BATCH, D)),
        'c_ctx': nrm((D,)),
        'ada_w': w((DEPTH, D, 6 * D), D, 0.5),
        'ada_b': nrm((DEPTH, 6 * D), 0.02),
        'norm_mix': gain((DEPTH, D)),
        'norm_ffn': gain((DEPTH, D)),
        'ffn_w_gate': w((DEPTH, D, D_FF), D),
        'ffn_w_up': w((DEPTH, D, D_FF), D),
        'ffn_w_down': w((DEPTH, D_FF, D), D_FF),
        'final_norm': gain((D,)),
        'mla_w_dq': w((N_MLA, D, MLA_Q_LORA), D),
        'mla_q_norm': gain((N_MLA, MLA_Q_LORA)),
        'mla_w_uq': w((N_MLA, MLA_Q_LORA, MLA_HEADS * (MLA_NOPE_DIM + MLA_ROPE_DIM)), MLA_Q_LORA),
        'mla_w_dkv': w((N_MLA, D, MLA_KV_LORA + MLA_ROPE_DIM), D),
        'mla_kv_norm': gain((N_MLA, MLA_KV_LORA)),
        'mla_w_uk': w((N_MLA, MLA_KV_LORA, MLA_HEADS * MLA_NOPE_DIM), MLA_KV_LORA),
        'mla_w_uv': w((N_MLA, MLA_KV_LORA, MLA_HEADS * MLA_V_DIM), MLA_KV_LORA),
        'mla_w_o': w((N_MLA, MLA_HEADS * MLA_V_DIM, D), MLA_HEADS * MLA_V_DIM),
        'hg_w_q': w((N_HGRN, D, HG_HEADS * HG_DK), D),
        'hg_w_f': w((N_HGRN, 2, D, HG_HEADS * HG_DK), D),
        'hg_w_i': w((N_HGRN, D, HG_HEADS * HG_DV), D),
        'hg_w_g': w((N_HGRN, D, HG_HEADS * HG_DV), D),
        'hg_o_norm': gain((N_HGRN, HG_DV)),
        'hg_w_o': w((N_HGRN, HG_HEADS * HG_DV, D), HG_HEADS * HG_DV),
        'hg_lb_logits': nrm((2, DEPTH, HG_HEADS * HG_DK), 0.5),
        'swa_w_q': w((N_SWA, D, SWA_HEADS * SWA_HEAD_DIM), D),
        'swa_w_k': w((N_SWA, D, SWA_KV_HEADS * SWA_HEAD_DIM), D),
        'swa_w_v': w((N_SWA, D, SWA_KV_HEADS * SWA_HEAD_DIM), D),
        'swa_w_o': w((N_SWA, SWA_HEADS * SWA_HEAD_DIM, D), SWA_HEADS * SWA_HEAD_DIM),
        'swa_sink': nrm((N_SWA, SWA_HEADS), 0.5),
    }


def reference(x_prompt, x_sample, cache_mla_ckv, cache_mla_krope, state_hgrn, cache_swa_k, cache_swa_v,
              c, c_ctx, ada_w, ada_b, norm_mix, norm_ffn, ffn_w_gate, ffn_w_up, ffn_w_down, final_norm,
              mla_w_dq, mla_q_norm, mla_w_uq, mla_w_dkv, mla_kv_norm, mla_w_uk, mla_w_uv, mla_w_o,
              hg_w_q, hg_w_f, hg_w_i, hg_w_g, hg_o_norm, hg_w_o, hg_lb_logits,
              swa_w_q, swa_w_k, swa_w_v, swa_w_o, swa_sink):
    x_p, x_s = x_prompt, x_sample
    n_prompt = x_p.shape[0]
    new_ckv, new_krope, new_hg, new_k, new_v = [], [], [], [], []
    for i in range(DEPTH):
        kind, j = i % N_MIXERS, i // N_MIXERS
        sh1p, sc1p, g1p, sh2p, sc2p, g2p = _adaln(c_ctx[None, :], ada_w[i], ada_b[i])
        sh1s, sc1s, g1s, sh2s, sc2s, g2s = _adaln(c, ada_w[i], ada_b[i])
        hp = _modulate(x_p, norm_mix[i], sh1p, sc1p)
        hs = _modulate(x_s, norm_mix[i], sh1s, sc1s)
        if kind == 0:
            p = (mla_w_dq[j], mla_q_norm[j], mla_w_uq[j], mla_w_dkv[j], mla_kv_norm[j],
                 mla_w_uk[j], mla_w_uv[j], mla_w_o[j])
            yp, ckv, krope = _mla_context(hp, *p)
            ys = _mla_latent(hs, cache_mla_ckv[:, j], cache_mla_krope[:, j], *p)
            new_ckv.append(ckv)
            new_krope.append(krope)
        elif kind == 1:
            lb_f = _hgrn_lower_bound(hg_lb_logits[0], i)
            lb_b = _hgrn_lower_bound(hg_lb_logits[1], i)
            p = (hg_w_q[j], hg_w_f[j], hg_w_i[j], hg_w_g[j], hg_o_norm[j], hg_w_o[j])
            zeros = jnp.zeros((n_prompt, HG_HEADS, HG_DK, HG_DV), F32)
            yp, s_f, s_b = _hgrn_mix(hp, zeros, zeros, lb_f, lb_b, *p)
            ys, _, _ = _hgrn_mix(hs, state_hgrn[:, j, 0], state_hgrn[:, j, 1], lb_f, lb_b, *p)
            new_hg.append(jnp.stack([s_f, s_b], axis=1))
        else:
            sink = swa_sink[j].reshape(SWA_KV_HEADS, SWA_GROUP)
            p = (swa_w_q[j], swa_w_k[j], swa_w_v[j], swa_w_o[j])
            yp, k_c, v_c = _swa_context(hp, sink, *p)
            ys = _swa_latent(hs, cache_swa_k[:, j], cache_swa_v[:, j], sink, *p)
            new_k.append(k_c)
            new_v.append(v_c)
        x_p = x_p + g1p * yp
        x_s = x_s + g1s * ys
        x_p = x_p + g2p * _swiglu(_modulate(x_p, norm_ffn[i], sh2p, sc2p), ffn_w_gate[i], ffn_w_up[i], ffn_w_down[i])
        x_s = x_s + g2s * _swiglu(_modulate(x_s, norm_ffn[i], sh2s, sc2s), ffn_w_gate[i], ffn_w_up[i], ffn_w_down[i])
    y_prompt = _rmsnorm(x_p, final_norm)
    y_sample = _rmsnorm(x_s, final_norm)
    return (y_prompt, y_sample, jnp.stack(new_ckv, axis=1), jnp.stack(new_krope, axis=1),
            jnp.stack(new_hg, axis=1), jnp.stack(new_k, axis=1), jnp.stack(new_v, axis=1))
```

```python
import functools

import jax
import jax.numpy as jnp
from jax import lax
from jax.experimental import pallas as pl
from jax.experimental.pallas import tpu as pltpu

F32 = jnp.float32
BF16 = jnp.bfloat16

GRID_W = 64
N_MIXERS = 3

MLA_HEADS = 8
MLA_KV_LORA = 256
MLA_NOPE_DIM = 128
MLA_ROPE_DIM = 64
MLA_V_DIM = 128
MLA_SCALE = (MLA_NOPE_DIM + MLA_ROPE_DIM) ** -0.5

HG_HEADS = 8
HG_DK = 128
HG_DV = 128
HG_SCAN_CHUNK = 32

SWA_HEADS = 16
SWA_KV_HEADS = 4
SWA_GROUP = SWA_HEADS // SWA_KV_HEADS
SWA_HEAD_DIM = 64
SWA_WINDOW = 128
SWA_SCALE = SWA_HEAD_DIM ** -0.5

ROPE_BASE = 10000.0
ROPE_PERIOD = 64
ROPE_QUARTER = 16
NORM_EPS = 1e-6
NEG_INF = -1e30

LANES = 128
COND_ROWS = 8
VMEM_LIMIT = 56 * 1024 * 1024


def _sigmoid(x):
    return jax.nn.sigmoid(x)


def _silu(x):
    return x * jax.nn.sigmoid(x)


def _rms(x, g):
    return x * lax.rsqrt(jnp.mean(x * x, axis=-1, keepdims=True) + NORM_EPS) * g


def _modulate(x, g, shift, scale):
    return _rms(x, g) * (1.0 + scale) + shift


def _dot(a, b):
    return jnp.dot(a, b, preferred_element_type=F32)


def _dot_nt(a, b):
    return lax.dot_general(a, b, (((1,), (1,)), ((), ())), preferred_element_type=F32)


def _dot_tn(a, b):
    return lax.dot_general(a, b, (((0,), (0,)), ((), ())), preferred_element_type=F32)


def _swap_pairs(x):
    n = x.shape[1]
    lane = lax.broadcasted_iota(jnp.int32, x.shape, 1)
    ahead = pltpu.roll(x, n - ROPE_QUARTER, 1)
    behind = pltpu.roll(x, ROPE_QUARTER, 1)
    return jnp.where((lane & (2 * ROPE_QUARTER - 1)) < ROPE_QUARTER, ahead, behind)


def _rope(x, cos, sin):
    reps = x.shape[1] // LANES
    if reps > 1:
        cos = jnp.concatenate([cos] * reps, axis=1)
        sin = jnp.concatenate([sin] * reps, axis=1)
    return x * cos + _swap_pairs(x) * sin


def _softmax_pv(scores, values, extra_logit=None):
    m = functools.reduce(jnp.maximum, [jnp.max(s, axis=1, keepdims=True) for s in scores])
    if extra_logit is not None:
        m = jnp.maximum(m, extra_logit)
    acc, den = None, None
    for s, v in zip(scores, values):
        p = jnp.exp(s - m)
        d = jnp.sum(p, axis=1, keepdims=True)
        a = _dot(p.astype(BF16), v)
        acc = a if acc is None else acc + a
        den = d if den is None else den + d
    if extra_logit is not None:
        den = den + jnp.exp(extra_logit - m)
    return acc / den


def _whole(shape):
    zeros = (0,) * len(shape)
    return pl.BlockSpec(shape, lambda *_: zeros, pipeline_mode=pl.Buffered(1))


def _params(*sem):
    return pltpu.CompilerParams(dimension_semantics=sem, vmem_limit_bytes=VMEM_LIMIT)


class _Rows:
    def __init__(self, n_prompt_rows, dec_seq, n_rows, tm):
        assert n_prompt_rows % tm == 0 and dec_seq % tm == 0
        self.tm = tm
        self.n_tiles = n_rows // tm
        self.prompt_tiles = n_prompt_rows // tm
        self.seq_tiles = dec_seq // tm

    def cond(self, i):
        return jnp.where(i < self.prompt_tiles, 0, 1 + jnp.maximum(i - self.prompt_tiles, 0) // self.seq_tiles)

    def rope_block(self, i):
        return jnp.where(i < self.prompt_tiles, 0, 1 + jnp.maximum(i - self.prompt_tiles, 0) % self.seq_tiles)

    def row_spec(self, width):
        return pl.BlockSpec((self.tm, width), lambda i: (i, 0))

    def mod_spec(self):
        return pl.BlockSpec((1, 6, 1024), lambda i: (self.cond(i), 0, 0))

    def rope_spec(self):
        return pl.BlockSpec((self.tm, LANES), lambda i: (self.rope_block(i), 0))


def _rope_tables(dec_seq, tm):
    pos = jnp.arange(dec_seq)
    row = (pos // GRID_W).astype(F32)
    col = (pos % GRID_W).astype(F32)
    inv_freq = ROPE_BASE ** (-jnp.arange(ROPE_QUARTER, dtype=F32) / ROPE_QUARTER)
    ang_r = row[:, None] * inv_freq[None, :]
    ang_c = col[:, None] * inv_freq[None, :]
    cos = jnp.concatenate([jnp.cos(ang_r), jnp.cos(ang_r), jnp.cos(ang_c), jnp.cos(ang_c)], axis=1)
    sin = jnp.concatenate([-jnp.sin(ang_r), jnp.sin(ang_r), -jnp.sin(ang_c), jnp.sin(ang_c)], axis=1)
    cos = jnp.concatenate([jnp.ones((tm, ROPE_PERIOD), F32), cos], axis=0)
    sin = jnp.concatenate([jnp.zeros((tm, ROPE_PERIOD), F32), sin], axis=0)
    return jnp.tile(cos, (1, LANES // ROPE_PERIOD)), jnp.tile(sin, (1, LANES // ROPE_PERIOD))


def _adaln_kernel(c_ref, w_ref, b_ref, o_ref):
    o_ref[0] = _dot(_silu(c_ref[...]), w_ref[0]) + b_ref[0]


def _adaln(cond, ada_w, ada_b, tn=1536):
    depth, d, n = ada_w.shape
    return pl.pallas_call(
        _adaln_kernel,
        grid=(depth, n // tn),
        in_specs=[pl.BlockSpec((COND_ROWS, d), lambda l, j: (0, 0)),
                  pl.BlockSpec((1, d, tn), lambda l, j: (l, 0, j)),
                  pl.BlockSpec((1, 1, tn), lambda l, j: (l, 0, j))],
        out_specs=pl.BlockSpec((1, COND_ROWS, tn), lambda l, j: (l, 0, j)),
        out_shape=jax.ShapeDtypeStruct((depth, COND_ROWS, n), F32),
        compiler_params=_params("arbitrary", "arbitrary"),
        name="adaln",
    )(cond, ada_w, ada_b.reshape(depth, 1, n))


def _mla_pre_kernel(x_ref, mod_ref, gn_ref, cos_ref, sin_ref, wdq_ref, qnorm_ref, wuqn_ref, wuqr_ref,
                    wdkv_ref, kvnorm_ref, wuk_ref, wuv_ref,
                    qn_ref, qr_ref, ckv_ref, krraw_ref, kr_ref, kn_ref, v_ref):
    m = mod_ref[0]
    h = _modulate(x_ref[...], gn_ref[...], m[0:1], m[1:2]).astype(BF16)
    q_lat = _rms(_dot(h, wdq_ref[...]), qnorm_ref[...]).astype(BF16)
    qn_ref[...] = _dot(q_lat, wuqn_ref[...]).astype(BF16)
    cos, sin = cos_ref[...], sin_ref[...]
    qr_ref[...] = _rope(_dot(q_lat, wuqr_ref[...]), cos, sin).astype(BF16)
    kv = _dot(h, wdkv_ref[...])
    ckv = _rms(kv[:, :MLA_KV_LORA], kvnorm_ref[...])
    ckv_ref[...] = ckv
    kr = kv[:, MLA_KV_LORA:]
    krraw_ref[...] = kr[:, :MLA_ROPE_DIM]
    kr_ref[...] = _rope(kr, cos, sin)[:, :MLA_ROPE_DIM].astype(BF16)
    cb = ckv.astype(BF16)
    kn_ref[...] = _dot(cb, wuk_ref[...]).astype(BF16)
    v_ref[...] = _dot(cb, wuv_ref[...]).astype(BF16)


def _mla_pre(x, mods, gn, cos, sin, w, rows):
    t, d = x.shape
    hn, hr = MLA_HEADS * MLA_NOPE_DIM, MLA_HEADS * MLA_ROPE_DIM
    ins = [x, mods, gn, cos, sin, w["dq"], w["q_norm"], w["uq_nope"], w["uq_rope"], w["dkv"], w["kv_norm"],
           w["uk"], w["uv"]]
    in_specs = [rows.row_spec(d), rows.mod_spec(), _whole(gn.shape), rows.rope_spec(), rows.rope_spec()]
    in_specs += [_whole(a.shape) for a in ins[5:]]
    outs = [(hn, BF16), (hr, BF16), (MLA_KV_LORA, F32), (MLA_ROPE_DIM, F32), (MLA_ROPE_DIM, BF16),
            (hn, BF16), (MLA_HEADS * MLA_V_DIM, BF16)]
    return pl.pallas_call(
        _mla_pre_kernel,
        grid=(rows.n_tiles,),
        in_specs=in_specs,
        out_specs=[rows.row_spec(n) for n, _ in outs],
        out_shape=[jax.ShapeDtypeStruct((t, n), dt) for n, dt in outs],
        compiler_params=_params("arbitrary"),
        name="mla_pre",
    )(*ins)


def _mla_expand_kernel(c_ref, wuk_ref, wuv_ref, kn_ref, v_ref):
    cb = c_ref[...].astype(BF16)
    kn_ref[...] = _dot(cb, wuk_ref[...]).astype(BF16)
    v_ref[...] = _dot(cb, wuv_ref[...]).astype(BF16)


def _mla_expand(ckv, wuk, wuv):
    n = ckv.shape[0]
    return pl.pallas_call(
        _mla_expand_kernel,
        out_shape=[jax.ShapeDtypeStruct((n, wuk.shape[1]), BF16), jax.ShapeDtypeStruct((n, wuv.shape[1]), BF16)],
        compiler_params=pltpu.CompilerParams(vmem_limit_bytes=VMEM_LIMIT),
        name="mla_expand",
    )(ckv, wuk, wuv)


def _mla_attn_kernel(*refs, n_src):
    qn_ref, qr_ref = refs[0], refs[1]
    srcs = [refs[2 + 3 * i: 5 + 3 * i] for i in range(n_src)]
    o_ref = refs[-1]
    for h in range(MLA_HEADS):
        nope = slice(h * MLA_NOPE_DIM, (h + 1) * MLA_NOPE_DIM)
        q1 = qn_ref[:, nope]
        q2 = qr_ref[:, h * MLA_ROPE_DIM:(h + 1) * MLA_ROPE_DIM]
        scores, values = [], []
        for kn_ref, kr_ref, v_ref in srcs:
            s = _dot_nt(q1, kn_ref[:, nope]) + _dot_nt(q2, kr_ref[...].astype(BF16))
            scores.append(s * MLA_SCALE)
            values.append(v_ref[:, h * MLA_V_DIM:(h + 1) * MLA_V_DIM])
        o_ref[:, h * MLA_V_DIM:(h + 1) * MLA_V_DIM] = _softmax_pv(scores, values).astype(BF16)


def _attn_out_args(o_prev, t, width):
    if o_prev is None:
        return [], [], {}
    return [o_prev], [pl.BlockSpec(memory_space=pl.ANY)], {0: 0}


def _mla_attn_prompt(qn, qr, kn, kr, v, batch, seq):
    t = qn.shape[0]
    blk = lambda w: pl.BlockSpec((seq, w), lambda b: (b, 0))
    return pl.pallas_call(
        functools.partial(_mla_attn_kernel, n_src=1),
        grid=(batch,),
        in_specs=[blk(qn.shape[1]), blk(qr.shape[1]), blk(kn.shape[1]), blk(kr.shape[1]), blk(v.shape[1])],
        out_specs=blk(v.shape[1]),
        out_shape=jax.ShapeDtypeStruct((t, v.shape[1]), BF16),
        compiler_params=_params("arbitrary"),
        name="mla_attn_prompt",
    )(qn, qr, kn, kr, v)


def _mla_attn_latent_kernel(o_prev_ref, *refs):
    del o_prev_ref
    _mla_attn_kernel(*refs, n_src=2)


def _mla_attn_latent(o_prev, qn, qr, kn, kr, v, kn_ctx, kr_ctx, v_ctx, n_prompt, dec_batch, dec_seq, past, tq):
    nq = dec_seq // tq
    qblk = lambda w: pl.BlockSpec((tq, w), lambda b, i: (n_prompt // tq + b * nq + i, 0))
    lat = lambda w: pl.BlockSpec((dec_seq, w), lambda b, i: (n_prompt // dec_seq + b, 0))
    ctx = lambda w: pl.BlockSpec((past, w), lambda b, i: (b, 0))
    return pl.pallas_call(
        _mla_attn_latent_kernel,
        grid=(dec_batch, nq),
        in_specs=[pl.BlockSpec(memory_space=pl.ANY), qblk(qn.shape[1]), qblk(qr.shape[1]),
                  ctx(kn.shape[1]), ctx(kr.shape[1]), ctx(v.shape[1]),
                  lat(kn.shape[1]), lat(kr.shape[1]), lat(v.shape[1])],
        out_specs=qblk(v.shape[1]),
        out_shape=jax.ShapeDtypeStruct(o_prev.shape, o_prev.dtype),
        input_output_aliases={0: 0},
        compiler_params=_params("arbitrary", "arbitrary"),
        name="mla_attn_latent",
    )(o_prev, qn, qr, kn_ctx, kr_ctx, v_ctx, kn, kr, v)


def _swa_pre_kernel(x_ref, mod_ref, gn_ref, cos_ref, sin_ref, wqkv_ref, q_ref, k_ref, v_ref, kraw_ref, vraw_ref):
    m = mod_ref[0]
    h = _modulate(x_ref[...], gn_ref[...], m[0:1], m[1:2]).astype(BF16)
    qkv = _dot(h, wqkv_ref[...])
    nq, nk = SWA_HEADS * SWA_HEAD_DIM, SWA_KV_HEADS * SWA_HEAD_DIM
    cos, sin = cos_ref[...], sin_ref[...]
    q_ref[...] = _rope(qkv[:, :nq], cos, sin).astype(BF16)
    k = qkv[:, nq:nq + nk]
    v = qkv[:, nq + nk:]
    kraw_ref[...] = k
    vraw_ref[...] = v
    k_ref[...] = _rope(k, cos, sin).astype(BF16)
    v_ref[...] = v.astype(BF16)


def _swa_pre(x, mods, gn, cos, sin, wqkv, rows):
    t, d = x.shape
    nq, nk = SWA_HEADS * SWA_HEAD_DIM, SWA_KV_HEADS * SWA_HEAD_DIM
    outs = [(nq, BF16), (nk, BF16), (nk, BF16), (nk, F32), (nk, F32)]
    return pl.pallas_call(
        _swa_pre_kernel,
        grid=(rows.n_tiles,),
        in_specs=[rows.row_spec(d), rows.mod_spec(), _whole(gn.shape), rows.rope_spec(), rows.rope_spec(),
                  _whole(wqkv.shape)],
        out_specs=[rows.row_spec(n) for n, _ in outs],
        out_shape=[jax.ShapeDtypeStruct((t, n), dt) for n, dt in outs],
        compiler_params=_params("arbitrary"),
        name="swa_pre",
    )(x, mods, gn, cos, sin, wqkv)


def _swa_heads(q_ref, sink_ref, o_ref, keys, values, masks):
    for kvh in range(SWA_KV_HEADS):
        kv_sl = slice(kvh * SWA_HEAD_DIM, (kvh + 1) * SWA_HEAD_DIM)
        ks = [k[:, kv_sl] for k in keys]
        vs = [v[:, kv_sl] for v in values]
        for g in range(SWA_GROUP):
            hq = kvh * SWA_GROUP + g
            q_sl = slice(hq * SWA_HEAD_DIM, (hq + 1) * SWA_HEAD_DIM)
            q = q_ref[:, q_sl]
            scores = []
            for k, mask in zip(ks, masks):
                s = _dot_nt(q, k) * SWA_SCALE
                scores.append(s if mask is None else jnp.where(mask, s, NEG_INF))
            o_ref[:, q_sl] = _softmax_pv(scores, vs, extra_logit=sink_ref[:, hq:hq + 1]).astype(BF16)


def _swa_attn_prompt_kernel(q_ref, k_ref, v_ref, sink_ref, o_ref):
    _swa_heads(q_ref, sink_ref, o_ref, [k_ref[...]], [v_ref[...]], [None])


def _swa_attn_prompt(q, k, v, sink, batch, seq):
    t = q.shape[0]
    blk = lambda w: pl.BlockSpec((seq, w), lambda b: (b, 0))
    return pl.pallas_call(
        _swa_attn_prompt_kernel,
        grid=(batch,),
        in_specs=[blk(q.shape[1]), blk(k.shape[1]), blk(v.shape[1]), _whole(sink.shape)],
        out_specs=blk(q.shape[1]),
        out_shape=jax.ShapeDtypeStruct((t, q.shape[1]), BF16),
        compiler_params=_params("arbitrary"),
        name="swa_attn_prompt",
    )(q, k, v, sink)


def _swa_attn_latent_kernel(o_prev_ref, q_ref, kc_ref, vc_ref, kp_ref, vp_ref, km_ref, vm_ref, kx_ref, vx_ref,
                            sink_ref, o_ref, *, tq, dec_seq):
    del o_prev_ref
    i = pl.program_id(1)
    qpos = i * tq + lax.broadcasted_iota(jnp.int32, (tq, 1), 0)

    def band(first, n):
        kpos = first + lax.broadcasted_iota(jnp.int32, (1, n), 1)
        return (jnp.abs(qpos - kpos) <= SWA_WINDOW) & (kpos >= 0) & (kpos < dec_seq)

    masks = [None, band(i * tq - SWA_WINDOW, SWA_WINDOW), band(i * tq, tq), band((i + 1) * tq, SWA_WINDOW)]
    keys = [kc_ref[...].astype(BF16), kp_ref[...], km_ref[...], kx_ref[...]]
    values = [vc_ref[...].astype(BF16), vp_ref[...], vm_ref[...], vx_ref[...]]
    _swa_heads(q_ref, sink_ref, o_ref, keys, values, masks)


def _swa_attn_latent(o_prev, q, k, v, k_ctx, v_ctx, sink, n_prompt, dec_batch, dec_seq, past, tq):
    t = q.shape[0]
    nq = dec_seq // tq
    w = SWA_WINDOW
    nk = k.shape[1]
    first = lambda b, i: n_prompt + b * dec_seq + i * tq
    qblk = lambda width: pl.BlockSpec((tq, width), lambda b, i: (first(b, i) // tq, 0))
    prev = pl.BlockSpec((w, nk), lambda b, i: (first(b, i) // w - 1, 0))
    nxt = pl.BlockSpec((w, nk), lambda b, i: (jnp.minimum((first(b, i) + tq) // w, t // w - 1), 0))
    ctx = pl.BlockSpec((past, nk), lambda b, i: (b, 0))
    return pl.pallas_call(
        functools.partial(_swa_attn_latent_kernel, tq=tq, dec_seq=dec_seq),
        grid=(dec_batch, nq),
        in_specs=[pl.BlockSpec(memory_space=pl.ANY), qblk(q.shape[1]), ctx, ctx, prev, prev, qblk(nk), qblk(nk),
                  nxt, nxt, _whole(sink.shape)],
        out_specs=qblk(q.shape[1]),
        out_shape=jax.ShapeDtypeStruct(o_prev.shape, o_prev.dtype),
        input_output_aliases={0: 0},
        compiler_params=_params("arbitrary", "arbitrary"),
        name="swa_attn_latent",
    )(o_prev, q, k_ctx, v_ctx, k, v, k, v, k, v, sink)


def _hgrn_pre_kernel(x_ref, mod_ref, gn_ref, w_ref, lbl_ref, q_ref, v_ref, g_ref, f_ref, *, layer):
    m = mod_ref[0]
    h = _modulate(x_ref[...], gn_ref[...], m[0:1], m[1:2]).astype(BF16)
    y = _dot(h, w_ref[...])
    n = HG_HEADS * HG_DK
    q_ref[...] = _silu(y[:, :n])
    v_ref[...] = y[:, n:2 * n]
    g_ref[...] = _silu(y[:, 2 * n:3 * n])
    for d in range(2):
        logits = lbl_ref[d]
        e = jnp.exp(logits - jnp.max(logits, axis=0, keepdims=True))
        s = e / jnp.sum(e, axis=0, keepdims=True)
        cs = s[0:1]
        for r in range(1, layer + 1):
            cs = cs + s[r:r + 1]
        lb = cs - s[0:1]
        f_ref[d] = lb + (1.0 - lb) * _sigmoid(y[:, (3 + d) * n:(4 + d) * n])


def _hgrn_pre(x, mods, gn, w5, lb_logits, layer, rows):
    t, d = x.shape
    n = HG_HEADS * HG_DK
    return pl.pallas_call(
        functools.partial(_hgrn_pre_kernel, layer=layer),
        grid=(rows.n_tiles,),
        in_specs=[rows.row_spec(d), rows.mod_spec(), _whole(gn.shape), _whole(w5.shape), _whole(lb_logits.shape)],
        out_specs=[rows.row_spec(n), rows.row_spec(n), rows.row_spec(n),
                   pl.BlockSpec((2, rows.tm, n), lambda i: (0, i, 0))],
        out_shape=[jax.ShapeDtypeStruct((t, n), F32)] * 3 + [jax.ShapeDtypeStruct((2, t, n), F32)],
        compiler_params=_params("arbitrary"),
        name="hgrn_pre",
    )(x, mods, gn, w5, lb_logits)


def _tri_cumsum(tri, x):
    hi = x.astype(BF16)
    r1 = x - hi.astype(F32)
    mid = r1.astype(BF16)
    lo = (r1 - mid.astype(F32)).astype(BF16)
    return _dot(tri, hi) + _dot(tri, mid) + _dot(tri, lo)


def _hgrn_scan_kernel(*refs, chunk, n_chunks, has_init):
    if has_init:
        o_prev_ref, q_ref, v_ref, f_ref, s0_ref, o_ref, sfin_ref, st_ref = refs
        del o_prev_ref
    else:
        q_ref, v_ref, f_ref, o_ref, sfin_ref, st_ref = refs
    d = pl.program_id(1)
    t = pl.program_id(2)

    @pl.when(t == 0)
    def _():
        for h in range(HG_HEADS):
            st_ref[h] = s0_ref[0, 0, h].T if has_init else jnp.zeros((HG_DV, HG_DK), F32)

    c = chunk
    row = lax.broadcasted_iota(jnp.int32, (c, c), 0)
    col = lax.broadcasted_iota(jnp.int32, (c, c), 1)
    keep = ((row - col) * (1 - 2 * d)) >= 0
    tri = jnp.where(keep, 1.0, 0.0).astype(BF16)

    def step(ci, carry):
        r0 = pl.multiple_of(jnp.where(d == 0, ci, n_chunks - 1 - ci) * c, c)
        q = q_ref[pl.ds(r0, c), :]
        f = f_ref[0, pl.ds(r0, c), :]
        vb = v_ref[pl.ds(r0, c), :].astype(BF16)
        lf = jnp.log(f)
        cum = _tri_cumsum(tri, lf)
        tot = jnp.sum(lf, axis=0, keepdims=True)
        mid = cum[c // 2:c // 2 + 1, :]
        kk = 1.0 - f
        q_in = (q * jnp.exp(cum)).astype(BF16)
        q_a = (q * jnp.exp(cum - mid)).astype(BF16)
        k_a = (kk * jnp.exp(mid - cum)).astype(BF16)
        k_d = (kk * jnp.exp(tot - cum)).astype(BF16)
        e_tot = jnp.exp(tot)
        outs = []
        for h in range(HG_HEADS):
            sl = slice(h * HG_DK, (h + 1) * HG_DK)
            st = st_ref[h]
            att = jnp.where(keep, _dot_nt(q_a[:, sl], k_a[:, sl]), 0.0).astype(BF16)
            outs.append(_dot_nt(q_in[:, sl], st.astype(BF16)) + _dot(att, vb[:, sl]))
            st_ref[h] = st * e_tot[:, sl] + _dot_tn(vb[:, sl], k_d[:, sl])
        o_ref[0, pl.ds(r0, c), :] = jnp.concatenate(outs, axis=1)
        return carry

    lax.fori_loop(0, n_chunks, step, 0)

    @pl.when(t == pl.num_programs(2) - 1)
    def _():
        for h in range(HG_HEADS):
            sfin_ref[0, 0, h] = st_ref[h].T


def _hgrn_scan(q, v, f2, batch, seq, first_row, rt, s0=None, o_prev=None):
    t, n = q.shape
    nt = seq // rt
    has_init = s0 is not None

    def rblk(b, d, i):
        return first_row // rt + b * nt + jnp.where(d == 0, i, nt - 1 - i)

    row = pl.BlockSpec((rt, n), lambda b, d, i: (rblk(b, d, i), 0))
    dir_row = pl.BlockSpec((1, rt, n), lambda b, d, i: (d, rblk(b, d, i), 0))
    state = pl.BlockSpec((1, 1, HG_HEADS, HG_DK, HG_DV), lambda b, d, i: (b, d, 0, 0, 0))
    ins, in_specs, alias = [q, v, f2], [row, row, dir_row], {}
    if has_init:
        ins = [o_prev] + ins + [s0]
        in_specs = [pl.BlockSpec(memory_space=pl.ANY)] + in_specs + [state]
        alias = {0: 0}
    return pl.pallas_call(
        functools.partial(_hgrn_scan_kernel, chunk=HG_SCAN_CHUNK, n_chunks=rt // HG_SCAN_CHUNK, has_init=has_init),
        grid=(batch, 2, nt),
        in_specs=in_specs,
        out_specs=[dir_row, state],
        out_shape=[jax.ShapeDtypeStruct((2, t, n), F32),
                   jax.ShapeDtypeStruct((batch, 2, HG_HEADS, HG_DK, HG_DV), F32)],
        scratch_shapes=[pltpu.VMEM((HG_HEADS, HG_DV, HG_DK), F32)],
        input_output_aliases=alias,
        compiler_params=_params("arbitrary", "arbitrary", "arbitrary"),
        name="hgrn_scan_latent" if has_init else "hgrn_scan_prompt",
    )(*ins)


def _post_kernel(*refs, hgrn, final):
    refs = list(refs)
    out_ref = refs.pop()
    if hgrn:
        od_ref, g_ref, onorm_ref = refs[:3]
        refs = refs[3:]
        o2 = od_ref[0] + od_ref[1]
        gate = g_ref[...]
        onorm = onorm_ref[...]
        parts = []
        for h in range(HG_HEADS):
            sl = slice(h * HG_DV, (h + 1) * HG_DV)
            parts.append(_rms(o2[:, sl], onorm) * gate[:, sl])
        o = jnp.concatenate(parts, axis=1).astype(BF16)
    else:
        o = refs[0][...]
        refs = refs[1:]
    x_ref, mod_ref, wo_ref, gn_ref, wg_ref, wu_ref, wd_ref = refs[:7]
    m = mod_ref[0]
    x1 = x_ref[...] + m[2:3] * _dot(o, wo_ref[...])
    h2 = _modulate(x1, gn_ref[...], m[3:4], m[4:5]).astype(BF16)
    a = (_silu(_dot(h2, wg_ref[...])) * _dot(h2, wu_ref[...])).astype(BF16)
    x2 = x1 + m[5:6] * _dot(a, wd_ref[...])
    if final:
        x2 = _rms(x2, refs[7][...])
    out_ref[...] = x2


def _post(mix, x, mods, wo, gn, wg, wu, wd, rows, final_norm=None, hgrn=False):
    t, d = x.shape
    if hgrn:
        od, gate, onorm = mix
        head = [od, gate, onorm]
        head_specs = [pl.BlockSpec((2, rows.tm, d), lambda i: (0, i, 0)), rows.row_spec(d), _whole(onorm.shape)]
    else:
        head = [mix]
        head_specs = [rows.row_spec(mix.shape[1])]
    ins = head + [x, mods, wo, gn, wg, wu, wd]
    in_specs = head_specs + [rows.row_spec(d), rows.mod_spec()] + [_whole(a.shape) for a in (wo, gn, wg, wu, wd)]
    if final_norm is not None:
        ins.append(final_norm)
        in_specs.append(_whole(final_norm.shape))
    return pl.pallas_call(
        functools.partial(_post_kernel, hgrn=hgrn, final=final_norm is not None),
        grid=(rows.n_tiles,),
        in_specs=in_specs,
        out_specs=rows.row_spec(d),
        out_shape=jax.ShapeDtypeStruct((t, d), F32),
        compiler_params=_params("arbitrary"),
        name="post",
    )(*ins)


def _pick_tile(n_prompt_rows, dec_seq, want):
    tm = want
    while n_prompt_rows % tm or dec_seq % tm:
        tm //= 2
    return tm


def kernel(x_prompt, x_sample, cache_mla_ckv, cache_mla_krope, state_hgrn, cache_swa_k, cache_swa_v, c, c_ctx, ada_w, ada_b, norm_mix, norm_ffn, ffn_w_gate, ffn_w_up, ffn_w_down, final_norm, mla_w_dq, mla_q_norm, mla_w_uq, mla_w_dkv, mla_kv_norm, mla_w_uk, mla_w_uv, mla_w_o, hg_w_q, hg_w_f, hg_w_i, hg_w_g, hg_o_norm, hg_w_o, hg_lb_logits, swa_w_q, swa_w_k, swa_w_v, swa_w_o, swa_sink):
    batch, seq, d = x_prompt.shape
    dec_batch, dec_seq, _ = x_sample.shape
    past = cache_mla_ckv.shape[2]
    depth = ada_w.shape[0]
    n_prompt = batch * seq
    n_rows = n_prompt + dec_batch * dec_seq
    assert dec_batch + 1 <= COND_ROWS and seq % SWA_WINDOW == 0 and dec_seq % (2 * SWA_WINDOW) == 0

    pre_rows = _Rows(n_prompt, dec_seq, n_rows, _pick_tile(n_prompt, dec_seq, 256))
    post_rows = _Rows(n_prompt, dec_seq, n_rows, _pick_tile(n_prompt, dec_seq, 512))
    tq = _pick_tile(n_prompt, dec_seq, 256)
    cos, sin = _rope_tables(dec_seq, pre_rows.tm)

    cond = jnp.concatenate([c_ctx[None, :], c, jnp.zeros((COND_ROWS - 1 - dec_batch, d), F32)], axis=0)
    mods = _adaln(cond, ada_w, ada_b).reshape(depth, COND_ROWS, 6, d)

    x = jnp.concatenate([x_prompt.reshape(n_prompt, d), x_sample.reshape(dec_batch * dec_seq, d)], axis=0)
    row1 = lambda a: a.reshape(1, -1)
    new_ckv, new_krope, new_hg, new_k, new_v = [], [], [], [], []
    for i in range(depth):
        kind, j = i % N_MIXERS, i // N_MIXERS
        gn = row1(norm_mix[i])
        if kind == 0:
            uq = mla_w_uq[j].reshape(-1, MLA_HEADS, MLA_NOPE_DIM + MLA_ROPE_DIM)
            w = {
                "dq": mla_w_dq[j].astype(BF16), "q_norm": row1(mla_q_norm[j]),
                "uq_nope": uq[:, :, :MLA_NOPE_DIM].reshape(uq.shape[0], -1).astype(BF16),
                "uq_rope": uq[:, :, MLA_NOPE_DIM:].reshape(uq.shape[0], -1).astype(BF16),
                "dkv": jnp.pad(mla_w_dkv[j], ((0, 0), (0, LANES - MLA_ROPE_DIM))).astype(BF16),
                "kv_norm": row1(mla_kv_norm[j]),
                "uk": mla_w_uk[j].astype(BF16), "uv": mla_w_uv[j].astype(BF16),
            }
            qn, qr, ckv, kr_raw, kr, kn, v = _mla_pre(x, mods[i], gn, cos, sin, w, pre_rows)
            kn_ctx, v_ctx = _mla_expand(cache_mla_ckv[:, j].reshape(dec_batch * past, -1), w["uk"], w["uv"])
            kr_ctx = cache_mla_krope[:, j].reshape(dec_batch * past, -1)
            o = _mla_attn_prompt(qn, qr, kn, kr, v, batch, seq)
            mix = _mla_attn_latent(o, qn, qr, kn, kr, v, kn_ctx, kr_ctx, v_ctx, n_prompt, dec_batch, dec_seq, past, tq)
            wo = mla_w_o[j].astype(BF16)
            new_ckv.append(ckv[:n_prompt].reshape(batch, seq, -1))
            new_krope.append(kr_raw[:n_prompt].reshape(batch, seq, -1))
        elif kind == 1:
            w5 = jnp.concatenate([hg_w_q[j], hg_w_i[j], hg_w_g[j], hg_w_f[j, 0], hg_w_f[j, 1]], axis=1).astype(BF16)
            q, v, gate, f2 = _hgrn_pre(x, mods[i], gn, w5, hg_lb_logits, i, pre_rows)
            rt = _pick_tile(seq, dec_seq, 256)
            od, s_prompt = _hgrn_scan(q, v, f2, batch, seq, 0, rt)
            od, _ = _hgrn_scan(q, v, f2, dec_batch, dec_seq, n_prompt, rt, s0=state_hgrn[:, j], o_prev=od)
            mix = (od, gate, row1(hg_o_norm[j]))
            wo = hg_w_o[j].astype(BF16)
            new_hg.append(s_prompt)
        else:
            wqkv = jnp.concatenate([swa_w_q[j], swa_w_k[j], swa_w_v[j]], axis=1).astype(BF16)
            q, k, v, k_raw, v_raw = _swa_pre(x, mods[i], gn, cos, sin, wqkv, pre_rows)
            sink = row1(swa_sink[j])
            o = _swa_attn_prompt(q, k, v, sink, batch, seq)
            k_ctx = cache_swa_k[:, j].reshape(dec_batch * past, -1)
            v_ctx = cache_swa_v[:, j].reshape(dec_batch * past, -1)
            mix = _swa_attn_latent(o, q, k, v, k_ctx, v_ctx, sink, n_prompt, dec_batch, dec_seq, past, tq)
            wo = swa_w_o[j].astype(BF16)
            new_k.append(k_raw[:n_prompt].reshape(batch, seq, SWA_KV_HEADS, SWA_HEAD_DIM))
            new_v.append(v_raw[:n_prompt].reshape(batch, seq, SWA_KV_HEADS, SWA_HEAD_DIM))
        x = _post(mix, x, mods[i], wo, row1(norm_ffn[i]), ffn_w_gate[i].astype(BF16), ffn_w_up[i].astype(BF16),
                  ffn_w_down[i].astype(BF16), post_rows,
                  final_norm=row1(final_norm) if i == depth - 1 else None, hgrn=kind == 1)
    y_prompt = x[:n_prompt].reshape(batch, seq, d)
    y_sample = x[n_prompt:].reshape(dec_batch, dec_seq, d)
    return (y_prompt, y_sample, jnp.stack(new_ckv, axis=1), jnp.stack(new_krope, axis=1),
            jnp.stack(new_hg, axis=1), jnp.stack(new_k, axis=1), jnp.stack(new_v, axis=1))
```

```python
import functools

import jax
import jax.numpy as jnp
from jax import lax
from jax.experimental import pallas as pl
from jax.experimental.pallas import tpu as pltpu

F32 = jnp.float32
BF16 = jnp.bfloat16

GRID_W = 64
N_MIXERS = 3

MLA_HEADS = 8
MLA_KV_LORA = 256
MLA_NOPE_DIM = 128
MLA_ROPE_DIM = 64
MLA_V_DIM = 128
MLA_SCALE = (MLA_NOPE_DIM + MLA_ROPE_DIM) ** -0.5

HG_HEADS = 8
HG_DK = 128
HG_DV = 128
HG_DIAG_BLOCK = 32

SWA_HEADS = 16
SWA_KV_HEADS = 4
SWA_GROUP = SWA_HEADS // SWA_KV_HEADS
SWA_HEAD_DIM = 64
SWA_WINDOW = 128
SWA_SCALE = SWA_HEAD_DIM ** -0.5

ROPE_BASE = 10000.0
ROPE_PERIOD = 64
ROPE_QUARTER = 16
NORM_EPS = 1e-6
NEG_INF = -1e30

LANES = 128
COND_ROWS = 8
VMEM_LIMIT = 56 * 1024 * 1024


def _sigmoid(x):
    return jax.nn.sigmoid(x)


def _silu(x):
    return x * jax.nn.sigmoid(x)


def _rms(x, g):
    return x * lax.rsqrt(jnp.mean(x * x, axis=-1, keepdims=True) + NORM_EPS) * g


def _modulate(x, g, shift, scale):
    return _rms(x, g) * (1.0 + scale) + shift


def _dot(a, b):
    return jnp.dot(a, b, preferred_element_type=F32)


def _dot_nt(a, b):
    return lax.dot_general(a, b, (((1,), (1,)), ((), ())), preferred_element_type=F32)


def _dot_tn(a, b):
    return lax.dot_general(a, b, (((0,), (0,)), ((), ())), preferred_element_type=F32)


def _swap_pairs(x):
    n = x.shape[1]
    lane = lax.broadcasted_iota(jnp.int32, x.shape, 1)
    ahead = pltpu.roll(x, n - ROPE_QUARTER, 1)
    behind = pltpu.roll(x, ROPE_QUARTER, 1)
    return jnp.where((lane & (2 * ROPE_QUARTER - 1)) < ROPE_QUARTER, ahead, behind)


def _rope(x, cos, sin):
    reps = x.shape[1] // LANES
    if reps > 1:
        cos = jnp.concatenate([cos] * reps, axis=1)
        sin = jnp.concatenate([sin] * reps, axis=1)
    return x * cos + _swap_pairs(x) * sin


def _softmax_pv(scores, values, extra_logit=None):
    m = functools.reduce(jnp.maximum, [jnp.max(s, axis=1, keepdims=True) for s in scores])
    if extra_logit is not None:
        m = jnp.maximum(m, extra_logit)
    acc, den = None, None
    for s, v in zip(scores, values):
        p = jnp.exp(s - m)
        d = jnp.sum(p, axis=1, keepdims=True)
        a = _dot(p.astype(BF16), v)
        acc = a if acc is None else acc + a
        den = d if den is None else den + d
    if extra_logit is not None:
        den = den + jnp.exp(extra_logit - m)
    return acc / den


def _whole(shape):
    zeros = (0,) * len(shape)
    return pl.BlockSpec(shape, lambda *_: zeros, pipeline_mode=pl.Buffered(1))


def _params(*sem):
    return pltpu.CompilerParams(dimension_semantics=sem, vmem_limit_bytes=VMEM_LIMIT)


class _Rows:
    def __init__(self, n_prompt_rows, dec_seq, n_rows, tm):
        assert n_prompt_rows % tm == 0 and dec_seq % tm == 0
        self.tm = tm
        self.n_tiles = n_rows // tm
        self.prompt_tiles = n_prompt_rows // tm
        self.seq_tiles = dec_seq // tm

    def cond(self, i):
        return jnp.where(i < self.prompt_tiles, 0, 1 + jnp.maximum(i - self.prompt_tiles, 0) // self.seq_tiles)

    def rope_block(self, i):
        return jnp.where(i < self.prompt_tiles, 0, 1 + jnp.maximum(i - self.prompt_tiles, 0) % self.seq_tiles)

    def row_spec(self, width):
        return pl.BlockSpec((self.tm, width), lambda i: (i, 0))

    def mod_spec(self, d):
        return pl.BlockSpec((1, 6, d), lambda i: (self.cond(i), 0, 0))

    def rope_spec(self):
        return pl.BlockSpec((self.tm, LANES), lambda i: (self.rope_block(i), 0))


def _rope_tables(dec_seq, tm):
    pos = jnp.arange(dec_seq)
    row = (pos // GRID_W).astype(F32)
    col = (pos % GRID_W).astype(F32)
    inv_freq = ROPE_BASE ** (-jnp.arange(ROPE_QUARTER, dtype=F32) / ROPE_QUARTER)
    ang_r = row[:, None] * inv_freq[None, :]
    ang_c = col[:, None] * inv_freq[None, :]
    cos = jnp.concatenate([jnp.cos(ang_r), jnp.cos(ang_r), jnp.cos(ang_c), jnp.cos(ang_c)], axis=1)
    sin = jnp.concatenate([-jnp.sin(ang_r), jnp.sin(ang_r), -jnp.sin(ang_c), jnp.sin(ang_c)], axis=1)
    cos = jnp.concatenate([jnp.ones((tm, ROPE_PERIOD), F32), cos], axis=0)
    sin = jnp.concatenate([jnp.zeros((tm, ROPE_PERIOD), F32), sin], axis=0)
    return jnp.tile(cos, (1, LANES // ROPE_PERIOD)), jnp.tile(sin, (1, LANES // ROPE_PERIOD))


def _adaln_kernel(c_ref, w_ref, b_ref, o_ref):
    o_ref[0] = _dot(_silu(c_ref[...]), w_ref[0]) + b_ref[0]


def _adaln(cond, ada_w, ada_b, tn=1536):
    depth, d, n = ada_w.shape
    return pl.pallas_call(
        _adaln_kernel,
        grid=(depth, n // tn),
        in_specs=[pl.BlockSpec((COND_ROWS, d), lambda l, j: (0, 0)),
                  pl.BlockSpec((1, d, tn), lambda l, j: (l, 0, j)),
                  pl.BlockSpec((1, 1, tn), lambda l, j: (l, 0, j))],
        out_specs=pl.BlockSpec((1, COND_ROWS, tn), lambda l, j: (l, 0, j)),
        out_shape=jax.ShapeDtypeStruct((depth, COND_ROWS, n), F32),
        compiler_params=_params("arbitrary", "arbitrary"),
        name="adaln",
    )(cond, ada_w, ada_b.reshape(depth, 1, n))


def _mla_pre_kernel(x_ref, mod_ref, gn_ref, cos_ref, sin_ref, wdq_ref, qnorm_ref, wuqn_ref, wuqr_ref,
                    wdkv_ref, kvnorm_ref, wuk_ref, wuv_ref,
                    qn_ref, qr_ref, ckv_ref, krraw_ref, kr_ref, kn_ref, v_ref):
    m = mod_ref[0]
    h = _modulate(x_ref[...], gn_ref[...], m[0:1], m[1:2]).astype(BF16)
    q_lat = _rms(_dot(h, wdq_ref[...]), qnorm_ref[...]).astype(BF16)
    qn_ref[...] = _dot(q_lat, wuqn_ref[...]).astype(BF16)
    cos, sin = cos_ref[...], sin_ref[...]
    qr_ref[...] = _rope(_dot(q_lat, wuqr_ref[...]), cos, sin).astype(BF16)
    kv = _dot(h, wdkv_ref[...])
    ckv = _rms(kv[:, :MLA_KV_LORA], kvnorm_ref[...])
    ckv_ref[...] = ckv
    kr = kv[:, MLA_KV_LORA:]
    krraw_ref[...] = kr[:, :MLA_ROPE_DIM]
    kr_ref[...] = _rope(kr, cos, sin)[:, :MLA_ROPE_DIM].astype(BF16)
    cb = ckv.astype(BF16)
    kn_ref[...] = _dot(cb, wuk_ref[...]).astype(BF16)
    v_ref[...] = _dot(cb, wuv_ref[...]).astype(BF16)


def _mla_pre(x, mods, gn, cos, sin, w, rows):
    t, d = x.shape
    hn, hr = MLA_HEADS * MLA_NOPE_DIM, MLA_HEADS * MLA_ROPE_DIM
    ins = [x, mods, gn, cos, sin, w["dq"], w["q_norm"], w["uq_nope"], w["uq_rope"], w["dkv"], w["kv_norm"],
           w["uk"], w["uv"]]
    in_specs = [rows.row_spec(d), rows.mod_spec(d), _whole(gn.shape), rows.rope_spec(), rows.rope_spec()]
    in_specs += [_whole(a.shape) for a in ins[5:]]
    outs = [(hn, BF16), (hr, BF16), (MLA_KV_LORA, F32), (MLA_ROPE_DIM, F32), (MLA_ROPE_DIM, BF16),
            (hn, BF16), (MLA_HEADS * MLA_V_DIM, BF16)]
    return pl.pallas_call(
        _mla_pre_kernel,
        grid=(rows.n_tiles,),
        in_specs=in_specs,
        out_specs=[rows.row_spec(n) for n, _ in outs],
        out_shape=[jax.ShapeDtypeStruct((t, n), dt) for n, dt in outs],
        compiler_params=_params("arbitrary"),
        name="mla_pre",
    )(*ins)


def _mla_expand_kernel(c_ref, wuk_ref, wuv_ref, kn_ref, v_ref):
    cb = c_ref[...].astype(BF16)
    kn_ref[...] = _dot(cb, wuk_ref[...]).astype(BF16)
    v_ref[...] = _dot(cb, wuv_ref[...]).astype(BF16)


def _mla_expand(ckv, wuk, wuv):
    n = ckv.shape[0]
    return pl.pallas_call(
        _mla_expand_kernel,
        out_shape=[jax.ShapeDtypeStruct((n, wuk.shape[1]), BF16), jax.ShapeDtypeStruct((n, wuv.shape[1]), BF16)],
        compiler_params=pltpu.CompilerParams(vmem_limit_bytes=VMEM_LIMIT),
        name="mla_expand",
    )(ckv, wuk, wuv)


def _mla_attn_kernel(*refs, n_src):
    qn_ref, qr_ref = refs[0], refs[1]
    srcs = [refs[2 + 3 * i: 5 + 3 * i] for i in range(n_src)]
    o_ref = refs[-1]
    for h in range(MLA_HEADS):
        nope = slice(h * MLA_NOPE_DIM, (h + 1) * MLA_NOPE_DIM)
        q1 = qn_ref[:, nope]
        q2 = qr_ref[:, h * MLA_ROPE_DIM:(h + 1) * MLA_ROPE_DIM]
        scores, values = [], []
        for kn_ref, kr_ref, v_ref in srcs:
            s = _dot_nt(q1, kn_ref[:, nope]) + _dot_nt(q2, kr_ref[...].astype(BF16))
            scores.append(s * MLA_SCALE)
            values.append(v_ref[:, h * MLA_V_DIM:(h + 1) * MLA_V_DIM])
        o_ref[:, h * MLA_V_DIM:(h + 1) * MLA_V_DIM] = _softmax_pv(scores, values).astype(BF16)


def _mla_attn_prompt(qn, qr, kn, kr, v, batch, seq):
    t = qn.shape[0]
    blk = lambda w: pl.BlockSpec((seq, w), lambda b: (b, 0))
    return pl.pallas_call(
        functools.partial(_mla_attn_kernel, n_src=1),
        grid=(batch,),
        in_specs=[blk(qn.shape[1]), blk(qr.shape[1]), blk(kn.shape[1]), blk(kr.shape[1]), blk(v.shape[1])],
        out_specs=blk(v.shape[1]),
        out_shape=jax.ShapeDtypeStruct((t, v.shape[1]), BF16),
        compiler_params=_params("arbitrary"),
        name="mla_attn_prompt",
    )(qn, qr, kn, kr, v)


def _mla_attn_latent_kernel(o_prev_ref, *refs):
    del o_prev_ref
    _mla_attn_kernel(*refs, n_src=2)


def _mla_attn_latent(o_prev, qn, qr, kn, kr, v, kn_ctx, kr_ctx, v_ctx, n_prompt, dec_batch, dec_seq, past, tq):
    nq = dec_seq // tq
    qblk = lambda w: pl.BlockSpec((tq, w), lambda b, i: (n_prompt // tq + b * nq + i, 0))
    lat = lambda w: pl.BlockSpec((dec_seq, w), lambda b, i: (n_prompt // dec_seq + b, 0))
    ctx = lambda w: pl.BlockSpec((past, w), lambda b, i: (b, 0))
    return pl.pallas_call(
        _mla_attn_latent_kernel,
        grid=(dec_batch, nq),
        in_specs=[pl.BlockSpec(memory_space=pl.ANY), qblk(qn.shape[1]), qblk(qr.shape[1]),
                  ctx(kn.shape[1]), ctx(kr.shape[1]), ctx(v.shape[1]),
                  lat(kn.shape[1]), lat(kr.shape[1]), lat(v.shape[1])],
        out_specs=qblk(v.shape[1]),
        out_shape=jax.ShapeDtypeStruct(o_prev.shape, o_prev.dtype),
        input_output_aliases={0: 0},
        compiler_params=_params("arbitrary", "arbitrary"),
        name="mla_attn_latent",
    )(o_prev, qn, qr, kn_ctx, kr_ctx, v_ctx, kn, kr, v)


def _swa_pre_kernel(x_ref, mod_ref, gn_ref, cos_ref, sin_ref, wqkv_ref, q_ref, k_ref, v_ref, kraw_ref, vraw_ref):
    m = mod_ref[0]
    h = _modulate(x_ref[...], gn_ref[...], m[0:1], m[1:2]).astype(BF16)
    qkv = _dot(h, wqkv_ref[...])
    nq, nk = SWA_HEADS * SWA_HEAD_DIM, SWA_KV_HEADS * SWA_HEAD_DIM
    cos, sin = cos_ref[...], sin_ref[...]
    q_ref[...] = _rope(qkv[:, :nq], cos, sin).astype(BF16)
    k = qkv[:, nq:nq + nk]
    v = qkv[:, nq + nk:]
    kraw_ref[...] = k
    vraw_ref[...] = v
    k_ref[...] = _rope(k, cos, sin).astype(BF16)
    v_ref[...] = v.astype(BF16)


def _swa_pre(x, mods, gn, cos, sin, wqkv, rows):
    t, d = x.shape
    nq, nk = SWA_HEADS * SWA_HEAD_DIM, SWA_KV_HEADS * SWA_HEAD_DIM
    outs = [(nq, BF16), (nk, BF16), (nk, BF16), (nk, F32), (nk, F32)]
    return pl.pallas_call(
        _swa_pre_kernel,
        grid=(rows.n_tiles,),
        in_specs=[rows.row_spec(d), rows.mod_spec(d), _whole(gn.shape), rows.rope_spec(), rows.rope_spec(),
                  _whole(wqkv.shape)],
        out_specs=[rows.row_spec(n) for n, _ in outs],
        out_shape=[jax.ShapeDtypeStruct((t, n), dt) for n, dt in outs],
        compiler_params=_params("arbitrary"),
        name="swa_pre",
    )(x, mods, gn, cos, sin, wqkv)


def _swa_heads(q_ref, sink_ref, o_ref, keys, values, masks):
    for kvh in range(SWA_KV_HEADS):
        kv_sl = slice(kvh * SWA_HEAD_DIM, (kvh + 1) * SWA_HEAD_DIM)
        ks = [k[:, kv_sl] for k in keys]
        vs = [v[:, kv_sl] for v in values]
        for g in range(SWA_GROUP):
            hq = kvh * SWA_GROUP + g
            q_sl = slice(hq * SWA_HEAD_DIM, (hq + 1) * SWA_HEAD_DIM)
            q = q_ref[:, q_sl]
            scores = []
            for k, mask in zip(ks, masks):
                s = _dot_nt(q, k) * SWA_SCALE
                scores.append(s if mask is None else jnp.where(mask, s, NEG_INF))
            o_ref[:, q_sl] = _softmax_pv(scores, vs, extra_logit=sink_ref[:, hq:hq + 1]).astype(BF16)


def _swa_attn_prompt_kernel(q_ref, k_ref, v_ref, sink_ref, o_ref):
    _swa_heads(q_ref, sink_ref, o_ref, [k_ref[...]], [v_ref[...]], [None])


def _swa_attn_prompt(q, k, v, sink, batch, seq):
    t = q.shape[0]
    blk = lambda w: pl.BlockSpec((seq, w), lambda b: (b, 0))
    return pl.pallas_call(
        _swa_attn_prompt_kernel,
        grid=(batch,),
        in_specs=[blk(q.shape[1]), blk(k.shape[1]), blk(v.shape[1]), _whole(sink.shape)],
        out_specs=blk(q.shape[1]),
        out_shape=jax.ShapeDtypeStruct((t, q.shape[1]), BF16),
        compiler_params=_params("arbitrary"),
        name="swa_attn_prompt",
    )(q, k, v, sink)


def _swa_attn_latent_kernel(o_prev_ref, q_ref, kc_ref, vc_ref, kp_ref, vp_ref, km_ref, vm_ref, kx_ref, vx_ref,
                            sink_ref, o_ref, *, tq, dec_seq):
    del o_prev_ref
    i = pl.program_id(1)
    qpos = i * tq + lax.broadcasted_iota(jnp.int32, (tq, 1), 0)

    def band(first, n):
        kpos = first + lax.broadcasted_iota(jnp.int32, (1, n), 1)
        return (jnp.abs(qpos - kpos) <= SWA_WINDOW) & (kpos >= 0) & (kpos < dec_seq)

    masks = [None, band(i * tq - SWA_WINDOW, SWA_WINDOW), band(i * tq, tq), band((i + 1) * tq, SWA_WINDOW)]
    keys = [kc_ref[...].astype(BF16), kp_ref[...], km_ref[...], kx_ref[...]]
    values = [vc_ref[...].astype(BF16), vp_ref[...], vm_ref[...], vx_ref[...]]
    _swa_heads(q_ref, sink_ref, o_ref, keys, values, masks)


def _swa_attn_latent(o_prev, q, k, v, k_ctx, v_ctx, sink, n_prompt, dec_batch, dec_seq, past, tq):
    t = q.shape[0]
    nq = dec_seq // tq
    w = SWA_WINDOW
    nk = k.shape[1]
    first = lambda b, i: n_prompt + b * dec_seq + i * tq
    qblk = lambda width: pl.BlockSpec((tq, width), lambda b, i: (first(b, i) // tq, 0))
    prev = pl.BlockSpec((w, nk), lambda b, i: (first(b, i) // w - 1, 0))
    nxt = pl.BlockSpec((w, nk), lambda b, i: (jnp.minimum((first(b, i) + tq) // w, t // w - 1), 0))
    ctx = pl.BlockSpec((past, nk), lambda b, i: (b, 0))
    return pl.pallas_call(
        functools.partial(_swa_attn_latent_kernel, tq=tq, dec_seq=dec_seq),
        grid=(dec_batch, nq),
        in_specs=[pl.BlockSpec(memory_space=pl.ANY), qblk(q.shape[1]), ctx, ctx, prev, prev, qblk(nk), qblk(nk),
                  nxt, nxt, _whole(sink.shape)],
        out_specs=qblk(q.shape[1]),
        out_shape=jax.ShapeDtypeStruct(o_prev.shape, o_prev.dtype),
        input_output_aliases={0: 0},
        compiler_params=_params("arbitrary", "arbitrary"),
        name="swa_attn_latent",
    )(o_prev, q, k_ctx, v_ctx, k, v, k, v, k, v, sink)


def _hgrn_pre_kernel(x_ref, mod_ref, gn_ref, w_ref, lbl_ref, q_ref, v_ref, g_ref, f_ref, *, layer):
    m = mod_ref[0]
    h = _modulate(x_ref[...], gn_ref[...], m[0:1], m[1:2]).astype(BF16)
    y = _dot(h, w_ref[...])
    n = HG_HEADS * HG_DK
    q_ref[...] = _silu(y[:, :n])
    v_ref[...] = y[:, n:2 * n]
    g_ref[...] = _silu(y[:, 2 * n:3 * n])
    for d in range(2):
        logits = lbl_ref[d]
        e = jnp.exp(logits - jnp.max(logits, axis=0, keepdims=True))
        s = e / jnp.sum(e, axis=0, keepdims=True)
        cs = s[0:1]
        for r in range(1, layer + 1):
            cs = cs + s[r:r + 1]
        lb = cs - s[0:1]
        f_ref[d] = lb + (1.0 - lb) * _sigmoid(y[:, (3 + d) * n:(4 + d) * n])


def _hgrn_pre(x, mods, gn, w5, lb_logits, layer, rows):
    t, d = x.shape
    n = HG_HEADS * HG_DK
    return pl.pallas_call(
        functools.partial(_hgrn_pre_kernel, layer=layer),
        grid=(rows.n_tiles,),
        in_specs=[rows.row_spec(d), rows.mod_spec(d), _whole(gn.shape), _whole(w5.shape), _whole(lb_logits.shape)],
        out_specs=[rows.row_spec(n), rows.row_spec(n), rows.row_spec(n),
                   pl.BlockSpec((2, rows.tm, n), lambda i: (0, i, 0))],
        out_shape=[jax.ShapeDtypeStruct((t, n), F32)] * 3 + [jax.ShapeDtypeStruct((2, t, n), F32)],
        compiler_params=_params("arbitrary"),
        name="hgrn_pre",
    )(x, mods, gn, w5, lb_logits)


def _tri_cumsum(tri, x):
    hi = x.astype(BF16)
    r1 = x - hi.astype(F32)
    mid = r1.astype(BF16)
    lo = (r1 - mid.astype(F32)).astype(BF16)
    return _dot(tri, hi) + _dot(tri, mid) + _dot(tri, lo)


def _hgrn_tile(q_ref, v_ref, f_ref, o_ref, st_ref, *, rows, reverse):
    r = rows
    a = lax.broadcasted_iota(jnp.int32, (r, r), 0)
    b = lax.broadcasted_iota(jnp.int32, (r, r), 1)
    seen = (b >= a) if reverse else (b <= a)
    tri = jnp.where(seen, 1.0, 0.0).astype(BF16)

    q = q_ref[...]
    f = f_ref[0]
    vb = v_ref[...].astype(BF16)
    lf = jnp.log(f)
    cum = _tri_cumsum(tri, lf)
    tot = cum[0:1, :] if reverse else cum[r - 1:r, :]
    kk = 1.0 - f
    q_in = (q * jnp.exp(cum)).astype(BF16)
    k_d = (kk * jnp.exp(tot - cum)).astype(BF16)
    e_tot = jnp.exp(tot)

    levels = []
    c = r // 2
    while c >= HG_DIAG_BLOCK:
        q_half = (a % (2 * c) < c) if reverse else (a % (2 * c) >= c)
        k_half = (b % (2 * c) >= c) if reverse else (b % (2 * c) < c)
        levels.append((2 * c, c if reverse else c - 1, ((a // (2 * c)) == (b // (2 * c))) & q_half & k_half))
        c //= 2
    blk = HG_DIAG_BLOCK
    levels.append((blk, blk // 2, ((a // blk) == (b // blk)) & seen))

    factors = []
    for size, ref_row, own in levels:
        ref = jnp.concatenate(
            [jnp.broadcast_to(cum[j * size + ref_row:j * size + ref_row + 1, :], (size, cum.shape[1]))
             for j in range(r // size)], axis=0)
        factors.append(((q * jnp.exp(cum - ref)).astype(BF16), (kk * jnp.exp(ref - cum)).astype(BF16), own))

    outs = []
    for h in range(HG_HEADS):
        sl = slice(h * HG_DK, (h + 1) * HG_DK)
        att = jnp.zeros((r, r), F32)
        for q_l, k_l, own in factors:
            att = jnp.where(own, _dot_nt(q_l[:, sl], k_l[:, sl]), att)
        st = st_ref[h]
        outs.append(_dot_nt(q_in[:, sl], st.astype(BF16)) + _dot(att.astype(BF16), vb[:, sl]))
        st_ref[h] = st * e_tot[:, sl] + _dot_tn(vb[:, sl], k_d[:, sl])
    o_ref[0] = jnp.concatenate(outs, axis=1)


def _hgrn_scan_kernel(*refs, rows, has_init):
    if has_init:
        o_prev_ref, q_ref, v_ref, f_ref, s0_ref, o_ref, sfin_ref, st_ref = refs
        del o_prev_ref
    else:
        q_ref, v_ref, f_ref, o_ref, sfin_ref, st_ref = refs
    d = pl.program_id(1)
    t = pl.program_id(2)

    @pl.when(t == 0)
    def _():
        for h in range(HG_HEADS):
            st_ref[h] = s0_ref[0, 0, h].T if has_init else jnp.zeros((HG_DV, HG_DK), F32)

    for reverse in (False, True):
        @pl.when(d == int(reverse))
        def _():
            _hgrn_tile(q_ref, v_ref, f_ref, o_ref, st_ref, rows=rows, reverse=reverse)

    @pl.when(t == pl.num_programs(2) - 1)
    def _():
        for h in range(HG_HEADS):
            sfin_ref[0, 0, h] = st_ref[h].T


def _hgrn_scan(q, v, f2, batch, seq, first_row, rt, s0=None, o_prev=None):
    t, n = q.shape
    nt = seq // rt
    has_init = s0 is not None

    def rblk(b, d, i):
        return first_row // rt + b * nt + jnp.where(d == 0, i, nt - 1 - i)

    row = pl.BlockSpec((rt, n), lambda b, d, i: (rblk(b, d, i), 0))
    dir_row = pl.BlockSpec((1, rt, n), lambda b, d, i: (d, rblk(b, d, i), 0))
    state = pl.BlockSpec((1, 1, HG_HEADS, HG_DK, HG_DV), lambda b, d, i: (b, d, 0, 0, 0))
    ins, in_specs, alias = [q, v, f2], [row, row, dir_row], {}
    if has_init:
        ins = [o_prev] + ins + [s0]
        in_specs = [pl.BlockSpec(memory_space=pl.ANY)] + in_specs + [state]
        alias = {0: 0}
    return pl.pallas_call(
        functools.partial(_hgrn_scan_kernel, rows=rt, has_init=has_init),
        grid=(batch, 2, nt),
        in_specs=in_specs,
        out_specs=[dir_row, state],
        out_shape=[jax.ShapeDtypeStruct((2, t, n), F32),
                   jax.ShapeDtypeStruct((batch, 2, HG_HEADS, HG_DK, HG_DV), F32)],
        scratch_shapes=[pltpu.VMEM((HG_HEADS, HG_DV, HG_DK), F32)],
        input_output_aliases=alias,
        compiler_params=_params("arbitrary", "arbitrary", "arbitrary"),
        name="hgrn_scan_latent" if has_init else "hgrn_scan_prompt",
    )(*ins)


def _post_kernel(*refs, hgrn, final):
    refs = list(refs)
    out_ref = refs.pop()
    if hgrn:
        od_ref, g_ref, onorm_ref = refs[:3]
        refs = refs[3:]
        o2 = od_ref[0] + od_ref[1]
        gate = g_ref[...]
        onorm = onorm_ref[...]
        parts = []
        for h in range(HG_HEADS):
            sl = slice(h * HG_DV, (h + 1) * HG_DV)
            parts.append(_rms(o2[:, sl], onorm) * gate[:, sl])
        o = jnp.concatenate(parts, axis=1).astype(BF16)
    else:
        o = refs[0][...]
        refs = refs[1:]
    x_ref, mod_ref, wo_ref, gn_ref, wg_ref, wu_ref, wd_ref = refs[:7]
    m = mod_ref[0]
    x1 = x_ref[...] + m[2:3] * _dot(o, wo_ref[...])
    h2 = _modulate(x1, gn_ref[...], m[3:4], m[4:5]).astype(BF16)
    a = (_silu(_dot(h2, wg_ref[...])) * _dot(h2, wu_ref[...])).astype(BF16)
    x2 = x1 + m[5:6] * _dot(a, wd_ref[...])
    if final:
        x2 = _rms(x2, refs[7][...])
    out_ref[...] = x2


def _post(mix, x, mods, wo, gn, wg, wu, wd, rows, final_norm=None, hgrn=False):
    t, d = x.shape
    if hgrn:
        od, gate, onorm = mix
        head = [od, gate, onorm]
        head_specs = [pl.BlockSpec((2, rows.tm, d), lambda i: (0, i, 0)), rows.row_spec(d), _whole(onorm.shape)]
    else:
        head = [mix]
        head_specs = [rows.row_spec(mix.shape[1])]
    ins = head + [x, mods, wo, gn, wg, wu, wd]
    in_specs = head_specs + [rows.row_spec(d), rows.mod_spec(d)] + [_whole(a.shape) for a in (wo, gn, wg, wu, wd)]
    if final_norm is not None:
        ins.append(final_norm)
        in_specs.append(_whole(final_norm.shape))
    return pl.pallas_call(
        functools.partial(_post_kernel, hgrn=hgrn, final=final_norm is not None),
        grid=(rows.n_tiles,),
        in_specs=in_specs,
        out_specs=rows.row_spec(d),
        out_shape=jax.ShapeDtypeStruct((t, d), F32),
        compiler_params=_params("arbitrary"),
        name="post",
    )(*ins)


def _pick_tile(n_prompt_rows, dec_seq, want):
    tm = want
    while n_prompt_rows % tm or dec_seq % tm:
        tm //= 2
    return tm


def kernel(x_prompt, x_sample, cache_mla_ckv, cache_mla_krope, state_hgrn, cache_swa_k, cache_swa_v, c, c_ctx, ada_w, ada_b, norm_mix, norm_ffn, ffn_w_gate, ffn_w_up, ffn_w_down, final_norm, mla_w_dq, mla_q_norm, mla_w_uq, mla_w_dkv, mla_kv_norm, mla_w_uk, mla_w_uv, mla_w_o, hg_w_q, hg_w_f, hg_w_i, hg_w_g, hg_o_norm, hg_w_o, hg_lb_logits, swa_w_q, swa_w_k, swa_w_v, swa_w_o, swa_sink):
    batch, seq, d = x_prompt.shape
    dec_batch, dec_seq, _ = x_sample.shape
    past = cache_mla_ckv.shape[2]
    depth = ada_w.shape[0]
    n_prompt = batch * seq
    n_rows = n_prompt + dec_batch * dec_seq
    assert dec_batch + 1 <= COND_ROWS and seq % SWA_WINDOW == 0 and dec_seq % (2 * SWA_WINDOW) == 0

    pre_rows = _Rows(n_prompt, dec_seq, n_rows, _pick_tile(n_prompt, dec_seq, 256))
    post_rows = _Rows(n_prompt, dec_seq, n_rows, _pick_tile(n_prompt, dec_seq, 512))
    tq = _pick_tile(n_prompt, dec_seq, 256)
    cos, sin = _rope_tables(dec_seq, pre_rows.tm)

    cond = jnp.concatenate([c_ctx[None, :], c, jnp.zeros((COND_ROWS - 1 - dec_batch, d), F32)], axis=0)
    mods = _adaln(cond, ada_w, ada_b).reshape(depth, COND_ROWS, 6, d)

    x = jnp.concatenate([x_prompt.reshape(n_prompt, d), x_sample.reshape(dec_batch * dec_seq, d)], axis=0)
    row1 = lambda a: a.reshape(1, -1)
    new_ckv, new_krope, new_hg, new_k, new_v = [], [], [], [], []
    for i in range(depth):
        kind, j = i % N_MIXERS, i // N_MIXERS
        gn = row1(norm_mix[i])
        if kind == 0:
            uq = mla_w_uq[j].reshape(-1, MLA_HEADS, MLA_NOPE_DIM + MLA_ROPE_DIM)
            w = {
                "dq": mla_w_dq[j].astype(BF16), "q_norm": row1(mla_q_norm[j]),
                "uq_nope": uq[:, :, :MLA_NOPE_DIM].reshape(uq.shape[0], -1).astype(BF16),
                "uq_rope": uq[:, :, MLA_NOPE_DIM:].reshape(uq.shape[0], -1).astype(BF16),
                "dkv": jnp.pad(mla_w_dkv[j], ((0, 0), (0, LANES - MLA_ROPE_DIM))).astype(BF16),
                "kv_norm": row1(mla_kv_norm[j]),
                "uk": mla_w_uk[j].astype(BF16), "uv": mla_w_uv[j].astype(BF16),
            }
            qn, qr, ckv, kr_raw, kr, kn, v = _mla_pre(x, mods[i], gn, cos, sin, w, pre_rows)
            kn_ctx, v_ctx = _mla_expand(cache_mla_ckv[:, j].reshape(dec_batch * past, -1), w["uk"], w["uv"])
            kr_ctx = cache_mla_krope[:, j].reshape(dec_batch * past, -1)
            o = _mla_attn_prompt(qn, qr, kn, kr, v, batch, seq)
            mix = _mla_attn_latent(o, qn, qr, kn, kr, v, kn_ctx, kr_ctx, v_ctx, n_prompt, dec_batch, dec_seq, past, tq)
            wo = mla_w_o[j].astype(BF16)
            new_ckv.append(ckv[:n_prompt].reshape(batch, seq, -1))
            new_krope.append(kr_raw[:n_prompt].reshape(batch, seq, -1))
        elif kind == 1:
            w5 = jnp.concatenate([hg_w_q[j], hg_w_i[j], hg_w_g[j], hg_w_f[j, 0], hg_w_f[j, 1]], axis=1).astype(BF16)
            q, v, gate, f2 = _hgrn_pre(x, mods[i], gn, w5, hg_lb_logits, i, pre_rows)
            rt = _pick_tile(seq, dec_seq, 256)
            od, s_prompt = _hgrn_scan(q, v, f2, batch, seq, 0, rt)
            od, _ = _hgrn_scan(q, v, f2, dec_batch, dec_seq, n_prompt, rt, s0=state_hgrn[:, j], o_prev=od)
            mix = (od, gate, row1(hg_o_norm[j]))
            wo = hg_w_o[j].astype(BF16)
            new_hg.append(s_prompt)
        else:
            wqkv = jnp.concatenate([swa_w_q[j], swa_w_k[j], swa_w_v[j]], axis=1).astype(BF16)
            q, k, v, k_raw, v_raw = _swa_pre(x, mods[i], gn, cos, sin, wqkv, pre_rows)
            sink = row1(swa_sink[j])
            o = _swa_attn_prompt(q, k, v, sink, batch, seq)
            k_ctx = cache_swa_k[:, j].reshape(dec_batch * past, -1)
            v_ctx = cache_swa_v[:, j].reshape(dec_batch * past, -1)
            mix = _swa_attn_latent(o, q, k, v, k_ctx, v_ctx, sink, n_prompt, dec_batch, dec_seq, past, tq)
            wo = swa_w_o[j].astype(BF16)
            new_k.append(k_raw[:n_prompt].reshape(batch, seq, SWA_KV_HEADS, SWA_HEAD_DIM))
            new_v.append(v_raw[:n_prompt].reshape(batch, seq, SWA_KV_HEADS, SWA_HEAD_DIM))
        x = _post(mix, x, mods[i], wo, row1(norm_ffn[i]), ffn_w_gate[i].astype(BF16), ffn_w_up[i].astype(BF16),
                  ffn_w_down[i].astype(BF16), post_rows,
                  final_norm=row1(final_norm) if i == depth - 1 else None, hgrn=kind == 1)
    y_prompt = x[:n_prompt].reshape(batch, seq, d)
    y_sample = x[n_prompt:].reshape(dec_batch, dec_seq, d)
    return (y_prompt, y_sample, jnp.stack(new_ckv, axis=1), jnp.stack(new_krope, axis=1),
            jnp.stack(new_hg, axis=1), jnp.stack(new_k, axis=1), jnp.stack(new_v, axis=1))
```

```python
import functools

import jax
import jax.numpy as jnp
from jax import lax
from jax.experimental import pallas as pl
from jax.experimental.pallas import tpu as pltpu

F32 = jnp.float32
BF16 = jnp.bfloat16

GRID_W = 64
N_MIXERS = 3

MLA_HEADS = 8
MLA_KV_LORA = 256
MLA_NOPE_DIM = 128
MLA_ROPE_DIM = 64
MLA_V_DIM = 128
MLA_QK_PAD = 256
MLA_SCALE = (MLA_NOPE_DIM + MLA_ROPE_DIM) ** -0.5

HG_HEADS = 8
HG_DK = 128
HG_DV = 128
HG_DIAG_BLOCK = 32

SWA_HEADS = 16
SWA_KV_HEADS = 4
SWA_GROUP = SWA_HEADS // SWA_KV_HEADS
SWA_HEAD_DIM = 64
SWA_WINDOW = 128
SWA_SCALE = SWA_HEAD_DIM ** -0.5

ROPE_BASE = 10000.0
ROPE_PERIOD = 64
ROPE_QUARTER = 16
NORM_EPS = 1e-6
NEG_INF = -1e30

LANES = 128
COND_ROWS = 8
VMEM_LIMIT = 56 * 1024 * 1024


def _sigmoid(x):
    return jax.nn.sigmoid(x)


def _silu(x):
    return x * jax.nn.sigmoid(x)


def _rms(x, g):
    return x * lax.rsqrt(jnp.mean(x * x, axis=-1, keepdims=True) + NORM_EPS) * g


def _modulate(x, g, shift, scale):
    return _rms(x, g) * (1.0 + scale) + shift


def _dot(a, b):
    return jnp.dot(a, b, preferred_element_type=F32)


def _dot_nt(a, b):
    return lax.dot_general(a, b, (((1,), (1,)), ((), ())), preferred_element_type=F32)


def _dot_tn(a, b):
    return lax.dot_general(a, b, (((0,), (0,)), ((), ())), preferred_element_type=F32)


def _swap_pairs(x):
    n = x.shape[1]
    lane = lax.broadcasted_iota(jnp.int32, x.shape, 1)
    ahead = pltpu.roll(x, n - ROPE_QUARTER, 1)
    behind = pltpu.roll(x, ROPE_QUARTER, 1)
    return jnp.where((lane & (2 * ROPE_QUARTER - 1)) < ROPE_QUARTER, ahead, behind)


def _rope(x, cos, sin):
    reps = x.shape[1] // cos.shape[1]
    if reps > 1:
        cos = jnp.concatenate([cos] * reps, axis=1)
        sin = jnp.concatenate([sin] * reps, axis=1)
    return x * cos + _swap_pairs(x) * sin


def _softmax_pv_t(scores_t, values_t, extra_logit=None):
    mx = functools.reduce(jnp.maximum, [jnp.max(s, axis=0, keepdims=True) for s in scores_t])
    if extra_logit is not None:
        mx = jnp.maximum(mx, extra_logit)
    acc, den = None, None
    for s, v in zip(scores_t, values_t):
        p = jnp.exp(s - mx)
        d = jnp.sum(p, axis=0, keepdims=True)
        a = _dot(v, p.astype(BF16))
        acc = a if acc is None else acc + a
        den = d if den is None else den + d
    if extra_logit is not None:
        den = den + jnp.exp(extra_logit - mx)
    return acc / den


def _whole(shape):
    zeros = (0,) * len(shape)
    return pl.BlockSpec(shape, lambda *_: zeros, pipeline_mode=pl.Buffered(1))


def _params(*sem):
    return pltpu.CompilerParams(dimension_semantics=sem, vmem_limit_bytes=VMEM_LIMIT)


class _Rows:
    def __init__(self, n_prompt_rows, dec_seq, n_rows, tm):
        assert n_prompt_rows % tm == 0 and dec_seq % tm == 0
        self.tm = tm
        self.n_tiles = n_rows // tm
        self.prompt_tiles = n_prompt_rows // tm
        self.seq_tiles = dec_seq // tm

    def cond(self, i):
        return jnp.where(i < self.prompt_tiles, 0, 1 + jnp.maximum(i - self.prompt_tiles, 0) // self.seq_tiles)

    def rope_block(self, i):
        return jnp.where(i < self.prompt_tiles, 0, 1 + jnp.maximum(i - self.prompt_tiles, 0) % self.seq_tiles)

    def row_spec(self, width):
        return pl.BlockSpec((self.tm, width), lambda i: (i, 0))

    def col_spec(self, height):
        return pl.BlockSpec((height, self.tm), lambda i: (0, i))

    def mod_spec(self, d):
        return pl.BlockSpec((1, 6, d), lambda i: (self.cond(i), 0, 0))

    def rope_spec(self, width):
        return pl.BlockSpec((self.tm, width), lambda i: (self.rope_block(i), 0))

    def prompt_spec(self, width):
        return pl.BlockSpec((self.tm, width), lambda i: (jnp.minimum(i, self.prompt_tiles - 1), 0))

    def latent_spec(self, width):
        return pl.BlockSpec((self.tm, width), lambda i: (jnp.maximum(i - self.prompt_tiles, 0), 0))


def _rope_tables(dec_seq, tm):
    pos = jnp.arange(dec_seq)
    row = (pos // GRID_W).astype(F32)
    col = (pos % GRID_W).astype(F32)
    inv_freq = ROPE_BASE ** (-jnp.arange(ROPE_QUARTER, dtype=F32) / ROPE_QUARTER)
    ang_r = row[:, None] * inv_freq[None, :]
    ang_c = col[:, None] * inv_freq[None, :]
    cos = jnp.concatenate([jnp.cos(ang_r), jnp.cos(ang_r), jnp.cos(ang_c), jnp.cos(ang_c)], axis=1)
    sin = jnp.concatenate([-jnp.sin(ang_r), jnp.sin(ang_r), -jnp.sin(ang_c), jnp.sin(ang_c)], axis=1)
    cos = jnp.concatenate([jnp.ones((tm, ROPE_PERIOD), F32), cos], axis=0)
    sin = jnp.concatenate([jnp.zeros((tm, ROPE_PERIOD), F32), sin], axis=0)
    return cos, sin


def _adaln_kernel(c_ref, w_ref, b_ref, o_ref):
    o_ref[0] = _dot(_silu(c_ref[...]), w_ref[0]) + b_ref[0]


def _adaln(cond, ada_w, ada_b, tn=1536):
    depth, d, n = ada_w.shape
    return pl.pallas_call(
        _adaln_kernel,
        grid=(depth, n // tn),
        in_specs=[pl.BlockSpec((COND_ROWS, d), lambda l, j: (0, 0)),
                  pl.BlockSpec((1, d, tn), lambda l, j: (l, 0, j)),
                  pl.BlockSpec((1, 1, tn), lambda l, j: (l, 0, j))],
        out_specs=pl.BlockSpec((1, COND_ROWS, tn), lambda l, j: (l, 0, j)),
        out_shape=jax.ShapeDtypeStruct((depth, COND_ROWS, n), F32),
        compiler_params=_params("arbitrary", "arbitrary"),
        name="adaln",
    )(cond, ada_w, ada_b.reshape(depth, 1, n))


def _mla_kcat(kn, kr_pad):
    parts = []
    for h in range(MLA_HEADS):
        parts += [kn[:, h * MLA_NOPE_DIM:(h + 1) * MLA_NOPE_DIM], kr_pad]
    return jnp.concatenate(parts, axis=1)


def _mla_pre_kernel(x_ref, mod_ref, gn_ref, cos_ref, sin_ref, wdq_ref, qnorm_ref, wuq_ref,
                    wdkv_ref, kvnorm_ref, wuk_ref, wuv_ref,
                    q_ref, ckv_ref, krraw_ref, kcat_ref, vt_ref):
    m = mod_ref[0]
    h = _modulate(x_ref[...], gn_ref[...], m[0:1], m[1:2]).astype(BF16)
    q_lat = _rms(_dot(h, wdq_ref[...]), qnorm_ref[...]).astype(BF16)
    cos, sin = cos_ref[...], sin_ref[...]
    q_ref[...] = _rope(_dot(q_lat, wuq_ref[...]), cos, sin).astype(BF16)
    kv = _dot(h, wdkv_ref[...])
    ckv = _rms(kv[:, :MLA_KV_LORA], kvnorm_ref[...])
    ckv_ref[...] = ckv
    kr = kv[:, MLA_KV_LORA:]
    krraw_ref[...] = kr[:, :MLA_ROPE_DIM]
    kr_pad = _rope(kr, cos[:, LANES:], sin[:, LANES:]).astype(BF16)
    cb = ckv.astype(BF16)
    kcat_ref[...] = _mla_kcat(_dot(cb, wuk_ref[...]).astype(BF16), kr_pad)
    vt_ref[...] = _dot(cb, wuv_ref[...]).T.astype(BF16)


def _mla_pre(x, mods, gn, cos, sin, w, rows):
    t, d = x.shape
    wq = MLA_HEADS * MLA_QK_PAD
    ins = [x, mods, gn, cos, sin, w["dq"], w["q_norm"], w["uq"], w["dkv"], w["kv_norm"], w["uk"], w["uv"]]
    in_specs = [rows.row_spec(d), rows.mod_spec(d), _whole(gn.shape),
                rows.rope_spec(cos.shape[1]), rows.rope_spec(sin.shape[1])]
    in_specs += [_whole(a.shape) for a in ins[5:]]
    hv = MLA_HEADS * MLA_V_DIM
    return pl.pallas_call(
        _mla_pre_kernel,
        grid=(rows.n_tiles,),
        in_specs=in_specs,
        out_specs=[rows.row_spec(wq), rows.row_spec(MLA_KV_LORA), rows.row_spec(MLA_ROPE_DIM), rows.row_spec(wq),
                   rows.col_spec(hv)],
        out_shape=[jax.ShapeDtypeStruct((t, wq), BF16), jax.ShapeDtypeStruct((t, MLA_KV_LORA), F32),
                   jax.ShapeDtypeStruct((t, MLA_ROPE_DIM), F32), jax.ShapeDtypeStruct((t, wq), BF16),
                   jax.ShapeDtypeStruct((hv, t), BF16)],
        compiler_params=_params("arbitrary"),
        name="mla_pre",
    )(*ins)


def _mla_expand_kernel(c_ref, kr_ref, wuk_ref, wuv_ref, kcat_ref, vt_ref):
    cb = c_ref[...].astype(BF16)
    kcat_ref[...] = _mla_kcat(_dot(cb, wuk_ref[...]).astype(BF16), kr_ref[...].astype(BF16))
    vt_ref[...] = _dot(cb, wuv_ref[...]).T.astype(BF16)


def _mla_expand(ckv, kr_pad, wuk, wuv):
    n = ckv.shape[0]
    return pl.pallas_call(
        _mla_expand_kernel,
        out_shape=[jax.ShapeDtypeStruct((n, MLA_HEADS * MLA_QK_PAD), BF16),
                   jax.ShapeDtypeStruct((wuv.shape[1], n), BF16)],
        compiler_params=pltpu.CompilerParams(vmem_limit_bytes=VMEM_LIMIT),
        name="mla_expand",
    )(ckv, kr_pad, wuk, wuv)


def _mla_attn_kernel(*refs):
    q_ref, o_ref = refs[0], refs[-1]
    srcs = [(refs[i], refs[i + 1]) for i in range(1, len(refs) - 1, 2)]
    for h in range(MLA_HEADS):
        qk = slice(h * MLA_QK_PAD, (h + 1) * MLA_QK_PAD)
        dv = slice(h * MLA_V_DIM, (h + 1) * MLA_V_DIM)
        q = q_ref[:, qk]
        scores = [_dot_nt(k_ref[:, qk], q) * MLA_SCALE for k_ref, _ in srcs]
        values = [vt_ref[dv, :] for _, vt_ref in srcs]
        o_ref[:, dv] = _softmax_pv_t(scores, values).T.astype(BF16)


def _mla_attn_prompt(q, kcat, vt, batch, seq):
    hv = vt.shape[0]
    row = lambda w: pl.BlockSpec((seq, w), lambda b: (b, 0))
    return pl.pallas_call(
        _mla_attn_kernel,
        grid=(batch,),
        in_specs=[row(q.shape[1]), row(kcat.shape[1]), pl.BlockSpec((hv, seq), lambda b: (0, b))],
        out_specs=row(hv),
        out_shape=jax.ShapeDtypeStruct((batch * seq, hv), BF16),
        compiler_params=_params("arbitrary"),
        name="mla_attn_prompt",
    )(q, kcat, vt)


def _mla_attn_latent(q, kcat, vt, kcat_ctx, vt_ctx, n_prompt, dec_batch, dec_seq, past, tq):
    hv = vt.shape[0]
    nq = dec_seq // tq
    w = q.shape[1]
    lat_blk = n_prompt // dec_seq
    return pl.pallas_call(
        _mla_attn_kernel,
        grid=(dec_batch, nq),
        in_specs=[pl.BlockSpec((tq, w), lambda b, i: (n_prompt // tq + b * nq + i, 0)),
                  pl.BlockSpec((past, w), lambda b, i: (b, 0)),
                  pl.BlockSpec((hv, past), lambda b, i: (0, b)),
                  pl.BlockSpec((dec_seq, w), lambda b, i: (lat_blk + b, 0)),
                  pl.BlockSpec((hv, dec_seq), lambda b, i: (0, lat_blk + b))],
        out_specs=pl.BlockSpec((tq, hv), lambda b, i: (b * nq + i, 0)),
        out_shape=jax.ShapeDtypeStruct((dec_batch * dec_seq, hv), BF16),
        compiler_params=_params("arbitrary", "arbitrary"),
        name="mla_attn_latent",
    )(q, kcat_ctx, vt_ctx, kcat, vt)


def _tile_heads(x):
    n = x.shape[1]
    block = lax.broadcasted_iota(jnp.int32, x.shape, 1) // SWA_HEAD_DIM
    rolled = [x] + [pltpu.roll(x, s * SWA_HEAD_DIM, 1) for s in range(1, SWA_KV_HEADS)]
    out = []
    for kvh in range(SWA_KV_HEADS):
        blk = rolled[(0 - kvh) % SWA_KV_HEADS]
        for g in range(1, n // SWA_HEAD_DIM):
            blk = jnp.where(block == g, rolled[(g - kvh) % SWA_KV_HEADS], blk)
        out.append(blk)
    return jnp.concatenate(out, axis=1)


def _swa_pre_kernel(x_ref, mod_ref, gn_ref, cos_ref, sin_ref, wqkv_ref, q_ref, kx_ref, vt_ref, kraw_ref, vraw_ref):
    m = mod_ref[0]
    h = _modulate(x_ref[...], gn_ref[...], m[0:1], m[1:2]).astype(BF16)
    qkv = _dot(h, wqkv_ref[...])
    nq, nk = SWA_HEADS * SWA_HEAD_DIM, SWA_KV_HEADS * SWA_HEAD_DIM
    cos, sin = cos_ref[...], sin_ref[...]
    q_ref[...] = _rope(qkv[:, :nq], cos, sin).astype(BF16)
    k = qkv[:, nq:nq + nk]
    v = qkv[:, nq + nk:]
    kraw_ref[...] = k
    vraw_ref[...] = v
    kx_ref[...] = _tile_heads(_rope(k, cos, sin)).astype(BF16)
    vt_ref[...] = v.T.astype(BF16)


def _swa_pre(x, mods, gn, cos, sin, wqkv, rows):
    t, d = x.shape
    nq, nk = SWA_HEADS * SWA_HEAD_DIM, SWA_KV_HEADS * SWA_HEAD_DIM
    return pl.pallas_call(
        _swa_pre_kernel,
        grid=(rows.n_tiles,),
        in_specs=[rows.row_spec(d), rows.mod_spec(d), _whole(gn.shape),
                  rows.rope_spec(cos.shape[1]), rows.rope_spec(sin.shape[1]), _whole(wqkv.shape)],
        out_specs=[rows.row_spec(nq), rows.row_spec(nq), rows.col_spec(nk), rows.row_spec(nk), rows.row_spec(nk)],
        out_shape=[jax.ShapeDtypeStruct((t, nq), BF16), jax.ShapeDtypeStruct((t, nq), BF16),
                   jax.ShapeDtypeStruct((nk, t), BF16), jax.ShapeDtypeStruct((t, nk), F32),
                   jax.ShapeDtypeStruct((t, nk), F32)],
        compiler_params=_params("arbitrary"),
        name="swa_pre",
    )(x, mods, gn, cos, sin, wqkv)


def _swa_heads(q_ref, sink_ref, o_ref, kx, vt, mask):
    gw = SWA_GROUP * SWA_HEAD_DIM
    group = lax.broadcasted_iota(jnp.int32, (q_ref.shape[0], gw), 1) // SWA_HEAD_DIM
    for kvh in range(SWA_KV_HEADS):
        q_all = q_ref[:, kvh * gw:(kvh + 1) * gw].astype(F32)
        k = kx[:, kvh * gw:(kvh + 1) * gw]
        v = vt[kvh * SWA_HEAD_DIM:(kvh + 1) * SWA_HEAD_DIM, :]
        parts = []
        for g in range(SWA_GROUP):
            hq = kvh * SWA_GROUP + g
            q = jnp.where(group == g, q_all, 0.0).astype(BF16)
            s = _dot_nt(k, q) * SWA_SCALE
            if mask is not None:
                s = jnp.where(mask, s, NEG_INF)
            parts.append(_softmax_pv_t([s], [v], extra_logit=sink_ref[:, hq:hq + 1]))
        o_ref[:, kvh * gw:(kvh + 1) * gw] = jnp.concatenate(parts, axis=0).T.astype(BF16)


def _swa_attn_prompt_kernel(q_ref, kx_ref, vt_ref, sink_ref, o_ref):
    _swa_heads(q_ref, sink_ref, o_ref, kx_ref[...], vt_ref[...], None)


def _swa_attn_prompt(q, kx, vt, sink, batch, seq):
    nk = vt.shape[0]
    row = lambda w: pl.BlockSpec((seq, w), lambda b: (b, 0))
    return pl.pallas_call(
        _swa_attn_prompt_kernel,
        grid=(batch,),
        in_specs=[row(q.shape[1]), row(kx.shape[1]), pl.BlockSpec((nk, seq), lambda b: (0, b)), _whole(sink.shape)],
        out_specs=row(q.shape[1]),
        out_shape=jax.ShapeDtypeStruct((batch * seq, q.shape[1]), BF16),
        compiler_params=_params("arbitrary"),
        name="swa_attn_prompt",
    )(q, kx, vt, sink)


def _swa_attn_latent_kernel(q_ref, kc_ref, vc_ref, kp_ref, vp_ref, km_ref, vm_ref, kn_ref, vn_ref,
                            sink_ref, o_ref, *, tq, dec_seq, past):
    i = pl.program_id(1)
    qpos = i * tq + lax.broadcasted_iota(jnp.int32, (1, tq), 1)

    def band(first, n):
        kpos = first + lax.broadcasted_iota(jnp.int32, (n, 1), 0)
        return (jnp.abs(qpos - kpos) <= SWA_WINDOW) & (kpos >= 0) & (kpos < dec_seq)

    mask = jnp.concatenate([jnp.full((past, tq), True), band(i * tq - SWA_WINDOW, SWA_WINDOW), band(i * tq, tq),
                            band((i + 1) * tq, SWA_WINDOW)], axis=0)
    kx = jnp.concatenate([kc_ref[...], kp_ref[...], km_ref[...], kn_ref[...]], axis=0)
    vt = jnp.concatenate([vc_ref[...], vp_ref[...], vm_ref[...], vn_ref[...]], axis=1)
    _swa_heads(q_ref, sink_ref, o_ref, kx, vt, mask)


def _swa_attn_latent(q, kx, vt, kx_ctx, vt_ctx, sink, n_prompt, dec_batch, dec_seq, past, tq):
    t, wq = q.shape
    nk = vt.shape[0]
    nq = dec_seq // tq
    w = SWA_WINDOW
    first = lambda b, i: n_prompt + b * dec_seq + i * tq
    prev = lambda b, i: first(b, i) // w - 1
    nxt = lambda b, i: jnp.minimum((first(b, i) + tq) // w, t // w - 1)
    return pl.pallas_call(
        functools.partial(_swa_attn_latent_kernel, tq=tq, dec_seq=dec_seq, past=past),
        grid=(dec_batch, nq),
        in_specs=[pl.BlockSpec((tq, wq), lambda b, i: (first(b, i) // tq, 0)),
                  pl.BlockSpec((past, wq), lambda b, i: (b, 0)),
                  pl.BlockSpec((nk, past), lambda b, i: (0, b)),
                  pl.BlockSpec((w, wq), lambda b, i: (prev(b, i), 0)),
                  pl.BlockSpec((nk, w), lambda b, i: (0, prev(b, i))),
                  pl.BlockSpec((tq, wq), lambda b, i: (first(b, i) // tq, 0)),
                  pl.BlockSpec((nk, tq), lambda b, i: (0, first(b, i) // tq)),
                  pl.BlockSpec((w, wq), lambda b, i: (nxt(b, i), 0)),
                  pl.BlockSpec((nk, w), lambda b, i: (0, nxt(b, i))),
                  _whole(sink.shape)],
        out_specs=pl.BlockSpec((tq, wq), lambda b, i: (b * nq + i, 0)),
        out_shape=jax.ShapeDtypeStruct((dec_batch * dec_seq, wq), BF16),
        compiler_params=_params("arbitrary", "arbitrary"),
        name="swa_attn_latent",
    )(q, kx_ctx, vt_ctx, kx, vt, kx, vt, kx, vt, sink)


def _hgrn_pre_kernel(x_ref, mod_ref, gn_ref, w_ref, lbl_ref, q_ref, v_ref, g_ref, f_ref, *, layer):
    m = mod_ref[0]
    h = _modulate(x_ref[...], gn_ref[...], m[0:1], m[1:2]).astype(BF16)
    y = _dot(h, w_ref[...])
    n = HG_HEADS * HG_DK
    q_ref[...] = _silu(y[:, :n])
    v_ref[...] = y[:, n:2 * n]
    g_ref[...] = _silu(y[:, 2 * n:3 * n])
    for d in range(2):
        logits = lbl_ref[d]
        e = jnp.exp(logits - jnp.max(logits, axis=0, keepdims=True))
        s = e / jnp.sum(e, axis=0, keepdims=True)
        cs = s[0:1]
        for r in range(1, layer + 1):
            cs = cs + s[r:r + 1]
        lb = cs - s[0:1]
        f_ref[d] = lb + (1.0 - lb) * _sigmoid(y[:, (3 + d) * n:(4 + d) * n])


def _hgrn_pre(x, mods, gn, w5, lb_logits, layer, rows):
    t, d = x.shape
    n = HG_HEADS * HG_DK
    return pl.pallas_call(
        functools.partial(_hgrn_pre_kernel, layer=layer),
        grid=(rows.n_tiles,),
        in_specs=[rows.row_spec(d), rows.mod_spec(d), _whole(gn.shape), _whole(w5.shape), _whole(lb_logits.shape)],
        out_specs=[rows.row_spec(n), rows.row_spec(n), rows.row_spec(n),
                   pl.BlockSpec((2, rows.tm, n), lambda i: (0, i, 0))],
        out_shape=[jax.ShapeDtypeStruct((t, n), F32)] * 3 + [jax.ShapeDtypeStruct((2, t, n), F32)],
        compiler_params=_params("arbitrary"),
        name="hgrn_pre",
    )(x, mods, gn, w5, lb_logits)


def _tri_cumsum(tri, x):
    hi = x.astype(BF16)
    r1 = x - hi.astype(F32)
    mid = r1.astype(BF16)
    lo = (r1 - mid.astype(F32)).astype(BF16)
    return _dot(tri, hi) + _dot(tri, mid) + _dot(tri, lo)


def _hgrn_tile(q_ref, v_ref, f_ref, o_ref, st_ref, *, rows, reverse):
    r = rows
    a = lax.broadcasted_iota(jnp.int32, (r, r), 0)
    b = lax.broadcasted_iota(jnp.int32, (r, r), 1)
    seen = (b >= a) if reverse else (b <= a)
    tri = jnp.where(seen, 1.0, 0.0).astype(BF16)

    q = q_ref[...]
    f = f_ref[0]
    vb = v_ref[...].astype(BF16)
    lf = jnp.log(f)
    cum = _tri_cumsum(tri, lf)
    tot = cum[0:1, :] if reverse else cum[r - 1:r, :]
    kk = 1.0 - f
    q_in = (q * jnp.exp(cum)).astype(BF16)
    k_d = (kk * jnp.exp(tot - cum)).astype(BF16)
    e_tot = jnp.exp(tot)

    levels = []
    c = r // 2
    while c >= HG_DIAG_BLOCK:
        q_half = (a % (2 * c) < c) if reverse else (a % (2 * c) >= c)
        k_half = (b % (2 * c) >= c) if reverse else (b % (2 * c) < c)
        levels.append((2 * c, c if reverse else c - 1, ((a // (2 * c)) == (b // (2 * c))) & q_half & k_half))
        c //= 2
    blk = HG_DIAG_BLOCK
    levels.append((blk, blk // 2, ((a // blk) == (b // blk)) & seen))

    factors = []
    for size, ref_row, own in levels:
        ref = jnp.concatenate(
            [jnp.broadcast_to(cum[j * size + ref_row:j * size + ref_row + 1, :], (size, cum.shape[1]))
             for j in range(r // size)], axis=0)
        factors.append(((q * jnp.exp(cum - ref)).astype(BF16), (kk * jnp.exp(ref - cum)).astype(BF16), own))

    outs = []
    for h in range(HG_HEADS):
        sl = slice(h * HG_DK, (h + 1) * HG_DK)
        att = jnp.zeros((r, r), F32)
        for q_l, k_l, own in factors:
            att = jnp.where(own, _dot_nt(q_l[:, sl], k_l[:, sl]), att)
        st = st_ref[h]
        outs.append(_dot_nt(q_in[:, sl], st.astype(BF16)) + _dot(att.astype(BF16), vb[:, sl]))
        st_ref[h] = st * e_tot[:, sl] + _dot_tn(vb[:, sl], k_d[:, sl])
    o_ref[0] = jnp.concatenate(outs, axis=1)


def _hgrn_scan_kernel(*refs, rows, has_init):
    if has_init:
        q_ref, v_ref, f_ref, s0_ref, o_ref, sfin_ref, st_ref = refs
    else:
        q_ref, v_ref, f_ref, o_ref, sfin_ref, st_ref = refs
    d = pl.program_id(1)
    t = pl.program_id(2)

    @pl.when(t == 0)
    def _():
        for h in range(HG_HEADS):
            st_ref[h] = s0_ref[0, 0, h].T if has_init else jnp.zeros((HG_DV, HG_DK), F32)

    for reverse in (False, True):
        @pl.when(d == int(reverse))
        def _():
            _hgrn_tile(q_ref, v_ref, f_ref, o_ref, st_ref, rows=rows, reverse=reverse)

    @pl.when(t == pl.num_programs(2) - 1)
    def _():
        for h in range(HG_HEADS):
            sfin_ref[0, 0, h] = st_ref[h].T


def _hgrn_scan(q, v, f2, batch, seq, first_row, rt, s0=None):
    n = q.shape[1]
    nt = seq // rt
    has_init = s0 is not None

    def local(b, d, i):
        return b * nt + jnp.where(d == 0, i, nt - 1 - i)

    def slab(b, d, i):
        return first_row // rt + local(b, d, i)

    row = pl.BlockSpec((rt, n), lambda b, d, i: (slab(b, d, i), 0))
    state = pl.BlockSpec((1, 1, HG_HEADS, HG_DK, HG_DV), lambda b, d, i: (b, d, 0, 0, 0))
    ins = [q, v, f2]
    in_specs = [row, row, pl.BlockSpec((1, rt, n), lambda b, d, i: (d, slab(b, d, i), 0))]
    if has_init:
        ins.append(s0)
        in_specs.append(state)
    return pl.pallas_call(
        functools.partial(_hgrn_scan_kernel, rows=rt, has_init=has_init),
        grid=(batch, 2, nt),
        in_specs=in_specs,
        out_specs=[pl.BlockSpec((1, rt, n), lambda b, d, i: (d, local(b, d, i), 0)), state],
        out_shape=[jax.ShapeDtypeStruct((2, batch * seq, n), F32),
                   jax.ShapeDtypeStruct((batch, 2, HG_HEADS, HG_DK, HG_DV), F32)],
        scratch_shapes=[pltpu.VMEM((HG_HEADS, HG_DV, HG_DK), F32)],
        compiler_params=_params("arbitrary", "arbitrary", "arbitrary"),
        name="hgrn_scan_latent" if has_init else "hgrn_scan_prompt",
    )(*ins)


def _post_kernel(*refs, hgrn, final, prompt_tiles):
    refs = list(refs)
    out_ref = refs.pop()
    is_prompt = pl.program_id(0) < prompt_tiles
    if hgrn:
        odp_ref, odl_ref, g_ref, onorm_ref = refs[:4]
        refs = refs[4:]
        o2 = jnp.where(is_prompt, odp_ref[0] + odp_ref[1], odl_ref[0] + odl_ref[1])
        gate = g_ref[...]
        onorm = onorm_ref[...]
        parts = []
        for h in range(HG_HEADS):
            sl = slice(h * HG_DV, (h + 1) * HG_DV)
            parts.append(_rms(o2[:, sl], onorm) * gate[:, sl])
        o = jnp.concatenate(parts, axis=1).astype(BF16)
    else:
        o = jnp.where(is_prompt, refs[0][...], refs[1][...])
        refs = refs[2:]
    x_ref, mod_ref, wo_ref, gn_ref, wg_ref, wu_ref, wd_ref = refs[:7]
    m = mod_ref[0]
    x1 = x_ref[...] + m[2:3] * _dot(o, wo_ref[...])
    h2 = _modulate(x1, gn_ref[...], m[3:4], m[4:5]).astype(BF16)
    a = (_silu(_dot(h2, wg_ref[...])) * _dot(h2, wu_ref[...])).astype(BF16)
    x2 = x1 + m[5:6] * _dot(a, wd_ref[...])
    if final:
        x2 = _rms(x2, refs[7][...])
    out_ref[...] = x2


def _post(mix, x, mods, wo, gn, wg, wu, wd, rows, final_norm=None, hgrn=False):
    t, d = x.shape
    tm = rows.tm
    if hgrn:
        od_p, od_l, gate, onorm = mix
        head = [od_p, od_l, gate, onorm]
        head_specs = [pl.BlockSpec((2, tm, d), lambda i: (0, jnp.minimum(i, rows.prompt_tiles - 1), 0)),
                      pl.BlockSpec((2, tm, d), lambda i: (0, jnp.maximum(i - rows.prompt_tiles, 0), 0)),
                      rows.row_spec(d), _whole(onorm.shape)]
    else:
        head = list(mix)
        head_specs = [rows.prompt_spec(mix[0].shape[1]), rows.latent_spec(mix[1].shape[1])]
    ins = head + [x, mods, wo, gn, wg, wu, wd]
    in_specs = head_specs + [rows.row_spec(d), rows.mod_spec(d)] + [_whole(a.shape) for a in (wo, gn, wg, wu, wd)]
    if final_norm is not None:
        ins.append(final_norm)
        in_specs.append(_whole(final_norm.shape))
    return pl.pallas_call(
        functools.partial(_post_kernel, hgrn=hgrn, final=final_norm is not None, prompt_tiles=rows.prompt_tiles),
        grid=(rows.n_tiles,),
        in_specs=in_specs,
        out_specs=rows.row_spec(d),
        out_shape=jax.ShapeDtypeStruct((t, d), F32),
        compiler_params=_params("arbitrary"),
        name="post",
    )(*ins)


def _pick_tile(n_prompt_rows, dec_seq, want):
    tm = want
    while n_prompt_rows % tm or dec_seq % tm:
        tm //= 2
    return tm


def kernel(x_prompt, x_sample, cache_mla_ckv, cache_mla_krope, state_hgrn, cache_swa_k, cache_swa_v, c, c_ctx, ada_w, ada_b, norm_mix, norm_ffn, ffn_w_gate, ffn_w_up, ffn_w_down, final_norm, mla_w_dq, mla_q_norm, mla_w_uq, mla_w_dkv, mla_kv_norm, mla_w_uk, mla_w_uv, mla_w_o, hg_w_q, hg_w_f, hg_w_i, hg_w_g, hg_o_norm, hg_w_o, hg_lb_logits, swa_w_q, swa_w_k, swa_w_v, swa_w_o, swa_sink):
    batch, seq, d = x_prompt.shape
    dec_batch, dec_seq, _ = x_sample.shape
    past = cache_mla_ckv.shape[2]
    depth = ada_w.shape[0]
    n_prompt = batch * seq
    n_rows = n_prompt + dec_batch * dec_seq
    assert dec_batch + 1 <= COND_ROWS and seq % SWA_WINDOW == 0 and dec_seq % (2 * SWA_WINDOW) == 0
    assert n_prompt % dec_seq == 0

    pre_rows = _Rows(n_prompt, dec_seq, n_rows, _pick_tile(n_prompt, dec_seq, 256))
    post_rows = _Rows(n_prompt, dec_seq, n_rows, _pick_tile(n_prompt, dec_seq, 512))
    tq = _pick_tile(n_prompt, dec_seq, 256)
    cos64, sin64 = _rope_tables(dec_seq, pre_rows.tm)
    n_tab = cos64.shape[0]
    cos_mla = jnp.concatenate([jnp.ones((n_tab, LANES), F32), cos64, jnp.ones((n_tab, ROPE_PERIOD), F32)], axis=1)
    sin_mla = jnp.concatenate([jnp.zeros((n_tab, LANES), F32), sin64, jnp.zeros((n_tab, ROPE_PERIOD), F32)], axis=1)
    cos_swa, sin_swa = jnp.tile(cos64, (1, 2)), jnp.tile(sin64, (1, 2))

    cond = jnp.concatenate([c_ctx[None, :], c, jnp.zeros((COND_ROWS - 1 - dec_batch, d), F32)], axis=0)
    mods = _adaln(cond, ada_w, ada_b).reshape(depth, COND_ROWS, 6, d)

    x = jnp.concatenate([x_prompt.reshape(n_prompt, d), x_sample.reshape(dec_batch * dec_seq, d)], axis=0)
    row1 = lambda a: a.reshape(1, -1)
    new_ckv, new_krope, new_hg, new_k, new_v = [], [], [], [], []
    for i in range(depth):
        kind, j = i % N_MIXERS, i // N_MIXERS
        gn = row1(norm_mix[i])
        if kind == 0:
            uq = mla_w_uq[j].reshape(-1, MLA_HEADS, MLA_NOPE_DIM + MLA_ROPE_DIM)
            uq = jnp.pad(uq, ((0, 0), (0, 0), (0, MLA_QK_PAD - uq.shape[2]))).reshape(uq.shape[0], -1)
            w = {
                "dq": mla_w_dq[j].astype(BF16), "q_norm": row1(mla_q_norm[j]), "uq": uq.astype(BF16),
                "dkv": jnp.pad(mla_w_dkv[j], ((0, 0), (0, LANES - MLA_ROPE_DIM))).astype(BF16),
                "kv_norm": row1(mla_kv_norm[j]),
                "uk": mla_w_uk[j].astype(BF16), "uv": mla_w_uv[j].astype(BF16),
            }
            q, ckv, kr_raw, kcat, vt = _mla_pre(x, mods[i], gn, cos_mla, sin_mla, w, pre_rows)
            kr_ctx = jnp.pad(cache_mla_krope[:, j].reshape(dec_batch * past, -1), ((0, 0), (0, LANES - MLA_ROPE_DIM)))
            kcat_ctx, vt_ctx = _mla_expand(cache_mla_ckv[:, j].reshape(dec_batch * past, -1), kr_ctx, w["uk"], w["uv"])
            mix = (_mla_attn_prompt(q, kcat, vt, batch, seq),
                   _mla_attn_latent(q, kcat, vt, kcat_ctx, vt_ctx, n_prompt, dec_batch, dec_seq, past, tq))
            wo = mla_w_o[j].astype(BF16)
            new_ckv.append(ckv[:n_prompt].reshape(batch, seq, -1))
            new_krope.append(kr_raw[:n_prompt].reshape(batch, seq, -1))
        elif kind == 1:
            w5 = jnp.concatenate([hg_w_q[j], hg_w_i[j], hg_w_g[j], hg_w_f[j, 0], hg_w_f[j, 1]], axis=1).astype(BF16)
            q, v, gate, f2 = _hgrn_pre(x, mods[i], gn, w5, hg_lb_logits, i, pre_rows)
            rt = _pick_tile(seq, dec_seq, 256)
            od_p, s_prompt = _hgrn_scan(q, v, f2, batch, seq, 0, rt)
            od_l, _ = _hgrn_scan(q, v, f2, dec_batch, dec_seq, n_prompt, rt, s0=state_hgrn[:, j])
            mix = (od_p, od_l, gate, row1(hg_o_norm[j]))
            wo = hg_w_o[j].astype(BF16)
            new_hg.append(s_prompt)
        else:
            wqkv = jnp.concatenate([swa_w_q[j], swa_w_k[j], swa_w_v[j]], axis=1).astype(BF16)
            q, kx, vt, k_raw, v_raw = _swa_pre(x, mods[i], gn, cos_swa, sin_swa, wqkv, pre_rows)
            sink = row1(swa_sink[j])
            k_ctx = cache_swa_k[:, j].reshape(dec_batch * past, SWA_KV_HEADS, 1, SWA_HEAD_DIM)
            kx_ctx = jnp.broadcast_to(k_ctx, (dec_batch * past, SWA_KV_HEADS, SWA_GROUP, SWA_HEAD_DIM))
            kx_ctx = kx_ctx.reshape(dec_batch * past, -1).astype(BF16)
            vt_ctx = cache_swa_v[:, j].reshape(dec_batch * past, -1).T.astype(BF16)
            mix = (_swa_attn_prompt(q, kx, vt, sink, batch, seq),
                   _swa_attn_latent(q, kx, vt, kx_ctx, vt_ctx, sink, n_prompt, dec_batch, dec_seq, past, tq))
            wo = swa_w_o[j].astype(BF16)
            new_k.append(k_raw[:n_prompt].reshape(batch, seq, SWA_KV_HEADS, SWA_HEAD_DIM))
            new_v.append(v_raw[:n_prompt].reshape(batch, seq, SWA_KV_HEADS, SWA_HEAD_DIM))
        x = _post(mix, x, mods[i], wo, row1(norm_ffn[i]), ffn_w_gate[i].astype(BF16), ffn_w_up[i].astype(BF16),
                  ffn_w_down[i].astype(BF16), post_rows,
                  final_norm=row1(final_norm) if i == depth - 1 else None, hgrn=kind == 1)
    y_prompt = x[:n_prompt].reshape(batch, seq, d)
    y_sample = x[n_prompt:].reshape(dec_batch, dec_seq, d)
    return (y_prompt, y_sample, jnp.stack(new_ckv, axis=1), jnp.stack(new_krope, axis=1),
            jnp.stack(new_hg, axis=1), jnp.stack(new_k, axis=1), jnp.stack(new_v, axis=1))
```

```python
import functools

import jax
import jax.numpy as jnp
from jax import lax
from jax.experimental import pallas as pl
from jax.experimental.pallas import tpu as pltpu

F32 = jnp.float32
BF16 = jnp.bfloat16

GRID_W = 64
N_MIXERS = 3

MLA_HEADS = 8
MLA_KV_LORA = 256
MLA_NOPE_DIM = 128
MLA_ROPE_DIM = 64
MLA_V_DIM = 128
MLA_QK_PAD = 256
MLA_SCALE = (MLA_NOPE_DIM + MLA_ROPE_DIM) ** -0.5

HG_HEADS = 8
HG_DK = 128
HG_DV = 128
HG_DIAG_BLOCK = 32

SWA_HEADS = 16
SWA_KV_HEADS = 4
SWA_GROUP = SWA_HEADS // SWA_KV_HEADS
SWA_HEAD_DIM = 64
SWA_GW = SWA_GROUP * SWA_HEAD_DIM
SWA_WINDOW = 128
SWA_SCALE = SWA_HEAD_DIM ** -0.5

ROPE_BASE = 10000.0
ROPE_PERIOD = 64
ROPE_QUARTER = 16
NORM_EPS = 1e-6
NEG_INF = -1e30
LOG2_E = 1.4426950408889634

LANES = 128
COND_ROWS = 8
VMEM_LIMIT = 56 * 1024 * 1024


def _sigmoid(x):
    return jax.nn.sigmoid(x)


def _silu(x):
    return x * jax.nn.sigmoid(x)


def _rms(x, g):
    return x * lax.rsqrt(jnp.mean(x * x, axis=-1, keepdims=True) + NORM_EPS) * g


def _modulate(x, g, shift, scale):
    return _rms(x, g) * (1.0 + scale) + shift


def _dot(a, b):
    return jnp.dot(a, b, preferred_element_type=F32)


def _dot_nt(a, b):
    return lax.dot_general(a, b, (((1,), (1,)), ((), ())), preferred_element_type=F32)


def _dot_tn(a, b):
    return lax.dot_general(a, b, (((0,), (0,)), ((), ())), preferred_element_type=F32)


def _swap_pairs(x):
    n = x.shape[1]
    lane = lax.broadcasted_iota(jnp.int32, x.shape, 1)
    ahead = pltpu.roll(x, n - ROPE_QUARTER, 1)
    behind = pltpu.roll(x, ROPE_QUARTER, 1)
    return jnp.where((lane & (2 * ROPE_QUARTER - 1)) < ROPE_QUARTER, ahead, behind)


def _rope(x, cos, sin):
    reps = x.shape[1] // cos.shape[1]
    if reps > 1:
        cos = jnp.concatenate([cos] * reps, axis=1)
        sin = jnp.concatenate([sin] * reps, axis=1)
    return x * cos + _swap_pairs(x) * sin


def _whole(shape):
    zeros = (0,) * len(shape)
    return pl.BlockSpec(shape, lambda *_: zeros, pipeline_mode=pl.Buffered(1))


def _params(*sem):
    return pltpu.CompilerParams(dimension_semantics=sem, vmem_limit_bytes=VMEM_LIMIT)


class _Rows:
    def __init__(self, n_prompt_rows, dec_seq, n_rows, tm):
        assert n_prompt_rows % tm == 0 and dec_seq % tm == 0
        self.tm = tm
        self.n_tiles = n_rows // tm
        self.prompt_tiles = n_prompt_rows // tm
        self.seq_tiles = dec_seq // tm

    def cond(self, i):
        return jnp.where(i < self.prompt_tiles, 0, 1 + jnp.maximum(i - self.prompt_tiles, 0) // self.seq_tiles)

    def rope_block(self, i):
        return jnp.where(i < self.prompt_tiles, 0, 1 + jnp.maximum(i - self.prompt_tiles, 0) % self.seq_tiles)

    def row_spec(self, width):
        return pl.BlockSpec((self.tm, width), lambda i: (i, 0))

    def col_spec(self, height):
        return pl.BlockSpec((height, self.tm), lambda i: (0, i))

    def mod_spec(self, d):
        return pl.BlockSpec((1, 6, d), lambda i: (self.cond(i), 0, 0))

    def rope_spec(self, width):
        return pl.BlockSpec((self.tm, width), lambda i: (self.rope_block(i), 0))

    def prompt_spec(self, width):
        return pl.BlockSpec((self.tm, width), lambda i: (jnp.minimum(i, self.prompt_tiles - 1), 0))

    def latent_spec(self, width):
        return pl.BlockSpec((self.tm, width), lambda i: (jnp.maximum(i - self.prompt_tiles, 0), 0))


def _rope_tables(dec_seq, tm):
    pos = jnp.arange(dec_seq)
    row = (pos // GRID_W).astype(F32)
    col = (pos % GRID_W).astype(F32)
    inv_freq = ROPE_BASE ** (-jnp.arange(ROPE_QUARTER, dtype=F32) / ROPE_QUARTER)
    ang_r = row[:, None] * inv_freq[None, :]
    ang_c = col[:, None] * inv_freq[None, :]
    cos = jnp.concatenate([jnp.cos(ang_r), jnp.cos(ang_r), jnp.cos(ang_c), jnp.cos(ang_c)], axis=1)
    sin = jnp.concatenate([-jnp.sin(ang_r), jnp.sin(ang_r), -jnp.sin(ang_c), jnp.sin(ang_c)], axis=1)
    cos = jnp.concatenate([jnp.ones((tm, ROPE_PERIOD), F32), cos], axis=0)
    sin = jnp.concatenate([jnp.zeros((tm, ROPE_PERIOD), F32), sin], axis=0)
    return cos, sin


def _adaln_kernel(c_ref, w_ref, b_ref, o_ref):
    o_ref[0] = _dot(_silu(c_ref[...]), w_ref[0]) + b_ref[0]


def _adaln(cond, ada_w, ada_b, tn=1536):
    depth, d, n = ada_w.shape
    return pl.pallas_call(
        _adaln_kernel,
        grid=(depth, n // tn),
        in_specs=[pl.BlockSpec((COND_ROWS, d), lambda l, j: (0, 0)),
                  pl.BlockSpec((1, d, tn), lambda l, j: (l, 0, j)),
                  pl.BlockSpec((1, 1, tn), lambda l, j: (l, 0, j))],
        out_specs=pl.BlockSpec((1, COND_ROWS, tn), lambda l, j: (l, 0, j)),
        out_shape=jax.ShapeDtypeStruct((depth, COND_ROWS, n), F32),
        compiler_params=_params("arbitrary", "arbitrary"),
        name="adaln",
    )(cond, ada_w, ada_b.reshape(depth, 1, n))


def _mla_store_heads(q_ref, kcat_ref, vt_ref, q_all, kn, kr_pad, v):
    vt = v.T.astype(BF16)
    for h in range(MLA_HEADS):
        if q_ref is not None:
            q_ref[h] = q_all[:, h * MLA_QK_PAD:(h + 1) * MLA_QK_PAD]
        kcat_ref[h] = jnp.concatenate([kn[:, h * MLA_NOPE_DIM:(h + 1) * MLA_NOPE_DIM], kr_pad], axis=1)
        vt_ref[h] = vt[h * MLA_V_DIM:(h + 1) * MLA_V_DIM, :]


def _mla_pre_kernel(x_ref, mod_ref, gn_ref, cos_ref, sin_ref, wdq_ref, qnorm_ref, wuq_ref,
                    wdkv_ref, kvnorm_ref, wuk_ref, wuv_ref,
                    q_ref, ckv_ref, krraw_ref, kcat_ref, vt_ref):
    m = mod_ref[0]
    h = _modulate(x_ref[...], gn_ref[...], m[0:1], m[1:2]).astype(BF16)
    q_lat = _rms(_dot(h, wdq_ref[...]), qnorm_ref[...]).astype(BF16)
    cos, sin = cos_ref[...], sin_ref[...]
    q_all = _rope(_dot(q_lat, wuq_ref[...]), cos, sin).astype(BF16)
    kv = _dot(h, wdkv_ref[...])
    ckv = _rms(kv[:, :MLA_KV_LORA], kvnorm_ref[...])
    ckv_ref[...] = ckv
    kr = kv[:, MLA_KV_LORA:]
    krraw_ref[...] = kr[:, :MLA_ROPE_DIM]
    kr_pad = _rope(kr, cos[:, LANES:], sin[:, LANES:]).astype(BF16)
    cb = ckv.astype(BF16)
    _mla_store_heads(q_ref, kcat_ref, vt_ref, q_all, _dot(cb, wuk_ref[...]).astype(BF16), kr_pad,
                     _dot(cb, wuv_ref[...]))


def _mla_pre(x, mods, gn, cos, sin, w, rows):
    t, d = x.shape
    tm = rows.tm
    ins = [x, mods, gn, cos, sin, w["dq"], w["q_norm"], w["uq"], w["dkv"], w["kv_norm"], w["uk"], w["uv"]]
    in_specs = [rows.row_spec(d), rows.mod_spec(d), _whole(gn.shape),
                rows.rope_spec(cos.shape[1]), rows.rope_spec(sin.shape[1])]
    in_specs += [_whole(a.shape) for a in ins[5:]]
    heads_rows = pl.BlockSpec((MLA_HEADS, tm, MLA_QK_PAD), lambda i: (0, i, 0))
    return pl.pallas_call(
        _mla_pre_kernel,
        grid=(rows.n_tiles,),
        in_specs=in_specs,
        out_specs=[heads_rows, rows.row_spec(MLA_KV_LORA), rows.row_spec(MLA_ROPE_DIM), heads_rows,
                   pl.BlockSpec((MLA_HEADS, MLA_V_DIM, tm), lambda i: (0, 0, i))],
        out_shape=[jax.ShapeDtypeStruct((MLA_HEADS, t, MLA_QK_PAD), BF16), jax.ShapeDtypeStruct((t, MLA_KV_LORA), F32),
                   jax.ShapeDtypeStruct((t, MLA_ROPE_DIM), F32),
                   jax.ShapeDtypeStruct((MLA_HEADS, t, MLA_QK_PAD), BF16),
                   jax.ShapeDtypeStruct((MLA_HEADS, MLA_V_DIM, t), BF16)],
        compiler_params=_params("arbitrary"),
        name="mla_pre",
    )(*ins)


def _mla_expand_kernel(c_ref, kr_ref, wuk_ref, wuv_ref, kcat_ref, vt_ref):
    cb = c_ref[...].astype(BF16)
    _mla_store_heads(None, kcat_ref, vt_ref, None, _dot(cb, wuk_ref[...]).astype(BF16), kr_ref[...].astype(BF16),
                     _dot(cb, wuv_ref[...]))


def _mla_expand(ckv, kr_pad, wuk, wuv):
    n = ckv.shape[0]
    return pl.pallas_call(
        _mla_expand_kernel,
        out_shape=[jax.ShapeDtypeStruct((MLA_HEADS, n, MLA_QK_PAD), BF16),
                   jax.ShapeDtypeStruct((MLA_HEADS, MLA_V_DIM, n), BF16)],
        compiler_params=pltpu.CompilerParams(vmem_limit_bytes=VMEM_LIMIT),
        name="mla_expand",
    )(ckv, kr_pad, wuk, wuv)


def _mla_attn_kernel(*refs, n_src):
    q_ref = refs[0]
    srcs = [(refs[1 + 2 * i], refs[2 + 2 * i]) for i in range(n_src)]
    o_ref, s_even, s_odd, o_buf = refs[1 + 2 * n_src:]
    n_keys = [k_ref.shape[1] for k_ref, _ in srcs]
    starts = [sum(n_keys[:i]) for i in range(n_src)]

    def put_scores(h, s_buf):
        q = q_ref[h]
        for (k_ref, _), first, n in zip(srcs, starts, n_keys):
            s_buf[first:first + n, :] = _dot_nt(k_ref[h], q)

    def finish(h, s_buf):
        s = s_buf[...]
        mx = jnp.max(s, axis=0, keepdims=True)
        p = jnp.exp2((s - mx) * (MLA_SCALE * LOG2_E))
        den = jnp.sum(p, axis=0, keepdims=True)
        p = p.astype(BF16)
        acc = None
        for (_, vt_ref), first, n in zip(srcs, starts, n_keys):
            a = _dot(vt_ref[h], p[first:first + n, :])
            acc = a if acc is None else acc + a
        o_buf[h] = (acc / den).T.astype(BF16)

    put_scores(0, s_even)

    def pair(j, carry):
        h = 2 * j
        put_scores(h + 1, s_odd)
        finish(h, s_even)
        put_scores(h + 2, s_even)
        finish(h + 1, s_odd)
        return carry

    lax.fori_loop(0, MLA_HEADS // 2 - 1, pair, 0)
    put_scores(MLA_HEADS - 1, s_odd)
    finish(MLA_HEADS - 2, s_even)
    finish(MLA_HEADS - 1, s_odd)
    for h in range(MLA_HEADS):
        o_ref[:, h * MLA_V_DIM:(h + 1) * MLA_V_DIM] = o_buf[h]


def _mla_attn_scratch(n_keys, tq):
    return [pltpu.VMEM((n_keys, tq), F32), pltpu.VMEM((n_keys, tq), F32), pltpu.VMEM((MLA_HEADS, tq, MLA_V_DIM), BF16)]


def _staged_attention(scores, values_t, scale, sinks=None):
    c = scale * LOG2_E
    mx = [jnp.max(s, axis=0, keepdims=True) for s in scores]
    if sinks is not None:
        mx = [jnp.maximum(m, z) for m, z in zip(mx, sinks)]
    p = [jnp.exp2((s - m) * c) for s, m in zip(scores, mx)]
    den = [jnp.sum(x, axis=0, keepdims=True) for x in p]
    if sinks is not None:
        den = [d + jnp.exp2((z - m) * c) for d, z, m in zip(den, sinks, mx)]
    acc = [_dot(v, x.astype(BF16)) for v, x in zip(values_t, p)]
    return [a / d for a, d in zip(acc, den)]


def _mla_attn_prompt_kernel(q_ref, k_ref, vt_ref, o_ref):
    scores = [_dot_nt(k_ref[h], q_ref[h]) for h in range(MLA_HEADS)]
    outs = _staged_attention(scores, [vt_ref[h] for h in range(MLA_HEADS)], MLA_SCALE)
    for h in range(MLA_HEADS):
        o_ref[:, h * MLA_V_DIM:(h + 1) * MLA_V_DIM] = outs[h].T.astype(BF16)


def _mla_attn_prompt(q, kcat, vt, batch, seq):
    hv = MLA_HEADS * MLA_V_DIM
    rows = pl.BlockSpec((MLA_HEADS, seq, MLA_QK_PAD), lambda b: (0, b, 0))
    return pl.pallas_call(
        _mla_attn_prompt_kernel,
        grid=(batch,),
        in_specs=[rows, rows, pl.BlockSpec((MLA_HEADS, MLA_V_DIM, seq), lambda b: (0, 0, b))],
        out_specs=pl.BlockSpec((seq, hv), lambda b: (b, 0)),
        out_shape=jax.ShapeDtypeStruct((batch * seq, hv), BF16),
        compiler_params=_params("arbitrary"),
        name="mla_attn_prompt",
    )(q, kcat, vt)


def _mla_attn_latent(q, kcat, vt, kcat_ctx, vt_ctx, n_prompt, dec_batch, dec_seq, past, tq):
    hv = MLA_HEADS * MLA_V_DIM
    nq = dec_seq // tq
    lat_blk = n_prompt // dec_seq
    return pl.pallas_call(
        functools.partial(_mla_attn_kernel, n_src=2),
        grid=(dec_batch, nq),
        in_specs=[pl.BlockSpec((MLA_HEADS, tq, MLA_QK_PAD), lambda b, i: (0, n_prompt // tq + b * nq + i, 0)),
                  pl.BlockSpec((MLA_HEADS, past, MLA_QK_PAD), lambda b, i: (0, b, 0)),
                  pl.BlockSpec((MLA_HEADS, MLA_V_DIM, past), lambda b, i: (0, 0, b)),
                  pl.BlockSpec((MLA_HEADS, dec_seq, MLA_QK_PAD), lambda b, i: (0, lat_blk + b, 0)),
                  pl.BlockSpec((MLA_HEADS, MLA_V_DIM, dec_seq), lambda b, i: (0, 0, lat_blk + b))],
        out_specs=pl.BlockSpec((tq, hv), lambda b, i: (b * nq + i, 0)),
        out_shape=jax.ShapeDtypeStruct((dec_batch * dec_seq, hv), BF16),
        scratch_shapes=_mla_attn_scratch(past + dec_seq, tq),
        compiler_params=_params("arbitrary", "arbitrary"),
        name="mla_attn_latent",
    )(q, kcat_ctx, vt_ctx, kcat, vt)


def _tile_heads(x):
    n = x.shape[1]
    block = lax.broadcasted_iota(jnp.int32, x.shape, 1) // SWA_HEAD_DIM
    rolled = [x] + [pltpu.roll(x, s * SWA_HEAD_DIM, 1) for s in range(1, SWA_KV_HEADS)]
    out = []
    for kvh in range(SWA_KV_HEADS):
        blk = rolled[(0 - kvh) % SWA_KV_HEADS]
        for g in range(1, n // SWA_HEAD_DIM):
            blk = jnp.where(block == g, rolled[(g - kvh) % SWA_KV_HEADS], blk)
        out.append(blk)
    return jnp.concatenate(out, axis=1)


def _swa_pre_kernel(x_ref, mod_ref, gn_ref, cos_ref, sin_ref, wqkv_ref, q_ref, kx_ref, vt_ref, kraw_ref, vraw_ref):
    m = mod_ref[0]
    h = _modulate(x_ref[...], gn_ref[...], m[0:1], m[1:2]).astype(BF16)
    qkv = _dot(h, wqkv_ref[...])
    nq, nk = SWA_HEADS * SWA_HEAD_DIM, SWA_KV_HEADS * SWA_HEAD_DIM
    cos, sin = cos_ref[...], sin_ref[...]
    q = _rope(qkv[:, :nq], cos, sin).astype(BF16)
    k = qkv[:, nq:nq + nk]
    v = qkv[:, nq + nk:]
    kraw_ref[...] = k
    vraw_ref[...] = v
    kx = _tile_heads(_rope(k, cos, sin)).astype(BF16)
    vt = v.T.astype(BF16)
    for kvh in range(SWA_KV_HEADS):
        q_ref[kvh] = q[:, kvh * SWA_GW:(kvh + 1) * SWA_GW]
        kx_ref[kvh] = kx[:, kvh * SWA_GW:(kvh + 1) * SWA_GW]
        vt_ref[kvh] = vt[kvh * SWA_HEAD_DIM:(kvh + 1) * SWA_HEAD_DIM, :]


def _swa_pre(x, mods, gn, cos, sin, wqkv, rows):
    t, d = x.shape
    tm = rows.tm
    nk = SWA_KV_HEADS * SWA_HEAD_DIM
    heads_rows = pl.BlockSpec((SWA_KV_HEADS, tm, SWA_GW), lambda i: (0, i, 0))
    return pl.pallas_call(
        _swa_pre_kernel,
        grid=(rows.n_tiles,),
        in_specs=[rows.row_spec(d), rows.mod_spec(d), _whole(gn.shape),
                  rows.rope_spec(cos.shape[1]), rows.rope_spec(sin.shape[1]), _whole(wqkv.shape)],
        out_specs=[heads_rows, heads_rows, pl.BlockSpec((SWA_KV_HEADS, SWA_HEAD_DIM, tm), lambda i: (0, 0, i)),
                   rows.row_spec(nk), rows.row_spec(nk)],
        out_shape=[jax.ShapeDtypeStruct((SWA_KV_HEADS, t, SWA_GW), BF16),
                   jax.ShapeDtypeStruct((SWA_KV_HEADS, t, SWA_GW), BF16),
                   jax.ShapeDtypeStruct((SWA_KV_HEADS, SWA_HEAD_DIM, t), BF16),
                   jax.ShapeDtypeStruct((t, nk), F32), jax.ShapeDtypeStruct((t, nk), F32)],
        compiler_params=_params("arbitrary"),
        name="swa_pre",
    )(x, mods, gn, cos, sin, wqkv)


def _swa_heads(q_ref, sink_ref, o_ref, srcs, bias_ref, s_even, s_odd, ot_buf):
    tq = q_ref.shape[1]
    n_keys = [kx_ref.shape[1] for kx_ref, _ in srcs]
    starts = [sum(n_keys[:i]) for i in range(len(srcs))]
    group = lax.broadcasted_iota(jnp.int32, (tq, SWA_GW), 1) // SWA_HEAD_DIM

    def put_scores(hq, s_buf):
        kvh, g = hq // SWA_GROUP, hq % SWA_GROUP
        q = jnp.where(group == g, q_ref[kvh].astype(F32), 0.0).astype(BF16)
        for (kx_ref, _), first, n in zip(srcs, starts, n_keys):
            s_buf[first:first + n, :] = _dot_nt(kx_ref[kvh], q)

    def finish(hq, s_buf):
        kvh, g = hq // SWA_GROUP, hq % SWA_GROUP
        s = s_buf[...]
        if bias_ref is not None:
            s = s + bias_ref[...]
        sink = jnp.full((1, 1), sink_ref[hq] * (1.0 / SWA_SCALE), F32)
        mx = jnp.maximum(jnp.max(s, axis=0, keepdims=True), sink)
        p = jnp.exp2((s - mx) * (SWA_SCALE * LOG2_E))
        den = jnp.sum(p, axis=0, keepdims=True) + jnp.exp2((sink - mx) * (SWA_SCALE * LOG2_E))
        p = p.astype(BF16)
        acc = None
        for (_, vt_ref), first, n in zip(srcs, starts, n_keys):
            a = _dot(vt_ref[kvh], p[first:first + n, :])
            acc = a if acc is None else acc + a
        ot_buf[kvh, pl.ds(pl.multiple_of(g * SWA_HEAD_DIM, SWA_HEAD_DIM), SWA_HEAD_DIM), :] = acc / den

    put_scores(0, s_even)

    def pair(j, carry):
        hq = 2 * j
        put_scores(hq + 1, s_odd)
        finish(hq, s_even)
        put_scores(hq + 2, s_even)
        finish(hq + 1, s_odd)
        return carry

    lax.fori_loop(0, SWA_HEADS // 2 - 1, pair, 0)
    put_scores(SWA_HEADS - 1, s_odd)
    finish(SWA_HEADS - 2, s_even)
    finish(SWA_HEADS - 1, s_odd)
    for kvh in range(SWA_KV_HEADS):
        o_ref[:, kvh * SWA_GW:(kvh + 1) * SWA_GW] = ot_buf[kvh].T.astype(BF16)


def _swa_attn_scratch(n_keys, tq):
    return [pltpu.VMEM((n_keys, tq), F32), pltpu.VMEM((n_keys, tq), F32), pltpu.VMEM((SWA_KV_HEADS, SWA_GW, tq), F32)]


def _swa_attn_prompt_kernel(q_ref, kx_ref, vt_ref, sink_ref, o_ref):
    group = lax.broadcasted_iota(jnp.int32, (q_ref.shape[1], SWA_GW), 1) // SWA_HEAD_DIM
    scores, values_t, sinks = [], [], []
    for kvh in range(SWA_KV_HEADS):
        q_all = q_ref[kvh].astype(F32)
        for g in range(SWA_GROUP):
            q = jnp.where(group == g, q_all, 0.0).astype(BF16)
            scores.append(_dot_nt(kx_ref[kvh], q))
            values_t.append(vt_ref[kvh])
            sinks.append(jnp.full((1, 1), sink_ref[kvh * SWA_GROUP + g] * (1.0 / SWA_SCALE), F32))
    outs = _staged_attention(scores, values_t, SWA_SCALE, sinks)
    for kvh in range(SWA_KV_HEADS):
        ot = jnp.concatenate(outs[kvh * SWA_GROUP:(kvh + 1) * SWA_GROUP], axis=0)
        o_ref[:, kvh * SWA_GW:(kvh + 1) * SWA_GW] = ot.T.astype(BF16)


def _swa_attn_prompt(q, kx, vt, sink, batch, seq):
    wq = SWA_HEADS * SWA_HEAD_DIM
    rows = pl.BlockSpec((SWA_KV_HEADS, seq, SWA_GW), lambda b: (0, b, 0))
    return pl.pallas_call(
        _swa_attn_prompt_kernel,
        grid=(batch,),
        in_specs=[rows, rows, pl.BlockSpec((SWA_KV_HEADS, SWA_HEAD_DIM, seq), lambda b: (0, 0, b)),
                  pl.BlockSpec(memory_space=pltpu.SMEM)],
        out_specs=pl.BlockSpec((seq, wq), lambda b: (b, 0)),
        out_shape=jax.ShapeDtypeStruct((batch * seq, wq), BF16),
        compiler_params=_params("arbitrary"),
        name="swa_attn_prompt",
    )(q, kx, vt, sink)


def _swa_attn_latent_kernel(q_ref, kc_ref, vc_ref, kp_ref, vp_ref, km_ref, vm_ref, kn_ref, vn_ref,
                            sink_ref, o_ref, s_even, s_odd, ot_buf, bias_ref, *, tq, dec_seq, past):
    i = pl.program_id(1)
    qpos = i * tq + lax.broadcasted_iota(jnp.int32, (1, tq), 1)

    def band(first, n):
        kpos = first + lax.broadcasted_iota(jnp.int32, (n, 1), 0)
        valid = (jnp.abs(qpos - kpos) <= SWA_WINDOW) & (kpos >= 0) & (kpos < dec_seq)
        return jnp.where(valid, 0.0, NEG_INF)

    bias_ref[...] = jnp.concatenate(
        [jnp.zeros((past, tq), F32), band(i * tq - SWA_WINDOW, SWA_WINDOW), band(i * tq, tq),
         band((i + 1) * tq, SWA_WINDOW)], axis=0)
    srcs = [(kc_ref, vc_ref), (kp_ref, vp_ref), (km_ref, vm_ref), (kn_ref, vn_ref)]
    _swa_heads(q_ref, sink_ref, o_ref, srcs, bias_ref, s_even, s_odd, ot_buf)


def _swa_attn_latent(q, kx, vt, kx_ctx, vt_ctx, sink, n_prompt, dec_batch, dec_seq, past, tq):
    t = q.shape[1]
    wq = SWA_HEADS * SWA_HEAD_DIM
    nq = dec_seq // tq
    w = SWA_WINDOW
    n_keys = past + tq + 2 * w
    first = lambda b, i: n_prompt + b * dec_seq + i * tq
    prev = lambda b, i: first(b, i) // w - 1
    nxt = lambda b, i: jnp.minimum((first(b, i) + tq) // w, t // w - 1)
    rows = lambda n, blk: pl.BlockSpec((SWA_KV_HEADS, n, SWA_GW), lambda b, i: (0, blk(b, i), 0))
    cols = lambda n, blk: pl.BlockSpec((SWA_KV_HEADS, SWA_HEAD_DIM, n), lambda b, i: (0, 0, blk(b, i)))
    main = lambda b, i: first(b, i) // tq
    ctx = lambda b, i: b
    return pl.pallas_call(
        functools.partial(_swa_attn_latent_kernel, tq=tq, dec_seq=dec_seq, past=past),
        grid=(dec_batch, nq),
        in_specs=[rows(tq, main), rows(past, ctx), cols(past, ctx), rows(w, prev), cols(w, prev),
                  rows(tq, main), cols(tq, main), rows(w, nxt), cols(w, nxt),
                  pl.BlockSpec(memory_space=pltpu.SMEM)],
        out_specs=pl.BlockSpec((tq, wq), lambda b, i: (b * nq + i, 0)),
        out_shape=jax.ShapeDtypeStruct((dec_batch * dec_seq, wq), BF16),
        scratch_shapes=_swa_attn_scratch(n_keys, tq) + [pltpu.VMEM((n_keys, tq), F32)],
        compiler_params=_params("arbitrary", "arbitrary"),
        name="swa_attn_latent",
    )(q, kx_ctx, vt_ctx, kx, vt, kx, vt, kx, vt, sink)


def _hgrn_pre_kernel(x_ref, mod_ref, gn_ref, w_ref, lbl_ref, q_ref, v_ref, g_ref, f_ref, *, layer):
    m = mod_ref[0]
    h = _modulate(x_ref[...], gn_ref[...], m[0:1], m[1:2]).astype(BF16)
    y = _dot(h, w_ref[...])
    n = HG_HEADS * HG_DK
    q_ref[...] = _silu(y[:, :n])
    v_ref[...] = y[:, n:2 * n]
    g_ref[...] = _silu(y[:, 2 * n:3 * n])
    for d in range(2):
        logits = lbl_ref[d]
        e = jnp.exp(logits - jnp.max(logits, axis=0, keepdims=True))
        s = e / jnp.sum(e, axis=0, keepdims=True)
        cs = s[0:1]
        for r in range(1, layer + 1):
            cs = cs + s[r:r + 1]
        lb = cs - s[0:1]
        f_ref[d] = lb + (1.0 - lb) * _sigmoid(y[:, (3 + d) * n:(4 + d) * n])


def _hgrn_pre(x, mods, gn, w5, lb_logits, layer, rows):
    t, d = x.shape
    n = HG_HEADS * HG_DK
    return pl.pallas_call(
        functools.partial(_hgrn_pre_kernel, layer=layer),
        grid=(rows.n_tiles,),
        in_specs=[rows.row_spec(d), rows.mod_spec(d), _whole(gn.shape), _whole(w5.shape), _whole(lb_logits.shape)],
        out_specs=[rows.row_spec(n), rows.row_spec(n), rows.row_spec(n),
                   pl.BlockSpec((2, rows.tm, n), lambda i: (0, i, 0))],
        out_shape=[jax.ShapeDtypeStruct((t, n), F32)] * 3 + [jax.ShapeDtypeStruct((2, t, n), F32)],
        compiler_params=_params("arbitrary"),
        name="hgrn_pre",
    )(x, mods, gn, w5, lb_logits)


def _tri_cumsum(tri, x):
    hi = x.astype(BF16)
    r1 = x - hi.astype(F32)
    mid = r1.astype(BF16)
    lo = (r1 - mid.astype(F32)).astype(BF16)
    return _dot(tri, hi) + _dot(tri, mid) + _dot(tri, lo)


def _hgrn_tile(q_ref, v_ref, f_ref, o_ref, st_ref, *, rows, reverse):
    r = rows
    a = lax.broadcasted_iota(jnp.int32, (r, r), 0)
    b = lax.broadcasted_iota(jnp.int32, (r, r), 1)
    seen = (b >= a) if reverse else (b <= a)
    tri = jnp.where(seen, 1.0, 0.0).astype(BF16)

    q = q_ref[...]
    f = f_ref[0]
    vb = v_ref[...].astype(BF16)
    lf = jnp.log(f)
    cum = _tri_cumsum(tri, lf)
    tot = cum[0:1, :] if reverse else cum[r - 1:r, :]
    kk = 1.0 - f
    q_in = (q * jnp.exp(cum)).astype(BF16)
    k_d = (kk * jnp.exp(tot - cum)).astype(BF16)
    e_tot = jnp.exp(tot)

    levels = []
    c = r // 2
    while c >= HG_DIAG_BLOCK:
        q_half = (a % (2 * c) < c) if reverse else (a % (2 * c) >= c)
        k_half = (b % (2 * c) >= c) if reverse else (b % (2 * c) < c)
        levels.append((2 * c, c if reverse else c - 1, ((a // (2 * c)) == (b // (2 * c))) & q_half & k_half))
        c //= 2
    blk = HG_DIAG_BLOCK
    levels.append((blk, blk // 2, ((a // blk) == (b // blk)) & seen))

    factors = []
    for size, ref_row, own in levels:
        ref = jnp.concatenate(
            [jnp.broadcast_to(cum[j * size + ref_row:j * size + ref_row + 1, :], (size, cum.shape[1]))
             for j in range(r // size)], axis=0)
        factors.append(((q * jnp.exp(cum - ref)).astype(BF16), (kk * jnp.exp(ref - cum)).astype(BF16), own))

    outs = []
    for h in range(HG_HEADS):
        sl = slice(h * HG_DK, (h + 1) * HG_DK)
        att = jnp.zeros((r, r), F32)
        for q_l, k_l, own in factors:
            att = jnp.where(own, _dot_nt(q_l[:, sl], k_l[:, sl]), att)
        st = st_ref[h]
        outs.append(_dot_nt(q_in[:, sl], st.astype(BF16)) + _dot(att.astype(BF16), vb[:, sl]))
        st_ref[h] = st * e_tot[:, sl] + _dot_tn(vb[:, sl], k_d[:, sl])
    o_ref[0] = jnp.concatenate(outs, axis=1)


def _hgrn_scan_kernel(*refs, rows, has_init):
    if has_init:
        q_ref, v_ref, f_ref, s0_ref, o_ref, sfin_ref, st_ref = refs
    else:
        q_ref, v_ref, f_ref, o_ref, sfin_ref, st_ref = refs
    d = pl.program_id(1)
    t = pl.program_id(2)

    @pl.when(t == 0)
    def _():
        for h in range(HG_HEADS):
            st_ref[h] = s0_ref[0, 0, h].T if has_init else jnp.zeros((HG_DV, HG_DK), F32)

    for reverse in (False, True):
        @pl.when(d == int(reverse))
        def _():
            _hgrn_tile(q_ref, v_ref, f_ref, o_ref, st_ref, rows=rows, reverse=reverse)

    @pl.when(t == pl.num_programs(2) - 1)
    def _():
        for h in range(HG_HEADS):
            sfin_ref[0, 0, h] = st_ref[h].T


def _hgrn_scan(q, v, f2, batch, seq, first_row, rt, s0=None):
    n = q.shape[1]
    nt = seq // rt
    has_init = s0 is not None

    def local(b, d, i):
        return b * nt + jnp.where(d == 0, i, nt - 1 - i)

    def slab(b, d, i):
        return first_row // rt + local(b, d, i)

    row = pl.BlockSpec((rt, n), lambda b, d, i: (slab(b, d, i), 0))
    state = pl.BlockSpec((1, 1, HG_HEADS, HG_DK, HG_DV), lambda b, d, i: (b, d, 0, 0, 0))
    ins = [q, v, f2]
    in_specs = [row, row, pl.BlockSpec((1, rt, n), lambda b, d, i: (d, slab(b, d, i), 0))]
    if has_init:
        ins.append(s0)
        in_specs.append(state)
    return pl.pallas_call(
        functools.partial(_hgrn_scan_kernel, rows=rt, has_init=has_init),
        grid=(batch, 2, nt),
        in_specs=in_specs,
        out_specs=[pl.BlockSpec((1, rt, n), lambda b, d, i: (d, local(b, d, i), 0)), state],
        out_shape=[jax.ShapeDtypeStruct((2, batch * seq, n), F32),
                   jax.ShapeDtypeStruct((batch, 2, HG_HEADS, HG_DK, HG_DV), F32)],
        scratch_shapes=[pltpu.VMEM((HG_HEADS, HG_DV, HG_DK), F32)],
        compiler_params=_params("arbitrary", "arbitrary", "arbitrary"),
        name="hgrn_scan_latent" if has_init else "hgrn_scan_prompt",
    )(*ins)


def _post_kernel(*refs, hgrn, final, prompt_tiles):
    refs = list(refs)
    out_ref = refs.pop()
    is_prompt = pl.program_id(0) < prompt_tiles
    if hgrn:
        odp_ref, odl_ref, g_ref, onorm_ref = refs[:4]
        refs = refs[4:]
        o2 = jnp.where(is_prompt, odp_ref[0] + odp_ref[1], odl_ref[0] + odl_ref[1])
        gate = g_ref[...]
        onorm = onorm_ref[...]
        parts = []
        for h in range(HG_HEADS):
            sl = slice(h * HG_DV, (h + 1) * HG_DV)
            parts.append(_rms(o2[:, sl], onorm) * gate[:, sl])
        o = jnp.concatenate(parts, axis=1).astype(BF16)
    else:
        o = jnp.where(is_prompt, refs[0][...], refs[1][...])
        refs = refs[2:]
    x_ref, mod_ref, wo_ref, gn_ref, wg_ref, wu_ref, wd_ref = refs[:7]
    m = mod_ref[0]
    x1 = x_ref[...] + m[2:3] * _dot(o, wo_ref[...])
    h2 = _modulate(x1, gn_ref[...], m[3:4], m[4:5]).astype(BF16)
    a = (_silu(_dot(h2, wg_ref[...])) * _dot(h2, wu_ref[...])).astype(BF16)
    x2 = x1 + m[5:6] * _dot(a, wd_ref[...])
    if final:
        x2 = _rms(x2, refs[7][...])
    out_ref[...] = x2


def _post(mix, x, mods, wo, gn, wg, wu, wd, rows, final_norm=None, hgrn=False):
    t, d = x.shape
    tm = rows.tm
    if hgrn:
        od_p, od_l, gate, onorm = mix
        head = [od_p, od_l, gate, onorm]
        head_specs = [pl.BlockSpec((2, tm, d), lambda i: (0, jnp.minimum(i, rows.prompt_tiles - 1), 0)),
                      pl.BlockSpec((2, tm, d), lambda i: (0, jnp.maximum(i - rows.prompt_tiles, 0), 0)),
                      rows.row_spec(d), _whole(onorm.shape)]
    else:
        head = list(mix)
        head_specs = [rows.prompt_spec(mix[0].shape[1]), rows.latent_spec(mix[1].shape[1])]
    ins = head + [x, mods, wo, gn, wg, wu, wd]
    in_specs = head_specs + [rows.row_spec(d), rows.mod_spec(d)] + [_whole(a.shape) for a in (wo, gn, wg, wu, wd)]
    if final_norm is not None:
        ins.append(final_norm)
        in_specs.append(_whole(final_norm.shape))
    return pl.pallas_call(
        functools.partial(_post_kernel, hgrn=hgrn, final=final_norm is not None, prompt_tiles=rows.prompt_tiles),
        grid=(rows.n_tiles,),
        in_specs=in_specs,
        out_specs=rows.row_spec(d),
        out_shape=jax.ShapeDtypeStruct((t, d), F32),
        compiler_params=_params("arbitrary"),
        name="post",
    )(*ins)


def _pick_tile(n_prompt_rows, dec_seq, want):
    tm = want
    while n_prompt_rows % tm or dec_seq % tm:
        tm //= 2
    return tm


def kernel(x_prompt, x_sample, cache_mla_ckv, cache_mla_krope, state_hgrn, cache_swa_k, cache_swa_v, c, c_ctx, ada_w, ada_b, norm_mix, norm_ffn, ffn_w_gate, ffn_w_up, ffn_w_down, final_norm, mla_w_dq, mla_q_norm, mla_w_uq, mla_w_dkv, mla_kv_norm, mla_w_uk, mla_w_uv, mla_w_o, hg_w_q, hg_w_f, hg_w_i, hg_w_g, hg_o_norm, hg_w_o, hg_lb_logits, swa_w_q, swa_w_k, swa_w_v, swa_w_o, swa_sink):
    batch, seq, d = x_prompt.shape
    dec_batch, dec_seq, _ = x_sample.shape
    past = cache_mla_ckv.shape[2]
    depth = ada_w.shape[0]
    n_prompt = batch * seq
    n_rows = n_prompt + dec_batch * dec_seq
    assert dec_batch + 1 <= COND_ROWS and seq % SWA_WINDOW == 0 and dec_seq % (2 * SWA_WINDOW) == 0
    assert n_prompt % dec_seq == 0

    pre_rows = _Rows(n_prompt, dec_seq, n_rows, _pick_tile(n_prompt, dec_seq, 256))
    post_rows = _Rows(n_prompt, dec_seq, n_rows, _pick_tile(n_prompt, dec_seq, 512))
    tq = _pick_tile(n_prompt, dec_seq, 256)
    cos64, sin64 = _rope_tables(dec_seq, pre_rows.tm)
    n_tab = cos64.shape[0]
    cos_mla = jnp.concatenate([jnp.ones((n_tab, LANES), F32), cos64, jnp.ones((n_tab, ROPE_PERIOD), F32)], axis=1)
    sin_mla = jnp.concatenate([jnp.zeros((n_tab, LANES), F32), sin64, jnp.zeros((n_tab, ROPE_PERIOD), F32)], axis=1)
    cos_swa, sin_swa = jnp.tile(cos64, (1, 2)), jnp.tile(sin64, (1, 2))

    cond = jnp.concatenate([c_ctx[None, :], c, jnp.zeros((COND_ROWS - 1 - dec_batch, d), F32)], axis=0)
    mods = _adaln(cond, ada_w, ada_b).reshape(depth, COND_ROWS, 6, d)

    x = jnp.concatenate([x_prompt.reshape(n_prompt, d), x_sample.reshape(dec_batch * dec_seq, d)], axis=0)
    row1 = lambda a: a.reshape(1, -1)
    new_ckv, new_krope, new_hg, new_k, new_v = [], [], [], [], []
    for i in range(depth):
        kind, j = i % N_MIXERS, i // N_MIXERS
        gn = row1(norm_mix[i])
        if kind == 0:
            uq = mla_w_uq[j].reshape(-1, MLA_HEADS, MLA_NOPE_DIM + MLA_ROPE_DIM)
            uq = jnp.pad(uq, ((0, 0), (0, 0), (0, MLA_QK_PAD - uq.shape[2]))).reshape(uq.shape[0], -1)
            w = {
                "dq": mla_w_dq[j].astype(BF16), "q_norm": row1(mla_q_norm[j]), "uq": uq.astype(BF16),
                "dkv": jnp.pad(mla_w_dkv[j], ((0, 0), (0, LANES - MLA_ROPE_DIM))).astype(BF16),
                "kv_norm": row1(mla_kv_norm[j]),
                "uk": mla_w_uk[j].astype(BF16), "uv": mla_w_uv[j].astype(BF16),
            }
            q, ckv, kr_raw, kcat, vt = _mla_pre(x, mods[i], gn, cos_mla, sin_mla, w, pre_rows)
            kr_ctx = jnp.pad(cache_mla_krope[:, j].reshape(dec_batch * past, -1), ((0, 0), (0, LANES - MLA_ROPE_DIM)))
            kcat_ctx, vt_ctx = _mla_expand(cache_mla_ckv[:, j].reshape(dec_batch * past, -1), kr_ctx, w["uk"], w["uv"])
            mix = (_mla_attn_prompt(q, kcat, vt, batch, seq),
                   _mla_attn_latent(q, kcat, vt, kcat_ctx, vt_ctx, n_prompt, dec_batch, dec_seq, past, tq))
            wo = mla_w_o[j].astype(BF16)
            new_ckv.append(ckv[:n_prompt].reshape(batch, seq, -1))
            new_krope.append(kr_raw[:n_prompt].reshape(batch, seq, -1))
        elif kind == 1:
            w5 = jnp.concatenate([hg_w_q[j], hg_w_i[j], hg_w_g[j], hg_w_f[j, 0], hg_w_f[j, 1]], axis=1).astype(BF16)
            q, v, gate, f2 = _hgrn_pre(x, mods[i], gn, w5, hg_lb_logits, i, pre_rows)
            rt = _pick_tile(seq, dec_seq, 256)
            od_p, s_prompt = _hgrn_scan(q, v, f2, batch, seq, 0, rt)
            od_l, _ = _hgrn_scan(q, v, f2, dec_batch, dec_seq, n_prompt, rt, s0=state_hgrn[:, j])
            mix = (od_p, od_l, gate, row1(hg_o_norm[j]))
            wo = hg_w_o[j].astype(BF16)
            new_hg.append(s_prompt)
        else:
            wqkv = jnp.concatenate([swa_w_q[j], swa_w_k[j], swa_w_v[j]], axis=1).astype(BF16)
            q, kx, vt, k_raw, v_raw = _swa_pre(x, mods[i], gn, cos_swa, sin_swa, wqkv, pre_rows)
            sink = swa_sink[j]
            k_ctx = cache_swa_k[:, j].reshape(dec_batch * past, SWA_KV_HEADS, 1, SWA_HEAD_DIM).transpose(1, 0, 2, 3)
            kx_ctx = jnp.broadcast_to(k_ctx, (SWA_KV_HEADS, dec_batch * past, SWA_GROUP, SWA_HEAD_DIM))
            kx_ctx = kx_ctx.reshape(SWA_KV_HEADS, dec_batch * past, SWA_GW).astype(BF16)
            vt_ctx = cache_swa_v[:, j].reshape(dec_batch * past, SWA_KV_HEADS, SWA_HEAD_DIM).transpose(1, 2, 0).astype(BF16)
            mix = (_swa_attn_prompt(q, kx, vt, sink, batch, seq),
                   _swa_attn_latent(q, kx, vt, kx_ctx, vt_ctx, sink, n_prompt, dec_batch, dec_seq, past, tq))
            wo = swa_w_o[j].astype(BF16)
            new_k.append(k_raw[:n_prompt].reshape(batch, seq, SWA_KV_HEADS, SWA_HEAD_DIM))
            new_v.append(v_raw[:n_prompt].reshape(batch, seq, SWA_KV_HEADS, SWA_HEAD_DIM))
        x = _post(mix, x, mods[i], wo, row1(norm_ffn[i]), ffn_w_gate[i].astype(BF16), ffn_w_up[i].astype(BF16),
                  ffn_w_down[i].astype(BF16), post_rows,
                  final_norm=row1(final_norm) if i == depth - 1 else None, hgrn=kind == 1)
    y_prompt = x[:n_prompt].reshape(batch, seq, d)
    y_sample = x[n_prompt:].reshape(dec_batch, dec_seq, d)
    return (y_prompt, y_sample, jnp.stack(new_ckv, axis=1), jnp.stack(new_krope, axis=1),
            jnp.stack(new_hg, axis=1), jnp.stack(new_k, axis=1), jnp.stack(new_v, axis=1))
```

```python
import functools

import jax
import jax.numpy as jnp
from jax import lax
from jax.experimental import pallas as pl
from jax.experimental.pallas import tpu as pltpu

F32 = jnp.float32
BF16 = jnp.bfloat16

GRID_W = 64
N_MIXERS = 3

MLA_HEADS = 8
MLA_KV_LORA = 256
MLA_NOPE_DIM = 128
MLA_ROPE_DIM = 64
MLA_V_DIM = 128
MLA_QK_PAD = 256
MLA_SCALE = (MLA_NOPE_DIM + MLA_ROPE_DIM) ** -0.5

HG_HEADS = 8
HG_DK = 128
HG_DV = 128
HG_DIAG_BLOCK = 32

SWA_HEADS = 16
SWA_KV_HEADS = 4
SWA_GROUP = SWA_HEADS // SWA_KV_HEADS
SWA_HEAD_DIM = 64
SWA_GW = SWA_GROUP * SWA_HEAD_DIM
SWA_WINDOW = 128
SWA_SCALE = SWA_HEAD_DIM ** -0.5

ROPE_BASE = 10000.0
ROPE_PERIOD = 64
ROPE_QUARTER = 16
NORM_EPS = 1e-6
NEG_INF = -1e30
LOG2_E = 1.4426950408889634

LANES = 128
COND_ROWS = 8
VMEM_LIMIT = 56 * 1024 * 1024


def _sigmoid(x):
    return jax.nn.sigmoid(x)


def _silu(x):
    return x * jax.nn.sigmoid(x)


def _rms(x, g):
    return x * lax.rsqrt(jnp.mean(x * x, axis=-1, keepdims=True) + NORM_EPS) * g


def _modulate(x, g, shift, scale):
    return _rms(x, g) * (1.0 + scale) + shift


def _dot(a, b):
    return jnp.dot(a, b, preferred_element_type=F32)


def _dot_nt(a, b):
    return lax.dot_general(a, b, (((1,), (1,)), ((), ())), preferred_element_type=F32)


def _dot_tn(a, b):
    return lax.dot_general(a, b, (((0,), (0,)), ((), ())), preferred_element_type=F32)


def _swap_pairs(x):
    n = x.shape[1]
    lane = lax.broadcasted_iota(jnp.int32, x.shape, 1)
    ahead = pltpu.roll(x, n - ROPE_QUARTER, 1)
    behind = pltpu.roll(x, ROPE_QUARTER, 1)
    return jnp.where((lane & (2 * ROPE_QUARTER - 1)) < ROPE_QUARTER, ahead, behind)


def _rope(x, cos, sin):
    reps = x.shape[1] // cos.shape[1]
    if reps > 1:
        cos = jnp.concatenate([cos] * reps, axis=1)
        sin = jnp.concatenate([sin] * reps, axis=1)
    return x * cos + _swap_pairs(x) * sin


def _whole(shape):
    zeros = (0,) * len(shape)
    return pl.BlockSpec(shape, lambda *_: zeros, pipeline_mode=pl.Buffered(1))


def _params(*sem):
    return pltpu.CompilerParams(dimension_semantics=sem, vmem_limit_bytes=VMEM_LIMIT)


class _Rows:
    def __init__(self, n_prompt_rows, dec_seq, n_rows, tm):
        assert n_prompt_rows % tm == 0 and dec_seq % tm == 0
        self.tm = tm
        self.n_tiles = n_rows // tm
        self.prompt_tiles = n_prompt_rows // tm
        self.seq_tiles = dec_seq // tm

    def cond(self, i):
        return jnp.where(i < self.prompt_tiles, 0, 1 + jnp.maximum(i - self.prompt_tiles, 0) // self.seq_tiles)

    def rope_block(self, i):
        return jnp.where(i < self.prompt_tiles, 0, 1 + jnp.maximum(i - self.prompt_tiles, 0) % self.seq_tiles)

    def row_spec(self, width):
        return pl.BlockSpec((self.tm, width), lambda i: (i, 0))

    def col_spec(self, height):
        return pl.BlockSpec((height, self.tm), lambda i: (0, i))

    def mod_spec(self, d):
        return pl.BlockSpec((1, 6, d), lambda i: (self.cond(i), 0, 0))

    def rope_spec(self, width):
        return pl.BlockSpec((self.tm, width), lambda i: (self.rope_block(i), 0))

    def x_specs(self, width):
        return [self.prompt_spec(width), self.latent_spec(width)]

    def is_prompt(self):
        return pl.program_id(0) < self.prompt_tiles

    def prompt_spec(self, width):
        return pl.BlockSpec((self.tm, width), lambda i: (jnp.minimum(i, self.prompt_tiles - 1), 0))

    def latent_spec(self, width):
        return pl.BlockSpec((self.tm, width), lambda i: (jnp.maximum(i - self.prompt_tiles, 0), 0))


def _rope_tables(dec_seq, tm):
    pos = jnp.arange(dec_seq)
    row = (pos // GRID_W).astype(F32)
    col = (pos % GRID_W).astype(F32)
    inv_freq = ROPE_BASE ** (-jnp.arange(ROPE_QUARTER, dtype=F32) / ROPE_QUARTER)
    ang_r = row[:, None] * inv_freq[None, :]
    ang_c = col[:, None] * inv_freq[None, :]
    cos = jnp.concatenate([jnp.cos(ang_r), jnp.cos(ang_r), jnp.cos(ang_c), jnp.cos(ang_c)], axis=1)
    sin = jnp.concatenate([-jnp.sin(ang_r), jnp.sin(ang_r), -jnp.sin(ang_c), jnp.sin(ang_c)], axis=1)
    cos = jnp.concatenate([jnp.ones((tm, ROPE_PERIOD), F32), cos], axis=0)
    sin = jnp.concatenate([jnp.zeros((tm, ROPE_PERIOD), F32), sin], axis=0)
    return cos, sin


def _adaln_kernel(c_ref, w_ref, b_ref, o_ref):
    o_ref[0] = _dot(_silu(c_ref[...]), w_ref[0]) + b_ref[0]


def _adaln(cond, ada_w, ada_b, tn=1536):
    depth, d, n = ada_w.shape
    return pl.pallas_call(
        _adaln_kernel,
        grid=(depth, n // tn),
        in_specs=[pl.BlockSpec((COND_ROWS, d), lambda l, j: (0, 0)),
                  pl.BlockSpec((1, d, tn), lambda l, j: (l, 0, j)),
                  pl.BlockSpec((1, 1, tn), lambda l, j: (l, 0, j))],
        out_specs=pl.BlockSpec((1, COND_ROWS, tn), lambda l, j: (l, 0, j)),
        out_shape=jax.ShapeDtypeStruct((depth, COND_ROWS, n), F32),
        compiler_params=_params("arbitrary", "arbitrary"),
        name="adaln",
    )(cond, ada_w, ada_b.reshape(depth, 1, n))


def _mla_store_heads(q_ref, kcat_ref, vt_ref, qn, qr_pad, kn, kr_pad, v):
    vt = v.T.astype(BF16)
    for h in range(MLA_HEADS):
        nope = slice(h * MLA_NOPE_DIM, (h + 1) * MLA_NOPE_DIM)
        if q_ref is not None:
            q_ref[h] = jnp.concatenate([qn[:, nope], qr_pad[:, h * LANES:(h + 1) * LANES]], axis=1)
        kcat_ref[h] = jnp.concatenate([kn[:, nope], kr_pad], axis=1)
        vt_ref[h] = vt[h * MLA_V_DIM:(h + 1) * MLA_V_DIM, :]


def _read_rows(is_prompt, p_ref, l_ref):
    return jnp.where(is_prompt, p_ref[...], l_ref[...])


def _mla_pre_kernel(xp_ref, xl_ref, mod_ref, gn_ref, cos_ref, sin_ref, wdq_ref, qnorm_ref, wuq_ref,
                    wdkv_ref, kvnorm_ref, wuk_ref, wuv_ref,
                    q_ref, ckv_ref, krraw_ref, kcat_ref, vt_ref, *, prompt_tiles):
    is_prompt = pl.program_id(0) < prompt_tiles
    m = mod_ref[0]
    h = _modulate(_read_rows(is_prompt, xp_ref, xl_ref), gn_ref[...], m[0:1], m[1:2]).astype(BF16)
    q_lat = _rms(_dot(h, wdq_ref[...]), qnorm_ref[...]).astype(BF16)
    cos, sin = cos_ref[...], sin_ref[...]
    nn = MLA_HEADS * MLA_NOPE_DIM
    qn = _dot(q_lat, wuq_ref[:, :nn]).astype(BF16)
    qr_pad = _rope(_dot(q_lat, wuq_ref[:, nn:]), cos, sin).astype(BF16)
    kv = _dot(h, wdkv_ref[...])
    ckv = _rms(kv[:, :MLA_KV_LORA], kvnorm_ref[...])
    kr = kv[:, MLA_KV_LORA:]

    @pl.when(is_prompt)
    def _():
        ckv_ref[...] = ckv
        krraw_ref[...] = kr[:, :MLA_ROPE_DIM]

    kr_pad = _rope(kr, cos, sin).astype(BF16)
    cb = ckv.astype(BF16)
    _mla_store_heads(q_ref, kcat_ref, vt_ref, qn, qr_pad, _dot(cb, wuk_ref[...]).astype(BF16), kr_pad,
                     _dot(cb, wuv_ref[...]))


def _mla_pre(x, mods, gn, cos, sin, w, rows):
    n_prompt, d = x[0].shape
    t = n_prompt + x[1].shape[0]
    tm = rows.tm
    ins = [*x, mods, gn, cos, sin, w["dq"], w["q_norm"], w["uq"], w["dkv"], w["kv_norm"], w["uk"], w["uv"]]
    in_specs = rows.x_specs(d) + [rows.mod_spec(d), _whole(gn.shape),
                                  rows.rope_spec(cos.shape[1]), rows.rope_spec(sin.shape[1])]
    in_specs += [_whole(a.shape) for a in ins[6:]]
    heads_rows = pl.BlockSpec((MLA_HEADS, tm, MLA_QK_PAD), lambda i: (0, i, 0))
    return pl.pallas_call(
        functools.partial(_mla_pre_kernel, prompt_tiles=rows.prompt_tiles),
        grid=(rows.n_tiles,),
        in_specs=in_specs,
        out_specs=[heads_rows, rows.prompt_spec(MLA_KV_LORA), rows.prompt_spec(MLA_ROPE_DIM), heads_rows,
                   pl.BlockSpec((MLA_HEADS, MLA_V_DIM, tm), lambda i: (0, 0, i))],
        out_shape=[jax.ShapeDtypeStruct((MLA_HEADS, t, MLA_QK_PAD), BF16),
                   jax.ShapeDtypeStruct((n_prompt, MLA_KV_LORA), F32),
                   jax.ShapeDtypeStruct((n_prompt, MLA_ROPE_DIM), F32),
                   jax.ShapeDtypeStruct((MLA_HEADS, t, MLA_QK_PAD), BF16),
                   jax.ShapeDtypeStruct((MLA_HEADS, MLA_V_DIM, t), BF16)],
        compiler_params=_params("arbitrary"),
        name="mla_pre",
    )(*ins)


def _mla_expand_kernel(c_ref, kr_ref, wuk_ref, wuv_ref, kcat_ref, vt_ref):
    cb = c_ref[...].astype(BF16)
    _mla_store_heads(None, kcat_ref, vt_ref, None, None, _dot(cb, wuk_ref[...]).astype(BF16),
                     kr_ref[...].astype(BF16), _dot(cb, wuv_ref[...]))


def _mla_expand(ckv, kr_pad, wuk, wuv):
    n = ckv.shape[0]
    return pl.pallas_call(
        _mla_expand_kernel,
        out_shape=[jax.ShapeDtypeStruct((MLA_HEADS, n, MLA_QK_PAD), BF16),
                   jax.ShapeDtypeStruct((MLA_HEADS, MLA_V_DIM, n), BF16)],
        compiler_params=pltpu.CompilerParams(vmem_limit_bytes=VMEM_LIMIT),
        name="mla_expand",
    )(ckv, kr_pad, wuk, wuv)


def _mla_attn_kernel(*refs, n_src):
    q_ref = refs[0]
    srcs = [(refs[1 + 2 * i], refs[2 + 2 * i]) for i in range(n_src)]
    o_ref, s_even, s_odd, o_buf = refs[1 + 2 * n_src:]
    n_keys = [k_ref.shape[1] for k_ref, _ in srcs]
    starts = [sum(n_keys[:i]) for i in range(n_src)]

    def put_scores(h, s_buf):
        q = q_ref[h]
        for (k_ref, _), first, n in zip(srcs, starts, n_keys):
            s_buf[first:first + n, :] = _dot_nt(k_ref[h], q)

    def finish(h, s_buf):
        s = s_buf[...]
        mx = jnp.max(s, axis=0, keepdims=True)
        p = jnp.exp2((s - mx) * (MLA_SCALE * LOG2_E))
        den = jnp.sum(p, axis=0, keepdims=True)
        p = p.astype(BF16)
        acc = None
        for (_, vt_ref), first, n in zip(srcs, starts, n_keys):
            a = _dot(vt_ref[h], p[first:first + n, :])
            acc = a if acc is None else acc + a
        o_buf[h] = (acc / den).T.astype(BF16)

    put_scores(0, s_even)

    def pair(j, carry):
        h = 2 * j
        put_scores(h + 1, s_odd)
        finish(h, s_even)
        put_scores(h + 2, s_even)
        finish(h + 1, s_odd)
        return carry

    lax.fori_loop(0, MLA_HEADS // 2 - 1, pair, 0)
    put_scores(MLA_HEADS - 1, s_odd)
    finish(MLA_HEADS - 2, s_even)
    finish(MLA_HEADS - 1, s_odd)
    for h in range(MLA_HEADS):
        o_ref[:, h * MLA_V_DIM:(h + 1) * MLA_V_DIM] = o_buf[h]


def _mla_attn_scratch(n_keys, tq):
    return [pltpu.VMEM((n_keys, tq), F32), pltpu.VMEM((n_keys, tq), F32), pltpu.VMEM((MLA_HEADS, tq, MLA_V_DIM), BF16)]


def _staged_attention(scores, values_t, scale, sinks=None):
    c = scale * LOG2_E
    mx = [jnp.max(s, axis=0, keepdims=True) for s in scores]
    if sinks is not None:
        mx = [jnp.maximum(m, z) for m, z in zip(mx, sinks)]
    p = [jnp.exp2((s - m) * c) for s, m in zip(scores, mx)]
    den = [jnp.sum(x, axis=0, keepdims=True) for x in p]
    if sinks is not None:
        den = [d + jnp.exp2((z - m) * c) for d, z, m in zip(den, sinks, mx)]
    acc = [_dot(v, x.astype(BF16)) for v, x in zip(values_t, p)]
    return [a / d for a, d in zip(acc, den)]


def _mla_attn_prompt_kernel(q_ref, k_ref, vt_ref, o_ref):
    scores = [_dot_nt(k_ref[h], q_ref[h]) for h in range(MLA_HEADS)]
    outs = _staged_attention(scores, [vt_ref[h] for h in range(MLA_HEADS)], MLA_SCALE)
    for h in range(MLA_HEADS):
        o_ref[:, h * MLA_V_DIM:(h + 1) * MLA_V_DIM] = outs[h].T.astype(BF16)


def _mla_attn_prompt(q, kcat, vt, batch, seq):
    hv = MLA_HEADS * MLA_V_DIM
    rows = pl.BlockSpec((MLA_HEADS, seq, MLA_QK_PAD), lambda b: (0, b, 0))
    return pl.pallas_call(
        _mla_attn_prompt_kernel,
        grid=(batch,),
        in_specs=[rows, rows, pl.BlockSpec((MLA_HEADS, MLA_V_DIM, seq), lambda b: (0, 0, b))],
        out_specs=pl.BlockSpec((seq, hv), lambda b: (b, 0)),
        out_shape=jax.ShapeDtypeStruct((batch * seq, hv), BF16),
        compiler_params=_params("arbitrary"),
        name="mla_attn_prompt",
    )(q, kcat, vt)


def _mla_attn_latent(q, kcat, vt, kcat_ctx, vt_ctx, n_prompt, dec_batch, dec_seq, past, tq):
    hv = MLA_HEADS * MLA_V_DIM
    nq = dec_seq // tq
    lat_blk = n_prompt // dec_seq
    return pl.pallas_call(
        functools.partial(_mla_attn_kernel, n_src=2),
        grid=(dec_batch, nq),
        in_specs=[pl.BlockSpec((MLA_HEADS, tq, MLA_QK_PAD), lambda b, i: (0, n_prompt // tq + b * nq + i, 0)),
                  pl.BlockSpec((MLA_HEADS, past, MLA_QK_PAD), lambda b, i: (0, b, 0)),
                  pl.BlockSpec((MLA_HEADS, MLA_V_DIM, past), lambda b, i: (0, 0, b)),
                  pl.BlockSpec((MLA_HEADS, dec_seq, MLA_QK_PAD), lambda b, i: (0, lat_blk + b, 0)),
                  pl.BlockSpec((MLA_HEADS, MLA_V_DIM, dec_seq), lambda b, i: (0, 0, lat_blk + b))],
        out_specs=pl.BlockSpec((tq, hv), lambda b, i: (b * nq + i, 0)),
        out_shape=jax.ShapeDtypeStruct((dec_batch * dec_seq, hv), BF16),
        scratch_shapes=_mla_attn_scratch(past + dec_seq, tq),
        compiler_params=_params("arbitrary", "arbitrary"),
        name="mla_attn_latent",
    )(q, kcat_ctx, vt_ctx, kcat, vt)


def _tile_heads(x):
    n = x.shape[1]
    block = lax.broadcasted_iota(jnp.int32, x.shape, 1) // SWA_HEAD_DIM
    rolled = [x] + [pltpu.roll(x, s * SWA_HEAD_DIM, 1) for s in range(1, SWA_KV_HEADS)]
    out = []
    for kvh in range(SWA_KV_HEADS):
        blk = rolled[(0 - kvh) % SWA_KV_HEADS]
        for g in range(1, n // SWA_HEAD_DIM):
            blk = jnp.where(block == g, rolled[(g - kvh) % SWA_KV_HEADS], blk)
        out.append(blk)
    return jnp.concatenate(out, axis=1)


def _swa_pre_kernel(xp_ref, xl_ref, mod_ref, gn_ref, cos_ref, sin_ref, wqkv_ref,
                    q_ref, kx_ref, vt_ref, kraw_ref, vraw_ref, *, prompt_tiles):
    is_prompt = pl.program_id(0) < prompt_tiles
    m = mod_ref[0]
    h = _modulate(_read_rows(is_prompt, xp_ref, xl_ref), gn_ref[...], m[0:1], m[1:2]).astype(BF16)
    qkv = _dot(h, wqkv_ref[...])
    nq, nk = SWA_HEADS * SWA_HEAD_DIM, SWA_KV_HEADS * SWA_HEAD_DIM
    cos, sin = cos_ref[...], sin_ref[...]
    q = _rope(qkv[:, :nq], cos, sin).astype(BF16)
    k = qkv[:, nq:nq + nk]
    v = qkv[:, nq + nk:]

    @pl.when(is_prompt)
    def _():
        kraw_ref[...] = k
        vraw_ref[...] = v

    kx = _tile_heads(_rope(k, cos, sin)).astype(BF16)
    vt = v.T.astype(BF16)
    for kvh in range(SWA_KV_HEADS):
        q_ref[kvh] = q[:, kvh * SWA_GW:(kvh + 1) * SWA_GW]
        kx_ref[kvh] = kx[:, kvh * SWA_GW:(kvh + 1) * SWA_GW]
        vt_ref[kvh] = vt[kvh * SWA_HEAD_DIM:(kvh + 1) * SWA_HEAD_DIM, :]


def _swa_pre(x, mods, gn, cos, sin, wqkv, rows):
    n_prompt, d = x[0].shape
    t = n_prompt + x[1].shape[0]
    tm = rows.tm
    nk = SWA_KV_HEADS * SWA_HEAD_DIM
    heads_rows = pl.BlockSpec((SWA_KV_HEADS, tm, SWA_GW), lambda i: (0, i, 0))
    return pl.pallas_call(
        functools.partial(_swa_pre_kernel, prompt_tiles=rows.prompt_tiles),
        grid=(rows.n_tiles,),
        in_specs=rows.x_specs(d) + [rows.mod_spec(d), _whole(gn.shape),
                                    rows.rope_spec(cos.shape[1]), rows.rope_spec(sin.shape[1]), _whole(wqkv.shape)],
        out_specs=[heads_rows, heads_rows, pl.BlockSpec((SWA_KV_HEADS, SWA_HEAD_DIM, tm), lambda i: (0, 0, i)),
                   rows.prompt_spec(nk), rows.prompt_spec(nk)],
        out_shape=[jax.ShapeDtypeStruct((SWA_KV_HEADS, t, SWA_GW), BF16),
                   jax.ShapeDtypeStruct((SWA_KV_HEADS, t, SWA_GW), BF16),
                   jax.ShapeDtypeStruct((SWA_KV_HEADS, SWA_HEAD_DIM, t), BF16),
                   jax.ShapeDtypeStruct((n_prompt, nk), F32), jax.ShapeDtypeStruct((n_prompt, nk), F32)],
        compiler_params=_params("arbitrary"),
        name="swa_pre",
    )(*x, mods, gn, cos, sin, wqkv)


def _swa_heads(q_ref, sink_ref, o_ref, srcs, bias_ref, s_even, s_odd, ot_buf):
    tq = q_ref.shape[1]
    n_keys = [kx_ref.shape[1] for kx_ref, _ in srcs]
    starts = [sum(n_keys[:i]) for i in range(len(srcs))]
    group = lax.broadcasted_iota(jnp.int32, (tq, SWA_GW), 1) // SWA_HEAD_DIM

    def put_scores(hq, s_buf):
        kvh, g = hq // SWA_GROUP, hq % SWA_GROUP
        q = jnp.where(group == g, q_ref[kvh].astype(F32), 0.0).astype(BF16)
        for (kx_ref, _), first, n in zip(srcs, starts, n_keys):
            s_buf[first:first + n, :] = _dot_nt(kx_ref[kvh], q)

    def finish(hq, s_buf):
        kvh, g = hq // SWA_GROUP, hq % SWA_GROUP
        s = s_buf[...]
        if bias_ref is not None:
            s = s + bias_ref[...]
        sink = jnp.full((1, 1), sink_ref[hq] * (1.0 / SWA_SCALE), F32)
        mx = jnp.maximum(jnp.max(s, axis=0, keepdims=True), sink)
        p = jnp.exp2((s - mx) * (SWA_SCALE * LOG2_E))
        den = jnp.sum(p, axis=0, keepdims=True) + jnp.exp2((sink - mx) * (SWA_SCALE * LOG2_E))
        p = p.astype(BF16)
        acc = None
        for (_, vt_ref), first, n in zip(srcs, starts, n_keys):
            a = _dot(vt_ref[kvh], p[first:first + n, :])
            acc = a if acc is None else acc + a
        ot_buf[kvh, pl.ds(pl.multiple_of(g * SWA_HEAD_DIM, SWA_HEAD_DIM), SWA_HEAD_DIM), :] = acc / den

    put_scores(0, s_even)

    def pair(j, carry):
        hq = 2 * j
        put_scores(hq + 1, s_odd)
        finish(hq, s_even)
        put_scores(hq + 2, s_even)
        finish(hq + 1, s_odd)
        return carry

    lax.fori_loop(0, SWA_HEADS // 2 - 1, pair, 0)
    put_scores(SWA_HEADS - 1, s_odd)
    finish(SWA_HEADS - 2, s_even)
    finish(SWA_HEADS - 1, s_odd)
    for kvh in range(SWA_KV_HEADS):
        o_ref[:, kvh * SWA_GW:(kvh + 1) * SWA_GW] = ot_buf[kvh].T.astype(BF16)


def _swa_attn_scratch(n_keys, tq):
    return [pltpu.VMEM((n_keys, tq), F32), pltpu.VMEM((n_keys, tq), F32), pltpu.VMEM((SWA_KV_HEADS, SWA_GW, tq), F32)]


def _swa_attn_prompt_kernel(q_ref, kx_ref, vt_ref, sink_ref, o_ref):
    group = lax.broadcasted_iota(jnp.int32, (q_ref.shape[1], SWA_GW), 1) // SWA_HEAD_DIM
    scores, values_t, sinks = [], [], []
    for kvh in range(SWA_KV_HEADS):
        q_all = q_ref[kvh].astype(F32)
        for g in range(SWA_GROUP):
            q = jnp.where(group == g, q_all, 0.0).astype(BF16)
            scores.append(_dot_nt(kx_ref[kvh], q))
            values_t.append(vt_ref[kvh])
            sinks.append(jnp.full((1, 1), sink_ref[kvh * SWA_GROUP + g] * (1.0 / SWA_SCALE), F32))
    outs = _staged_attention(scores, values_t, SWA_SCALE, sinks)
    for kvh in range(SWA_KV_HEADS):
        ot = jnp.concatenate(outs[kvh * SWA_GROUP:(kvh + 1) * SWA_GROUP], axis=0)
        o_ref[:, kvh * SWA_GW:(kvh + 1) * SWA_GW] = ot.T.astype(BF16)


def _swa_attn_prompt(q, kx, vt, sink, batch, seq):
    wq = SWA_HEADS * SWA_HEAD_DIM
    rows = pl.BlockSpec((SWA_KV_HEADS, seq, SWA_GW), lambda b: (0, b, 0))
    return pl.pallas_call(
        _swa_attn_prompt_kernel,
        grid=(batch,),
        in_specs=[rows, rows, pl.BlockSpec((SWA_KV_HEADS, SWA_HEAD_DIM, seq), lambda b: (0, 0, b)),
                  pl.BlockSpec(memory_space=pltpu.SMEM)],
        out_specs=pl.BlockSpec((seq, wq), lambda b: (b, 0)),
        out_shape=jax.ShapeDtypeStruct((batch * seq, wq), BF16),
        compiler_params=_params("arbitrary"),
        name="swa_attn_prompt",
    )(q, kx, vt, sink)


def _swa_attn_latent_kernel(q_ref, kc_ref, vc_ref, kp_ref, vp_ref, km_ref, vm_ref, kn_ref, vn_ref,
                            sink_ref, o_ref, s_even, s_odd, ot_buf, bias_ref, *, tq, dec_seq, past):
    i = pl.program_id(1)
    qpos = i * tq + lax.broadcasted_iota(jnp.int32, (1, tq), 1)

    def band(first, n):
        kpos = first + lax.broadcasted_iota(jnp.int32, (n, 1), 0)
        valid = (jnp.abs(qpos - kpos) <= SWA_WINDOW) & (kpos >= 0) & (kpos < dec_seq)
        return jnp.where(valid, 0.0, NEG_INF)

    bias_ref[...] = jnp.concatenate(
        [jnp.zeros((past, tq), F32), band(i * tq - SWA_WINDOW, SWA_WINDOW), band(i * tq, tq),
         band((i + 1) * tq, SWA_WINDOW)], axis=0)
    srcs = [(kc_ref, vc_ref), (kp_ref, vp_ref), (km_ref, vm_ref), (kn_ref, vn_ref)]
    _swa_heads(q_ref, sink_ref, o_ref, srcs, bias_ref, s_even, s_odd, ot_buf)


def _swa_attn_latent(q, kx, vt, kx_ctx, vt_ctx, sink, n_prompt, dec_batch, dec_seq, past, tq):
    t = q.shape[1]
    wq = SWA_HEADS * SWA_HEAD_DIM
    nq = dec_seq // tq
    w = SWA_WINDOW
    n_keys = past + tq + 2 * w
    first = lambda b, i: n_prompt + b * dec_seq + i * tq
    prev = lambda b, i: first(b, i) // w - 1
    nxt = lambda b, i: jnp.minimum((first(b, i) + tq) // w, t // w - 1)
    rows = lambda n, blk: pl.BlockSpec((SWA_KV_HEADS, n, SWA_GW), lambda b, i: (0, blk(b, i), 0))
    cols = lambda n, blk: pl.BlockSpec((SWA_KV_HEADS, SWA_HEAD_DIM, n), lambda b, i: (0, 0, blk(b, i)))
    main = lambda b, i: first(b, i) // tq
    ctx = lambda b, i: b
    return pl.pallas_call(
        functools.partial(_swa_attn_latent_kernel, tq=tq, dec_seq=dec_seq, past=past),
        grid=(dec_batch, nq),
        in_specs=[rows(tq, main), rows(past, ctx), cols(past, ctx), rows(w, prev), cols(w, prev),
                  rows(tq, main), cols(tq, main), rows(w, nxt), cols(w, nxt),
                  pl.BlockSpec(memory_space=pltpu.SMEM)],
        out_specs=pl.BlockSpec((tq, wq), lambda b, i: (b * nq + i, 0)),
        out_shape=jax.ShapeDtypeStruct((dec_batch * dec_seq, wq), BF16),
        scratch_shapes=_swa_attn_scratch(n_keys, tq) + [pltpu.VMEM((n_keys, tq), F32)],
        compiler_params=_params("arbitrary", "arbitrary"),
        name="swa_attn_latent",
    )(q, kx_ctx, vt_ctx, kx, vt, kx, vt, kx, vt, sink)


def _hgrn_pre_kernel(xp_ref, xl_ref, mod_ref, gn_ref, w_ref, lbl_ref, q_ref, v_ref, g_ref, f_ref, *,
                     layer, prompt_tiles):
    is_prompt = pl.program_id(0) < prompt_tiles
    m = mod_ref[0]
    h = _modulate(_read_rows(is_prompt, xp_ref, xl_ref), gn_ref[...], m[0:1], m[1:2]).astype(BF16)
    y = _dot(h, w_ref[...])
    n = HG_HEADS * HG_DK
    q_ref[...] = _silu(y[:, :n])
    v_ref[...] = y[:, n:2 * n]
    g_ref[...] = _silu(y[:, 2 * n:3 * n])
    for d in range(2):
        logits = lbl_ref[d]
        e = jnp.exp(logits - jnp.max(logits, axis=0, keepdims=True))
        s = e / jnp.sum(e, axis=0, keepdims=True)
        cs = s[0:1]
        for r in range(1, layer + 1):
            cs = cs + s[r:r + 1]
        lb = cs - s[0:1]
        f_ref[d] = lb + (1.0 - lb) * _sigmoid(y[:, (3 + d) * n:(4 + d) * n])


def _hgrn_pre(x, mods, gn, w5, lb_logits, layer, rows):
    d = x[0].shape[1]
    t = x[0].shape[0] + x[1].shape[0]
    n = HG_HEADS * HG_DK
    return pl.pallas_call(
        functools.partial(_hgrn_pre_kernel, layer=layer, prompt_tiles=rows.prompt_tiles),
        grid=(rows.n_tiles,),
        in_specs=rows.x_specs(d) + [rows.mod_spec(d), _whole(gn.shape), _whole(w5.shape), _whole(lb_logits.shape)],
        out_specs=[rows.row_spec(n), rows.row_spec(n), rows.row_spec(n),
                   pl.BlockSpec((2, rows.tm, n), lambda i: (0, i, 0))],
        out_shape=[jax.ShapeDtypeStruct((t, n), F32)] * 3 + [jax.ShapeDtypeStruct((2, t, n), F32)],
        compiler_params=_params("arbitrary"),
        name="hgrn_pre",
    )(*x, mods, gn, w5, lb_logits)


def _tri_cumsum(tri, x):
    hi = x.astype(BF16)
    r1 = x - hi.astype(F32)
    mid = r1.astype(BF16)
    lo = (r1 - mid.astype(F32)).astype(BF16)
    return _dot(tri, hi) + _dot(tri, mid) + _dot(tri, lo)


def _hgrn_tile(q_ref, v_ref, f_ref, o_ref, st_ref, *, rows, reverse):
    r = rows
    a = lax.broadcasted_iota(jnp.int32, (r, r), 0)
    b = lax.broadcasted_iota(jnp.int32, (r, r), 1)
    seen = (b >= a) if reverse else (b <= a)
    tri = jnp.where(seen, 1.0, 0.0).astype(BF16)

    q = q_ref[...]
    f = f_ref[0]
    vb = v_ref[...].astype(BF16)
    lf = jnp.log(f)
    cum = _tri_cumsum(tri, lf)
    tot = cum[0:1, :] if reverse else cum[r - 1:r, :]
    kk = 1.0 - f
    q_in = (q * jnp.exp(cum)).astype(BF16)
    k_d = (kk * jnp.exp(tot - cum)).astype(BF16)
    e_tot = jnp.exp(tot)

    levels = []
    c = r // 2
    while c >= HG_DIAG_BLOCK:
        q_half = (a % (2 * c) < c) if reverse else (a % (2 * c) >= c)
        k_half = (b % (2 * c) >= c) if reverse else (b % (2 * c) < c)
        levels.append((2 * c, c if reverse else c - 1, ((a // (2 * c)) == (b // (2 * c))) & q_half & k_half))
        c //= 2
    blk = HG_DIAG_BLOCK
    levels.append((blk, blk // 2, ((a // blk) == (b // blk)) & seen))

    factors = []
    for size, ref_row, own in levels:
        ref = jnp.concatenate(
            [jnp.broadcast_to(cum[j * size + ref_row:j * size + ref_row + 1, :], (size, cum.shape[1]))
             for j in range(r // size)], axis=0)
        factors.append(((q * jnp.exp(cum - ref)).astype(BF16), (kk * jnp.exp(ref - cum)).astype(BF16), own))

    outs = []
    for h in range(HG_HEADS):
        sl = slice(h * HG_DK, (h + 1) * HG_DK)
        att = jnp.zeros((r, r), F32)
        for q_l, k_l, own in factors:
            att = jnp.where(own, _dot_nt(q_l[:, sl], k_l[:, sl]), att)
        st = st_ref[h]
        outs.append(_dot_nt(q_in[:, sl], st.astype(BF16)) + _dot(att.astype(BF16), vb[:, sl]))
        st_ref[h] = st * e_tot[:, sl] + _dot_tn(vb[:, sl], k_d[:, sl])
    o_ref[0] = jnp.concatenate(outs, axis=1)


def _hgrn_scan_kernel(*refs, rows, has_init):
    if has_init:
        q_ref, v_ref, f_ref, s0_ref, o_ref, sfin_ref, st_ref = refs
    else:
        q_ref, v_ref, f_ref, o_ref, sfin_ref, st_ref = refs
    d = pl.program_id(1)
    t = pl.program_id(2)

    @pl.when(t == 0)
    def _():
        for h in range(HG_HEADS):
            st_ref[h] = s0_ref[0, 0, h].T if has_init else jnp.zeros((HG_DV, HG_DK), F32)

    for reverse in (False, True):
        @pl.when(d == int(reverse))
        def _():
            _hgrn_tile(q_ref, v_ref, f_ref, o_ref, st_ref, rows=rows, reverse=reverse)

    @pl.when(t == pl.num_programs(2) - 1)
    def _():
        for h in range(HG_HEADS):
            sfin_ref[0, 0, h] = st_ref[h].T


def _hgrn_scan(q, v, f2, batch, seq, first_row, rt, s0=None):
    n = q.shape[1]
    nt = seq // rt
    has_init = s0 is not None

    def local(b, d, i):
        return b * nt + jnp.where(d == 0, i, nt - 1 - i)

    def slab(b, d, i):
        return first_row // rt + local(b, d, i)

    row = pl.BlockSpec((rt, n), lambda b, d, i: (slab(b, d, i), 0))
    state = pl.BlockSpec((1, 1, HG_HEADS, HG_DK, HG_DV), lambda b, d, i: (b, d, 0, 0, 0))
    ins = [q, v, f2]
    in_specs = [row, row, pl.BlockSpec((1, rt, n), lambda b, d, i: (d, slab(b, d, i), 0))]
    if has_init:
        ins.append(s0)
        in_specs.append(state)
    return pl.pallas_call(
        functools.partial(_hgrn_scan_kernel, rows=rt, has_init=has_init),
        grid=(batch, 2, nt),
        in_specs=in_specs,
        out_specs=[pl.BlockSpec((1, rt, n), lambda b, d, i: (d, local(b, d, i), 0)), state],
        out_shape=[jax.ShapeDtypeStruct((2, batch * seq, n), F32),
                   jax.ShapeDtypeStruct((batch, 2, HG_HEADS, HG_DK, HG_DV), F32)],
        scratch_shapes=[pltpu.VMEM((HG_HEADS, HG_DV, HG_DK), F32)],
        compiler_params=_params("arbitrary", "arbitrary", "arbitrary"),
        name="hgrn_scan_latent" if has_init else "hgrn_scan_prompt",
    )(*ins)


def _post_kernel(*refs, hgrn, final, prompt_tiles):
    refs = list(refs)
    outl_ref = refs.pop()
    outp_ref = refs.pop()
    is_prompt = pl.program_id(0) < prompt_tiles
    if hgrn:
        odp_ref, odl_ref, g_ref, onorm_ref = refs[:4]
        refs = refs[4:]
        o2 = jnp.where(is_prompt, odp_ref[0] + odp_ref[1], odl_ref[0] + odl_ref[1])
        gate = g_ref[...]
        onorm = onorm_ref[...]
        parts = []
        for h in range(HG_HEADS):
            sl = slice(h * HG_DV, (h + 1) * HG_DV)
            parts.append(_rms(o2[:, sl], onorm) * gate[:, sl])
        o = jnp.concatenate(parts, axis=1).astype(BF16)
    else:
        o = jnp.where(is_prompt, refs[0][...], refs[1][...])
        refs = refs[2:]
    xp_ref, xl_ref, mod_ref, wo_ref, gn_ref, wg_ref, wu_ref, wd_ref = refs[:8]
    m = mod_ref[0]
    x1 = _read_rows(is_prompt, xp_ref, xl_ref) + m[2:3] * _dot(o, wo_ref[...])
    h2 = _modulate(x1, gn_ref[...], m[3:4], m[4:5]).astype(BF16)
    a = (_silu(_dot(h2, wg_ref[...])) * _dot(h2, wu_ref[...])).astype(BF16)
    x2 = x1 + m[5:6] * _dot(a, wd_ref[...])
    if final:
        x2 = _rms(x2, refs[8][...])

    @pl.when(is_prompt)
    def _():
        outp_ref[...] = x2

    @pl.when(jnp.logical_not(is_prompt))
    def _():
        outl_ref[...] = x2


def _post(mix, x, mods, wo, gn, wg, wu, wd, rows, final_norm=None, hgrn=False):
    d = x[0].shape[1]
    tm = rows.tm
    if hgrn:
        od_p, od_l, gate, onorm = mix
        head = [od_p, od_l, gate, onorm]
        head_specs = [pl.BlockSpec((2, tm, d), lambda i: (0, jnp.minimum(i, rows.prompt_tiles - 1), 0)),
                      pl.BlockSpec((2, tm, d), lambda i: (0, jnp.maximum(i - rows.prompt_tiles, 0), 0)),
                      rows.row_spec(d), _whole(onorm.shape)]
    else:
        head = list(mix)
        head_specs = [rows.prompt_spec(mix[0].shape[1]), rows.latent_spec(mix[1].shape[1])]
    ins = head + [*x, mods, wo, gn, wg, wu, wd]
    in_specs = head_specs + rows.x_specs(d) + [rows.mod_spec(d)] + [_whole(a.shape) for a in (wo, gn, wg, wu, wd)]
    if final_norm is not None:
        ins.append(final_norm)
        in_specs.append(_whole(final_norm.shape))
    return pl.pallas_call(
        functools.partial(_post_kernel, hgrn=hgrn, final=final_norm is not None, prompt_tiles=rows.prompt_tiles),
        grid=(rows.n_tiles,),
        in_specs=in_specs,
        out_specs=rows.x_specs(d),
        out_shape=[jax.ShapeDtypeStruct(x[0].shape, F32), jax.ShapeDtypeStruct(x[1].shape, F32)],
        compiler_params=_params("arbitrary"),
        name="post",
    )(*ins)


def _pick_tile(n_prompt_rows, dec_seq, want):
    tm = want
    while n_prompt_rows % tm or dec_seq % tm:
        tm //= 2
    return tm


def kernel(x_prompt, x_sample, cache_mla_ckv, cache_mla_krope, state_hgrn, cache_swa_k, cache_swa_v, c, c_ctx, ada_w, ada_b, norm_mix, norm_ffn, ffn_w_gate, ffn_w_up, ffn_w_down, final_norm, mla_w_dq, mla_q_norm, mla_w_uq, mla_w_dkv, mla_kv_norm, mla_w_uk, mla_w_uv, mla_w_o, hg_w_q, hg_w_f, hg_w_i, hg_w_g, hg_o_norm, hg_w_o, hg_lb_logits, swa_w_q, swa_w_k, swa_w_v, swa_w_o, swa_sink):
    batch, seq, d = x_prompt.shape
    dec_batch, dec_seq, _ = x_sample.shape
    past = cache_mla_ckv.shape[2]
    depth = ada_w.shape[0]
    n_prompt = batch * seq
    n_rows = n_prompt + dec_batch * dec_seq
    assert dec_batch + 1 <= COND_ROWS and seq % SWA_WINDOW == 0 and dec_seq % (2 * SWA_WINDOW) == 0
    assert n_prompt % dec_seq == 0

    pre_rows = _Rows(n_prompt, dec_seq, n_rows, _pick_tile(n_prompt, dec_seq, 256))
    post_rows = _Rows(n_prompt, dec_seq, n_rows, _pick_tile(n_prompt, dec_seq, 512))
    tq = _pick_tile(n_prompt, dec_seq, 256)
    cos64, sin64 = _rope_tables(dec_seq, pre_rows.tm)
    n_tab = cos64.shape[0]
    cos_mla = jnp.concatenate([cos64, jnp.ones((n_tab, ROPE_PERIOD), F32)], axis=1)
    sin_mla = jnp.concatenate([sin64, jnp.zeros((n_tab, ROPE_PERIOD), F32)], axis=1)
    cos_swa, sin_swa = jnp.tile(cos64, (1, 2)), jnp.tile(sin64, (1, 2))

    cond = jnp.concatenate([c_ctx[None, :], c, jnp.zeros((COND_ROWS - 1 - dec_batch, d), F32)], axis=0)
    mods = _adaln(cond, ada_w, ada_b).reshape(depth, COND_ROWS, 6, d)

    x = (x_prompt.reshape(n_prompt, d), x_sample.reshape(dec_batch * dec_seq, d))
    row1 = lambda a: a.reshape(1, -1)
    new_ckv, new_krope, new_hg, new_k, new_v = [], [], [], [], []
    for i in range(depth):
        kind, j = i % N_MIXERS, i // N_MIXERS
        gn = row1(norm_mix[i])
        if kind == 0:
            uq = mla_w_uq[j].reshape(-1, MLA_HEADS, MLA_NOPE_DIM + MLA_ROPE_DIM)
            uq_rope = jnp.pad(uq[:, :, MLA_NOPE_DIM:], ((0, 0), (0, 0), (0, LANES - MLA_ROPE_DIM)))
            uq = jnp.concatenate([uq[:, :, :MLA_NOPE_DIM].reshape(uq.shape[0], -1),
                                  uq_rope.reshape(uq.shape[0], -1)], axis=1)
            w = {
                "dq": mla_w_dq[j].astype(BF16), "q_norm": row1(mla_q_norm[j]), "uq": uq.astype(BF16),
                "dkv": jnp.pad(mla_w_dkv[j], ((0, 0), (0, LANES - MLA_ROPE_DIM))).astype(BF16),
                "kv_norm": row1(mla_kv_norm[j]),
                "uk": mla_w_uk[j].astype(BF16), "uv": mla_w_uv[j].astype(BF16),
            }
            q, ckv, kr_raw, kcat, vt = _mla_pre(x, mods[i], gn, cos_mla, sin_mla, w, pre_rows)
            kr_ctx = jnp.pad(cache_mla_krope[:, j].reshape(dec_batch * past, -1), ((0, 0), (0, LANES - MLA_ROPE_DIM)))
            kcat_ctx, vt_ctx = _mla_expand(cache_mla_ckv[:, j].reshape(dec_batch * past, -1), kr_ctx, w["uk"], w["uv"])
            mix = (_mla_attn_prompt(q, kcat, vt, batch, seq),
                   _mla_attn_latent(q, kcat, vt, kcat_ctx, vt_ctx, n_prompt, dec_batch, dec_seq, past, tq))
            wo = mla_w_o[j].astype(BF16)
            new_ckv.append(ckv.reshape(batch, seq, -1))
            new_krope.append(kr_raw.reshape(batch, seq, -1))
        elif kind == 1:
            w5 = jnp.concatenate([hg_w_q[j], hg_w_i[j], hg_w_g[j], hg_w_f[j, 0], hg_w_f[j, 1]], axis=1).astype(BF16)
            q, v, gate, f2 = _hgrn_pre(x, mods[i], gn, w5, hg_lb_logits, i, pre_rows)
            rt = _pick_tile(seq, dec_seq, 256)
            od_p, s_prompt = _hgrn_scan(q, v, f2, batch, seq, 0, rt)
            od_l, _ = _hgrn_scan(q, v, f2, dec_batch, dec_seq, n_prompt, rt, s0=state_hgrn[:, j])
            mix = (od_p, od_l, gate, row1(hg_o_norm[j]))
            wo = hg_w_o[j].astype(BF16)
            new_hg.append(s_prompt)
        else:
            wqkv = jnp.concatenate([swa_w_q[j], swa_w_k[j], swa_w_v[j]], axis=1).astype(BF16)
            q, kx, vt, k_raw, v_raw = _swa_pre(x, mods[i], gn, cos_swa, sin_swa, wqkv, pre_rows)
            sink = swa_sink[j]
            k_ctx = cache_swa_k[:, j].reshape(dec_batch * past, SWA_KV_HEADS, 1, SWA_HEAD_DIM).transpose(1, 0, 2, 3)
            kx_ctx = jnp.broadcast_to(k_ctx, (SWA_KV_HEADS, dec_batch * past, SWA_GROUP, SWA_HEAD_DIM))
            kx_ctx = kx_ctx.reshape(SWA_KV_HEADS, dec_batch * past, SWA_GW).astype(BF16)
            vt_ctx = cache_swa_v[:, j].reshape(dec_batch * past, SWA_KV_HEADS, SWA_HEAD_DIM).transpose(1, 2, 0).astype(BF16)
            mix = (_swa_attn_prompt(q, kx, vt, sink, batch, seq),
                   _swa_attn_latent(q, kx, vt, kx_ctx, vt_ctx, sink, n_prompt, dec_batch, dec_seq, past, tq))
            wo = swa_w_o[j].astype(BF16)
            new_k.append(k_raw.reshape(batch, seq, SWA_KV_HEADS, SWA_HEAD_DIM))
            new_v.append(v_raw.reshape(batch, seq, SWA_KV_HEADS, SWA_HEAD_DIM))
        x = _post(mix, x, mods[i], wo, row1(norm_ffn[i]), ffn_w_gate[i].astype(BF16), ffn_w_up[i].astype(BF16),
                  ffn_w_down[i].astype(BF16), pre_rows if kind == 1 else post_rows,
                  final_norm=row1(final_norm) if i == depth - 1 else None, hgrn=kind == 1)
    y_prompt = x[0].reshape(batch, seq, d)
    y_sample = x[1].reshape(dec_batch, dec_seq, d)
    return (y_prompt, y_sample, jnp.stack(new_ckv, axis=1), jnp.stack(new_krope, axis=1),
            jnp.stack(new_hg, axis=1), jnp.stack(new_k, axis=1), jnp.stack(new_v, axis=1))
```

```python
import functools

import jax
import jax.numpy as jnp
from jax import lax
from jax.experimental import pallas as pl
from jax.experimental.pallas import tpu as pltpu

F32 = jnp.float32
BF16 = jnp.bfloat16

GRID_W = 64
N_MIXERS = 3

MLA_HEADS = 8
MLA_KV_LORA = 256
MLA_NOPE_DIM = 128
MLA_ROPE_DIM = 64
MLA_V_DIM = 128
MLA_QK_PAD = 256
MLA_SCALE = (MLA_NOPE_DIM + MLA_ROPE_DIM) ** -0.5

HG_HEADS = 8
HG_DK = 128
HG_DV = 128
HG_DIAG_BLOCK = 32
HG_MAX_EXPONENT = 80.0

SWA_HEADS = 16
SWA_KV_HEADS = 4
SWA_GROUP = SWA_HEADS // SWA_KV_HEADS
SWA_HEAD_DIM = 64
SWA_GW = SWA_GROUP * SWA_HEAD_DIM
SWA_WINDOW = 128
SWA_SCALE = SWA_HEAD_DIM ** -0.5

ROPE_BASE = 10000.0
ROPE_PERIOD = 64
ROPE_QUARTER = 16
NORM_EPS = 1e-6
NEG_INF = -1e30
LOG2_E = 1.4426950408889634

LANES = 128
COND_ROWS = 8
VMEM_LIMIT = 56 * 1024 * 1024


def _sigmoid(x):
    return jax.nn.sigmoid(x)


def _silu(x):
    return x * jax.nn.sigmoid(x)


def _rms(x, g):
    return x * lax.rsqrt(jnp.mean(x * x, axis=-1, keepdims=True) + NORM_EPS) * g


def _modulate(x, g, shift, scale):
    return _rms(x, g) * (1.0 + scale) + shift


def _dot(a, b):
    return jnp.dot(a, b, preferred_element_type=F32)


def _dot_nt(a, b):
    return lax.dot_general(a, b, (((1,), (1,)), ((), ())), preferred_element_type=F32)


def _dot_tn(a, b):
    return lax.dot_general(a, b, (((0,), (0,)), ((), ())), preferred_element_type=F32)


def _swap_pairs(x):
    n = x.shape[1]
    lane = lax.broadcasted_iota(jnp.int32, x.shape, 1)
    ahead = pltpu.roll(x, n - ROPE_QUARTER, 1)
    behind = pltpu.roll(x, ROPE_QUARTER, 1)
    return jnp.where((lane & (2 * ROPE_QUARTER - 1)) < ROPE_QUARTER, ahead, behind)


def _rope(x, cos, sin):
    reps = x.shape[1] // cos.shape[1]
    if reps > 1:
        cos = jnp.concatenate([cos] * reps, axis=1)
        sin = jnp.concatenate([sin] * reps, axis=1)
    return x * cos + _swap_pairs(x) * sin


def _whole(shape):
    zeros = (0,) * len(shape)
    return pl.BlockSpec(shape, lambda *_: zeros, pipeline_mode=pl.Buffered(1))


def _params(*sem):
    return pltpu.CompilerParams(dimension_semantics=sem, vmem_limit_bytes=VMEM_LIMIT)


class _Rows:
    def __init__(self, n_prompt_rows, dec_seq, n_rows, tm):
        assert n_prompt_rows % tm == 0 and dec_seq % tm == 0
        self.tm = tm
        self.n_tiles = n_rows // tm
        self.prompt_tiles = n_prompt_rows // tm
        self.seq_tiles = dec_seq // tm

    def cond(self, i):
        return jnp.where(i < self.prompt_tiles, 0, 1 + jnp.maximum(i - self.prompt_tiles, 0) // self.seq_tiles)

    def rope_block(self, i):
        return jnp.where(i < self.prompt_tiles, 0, 1 + jnp.maximum(i - self.prompt_tiles, 0) % self.seq_tiles)

    def row_spec(self, width):
        return pl.BlockSpec((self.tm, width), lambda i: (i, 0))

    def col_spec(self, height):
        return pl.BlockSpec((height, self.tm), lambda i: (0, i))

    def mod_spec(self, d):
        return pl.BlockSpec((1, 6, d), lambda i: (self.cond(i), 0, 0))

    def rope_spec(self, width):
        return pl.BlockSpec((self.tm, width), lambda i: (self.rope_block(i), 0))

    def x_specs(self, width):
        return [self.prompt_spec(width), self.latent_spec(width)]

    def is_prompt(self):
        return pl.program_id(0) < self.prompt_tiles

    def prompt_spec(self, width):
        return pl.BlockSpec((self.tm, width), lambda i: (jnp.minimum(i, self.prompt_tiles - 1), 0))

    def latent_spec(self, width):
        return pl.BlockSpec((self.tm, width), lambda i: (jnp.maximum(i - self.prompt_tiles, 0), 0))


def _rope_tables(dec_seq, tm):
    pos = jnp.arange(dec_seq)
    row = (pos // GRID_W).astype(F32)
    col = (pos % GRID_W).astype(F32)
    inv_freq = ROPE_BASE ** (-jnp.arange(ROPE_QUARTER, dtype=F32) / ROPE_QUARTER)
    ang_r = row[:, None] * inv_freq[None, :]
    ang_c = col[:, None] * inv_freq[None, :]
    cos = jnp.concatenate([jnp.cos(ang_r), jnp.cos(ang_r), jnp.cos(ang_c), jnp.cos(ang_c)], axis=1)
    sin = jnp.concatenate([-jnp.sin(ang_r), jnp.sin(ang_r), -jnp.sin(ang_c), jnp.sin(ang_c)], axis=1)
    cos = jnp.concatenate([jnp.ones((tm, ROPE_PERIOD), F32), cos], axis=0)
    sin = jnp.concatenate([jnp.zeros((tm, ROPE_PERIOD), F32), sin], axis=0)
    return cos, sin


def _adaln_kernel(c_ref, w_ref, b_ref, o_ref):
    o_ref[0] = _dot(_silu(c_ref[...]), w_ref[0]) + b_ref[0]


def _adaln(cond, ada_w, ada_b, tn=1536):
    depth, d, n = ada_w.shape
    return pl.pallas_call(
        _adaln_kernel,
        grid=(depth, n // tn),
        in_specs=[pl.BlockSpec((COND_ROWS, d), lambda l, j: (0, 0)),
                  pl.BlockSpec((1, d, tn), lambda l, j: (l, 0, j)),
                  pl.BlockSpec((1, 1, tn), lambda l, j: (l, 0, j))],
        out_specs=pl.BlockSpec((1, COND_ROWS, tn), lambda l, j: (l, 0, j)),
        out_shape=jax.ShapeDtypeStruct((depth, COND_ROWS, n), F32),
        compiler_params=_params("arbitrary", "arbitrary"),
        name="adaln",
    )(cond, ada_w, ada_b.reshape(depth, 1, n))


def _mla_store_heads(q_ref, kcat_ref, vt_ref, qn, qr_pad, kn, kr_pad, v):
    vt = v.T.astype(BF16)
    for h in range(MLA_HEADS):
        nope = slice(h * MLA_NOPE_DIM, (h + 1) * MLA_NOPE_DIM)
        if q_ref is not None:
            q_ref[h] = jnp.concatenate([qn[:, nope], qr_pad[:, h * LANES:(h + 1) * LANES]], axis=1)
        kcat_ref[h] = jnp.concatenate([kn[:, nope], kr_pad], axis=1)
        vt_ref[h] = vt[h * MLA_V_DIM:(h + 1) * MLA_V_DIM, :]


def _read_rows(is_prompt, p_ref, l_ref):
    return jnp.where(is_prompt, p_ref[...], l_ref[...])


def _mla_pre_kernel(xp_ref, xl_ref, mod_ref, gn_ref, cos_ref, sin_ref, wdq_ref, qnorm_ref, wuq_ref,
                    wdkv_ref, kvnorm_ref, wuk_ref, wuv_ref,
                    q_ref, ckv_ref, krraw_ref, kcat_ref, vt_ref, *, prompt_tiles):
    is_prompt = pl.program_id(0) < prompt_tiles
    m = mod_ref[0]
    h = _modulate(_read_rows(is_prompt, xp_ref, xl_ref), gn_ref[...], m[0:1], m[1:2]).astype(BF16)
    q_lat = _rms(_dot(h, wdq_ref[...]), qnorm_ref[...]).astype(BF16)
    cos, sin = cos_ref[...], sin_ref[...]
    nn = MLA_HEADS * MLA_NOPE_DIM
    qn = _dot(q_lat, wuq_ref[:, :nn]).astype(BF16)
    qr_pad = _rope(_dot(q_lat, wuq_ref[:, nn:]), cos, sin).astype(BF16)
    kv = _dot(h, wdkv_ref[...])
    ckv = _rms(kv[:, :MLA_KV_LORA], kvnorm_ref[...])
    kr = kv[:, MLA_KV_LORA:]

    @pl.when(is_prompt)
    def _():
        ckv_ref[...] = ckv
        krraw_ref[...] = kr[:, :MLA_ROPE_DIM]

    kr_pad = _rope(kr, cos, sin).astype(BF16)
    cb = ckv.astype(BF16)
    _mla_store_heads(q_ref, kcat_ref, vt_ref, qn, qr_pad, _dot(cb, wuk_ref[...]).astype(BF16), kr_pad,
                     _dot(cb, wuv_ref[...]))


def _mla_pre(x, mods, gn, cos, sin, w, rows):
    n_prompt, d = x[0].shape
    t = n_prompt + x[1].shape[0]
    tm = rows.tm
    ins = [*x, mods, gn, cos, sin, w["dq"], w["q_norm"], w["uq"], w["dkv"], w["kv_norm"], w["uk"], w["uv"]]
    in_specs = rows.x_specs(d) + [rows.mod_spec(d), _whole(gn.shape),
                                  rows.rope_spec(cos.shape[1]), rows.rope_spec(sin.shape[1])]
    in_specs += [_whole(a.shape) for a in ins[6:]]
    heads_rows = pl.BlockSpec((MLA_HEADS, tm, MLA_QK_PAD), lambda i: (0, i, 0))
    return pl.pallas_call(
        functools.partial(_mla_pre_kernel, prompt_tiles=rows.prompt_tiles),
        grid=(rows.n_tiles,),
        in_specs=in_specs,
        out_specs=[heads_rows, rows.prompt_spec(MLA_KV_LORA), rows.prompt_spec(MLA_ROPE_DIM), heads_rows,
                   pl.BlockSpec((MLA_HEADS, MLA_V_DIM, tm), lambda i: (0, 0, i))],
        out_shape=[jax.ShapeDtypeStruct((MLA_HEADS, t, MLA_QK_PAD), BF16),
                   jax.ShapeDtypeStruct((n_prompt, MLA_KV_LORA), F32),
                   jax.ShapeDtypeStruct((n_prompt, MLA_ROPE_DIM), F32),
                   jax.ShapeDtypeStruct((MLA_HEADS, t, MLA_QK_PAD), BF16),
                   jax.ShapeDtypeStruct((MLA_HEADS, MLA_V_DIM, t), BF16)],
        compiler_params=_params("arbitrary"),
        name="mla_pre",
    )(*ins)


def _mla_expand_kernel(c_ref, kr_ref, wuk_ref, wuv_ref, kcat_ref, vt_ref):
    cb = c_ref[...].astype(BF16)
    _mla_store_heads(None, kcat_ref, vt_ref, None, None, _dot(cb, wuk_ref[...]).astype(BF16),
                     kr_ref[...].astype(BF16), _dot(cb, wuv_ref[...]))


def _mla_expand(ckv, kr_pad, wuk, wuv):
    n = ckv.shape[0]
    return pl.pallas_call(
        _mla_expand_kernel,
        out_shape=[jax.ShapeDtypeStruct((MLA_HEADS, n, MLA_QK_PAD), BF16),
                   jax.ShapeDtypeStruct((MLA_HEADS, MLA_V_DIM, n), BF16)],
        compiler_params=pltpu.CompilerParams(vmem_limit_bytes=VMEM_LIMIT),
        name="mla_expand",
    )(ckv, kr_pad, wuk, wuv)


def _mla_attn_kernel(*refs, n_src):
    q_ref = refs[0]
    srcs = [(refs[1 + 2 * i], refs[2 + 2 * i]) for i in range(n_src)]
    o_ref, s_even, s_odd, o_buf = refs[1 + 2 * n_src:]
    n_keys = [k_ref.shape[1] for k_ref, _ in srcs]
    starts = [sum(n_keys[:i]) for i in range(n_src)]

    def put_scores(h, s_buf):
        q = q_ref[h]
        for (k_ref, _), first, n in zip(srcs, starts, n_keys):
            s_buf[first:first + n, :] = _dot_nt(k_ref[h], q)

    def finish(h, s_buf):
        s = s_buf[...]
        mx = jnp.max(s, axis=0, keepdims=True)
        p = jnp.exp2((s - mx) * (MLA_SCALE * LOG2_E))
        den = jnp.sum(p, axis=0, keepdims=True)
        p = p.astype(BF16)
        acc = None
        for (_, vt_ref), first, n in zip(srcs, starts, n_keys):
            a = _dot(vt_ref[h], p[first:first + n, :])
            acc = a if acc is None else acc + a
        o_buf[h] = (acc / den).T.astype(BF16)

    put_scores(0, s_even)

    def pair(j, carry):
        h = 2 * j
        put_scores(h + 1, s_odd)
        finish(h, s_even)
        put_scores(h + 2, s_even)
        finish(h + 1, s_odd)
        return carry

    lax.fori_loop(0, MLA_HEADS // 2 - 1, pair, 0)
    put_scores(MLA_HEADS - 1, s_odd)
    finish(MLA_HEADS - 2, s_even)
    finish(MLA_HEADS - 1, s_odd)
    for h in range(MLA_HEADS):
        o_ref[:, h * MLA_V_DIM:(h + 1) * MLA_V_DIM] = o_buf[h]


def _mla_attn_scratch(n_keys, tq):
    return [pltpu.VMEM((n_keys, tq), F32), pltpu.VMEM((n_keys, tq), F32), pltpu.VMEM((MLA_HEADS, tq, MLA_V_DIM), BF16)]


def _staged_attention(scores, values_t, scale, sinks=None):
    c = scale * LOG2_E
    mx = [jnp.max(s, axis=0, keepdims=True) for s in scores]
    if sinks is not None:
        mx = [jnp.maximum(m, z) for m, z in zip(mx, sinks)]
    p = [jnp.exp2((s - m) * c) for s, m in zip(scores, mx)]
    den = [jnp.sum(x, axis=0, keepdims=True) for x in p]
    if sinks is not None:
        den = [d + jnp.exp2((z - m) * c) for d, z, m in zip(den, sinks, mx)]
    acc = [_dot(v, x.astype(BF16)) for v, x in zip(values_t, p)]
    return [a / d for a, d in zip(acc, den)]


def _mla_attn_prompt_kernel(q_ref, k_ref, vt_ref, o_ref):
    scores = [_dot_nt(k_ref[h], q_ref[h]) for h in range(MLA_HEADS)]
    outs = _staged_attention(scores, [vt_ref[h] for h in range(MLA_HEADS)], MLA_SCALE)
    for h in range(MLA_HEADS):
        o_ref[:, h * MLA_V_DIM:(h + 1) * MLA_V_DIM] = outs[h].T.astype(BF16)


def _mla_attn_prompt(q, kcat, vt, batch, seq):
    hv = MLA_HEADS * MLA_V_DIM
    rows = pl.BlockSpec((MLA_HEADS, seq, MLA_QK_PAD), lambda b: (0, b, 0))
    return pl.pallas_call(
        _mla_attn_prompt_kernel,
        grid=(batch,),
        in_specs=[rows, rows, pl.BlockSpec((MLA_HEADS, MLA_V_DIM, seq), lambda b: (0, 0, b))],
        out_specs=pl.BlockSpec((seq, hv), lambda b: (b, 0)),
        out_shape=jax.ShapeDtypeStruct((batch * seq, hv), BF16),
        compiler_params=_params("arbitrary"),
        name="mla_attn_prompt",
    )(q, kcat, vt)


def _mla_attn_latent(q, kcat, vt, kcat_ctx, vt_ctx, n_prompt, dec_batch, dec_seq, past, tq):
    hv = MLA_HEADS * MLA_V_DIM
    nq = dec_seq // tq
    lat_blk = n_prompt // dec_seq
    return pl.pallas_call(
        functools.partial(_mla_attn_kernel, n_src=2),
        grid=(dec_batch, nq),
        in_specs=[pl.BlockSpec((MLA_HEADS, tq, MLA_QK_PAD), lambda b, i: (0, n_prompt // tq + b * nq + i, 0)),
                  pl.BlockSpec((MLA_HEADS, past, MLA_QK_PAD), lambda b, i: (0, b, 0)),
                  pl.BlockSpec((MLA_HEADS, MLA_V_DIM, past), lambda b, i: (0, 0, b)),
                  pl.BlockSpec((MLA_HEADS, dec_seq, MLA_QK_PAD), lambda b, i: (0, lat_blk + b, 0)),
                  pl.BlockSpec((MLA_HEADS, MLA_V_DIM, dec_seq), lambda b, i: (0, 0, lat_blk + b))],
        out_specs=pl.BlockSpec((tq, hv), lambda b, i: (b * nq + i, 0)),
        out_shape=jax.ShapeDtypeStruct((dec_batch * dec_seq, hv), BF16),
        scratch_shapes=_mla_attn_scratch(past + dec_seq, tq),
        compiler_params=_params("arbitrary", "arbitrary"),
        name="mla_attn_latent",
    )(q, kcat_ctx, vt_ctx, kcat, vt)


def _tile_heads(x):
    n = x.shape[1]
    block = lax.broadcasted_iota(jnp.int32, x.shape, 1) // SWA_HEAD_DIM
    rolled = [x] + [pltpu.roll(x, s * SWA_HEAD_DIM, 1) for s in range(1, SWA_KV_HEADS)]
    out = []
    for kvh in range(SWA_KV_HEADS):
        blk = rolled[(0 - kvh) % SWA_KV_HEADS]
        for g in range(1, n // SWA_HEAD_DIM):
            blk = jnp.where(block == g, rolled[(g - kvh) % SWA_KV_HEADS], blk)
        out.append(blk)
    return jnp.concatenate(out, axis=1)


def _swa_pre_kernel(xp_ref, xl_ref, mod_ref, gn_ref, cos_ref, sin_ref, wqkv_ref,
                    q_ref, kx_ref, vt_ref, kraw_ref, vraw_ref, *, prompt_tiles):
    is_prompt = pl.program_id(0) < prompt_tiles
    m = mod_ref[0]
    h = _modulate(_read_rows(is_prompt, xp_ref, xl_ref), gn_ref[...], m[0:1], m[1:2]).astype(BF16)
    qkv = _dot(h, wqkv_ref[...])
    nq, nk = SWA_HEADS * SWA_HEAD_DIM, SWA_KV_HEADS * SWA_HEAD_DIM
    cos, sin = cos_ref[...], sin_ref[...]
    q = _rope(qkv[:, :nq], cos, sin).astype(BF16)
    k = qkv[:, nq:nq + nk]
    v = qkv[:, nq + nk:]

    @pl.when(is_prompt)
    def _():
        kraw_ref[...] = k
        vraw_ref[...] = v

    kx = _tile_heads(_rope(k, cos, sin)).astype(BF16)
    vt = v.T.astype(BF16)
    for kvh in range(SWA_KV_HEADS):
        q_ref[kvh] = q[:, kvh * SWA_GW:(kvh + 1) * SWA_GW]
        kx_ref[kvh] = kx[:, kvh * SWA_GW:(kvh + 1) * SWA_GW]
        vt_ref[kvh] = vt[kvh * SWA_HEAD_DIM:(kvh + 1) * SWA_HEAD_DIM, :]


def _swa_pre(x, mods, gn, cos, sin, wqkv, rows):
    n_prompt, d = x[0].shape
    t = n_prompt + x[1].shape[0]
    tm = rows.tm
    nk = SWA_KV_HEADS * SWA_HEAD_DIM
    heads_rows = pl.BlockSpec((SWA_KV_HEADS, tm, SWA_GW), lambda i: (0, i, 0))
    return pl.pallas_call(
        functools.partial(_swa_pre_kernel, prompt_tiles=rows.prompt_tiles),
        grid=(rows.n_tiles,),
        in_specs=rows.x_specs(d) + [rows.mod_spec(d), _whole(gn.shape),
                                    rows.rope_spec(cos.shape[1]), rows.rope_spec(sin.shape[1]), _whole(wqkv.shape)],
        out_specs=[heads_rows, heads_rows, pl.BlockSpec((SWA_KV_HEADS, SWA_HEAD_DIM, tm), lambda i: (0, 0, i)),
                   rows.prompt_spec(nk), rows.prompt_spec(nk)],
        out_shape=[jax.ShapeDtypeStruct((SWA_KV_HEADS, t, SWA_GW), BF16),
                   jax.ShapeDtypeStruct((SWA_KV_HEADS, t, SWA_GW), BF16),
                   jax.ShapeDtypeStruct((SWA_KV_HEADS, SWA_HEAD_DIM, t), BF16),
                   jax.ShapeDtypeStruct((n_prompt, nk), F32), jax.ShapeDtypeStruct((n_prompt, nk), F32)],
        compiler_params=_params("arbitrary"),
        name="swa_pre",
    )(*x, mods, gn, cos, sin, wqkv)


def _swa_heads(q_ref, sink_ref, o_ref, srcs, bias_ref, s_even, s_odd, ot_buf):
    tq = q_ref.shape[1]
    n_keys = [kx_ref.shape[1] for kx_ref, _ in srcs]
    starts = [sum(n_keys[:i]) for i in range(len(srcs))]
    group = lax.broadcasted_iota(jnp.int32, (tq, SWA_GW), 1) // SWA_HEAD_DIM

    def put_scores(hq, s_buf):
        kvh, g = hq // SWA_GROUP, hq % SWA_GROUP
        q = jnp.where(group == g, q_ref[kvh].astype(F32), 0.0).astype(BF16)
        for (kx_ref, _), first, n in zip(srcs, starts, n_keys):
            s_buf[first:first + n, :] = _dot_nt(kx_ref[kvh], q)

    def finish(hq, s_buf):
        kvh, g = hq // SWA_GROUP, hq % SWA_GROUP
        s = s_buf[...]
        if bias_ref is not None:
            s = s + bias_ref[...]
        sink = jnp.full((1, 1), sink_ref[hq] * (1.0 / SWA_SCALE), F32)
        mx = jnp.maximum(jnp.max(s, axis=0, keepdims=True), sink)
        p = jnp.exp2((s - mx) * (SWA_SCALE * LOG2_E))
        den = jnp.sum(p, axis=0, keepdims=True) + jnp.exp2((sink - mx) * (SWA_SCALE * LOG2_E))
        p = p.astype(BF16)
        acc = None
        for (_, vt_ref), first, n in zip(srcs, starts, n_keys):
            a = _dot(vt_ref[kvh], p[first:first + n, :])
            acc = a if acc is None else acc + a
        ot_buf[kvh, pl.ds(pl.multiple_of(g * SWA_HEAD_DIM, SWA_HEAD_DIM), SWA_HEAD_DIM), :] = acc / den

    put_scores(0, s_even)

    def pair(j, carry):
        hq = 2 * j
        put_scores(hq + 1, s_odd)
        finish(hq, s_even)
        put_scores(hq + 2, s_even)
        finish(hq + 1, s_odd)
        return carry

    lax.fori_loop(0, SWA_HEADS // 2 - 1, pair, 0)
    put_scores(SWA_HEADS - 1, s_odd)
    finish(SWA_HEADS - 2, s_even)
    finish(SWA_HEADS - 1, s_odd)
    for kvh in range(SWA_KV_HEADS):
        o_ref[:, kvh * SWA_GW:(kvh + 1) * SWA_GW] = ot_buf[kvh].T.astype(BF16)


def _swa_attn_scratch(n_keys, tq):
    return [pltpu.VMEM((n_keys, tq), F32), pltpu.VMEM((n_keys, tq), F32), pltpu.VMEM((SWA_KV_HEADS, SWA_GW, tq), F32)]


def _swa_attn_prompt_kernel(q_ref, kx_ref, vt_ref, sink_ref, o_ref):
    group = lax.broadcasted_iota(jnp.int32, (q_ref.shape[1], SWA_GW), 1) // SWA_HEAD_DIM
    scores, values_t, sinks = [], [], []
    for kvh in range(SWA_KV_HEADS):
        q_all = q_ref[kvh].astype(F32)
        for g in range(SWA_GROUP):
            q = jnp.where(group == g, q_all, 0.0).astype(BF16)
            scores.append(_dot_nt(kx_ref[kvh], q))
            values_t.append(vt_ref[kvh])
            sinks.append(jnp.full((1, 1), sink_ref[kvh * SWA_GROUP + g] * (1.0 / SWA_SCALE), F32))
    outs = _staged_attention(scores, values_t, SWA_SCALE, sinks)
    for kvh in range(SWA_KV_HEADS):
        ot = jnp.concatenate(outs[kvh * SWA_GROUP:(kvh + 1) * SWA_GROUP], axis=0)
        o_ref[:, kvh * SWA_GW:(kvh + 1) * SWA_GW] = ot.T.astype(BF16)


def _swa_attn_prompt(q, kx, vt, sink, batch, seq):
    wq = SWA_HEADS * SWA_HEAD_DIM
    rows = pl.BlockSpec((SWA_KV_HEADS, seq, SWA_GW), lambda b: (0, b, 0))
    return pl.pallas_call(
        _swa_attn_prompt_kernel,
        grid=(batch,),
        in_specs=[rows, rows, pl.BlockSpec((SWA_KV_HEADS, SWA_HEAD_DIM, seq), lambda b: (0, 0, b)),
                  pl.BlockSpec(memory_space=pltpu.SMEM)],
        out_specs=pl.BlockSpec((seq, wq), lambda b: (b, 0)),
        out_shape=jax.ShapeDtypeStruct((batch * seq, wq), BF16),
        compiler_params=_params("arbitrary"),
        name="swa_attn_prompt",
    )(q, kx, vt, sink)


def _swa_attn_latent_kernel(q_ref, kc_ref, vc_ref, kp_ref, vp_ref, km_ref, vm_ref, kn_ref, vn_ref,
                            sink_ref, o_ref, s_even, s_odd, ot_buf, bias_ref, *, tq, dec_seq, past):
    i = pl.program_id(1)
    qpos = i * tq + lax.broadcasted_iota(jnp.int32, (1, tq), 1)

    def band(first, n):
        kpos = first + lax.broadcasted_iota(jnp.int32, (n, 1), 0)
        valid = (jnp.abs(qpos - kpos) <= SWA_WINDOW) & (kpos >= 0) & (kpos < dec_seq)
        return jnp.where(valid, 0.0, NEG_INF)

    bias_ref[...] = jnp.concatenate(
        [jnp.zeros((past, tq), F32), band(i * tq - SWA_WINDOW, SWA_WINDOW), band(i * tq, tq),
         band((i + 1) * tq, SWA_WINDOW)], axis=0)
    srcs = [(kc_ref, vc_ref), (kp_ref, vp_ref), (km_ref, vm_ref), (kn_ref, vn_ref)]
    _swa_heads(q_ref, sink_ref, o_ref, srcs, bias_ref, s_even, s_odd, ot_buf)


def _swa_attn_latent(q, kx, vt, kx_ctx, vt_ctx, sink, n_prompt, dec_batch, dec_seq, past, tq):
    t = q.shape[1]
    wq = SWA_HEADS * SWA_HEAD_DIM
    nq = dec_seq // tq
    w = SWA_WINDOW
    n_keys = past + tq + 2 * w
    first = lambda b, i: n_prompt + b * dec_seq + i * tq
    prev = lambda b, i: first(b, i) // w - 1
    nxt = lambda b, i: jnp.minimum((first(b, i) + tq) // w, t // w - 1)
    rows = lambda n, blk: pl.BlockSpec((SWA_KV_HEADS, n, SWA_GW), lambda b, i: (0, blk(b, i), 0))
    cols = lambda n, blk: pl.BlockSpec((SWA_KV_HEADS, SWA_HEAD_DIM, n), lambda b, i: (0, 0, blk(b, i)))
    main = lambda b, i: first(b, i) // tq
    ctx = lambda b, i: b
    return pl.pallas_call(
        functools.partial(_swa_attn_latent_kernel, tq=tq, dec_seq=dec_seq, past=past),
        grid=(dec_batch, nq),
        in_specs=[rows(tq, main), rows(past, ctx), cols(past, ctx), rows(w, prev), cols(w, prev),
                  rows(tq, main), cols(tq, main), rows(w, nxt), cols(w, nxt),
                  pl.BlockSpec(memory_space=pltpu.SMEM)],
        out_specs=pl.BlockSpec((tq, wq), lambda b, i: (b * nq + i, 0)),
        out_shape=jax.ShapeDtypeStruct((dec_batch * dec_seq, wq), BF16),
        scratch_shapes=_swa_attn_scratch(n_keys, tq) + [pltpu.VMEM((n_keys, tq), F32)],
        compiler_params=_params("arbitrary", "arbitrary"),
        name="swa_attn_latent",
    )(q, kx_ctx, vt_ctx, kx, vt, kx, vt, kx, vt, sink)


def _hgrn_pre_kernel(xp_ref, xl_ref, mod_ref, gn_ref, w_ref, lbl_ref, q_ref, v_ref, g_ref, f_ref, qmax_ref, *,
                     layer, prompt_tiles):
    is_prompt = pl.program_id(0) < prompt_tiles
    m = mod_ref[0]
    h = _modulate(_read_rows(is_prompt, xp_ref, xl_ref), gn_ref[...], m[0:1], m[1:2]).astype(BF16)
    y = _dot(h, w_ref[...])
    n = HG_HEADS * HG_DK
    q = _silu(y[:, :n])
    q_ref[...] = q
    qmax_ref[0] = jnp.max(jnp.abs(q), axis=0, keepdims=True)
    v_ref[...] = y[:, n:2 * n].astype(BF16)
    g_ref[...] = _silu(y[:, 2 * n:3 * n])
    for d in range(2):
        logits = lbl_ref[d]
        e = jnp.exp(logits - jnp.max(logits, axis=0, keepdims=True))
        s = e / jnp.sum(e, axis=0, keepdims=True)
        cs = s[0:1]
        for r in range(1, layer + 1):
            cs = cs + s[r:r + 1]
        lb = cs - s[0:1]
        f_ref[d] = lb + (1.0 - lb) * _sigmoid(y[:, (3 + d) * n:(4 + d) * n])


def _hgrn_pre(x, mods, gn, w5, lb_logits, layer, rows):
    d = x[0].shape[1]
    t = x[0].shape[0] + x[1].shape[0]
    n = HG_HEADS * HG_DK
    return pl.pallas_call(
        functools.partial(_hgrn_pre_kernel, layer=layer, prompt_tiles=rows.prompt_tiles),
        grid=(rows.n_tiles,),
        in_specs=rows.x_specs(d) + [rows.mod_spec(d), _whole(gn.shape), _whole(w5.shape), _whole(lb_logits.shape)],
        out_specs=[rows.row_spec(n), rows.row_spec(n), rows.row_spec(n),
                   pl.BlockSpec((2, rows.tm, n), lambda i: (0, i, 0)), pl.BlockSpec((1, 1, n), lambda i: (i, 0, 0))],
        out_shape=[jax.ShapeDtypeStruct((t, n), F32), jax.ShapeDtypeStruct((t, n), BF16),
                   jax.ShapeDtypeStruct((t, n), F32), jax.ShapeDtypeStruct((2, t, n), F32),
                   jax.ShapeDtypeStruct((rows.n_tiles, 1, n), F32)],
        compiler_params=_params("arbitrary"),
        name="hgrn_pre",
    )(*x, mods, gn, w5, lb_logits)


def _tri_cumsum(tri, x):
    hi = x.astype(BF16)
    r1 = x - hi.astype(F32)
    mid = r1.astype(BF16)
    lo = (r1 - mid.astype(F32)).astype(BF16)
    return _dot(tri, hi) + _dot(tri, mid) + _dot(tri, lo)


def _hgrn_tile(q_ref, v_ref, f_ref, o_ref, st_ref, *, rows, reverse, bounded):
    r = rows
    diag = HG_DIAG_BLOCK if bounded else 1
    a = lax.broadcasted_iota(jnp.int32, (r, r), 0)
    b = lax.broadcasted_iota(jnp.int32, (r, r), 1)
    seen = (b >= a) if reverse else (b <= a)
    tri = jnp.where(seen, 1.0, 0.0).astype(BF16)

    q = q_ref[...]
    f = f_ref[0]
    vb = v_ref[...].astype(BF16)
    lf = jnp.log(f)
    cum = _tri_cumsum(tri, lf)
    tot = cum[0:1, :] if reverse else cum[r - 1:r, :]
    kk = 1.0 - f
    q_in = (q * jnp.exp(cum)).astype(BF16)
    k_d = (kk * jnp.exp(tot - cum)).astype(BF16)
    e_tot = jnp.exp(tot)

    levels = []
    c = r // 2
    while c >= diag:
        q_half = (a % (2 * c) < c) if reverse else (a % (2 * c) >= c)
        k_half = (b % (2 * c) >= c) if reverse else (b % (2 * c) < c)
        levels.append((2 * c, c if reverse else c - 1, ((a // (2 * c)) == (b // (2 * c))) & q_half & k_half))
        c //= 2
    levels.append((diag, diag // 2, ((a // diag) == (b // diag)) & seen))

    factors = []
    for size, ref_row, own in levels:
        if size == 1:
            factors.append((q.astype(BF16), kk.astype(BF16), own))
            continue
        if size >= 8:
            ref = jnp.concatenate(
                [jnp.broadcast_to(cum[j * size + ref_row:j * size + ref_row + 1, :], (size, cum.shape[1]))
                 for j in range(r // size)], axis=0)
        else:
            ref = _tri_cumsum(jnp.where(b == (a // size) * size + ref_row, 1.0, 0.0).astype(BF16), cum)
        factors.append(((q * jnp.exp(cum - ref)).astype(BF16), (kk * jnp.exp(ref - cum)).astype(BF16), own))

    outs = []
    for h in range(HG_HEADS):
        sl = slice(h * HG_DK, (h + 1) * HG_DK)
        att = jnp.zeros((r, r), F32)
        for q_l, k_l, own in factors:
            att = jnp.where(own, _dot_nt(q_l[:, sl], k_l[:, sl]), att)
        st = st_ref[h]
        outs.append(_dot_nt(q_in[:, sl], st.astype(BF16)) + _dot(att.astype(BF16), vb[:, sl]))
        st_ref[h] = st * e_tot[:, sl] + _dot_tn(vb[:, sl], k_d[:, sl])
    o_ref[0] = jnp.concatenate(outs, axis=1)


def _hgrn_scan_kernel(*refs, rows, has_init, bounded):
    if has_init:
        q_ref, v_ref, f_ref, s0_ref, o_ref, sfin_ref, st_ref = refs
    else:
        q_ref, v_ref, f_ref, o_ref, sfin_ref, st_ref = refs
    d = pl.program_id(1)
    t = pl.program_id(2)

    @pl.when(t == 0)
    def _():
        for h in range(HG_HEADS):
            st_ref[h] = s0_ref[0, 0, h].T if has_init else jnp.zeros((HG_DV, HG_DK), F32)

    for reverse in (False, True):
        @pl.when(d == int(reverse))
        def _():
            _hgrn_tile(q_ref, v_ref, f_ref, o_ref, st_ref, rows=rows, reverse=reverse, bounded=bounded)

    @pl.when(t == pl.num_programs(2) - 1)
    def _():
        for h in range(HG_HEADS):
            sfin_ref[0, 0, h] = st_ref[h].T


def _hgrn_scan(q, v, f2, batch, seq, first_row, rt, s0=None, bounded=True):
    n = q.shape[1]
    nt = seq // rt
    has_init = s0 is not None

    def local(b, d, i):
        return b * nt + jnp.where(d == 0, i, nt - 1 - i)

    def slab(b, d, i):
        return first_row // rt + local(b, d, i)

    row = pl.BlockSpec((rt, n), lambda b, d, i: (slab(b, d, i), 0))
    state = pl.BlockSpec((1, 1, HG_HEADS, HG_DK, HG_DV), lambda b, d, i: (b, d, 0, 0, 0))
    ins = [q, v, f2]
    in_specs = [row, row, pl.BlockSpec((1, rt, n), lambda b, d, i: (d, slab(b, d, i), 0))]
    if has_init:
        ins.append(s0)
        in_specs.append(state)
    return pl.pallas_call(
        functools.partial(_hgrn_scan_kernel, rows=rt, has_init=has_init, bounded=bounded),
        grid=(batch, 2, nt),
        in_specs=in_specs,
        out_specs=[pl.BlockSpec((1, rt, n), lambda b, d, i: (d, local(b, d, i), 0)), state],
        out_shape=[jax.ShapeDtypeStruct((2, batch * seq, n), F32),
                   jax.ShapeDtypeStruct((batch, 2, HG_HEADS, HG_DK, HG_DV), F32)],
        scratch_shapes=[pltpu.VMEM((HG_HEADS, HG_DV, HG_DK), F32)],
        compiler_params=_params("arbitrary", "arbitrary", "arbitrary"),
        name="hgrn_scan_latent" if has_init else "hgrn_scan_prompt",
    )(*ins)


def _post_kernel(*refs, hgrn, final, prompt_tiles):
    refs = list(refs)
    outl_ref = refs.pop()
    outp_ref = refs.pop()
    is_prompt = pl.program_id(0) < prompt_tiles
    if hgrn:
        odp_ref, odl_ref, g_ref, onorm_ref = refs[:4]
        refs = refs[4:]
        o2 = jnp.where(is_prompt, odp_ref[0] + odp_ref[1], odl_ref[0] + odl_ref[1])
        gate = g_ref[...]
        onorm = onorm_ref[...]
        parts = []
        for h in range(HG_HEADS):
            sl = slice(h * HG_DV, (h + 1) * HG_DV)
            parts.append(_rms(o2[:, sl], onorm) * gate[:, sl])
        o = jnp.concatenate(parts, axis=1).astype(BF16)
    else:
        o = jnp.where(is_prompt, refs[0][...], refs[1][...])
        refs = refs[2:]
    xp_ref, xl_ref, mod_ref, wo_ref, gn_ref, wg_ref, wu_ref, wd_ref = refs[:8]
    m = mod_ref[0]
    x1 = _read_rows(is_prompt, xp_ref, xl_ref) + m[2:3] * _dot(o, wo_ref[...])
    h2 = _modulate(x1, gn_ref[...], m[3:4], m[4:5]).astype(BF16)
    a = (_silu(_dot(h2, wg_ref[0])) * _dot(h2, wu_ref[0])).astype(BF16)
    x2 = x1 + m[5:6] * _dot(a, wd_ref[0])
    if final:
        x2 = _rms(x2, refs[8][...])

    @pl.when(is_prompt)
    def _():
        outp_ref[...] = x2

    @pl.when(jnp.logical_not(is_prompt))
    def _():
        outl_ref[...] = x2


def _layer_of(stack, layer):
    shape = (1,) + stack.shape[1:]
    return pl.BlockSpec(shape, lambda *_: (layer,) + (0,) * (len(shape) - 1), pipeline_mode=pl.Buffered(1))


def _post(mix, x, mods, wo, gn, wg, wu, wd, layer, rows, final_norm=None, hgrn=False):
    d = x[0].shape[1]
    tm = rows.tm
    if hgrn:
        od_p, od_l, gate, onorm = mix
        head = [od_p, od_l, gate, onorm]
        head_specs = [pl.BlockSpec((2, tm, d), lambda i: (0, jnp.minimum(i, rows.prompt_tiles - 1), 0)),
                      pl.BlockSpec((2, tm, d), lambda i: (0, jnp.maximum(i - rows.prompt_tiles, 0), 0)),
                      rows.row_spec(d), _whole(onorm.shape)]
    else:
        head = list(mix)
        head_specs = [rows.prompt_spec(mix[0].shape[1]), rows.latent_spec(mix[1].shape[1])]
    ins = head + [*x, mods, wo, gn, wg, wu, wd]
    in_specs = (head_specs + rows.x_specs(d) + [rows.mod_spec(d), _whole(wo.shape), _whole(gn.shape)]
                + [_layer_of(a, layer) for a in (wg, wu, wd)])
    if final_norm is not None:
        ins.append(final_norm)
        in_specs.append(_whole(final_norm.shape))
    return pl.pallas_call(
        functools.partial(_post_kernel, hgrn=hgrn, final=final_norm is not None, prompt_tiles=rows.prompt_tiles),
        grid=(rows.n_tiles,),
        in_specs=in_specs,
        out_specs=rows.x_specs(d),
        out_shape=[jax.ShapeDtypeStruct(x[0].shape, F32), jax.ShapeDtypeStruct(x[1].shape, F32)],
        compiler_params=_params("arbitrary"),
        name="post",
    )(*ins)


def _pick_tile(n_prompt_rows, dec_seq, want):
    tm = want
    while n_prompt_rows % tm or dec_seq % tm:
        tm //= 2
    return tm


def kernel(x_prompt, x_sample, cache_mla_ckv, cache_mla_krope, state_hgrn, cache_swa_k, cache_swa_v, c, c_ctx, ada_w, ada_b, norm_mix, norm_ffn, ffn_w_gate, ffn_w_up, ffn_w_down, final_norm, mla_w_dq, mla_q_norm, mla_w_uq, mla_w_dkv, mla_kv_norm, mla_w_uk, mla_w_uv, mla_w_o, hg_w_q, hg_w_f, hg_w_i, hg_w_g, hg_o_norm, hg_w_o, hg_lb_logits, swa_w_q, swa_w_k, swa_w_v, swa_w_o, swa_sink):
    batch, seq, d = x_prompt.shape
    dec_batch, dec_seq, _ = x_sample.shape
    past = cache_mla_ckv.shape[2]
    depth = ada_w.shape[0]
    n_prompt = batch * seq
    n_rows = n_prompt + dec_batch * dec_seq
    assert dec_batch + 1 <= COND_ROWS and seq % SWA_WINDOW == 0 and dec_seq % (2 * SWA_WINDOW) == 0
    assert n_prompt % dec_seq == 0

    pre_rows = _Rows(n_prompt, dec_seq, n_rows, _pick_tile(n_prompt, dec_seq, 256))
    post_rows = _Rows(n_prompt, dec_seq, n_rows, _pick_tile(n_prompt, dec_seq, 512))
    tq = _pick_tile(n_prompt, dec_seq, 256)
    cos64, sin64 = _rope_tables(dec_seq, pre_rows.tm)
    n_tab = cos64.shape[0]
    cos_mla = jnp.concatenate([cos64, jnp.ones((n_tab, ROPE_PERIOD), F32)], axis=1)
    sin_mla = jnp.concatenate([sin64, jnp.zeros((n_tab, ROPE_PERIOD), F32)], axis=1)
    cos_swa, sin_swa = jnp.tile(cos64, (1, 2)), jnp.tile(sin64, (1, 2))

    cond = jnp.concatenate([c_ctx[None, :], c, jnp.zeros((COND_ROWS - 1 - dec_batch, d), F32)], axis=0)
    mods = _adaln(cond, ada_w, ada_b).reshape(depth, COND_ROWS, 6, d)

    x = (x_prompt.reshape(n_prompt, d), x_sample.reshape(dec_batch * dec_seq, d))
    ffn_wg, ffn_wu, ffn_wd = ffn_w_gate.astype(BF16), ffn_w_up.astype(BF16), ffn_w_down.astype(BF16)
    row1 = lambda a: a.reshape(1, -1)
    new_ckv, new_krope, new_hg, new_k, new_v = [], [], [], [], []
    for i in range(depth):
        kind, j = i % N_MIXERS, i // N_MIXERS
        gn = row1(norm_mix[i])
        if kind == 0:
            uq = mla_w_uq[j].reshape(-1, MLA_HEADS, MLA_NOPE_DIM + MLA_ROPE_DIM)
            uq_rope = jnp.pad(uq[:, :, MLA_NOPE_DIM:], ((0, 0), (0, 0), (0, LANES - MLA_ROPE_DIM)))
            uq = jnp.concatenate([uq[:, :, :MLA_NOPE_DIM].reshape(uq.shape[0], -1),
                                  uq_rope.reshape(uq.shape[0], -1)], axis=1)
            w = {
                "dq": mla_w_dq[j].astype(BF16), "q_norm": row1(mla_q_norm[j]), "uq": uq.astype(BF16),
                "dkv": jnp.pad(mla_w_dkv[j], ((0, 0), (0, LANES - MLA_ROPE_DIM))).astype(BF16),
                "kv_norm": row1(mla_kv_norm[j]),
                "uk": mla_w_uk[j].astype(BF16), "uv": mla_w_uv[j].astype(BF16),
            }
            q, ckv, kr_raw, kcat, vt = _mla_pre(x, mods[i], gn, cos_mla, sin_mla, w, pre_rows)
            kr_ctx = jnp.pad(cache_mla_krope[:, j].reshape(dec_batch * past, -1), ((0, 0), (0, LANES - MLA_ROPE_DIM)))
            kcat_ctx, vt_ctx = _mla_expand(cache_mla_ckv[:, j].reshape(dec_batch * past, -1), kr_ctx, w["uk"], w["uv"])
            mix = (_mla_attn_prompt(q, kcat, vt, batch, seq),
                   _mla_attn_latent(q, kcat, vt, kcat_ctx, vt_ctx, n_prompt, dec_batch, dec_seq, past, tq))
            wo = mla_w_o[j].astype(BF16)
            new_ckv.append(ckv.reshape(batch, seq, -1))
            new_krope.append(kr_raw.reshape(batch, seq, -1))
        elif kind == 1:
            w5 = jnp.concatenate([hg_w_q[j], hg_w_i[j], hg_w_g[j], hg_w_f[j, 0], hg_w_f[j, 1]], axis=1).astype(BF16)
            q, v, gate, f2, q_max = _hgrn_pre(x, mods[i], gn, w5, hg_lb_logits, i, pre_rows)
            rt = _pick_tile(seq, dec_seq, 256)
            sm = jax.nn.softmax(hg_lb_logits.astype(F32), axis=1)
            lb_min = jnp.min(jnp.cumsum(sm, axis=1)[:, i] - sm[:, 0])
            worst = (HG_DIAG_BLOCK // 2) * -jnp.log(lb_min) + jnp.log(jnp.maximum(jnp.max(q_max), 1.0))
            fits = worst < HG_MAX_EXPONENT

            def scans(bounded):
                od_p, s_p = _hgrn_scan(q, v, f2, batch, seq, 0, rt, bounded=bounded)
                od_l, _ = _hgrn_scan(q, v, f2, dec_batch, dec_seq, n_prompt, rt, s0=state_hgrn[:, j], bounded=bounded)
                return od_p, od_l, s_p

            od_p, od_l, s_prompt = lax.cond(fits, lambda: scans(True), lambda: scans(False))
            mix = (od_p, od_l, gate, row1(hg_o_norm[j]))
            wo = hg_w_o[j].astype(BF16)
            new_hg.append(s_prompt)
        else:
            wqkv = jnp.concatenate([swa_w_q[j], swa_w_k[j], swa_w_v[j]], axis=1).astype(BF16)
            q, kx, vt, k_raw, v_raw = _swa_pre(x, mods[i], gn, cos_swa, sin_swa, wqkv, pre_rows)
            sink = swa_sink[j]
            k_ctx = cache_swa_k[:, j].reshape(dec_batch * past, SWA_KV_HEADS, 1, SWA_HEAD_DIM).transpose(1, 0, 2, 3)
            kx_ctx = jnp.broadcast_to(k_ctx, (SWA_KV_HEADS, dec_batch * past, SWA_GROUP, SWA_HEAD_DIM))
            kx_ctx = kx_ctx.reshape(SWA_KV_HEADS, dec_batch * past, SWA_GW).astype(BF16)
            vt_ctx = cache_swa_v[:, j].reshape(dec_batch * past, SWA_KV_HEADS, SWA_HEAD_DIM).transpose(1, 2, 0).astype(BF16)
            mix = (_swa_attn_prompt(q, kx, vt, sink, batch, seq),
                   _swa_attn_latent(q, kx, vt, kx_ctx, vt_ctx, sink, n_prompt, dec_batch, dec_seq, past, tq))
            wo = swa_w_o[j].astype(BF16)
            new_k.append(k_raw.reshape(batch, seq, SWA_KV_HEADS, SWA_HEAD_DIM))
            new_v.append(v_raw.reshape(batch, seq, SWA_KV_HEADS, SWA_HEAD_DIM))
        x = _post(mix, x, mods[i], wo, row1(norm_ffn[i]), ffn_wg, ffn_wu, ffn_wd, i,
                  pre_rows if kind == 1 else post_rows,
                  final_norm=row1(final_norm) if i == depth - 1 else None, hgrn=kind == 1)
    y_prompt = x[0].reshape(batch, seq, d)
    y_sample = x[1].reshape(dec_batch, dec_seq, d)
    return (y_prompt, y_sample, jnp.stack(new_ckv, axis=1), jnp.stack(new_krope, axis=1),
            jnp.stack(new_hg, axis=1), jnp.stack(new_k, axis=1), jnp.stack(new_v, axis=1))
```

```python
import functools

import jax
import jax.numpy as jnp
from jax import lax
from jax.experimental import pallas as pl
from jax.experimental.pallas import tpu as pltpu

F32 = jnp.float32
BF16 = jnp.bfloat16

GRID_W = 64
N_MIXERS = 3

MLA_HEADS = 8
MLA_KV_LORA = 256
MLA_NOPE_DIM = 128
MLA_ROPE_DIM = 64
MLA_V_DIM = 128
MLA_QK_PAD = 256
MLA_SCALE = (MLA_NOPE_DIM + MLA_ROPE_DIM) ** -0.5

HG_HEADS = 8
HG_DK = 128
HG_DV = 128
HG_DIAG_BLOCK = 32
HG_MAX_EXPONENT = 80.0

SWA_HEADS = 16
SWA_KV_HEADS = 4
SWA_GROUP = SWA_HEADS // SWA_KV_HEADS
SWA_HEAD_DIM = 64
SWA_GW = SWA_GROUP * SWA_HEAD_DIM
SWA_WINDOW = 128
SWA_SCALE = SWA_HEAD_DIM ** -0.5

ROPE_BASE = 10000.0
ROPE_PERIOD = 64
ROPE_QUARTER = 16
NORM_EPS = 1e-6
NEG_INF = -1e30
LOG2_E = 1.4426950408889634

LANES = 128
COND_ROWS = 8
VMEM_LIMIT = 56 * 1024 * 1024


def _sigmoid(x):
    return jax.nn.sigmoid(x)


def _silu(x):
    return x * jax.nn.sigmoid(x)


def _rms(x, g):
    return x * lax.rsqrt(jnp.mean(x * x, axis=-1, keepdims=True) + NORM_EPS) * g


def _modulate(x, g, shift, scale):
    return _rms(x, g) * (1.0 + scale) + shift


def _dot(a, b):
    return jnp.dot(a, b, preferred_element_type=F32)


def _dot_nt(a, b):
    return lax.dot_general(a, b, (((1,), (1,)), ((), ())), preferred_element_type=F32)


def _dot_tn(a, b):
    return lax.dot_general(a, b, (((0,), (0,)), ((), ())), preferred_element_type=F32)


def _swap_pairs(x):
    n = x.shape[1]
    lane = lax.broadcasted_iota(jnp.int32, x.shape, 1)
    ahead = pltpu.roll(x, n - ROPE_QUARTER, 1)
    behind = pltpu.roll(x, ROPE_QUARTER, 1)
    return jnp.where((lane & (2 * ROPE_QUARTER - 1)) < ROPE_QUARTER, ahead, behind)


def _rope(x, cos, sin):
    reps = x.shape[1] // cos.shape[1]
    if reps > 1:
        cos = jnp.concatenate([cos] * reps, axis=1)
        sin = jnp.concatenate([sin] * reps, axis=1)
    return x * cos + _swap_pairs(x) * sin


def _whole(shape):
    zeros = (0,) * len(shape)
    return pl.BlockSpec(shape, lambda *_: zeros, pipeline_mode=pl.Buffered(1))


def _params(*sem):
    return pltpu.CompilerParams(dimension_semantics=sem, vmem_limit_bytes=VMEM_LIMIT)


class _Rows:
    def __init__(self, n_prompt_rows, dec_seq, n_rows, tm):
        assert n_prompt_rows % tm == 0 and dec_seq % tm == 0
        self.tm = tm
        self.n_tiles = n_rows // tm
        self.prompt_tiles = n_prompt_rows // tm
        self.seq_tiles = dec_seq // tm

    def cond(self, i):
        return jnp.where(i < self.prompt_tiles, 0, 1 + jnp.maximum(i - self.prompt_tiles, 0) // self.seq_tiles)

    def rope_block(self, i):
        return jnp.where(i < self.prompt_tiles, 0, 1 + jnp.maximum(i - self.prompt_tiles, 0) % self.seq_tiles)

    def row_spec(self, width):
        return pl.BlockSpec((self.tm, width), lambda i: (i, 0))

    def col_spec(self, height):
        return pl.BlockSpec((height, self.tm), lambda i: (0, i))

    def mod_spec(self, d):
        return pl.BlockSpec((1, 6, d), lambda i: (self.cond(i), 0, 0))

    def rope_spec(self, width):
        return pl.BlockSpec((self.tm, width), lambda i: (self.rope_block(i), 0))

    def x_specs(self, width):
        return [self.prompt_spec(width), self.latent_spec(width)]

    def is_prompt(self):
        return pl.program_id(0) < self.prompt_tiles

    def prompt_spec(self, width):
        return pl.BlockSpec((self.tm, width), lambda i: (jnp.minimum(i, self.prompt_tiles - 1), 0))

    def latent_spec(self, width):
        return pl.BlockSpec((self.tm, width), lambda i: (jnp.maximum(i - self.prompt_tiles, 0), 0))


def _rope_tables(dec_seq, tm):
    pos = jnp.arange(dec_seq)
    row = (pos // GRID_W).astype(F32)
    col = (pos % GRID_W).astype(F32)
    inv_freq = ROPE_BASE ** (-jnp.arange(ROPE_QUARTER, dtype=F32) / ROPE_QUARTER)
    ang_r = row[:, None] * inv_freq[None, :]
    ang_c = col[:, None] * inv_freq[None, :]
    cos = jnp.concatenate([jnp.cos(ang_r), jnp.cos(ang_r), jnp.cos(ang_c), jnp.cos(ang_c)], axis=1)
    sin = jnp.concatenate([-jnp.sin(ang_r), jnp.sin(ang_r), -jnp.sin(ang_c), jnp.sin(ang_c)], axis=1)
    cos = jnp.concatenate([jnp.ones((tm, ROPE_PERIOD), F32), cos], axis=0)
    sin = jnp.concatenate([jnp.zeros((tm, ROPE_PERIOD), F32), sin], axis=0)
    return cos, sin


def _adaln_kernel(c_ref, w_ref, b_ref, o_ref):
    o_ref[0] = _dot(_silu(c_ref[...]), w_ref[0]) + b_ref[0]


def _adaln(cond, ada_w, ada_b, tn=1536):
    depth, d, n = ada_w.shape
    return pl.pallas_call(
        _adaln_kernel,
        grid=(depth, n // tn),
        in_specs=[pl.BlockSpec((COND_ROWS, d), lambda l, j: (0, 0)),
                  pl.BlockSpec((1, d, tn), lambda l, j: (l, 0, j)),
                  pl.BlockSpec((1, 1, tn), lambda l, j: (l, 0, j))],
        out_specs=pl.BlockSpec((1, COND_ROWS, tn), lambda l, j: (l, 0, j)),
        out_shape=jax.ShapeDtypeStruct((depth, COND_ROWS, n), F32),
        compiler_params=_params("arbitrary", "arbitrary"),
        name="adaln",
    )(cond, ada_w, ada_b.reshape(depth, 1, n))


def _mla_store_heads(q_ref, kcat_ref, vt_ref, qn, qr_pad, kn, kr_pad, v):
    vt = v.T.astype(BF16)
    for h in range(MLA_HEADS):
        nope = slice(h * MLA_NOPE_DIM, (h + 1) * MLA_NOPE_DIM)
        if q_ref is not None:
            q_ref[h] = jnp.concatenate([qn[:, nope], qr_pad[:, h * LANES:(h + 1) * LANES]], axis=1)
        kcat_ref[h] = jnp.concatenate([kn[:, nope], kr_pad], axis=1)
        vt_ref[h] = vt[h * MLA_V_DIM:(h + 1) * MLA_V_DIM, :]


def _read_rows(is_prompt, p_ref, l_ref):
    return jnp.where(is_prompt, p_ref[...], l_ref[...])


def _mla_pre_kernel(xp_ref, xl_ref, mod_ref, gn_ref, cos_ref, sin_ref, wdq_ref, qnorm_ref, wuq_ref,
                    wdkv_ref, kvnorm_ref, wuk_ref, wuv_ref,
                    q_ref, ckv_ref, krraw_ref, kcat_ref, vt_ref, *, prompt_tiles):
    is_prompt = pl.program_id(0) < prompt_tiles
    m = mod_ref[0]
    h = _modulate(_read_rows(is_prompt, xp_ref, xl_ref), gn_ref[...], m[0:1], m[1:2]).astype(BF16)
    q_lat = _rms(_dot(h, wdq_ref[...]), qnorm_ref[...]).astype(BF16)
    cos, sin = cos_ref[...], sin_ref[...]
    nn = MLA_HEADS * MLA_NOPE_DIM
    qn = _dot(q_lat, wuq_ref[:, :nn]).astype(BF16)
    qr_pad = _rope(_dot(q_lat, wuq_ref[:, nn:]), cos, sin).astype(BF16)
    kv = _dot(h, wdkv_ref[...])
    ckv = _rms(kv[:, :MLA_KV_LORA], kvnorm_ref[...])
    kr = kv[:, MLA_KV_LORA:]

    @pl.when(is_prompt)
    def _():
        ckv_ref[...] = ckv
        krraw_ref[...] = kr[:, :MLA_ROPE_DIM]

    kr_pad = _rope(kr, cos, sin).astype(BF16)
    cb = ckv.astype(BF16)
    _mla_store_heads(q_ref, kcat_ref, vt_ref, qn, qr_pad, _dot(cb, wuk_ref[...]).astype(BF16), kr_pad,
                     _dot(cb, wuv_ref[...]))


def _mla_pre(x, mods, gn, cos, sin, w, rows):
    n_prompt, d = x[0].shape
    t = n_prompt + x[1].shape[0]
    tm = rows.tm
    ins = [*x, mods, gn, cos, sin, w["dq"], w["q_norm"], w["uq"], w["dkv"], w["kv_norm"], w["uk"], w["uv"]]
    in_specs = rows.x_specs(d) + [rows.mod_spec(d), _whole(gn.shape),
                                  rows.rope_spec(cos.shape[1]), rows.rope_spec(sin.shape[1])]
    in_specs += [_whole(a.shape) for a in ins[6:]]
    heads_rows = pl.BlockSpec((MLA_HEADS, tm, MLA_QK_PAD), lambda i: (0, i, 0))
    return pl.pallas_call(
        functools.partial(_mla_pre_kernel, prompt_tiles=rows.prompt_tiles),
        grid=(rows.n_tiles,),
        in_specs=in_specs,
        out_specs=[heads_rows, rows.prompt_spec(MLA_KV_LORA), rows.prompt_spec(MLA_ROPE_DIM), heads_rows,
                   pl.BlockSpec((MLA_HEADS, MLA_V_DIM, tm), lambda i: (0, 0, i))],
        out_shape=[jax.ShapeDtypeStruct((MLA_HEADS, t, MLA_QK_PAD), BF16),
                   jax.ShapeDtypeStruct((n_prompt, MLA_KV_LORA), F32),
                   jax.ShapeDtypeStruct((n_prompt, MLA_ROPE_DIM), F32),
                   jax.ShapeDtypeStruct((MLA_HEADS, t, MLA_QK_PAD), BF16),
                   jax.ShapeDtypeStruct((MLA_HEADS, MLA_V_DIM, t), BF16)],
        compiler_params=_params("arbitrary"),
        name="mla_pre",
    )(*ins)


def _mla_expand_kernel(c_ref, kr_ref, wuk_ref, wuv_ref, kcat_ref, vt_ref):
    cb = c_ref[...].astype(BF16)
    _mla_store_heads(None, kcat_ref, vt_ref, None, None, _dot(cb, wuk_ref[...]).astype(BF16),
                     kr_ref[...].astype(BF16), _dot(cb, wuv_ref[...]))


def _mla_expand(ckv, kr_pad, wuk, wuv):
    n = ckv.shape[0]
    return pl.pallas_call(
        _mla_expand_kernel,
        out_shape=[jax.ShapeDtypeStruct((MLA_HEADS, n, MLA_QK_PAD), BF16),
                   jax.ShapeDtypeStruct((MLA_HEADS, MLA_V_DIM, n), BF16)],
        compiler_params=pltpu.CompilerParams(vmem_limit_bytes=VMEM_LIMIT),
        name="mla_expand",
    )(ckv, kr_pad, wuk, wuv)


def _mla_attn_kernel(*refs, n_src):
    q_ref = refs[0]
    srcs = [(refs[1 + 2 * i], refs[2 + 2 * i]) for i in range(n_src)]
    o_ref, o_buf = refs[1 + 2 * n_src:3 + 2 * n_src]
    s_bufs = refs[3 + 2 * n_src:]
    n_keys = [k_ref.shape[1] for k_ref, _ in srcs]
    starts = [sum(n_keys[:i]) for i in range(n_src)]

    def put_scores(h, s_buf):
        q = q_ref[h]
        for (k_ref, _), first, n in zip(srcs, starts, n_keys):
            s_buf[first:first + n, :] = _dot_nt(k_ref[h], q)

    def finish(h, s_buf):
        s = s_buf[...]
        mx = jnp.max(s, axis=0, keepdims=True)
        p = jnp.exp2((s - mx) * (MLA_SCALE * LOG2_E))
        den = jnp.sum(p, axis=0, keepdims=True)
        p = p.astype(BF16)
        acc = None
        for (_, vt_ref), first, n in zip(srcs, starts, n_keys):
            a = _dot(vt_ref[h], p[first:first + n, :])
            acc = a if acc is None else acc + a
        o_buf[h] = (acc / den).T.astype(BF16)

    _pipelined_heads(MLA_HEADS, put_scores, finish, s_bufs)
    for h in range(MLA_HEADS):
        o_ref[:, h * MLA_V_DIM:(h + 1) * MLA_V_DIM] = o_buf[h]


def _mla_attn_scratch(n_keys, tq):
    return [pltpu.VMEM((MLA_HEADS, tq, MLA_V_DIM), BF16)] + [pltpu.VMEM((n_keys, tq), F32)] * 4


def _pipelined_heads(n_heads, put_scores, finish, bufs):
    a, b, c, d = bufs
    assert n_heads % 4 == 0
    put_scores(0, a)
    put_scores(1, b)

    def quad(j, carry):
        h = 4 * j
        put_scores(h + 2, c)
        put_scores(h + 3, d)
        finish(h, a)
        finish(h + 1, b)
        put_scores(h + 4, a)
        put_scores(h + 5, b)
        finish(h + 2, c)
        finish(h + 3, d)
        return carry

    lax.fori_loop(0, n_heads // 4 - 1, quad, 0)
    h = n_heads - 4
    put_scores(h + 2, c)
    put_scores(h + 3, d)
    finish(h, a)
    finish(h + 1, b)
    finish(h + 2, c)
    finish(h + 3, d)


def _staged_attention(scores, values_t, scale, sinks=None):
    c = scale * LOG2_E
    mx = [jnp.max(s, axis=0, keepdims=True) for s in scores]
    if sinks is not None:
        mx = [jnp.maximum(m, z) for m, z in zip(mx, sinks)]
    p = [jnp.exp2((s - m) * c) for s, m in zip(scores, mx)]
    den = [jnp.sum(x, axis=0, keepdims=True) for x in p]
    if sinks is not None:
        den = [d + jnp.exp2((z - m) * c) for d, z, m in zip(den, sinks, mx)]
    acc = [_dot(v, x.astype(BF16)) for v, x in zip(values_t, p)]
    return [a / d for a, d in zip(acc, den)]


def _mla_attn_prompt_kernel(q_ref, k_ref, vt_ref, o_ref):
    scores = [_dot_nt(k_ref[h], q_ref[h]) for h in range(MLA_HEADS)]
    outs = _staged_attention(scores, [vt_ref[h] for h in range(MLA_HEADS)], MLA_SCALE)
    for h in range(MLA_HEADS):
        o_ref[:, h * MLA_V_DIM:(h + 1) * MLA_V_DIM] = outs[h].T.astype(BF16)


def _mla_attn_prompt(q, kcat, vt, batch, seq):
    hv = MLA_HEADS * MLA_V_DIM
    rows = pl.BlockSpec((MLA_HEADS, seq, MLA_QK_PAD), lambda b: (0, b, 0))
    return pl.pallas_call(
        _mla_attn_prompt_kernel,
        grid=(batch,),
        in_specs=[rows, rows, pl.BlockSpec((MLA_HEADS, MLA_V_DIM, seq), lambda b: (0, 0, b))],
        out_specs=pl.BlockSpec((seq, hv), lambda b: (b, 0)),
        out_shape=jax.ShapeDtypeStruct((batch * seq, hv), BF16),
        compiler_params=_params("arbitrary"),
        name="mla_attn_prompt",
    )(q, kcat, vt)


def _mla_attn_latent(q, kcat, vt, kcat_ctx, vt_ctx, n_prompt, dec_batch, dec_seq, past, tq):
    hv = MLA_HEADS * MLA_V_DIM
    nq = dec_seq // tq
    lat_blk = n_prompt // dec_seq
    return pl.pallas_call(
        functools.partial(_mla_attn_kernel, n_src=2),
        grid=(dec_batch, nq),
        in_specs=[pl.BlockSpec((MLA_HEADS, tq, MLA_QK_PAD), lambda b, i: (0, n_prompt // tq + b * nq + i, 0)),
                  pl.BlockSpec((MLA_HEADS, past, MLA_QK_PAD), lambda b, i: (0, b, 0)),
                  pl.BlockSpec((MLA_HEADS, MLA_V_DIM, past), lambda b, i: (0, 0, b)),
                  pl.BlockSpec((MLA_HEADS, dec_seq, MLA_QK_PAD), lambda b, i: (0, lat_blk + b, 0)),
                  pl.BlockSpec((MLA_HEADS, MLA_V_DIM, dec_seq), lambda b, i: (0, 0, lat_blk + b))],
        out_specs=pl.BlockSpec((tq, hv), lambda b, i: (b * nq + i, 0)),
        out_shape=jax.ShapeDtypeStruct((dec_batch * dec_seq, hv), BF16),
        scratch_shapes=_mla_attn_scratch(past + dec_seq, tq),
        compiler_params=_params("arbitrary", "arbitrary"),
        name="mla_attn_latent",
    )(q, kcat_ctx, vt_ctx, kcat, vt)


def _tile_heads(x):
    n = x.shape[1]
    block = lax.broadcasted_iota(jnp.int32, x.shape, 1) // SWA_HEAD_DIM
    rolled = [x] + [pltpu.roll(x, s * SWA_HEAD_DIM, 1) for s in range(1, SWA_KV_HEADS)]
    out = []
    for kvh in range(SWA_KV_HEADS):
        blk = rolled[(0 - kvh) % SWA_KV_HEADS]
        for g in range(1, n // SWA_HEAD_DIM):
            blk = jnp.where(block == g, rolled[(g - kvh) % SWA_KV_HEADS], blk)
        out.append(blk)
    return jnp.concatenate(out, axis=1)


def _swa_pre_kernel(xp_ref, xl_ref, mod_ref, gn_ref, cos_ref, sin_ref, wqkv_ref,
                    q_ref, kx_ref, vt_ref, kraw_ref, vraw_ref, *, prompt_tiles):
    is_prompt = pl.program_id(0) < prompt_tiles
    m = mod_ref[0]
    h = _modulate(_read_rows(is_prompt, xp_ref, xl_ref), gn_ref[...], m[0:1], m[1:2]).astype(BF16)
    qkv = _dot(h, wqkv_ref[...])
    nq, nk = SWA_HEADS * SWA_HEAD_DIM, SWA_KV_HEADS * SWA_HEAD_DIM
    cos, sin = cos_ref[...], sin_ref[...]
    q = _rope(qkv[:, :nq], cos, sin).astype(BF16)
    k = qkv[:, nq:nq + nk]
    v = qkv[:, nq + nk:]

    @pl.when(is_prompt)
    def _():
        kraw_ref[...] = k
        vraw_ref[...] = v

    kx = _tile_heads(_rope(k, cos, sin)).astype(BF16)
    vt = v.T.astype(BF16)
    for kvh in range(SWA_KV_HEADS):
        q_ref[kvh] = q[:, kvh * SWA_GW:(kvh + 1) * SWA_GW]
        kx_ref[kvh] = kx[:, kvh * SWA_GW:(kvh + 1) * SWA_GW]
        vt_ref[kvh] = vt[kvh * SWA_HEAD_DIM:(kvh + 1) * SWA_HEAD_DIM, :]


def _swa_pre(x, mods, gn, cos, sin, wqkv, rows):
    n_prompt, d = x[0].shape
    t = n_prompt + x[1].shape[0]
    tm = rows.tm
    nk = SWA_KV_HEADS * SWA_HEAD_DIM
    heads_rows = pl.BlockSpec((SWA_KV_HEADS, tm, SWA_GW), lambda i: (0, i, 0))
    return pl.pallas_call(
        functools.partial(_swa_pre_kernel, prompt_tiles=rows.prompt_tiles),
        grid=(rows.n_tiles,),
        in_specs=rows.x_specs(d) + [rows.mod_spec(d), _whole(gn.shape),
                                    rows.rope_spec(cos.shape[1]), rows.rope_spec(sin.shape[1]), _whole(wqkv.shape)],
        out_specs=[heads_rows, heads_rows, pl.BlockSpec((SWA_KV_HEADS, SWA_HEAD_DIM, tm), lambda i: (0, 0, i)),
                   rows.prompt_spec(nk), rows.prompt_spec(nk)],
        out_shape=[jax.ShapeDtypeStruct((SWA_KV_HEADS, t, SWA_GW), BF16),
                   jax.ShapeDtypeStruct((SWA_KV_HEADS, t, SWA_GW), BF16),
                   jax.ShapeDtypeStruct((SWA_KV_HEADS, SWA_HEAD_DIM, t), BF16),
                   jax.ShapeDtypeStruct((n_prompt, nk), F32), jax.ShapeDtypeStruct((n_prompt, nk), F32)],
        compiler_params=_params("arbitrary"),
        name="swa_pre",
    )(*x, mods, gn, cos, sin, wqkv)


def _swa_heads(q_ref, sink_ref, o_ref, srcs, bias_ref, ot_buf, s_bufs):
    tq = q_ref.shape[1]
    n_keys = [kx_ref.shape[1] for kx_ref, _ in srcs]
    starts = [sum(n_keys[:i]) for i in range(len(srcs))]
    group = lax.broadcasted_iota(jnp.int32, (tq, SWA_GW), 1) // SWA_HEAD_DIM

    def put_scores(hq, s_buf):
        kvh, g = hq // SWA_GROUP, hq % SWA_GROUP
        q = jnp.where(group == g, q_ref[kvh].astype(F32), 0.0).astype(BF16)
        for (kx_ref, _), first, n in zip(srcs, starts, n_keys):
            s_buf[first:first + n, :] = _dot_nt(kx_ref[kvh], q)

    def finish(hq, s_buf):
        kvh, g = hq // SWA_GROUP, hq % SWA_GROUP
        s = s_buf[...]
        if bias_ref is not None:
            s = s + bias_ref[...]
        sink = jnp.full((1, 1), sink_ref[hq] * (1.0 / SWA_SCALE), F32)
        mx = jnp.maximum(jnp.max(s, axis=0, keepdims=True), sink)
        p = jnp.exp2((s - mx) * (SWA_SCALE * LOG2_E))
        den = jnp.sum(p, axis=0, keepdims=True) + jnp.exp2((sink - mx) * (SWA_SCALE * LOG2_E))
        p = p.astype(BF16)
        acc = None
        for (_, vt_ref), first, n in zip(srcs, starts, n_keys):
            a = _dot(vt_ref[kvh], p[first:first + n, :])
            acc = a if acc is None else acc + a
        ot_buf[kvh, pl.ds(pl.multiple_of(g * SWA_HEAD_DIM, SWA_HEAD_DIM), SWA_HEAD_DIM), :] = acc / den

    _pipelined_heads(SWA_HEADS, put_scores, finish, s_bufs)
    for kvh in range(SWA_KV_HEADS):
        o_ref[:, kvh * SWA_GW:(kvh + 1) * SWA_GW] = ot_buf[kvh].T.astype(BF16)


def _swa_attn_scratch(n_keys, tq):
    return [pltpu.VMEM((SWA_KV_HEADS, SWA_GW, tq), F32)] + [pltpu.VMEM((n_keys, tq), F32)] * 5


def _swa_attn_prompt_kernel(q_ref, kx_ref, vt_ref, sink_ref, o_ref):
    group = lax.broadcasted_iota(jnp.int32, (q_ref.shape[1], SWA_GW), 1) // SWA_HEAD_DIM
    scores, values_t, sinks = [], [], []
    for kvh in range(SWA_KV_HEADS):
        q_all = q_ref[kvh].astype(F32)
        for g in range(SWA_GROUP):
            q = jnp.where(group == g, q_all, 0.0).astype(BF16)
            scores.append(_dot_nt(kx_ref[kvh], q))
            values_t.append(vt_ref[kvh])
            sinks.append(jnp.full((1, 1), sink_ref[kvh * SWA_GROUP + g] * (1.0 / SWA_SCALE), F32))
    outs = _staged_attention(scores, values_t, SWA_SCALE, sinks)
    for kvh in range(SWA_KV_HEADS):
        ot = jnp.concatenate(outs[kvh * SWA_GROUP:(kvh + 1) * SWA_GROUP], axis=0)
        o_ref[:, kvh * SWA_GW:(kvh + 1) * SWA_GW] = ot.T.astype(BF16)


def _swa_attn_prompt(q, kx, vt, sink, batch, seq):
    wq = SWA_HEADS * SWA_HEAD_DIM
    rows = pl.BlockSpec((SWA_KV_HEADS, seq, SWA_GW), lambda b: (0, b, 0))
    return pl.pallas_call(
        _swa_attn_prompt_kernel,
        grid=(batch,),
        in_specs=[rows, rows, pl.BlockSpec((SWA_KV_HEADS, SWA_HEAD_DIM, seq), lambda b: (0, 0, b)),
                  pl.BlockSpec(memory_space=pltpu.SMEM)],
        out_specs=pl.BlockSpec((seq, wq), lambda b: (b, 0)),
        out_shape=jax.ShapeDtypeStruct((batch * seq, wq), BF16),
        compiler_params=_params("arbitrary"),
        name="swa_attn_prompt",
    )(q, kx, vt, sink)


def _swa_attn_latent_kernel(q_ref, kc_ref, vc_ref, kp_ref, vp_ref, km_ref, vm_ref, kn_ref, vn_ref,
                            sink_ref, o_ref, ot_buf, s_a, s_b, s_c, s_d, bias_ref, *, tq, dec_seq, past):
    i = pl.program_id(1)
    qpos = i * tq + lax.broadcasted_iota(jnp.int32, (1, tq), 1)

    def band(first, n):
        kpos = first + lax.broadcasted_iota(jnp.int32, (n, 1), 0)
        valid = (jnp.abs(qpos - kpos) <= SWA_WINDOW) & (kpos >= 0) & (kpos < dec_seq)
        return jnp.where(valid, 0.0, NEG_INF)

    bias_ref[...] = jnp.concatenate(
        [jnp.zeros((past, tq), F32), band(i * tq - SWA_WINDOW, SWA_WINDOW), band(i * tq, tq),
         band((i + 1) * tq, SWA_WINDOW)], axis=0)
    srcs = [(kc_ref, vc_ref), (kp_ref, vp_ref), (km_ref, vm_ref), (kn_ref, vn_ref)]
    _swa_heads(q_ref, sink_ref, o_ref, srcs, bias_ref, ot_buf, (s_a, s_b, s_c, s_d))


def _swa_attn_latent(q, kx, vt, kx_ctx, vt_ctx, sink, n_prompt, dec_batch, dec_seq, past, tq):
    t = q.shape[1]
    wq = SWA_HEADS * SWA_HEAD_DIM
    nq = dec_seq // tq
    w = SWA_WINDOW
    n_keys = past + tq + 2 * w
    first = lambda b, i: n_prompt + b * dec_seq + i * tq
    prev = lambda b, i: first(b, i) // w - 1
    nxt = lambda b, i: jnp.minimum((first(b, i) + tq) // w, t // w - 1)
    rows = lambda n, blk: pl.BlockSpec((SWA_KV_HEADS, n, SWA_GW), lambda b, i: (0, blk(b, i), 0))
    cols = lambda n, blk: pl.BlockSpec((SWA_KV_HEADS, SWA_HEAD_DIM, n), lambda b, i: (0, 0, blk(b, i)))
    main = lambda b, i: first(b, i) // tq
    ctx = lambda b, i: b
    return pl.pallas_call(
        functools.partial(_swa_attn_latent_kernel, tq=tq, dec_seq=dec_seq, past=past),
        grid=(dec_batch, nq),
        in_specs=[rows(tq, main), rows(past, ctx), cols(past, ctx), rows(w, prev), cols(w, prev),
                  rows(tq, main), cols(tq, main), rows(w, nxt), cols(w, nxt),
                  pl.BlockSpec(memory_space=pltpu.SMEM)],
        out_specs=pl.BlockSpec((tq, wq), lambda b, i: (b * nq + i, 0)),
        out_shape=jax.ShapeDtypeStruct((dec_batch * dec_seq, wq), BF16),
        scratch_shapes=_swa_attn_scratch(n_keys, tq),
        compiler_params=_params("arbitrary", "arbitrary"),
        name="swa_attn_latent",
    )(q, kx_ctx, vt_ctx, kx, vt, kx, vt, kx, vt, sink)


def _hgrn_pre_kernel(xp_ref, xl_ref, mod_ref, gn_ref, w_ref, lbl_ref, q_ref, v_ref, g_ref, f_ref, qmax_ref, *,
                     layer, prompt_tiles):
    is_prompt = pl.program_id(0) < prompt_tiles
    m = mod_ref[0]
    h = _modulate(_read_rows(is_prompt, xp_ref, xl_ref), gn_ref[...], m[0:1], m[1:2]).astype(BF16)
    y = _dot(h, w_ref[...])
    n = HG_HEADS * HG_DK
    q = _silu(y[:, :n])
    q_ref[...] = q
    qmax_ref[0] = jnp.max(jnp.abs(q), axis=0, keepdims=True)
    v_ref[...] = y[:, n:2 * n].astype(BF16)
    g_ref[...] = _silu(y[:, 2 * n:3 * n])
    for d in range(2):
        logits = lbl_ref[d]
        e = jnp.exp(logits - jnp.max(logits, axis=0, keepdims=True))
        s = e / jnp.sum(e, axis=0, keepdims=True)
        cs = s[0:1]
        for r in range(1, layer + 1):
            cs = cs + s[r:r + 1]
        lb = cs - s[0:1]
        f_ref[d] = lb + (1.0 - lb) * _sigmoid(y[:, (3 + d) * n:(4 + d) * n])


def _hgrn_pre(x, mods, gn, w5, lb_logits, layer, rows):
    d = x[0].shape[1]
    t = x[0].shape[0] + x[1].shape[0]
    n = HG_HEADS * HG_DK
    return pl.pallas_call(
        functools.partial(_hgrn_pre_kernel, layer=layer, prompt_tiles=rows.prompt_tiles),
        grid=(rows.n_tiles,),
        in_specs=rows.x_specs(d) + [rows.mod_spec(d), _whole(gn.shape), _whole(w5.shape), _whole(lb_logits.shape)],
        out_specs=[rows.row_spec(n), rows.row_spec(n), rows.row_spec(n),
                   pl.BlockSpec((2, rows.tm, n), lambda i: (0, i, 0)), pl.BlockSpec((1, 1, n), lambda i: (i, 0, 0))],
        out_shape=[jax.ShapeDtypeStruct((t, n), F32), jax.ShapeDtypeStruct((t, n), BF16),
                   jax.ShapeDtypeStruct((t, n), F32), jax.ShapeDtypeStruct((2, t, n), F32),
                   jax.ShapeDtypeStruct((rows.n_tiles, 1, n), F32)],
        compiler_params=_params("arbitrary"),
        name="hgrn_pre",
    )(*x, mods, gn, w5, lb_logits)


def _tri_cumsum(tri, x):
    hi = x.astype(BF16)
    r1 = x - hi.astype(F32)
    mid = r1.astype(BF16)
    lo = (r1 - mid.astype(F32)).astype(BF16)
    return _dot(tri, hi) + _dot(tri, mid) + _dot(tri, lo)


def _hgrn_tile(q_ref, v_ref, f_ref, o_ref, st_ref, *, rows, reverse, bounded):
    r = rows
    diag = HG_DIAG_BLOCK if bounded else 1
    a = lax.broadcasted_iota(jnp.int32, (r, r), 0)
    b = lax.broadcasted_iota(jnp.int32, (r, r), 1)
    seen = (b >= a) if reverse else (b <= a)
    tri = jnp.where(seen, 1.0, 0.0).astype(BF16)

    q = q_ref[...]
    f = f_ref[0]
    vb = v_ref[...].astype(BF16)
    lf = jnp.log(f)
    cum = _tri_cumsum(tri, lf)
    tot = cum[0:1, :] if reverse else cum[r - 1:r, :]
    kk = 1.0 - f
    q_in = (q * jnp.exp(cum)).astype(BF16)
    k_d = (kk * jnp.exp(tot - cum)).astype(BF16)
    e_tot = jnp.exp(tot)

    levels = []
    c = r // 2
    while c >= diag:
        q_half = (a % (2 * c) < c) if reverse else (a % (2 * c) >= c)
        k_half = (b % (2 * c) >= c) if reverse else (b % (2 * c) < c)
        levels.append((2 * c, c if reverse else c - 1, ((a // (2 * c)) == (b // (2 * c))) & q_half & k_half))
        c //= 2
    levels.append((diag, diag // 2, ((a // diag) == (b // diag)) & seen))

    factors = []
    for size, ref_row, own in levels:
        if size == 1:
            factors.append((q.astype(BF16), kk.astype(BF16), own))
            continue
        if size >= 8:
            ref = jnp.concatenate(
                [jnp.broadcast_to(cum[j * size + ref_row:j * size + ref_row + 1, :], (size, cum.shape[1]))
                 for j in range(r // size)], axis=0)
        else:
            ref = _tri_cumsum(jnp.where(b == (a // size) * size + ref_row, 1.0, 0.0).astype(BF16), cum)
        factors.append(((q * jnp.exp(cum - ref)).astype(BF16), (kk * jnp.exp(ref - cum)).astype(BF16), own))

    outs = []
    for h in range(HG_HEADS):
        sl = slice(h * HG_DK, (h + 1) * HG_DK)
        att = jnp.zeros((r, r), F32)
        for q_l, k_l, own in factors:
            att = jnp.where(own, _dot_nt(q_l[:, sl], k_l[:, sl]), att)
        st = st_ref[h]
        outs.append(_dot_nt(q_in[:, sl], st.astype(BF16)) + _dot(att.astype(BF16), vb[:, sl]))
        st_ref[h] = st * e_tot[:, sl] + _dot_tn(vb[:, sl], k_d[:, sl])
    o_ref[0] = jnp.concatenate(outs, axis=1)


def _hgrn_scan_kernel(*refs, rows, has_init, bounded):
    if has_init:
        q_ref, v_ref, f_ref, s0_ref, o_ref, sfin_ref, st_ref = refs
    else:
        q_ref, v_ref, f_ref, o_ref, sfin_ref, st_ref = refs
    d = pl.program_id(1)
    t = pl.program_id(2)

    @pl.when(t == 0)
    def _():
        for h in range(HG_HEADS):
            st_ref[h] = s0_ref[0, 0, h].T if has_init else jnp.zeros((HG_DV, HG_DK), F32)

    for reverse in (False, True):
        @pl.when(d == int(reverse))
        def _():
            _hgrn_tile(q_ref, v_ref, f_ref, o_ref, st_ref, rows=rows, reverse=reverse, bounded=bounded)

    @pl.when(t == pl.num_programs(2) - 1)
    def _():
        for h in range(HG_HEADS):
            sfin_ref[0, 0, h] = st_ref[h].T


def _hgrn_scan(q, v, f2, batch, seq, first_row, rt, s0=None, bounded=True):
    n = q.shape[1]
    nt = seq // rt
    has_init = s0 is not None

    def local(b, d, i):
        return b * nt + jnp.where(d == 0, i, nt - 1 - i)

    def slab(b, d, i):
        return first_row // rt + local(b, d, i)

    row = pl.BlockSpec((rt, n), lambda b, d, i: (slab(b, d, i), 0))
    state = pl.BlockSpec((1, 1, HG_HEADS, HG_DK, HG_DV), lambda b, d, i: (b, d, 0, 0, 0))
    ins = [q, v, f2]
    in_specs = [row, row, pl.BlockSpec((1, rt, n), lambda b, d, i: (d, slab(b, d, i), 0))]
    if has_init:
        ins.append(s0)
        in_specs.append(state)
    return pl.pallas_call(
        functools.partial(_hgrn_scan_kernel, rows=rt, has_init=has_init, bounded=bounded),
        grid=(batch, 2, nt),
        in_specs=in_specs,
        out_specs=[pl.BlockSpec((1, rt, n), lambda b, d, i: (d, local(b, d, i), 0)), state],
        out_shape=[jax.ShapeDtypeStruct((2, batch * seq, n), F32),
                   jax.ShapeDtypeStruct((batch, 2, HG_HEADS, HG_DK, HG_DV), F32)],
        scratch_shapes=[pltpu.VMEM((HG_HEADS, HG_DV, HG_DK), F32)],
        compiler_params=_params("arbitrary", "arbitrary", "arbitrary"),
        name="hgrn_scan_latent" if has_init else "hgrn_scan_prompt",
    )(*ins)


def _post_kernel(*refs, hgrn, final, prompt_tiles):
    refs = list(refs)
    outl_ref = refs.pop()
    outp_ref = refs.pop()
    is_prompt = pl.program_id(0) < prompt_tiles
    if hgrn:
        odp_ref, odl_ref, g_ref, onorm_ref = refs[:4]
        refs = refs[4:]
        o2 = jnp.where(is_prompt, odp_ref[0] + odp_ref[1], odl_ref[0] + odl_ref[1])
        gate = g_ref[...]
        onorm = onorm_ref[...]
        parts = []
        for h in range(HG_HEADS):
            sl = slice(h * HG_DV, (h + 1) * HG_DV)
            parts.append(_rms(o2[:, sl], onorm) * gate[:, sl])
        o = jnp.concatenate(parts, axis=1).astype(BF16)
    else:
        o = jnp.where(is_prompt, refs[0][...], refs[1][...])
        refs = refs[2:]
    xp_ref, xl_ref, mod_ref, wo_ref, gn_ref, wg_ref, wu_ref, wd_ref = refs[:8]
    m = mod_ref[0]
    x1 = _read_rows(is_prompt, xp_ref, xl_ref) + m[2:3] * _dot(o, wo_ref[...])
    h2 = _modulate(x1, gn_ref[...], m[3:4], m[4:5]).astype(BF16)
    a = (_silu(_dot(h2, wg_ref[0])) * _dot(h2, wu_ref[0])).astype(BF16)
    x2 = x1 + m[5:6] * _dot(a, wd_ref[0])
    if final:
        x2 = _rms(x2, refs[8][...])

    @pl.when(is_prompt)
    def _():
        outp_ref[...] = x2

    @pl.when(jnp.logical_not(is_prompt))
    def _():
        outl_ref[...] = x2


def _layer_of(stack, layer):
    shape = (1,) + stack.shape[1:]
    return pl.BlockSpec(shape, lambda *_: (layer,) + (0,) * (len(shape) - 1), pipeline_mode=pl.Buffered(1))


def _post(mix, x, mods, wo, gn, wg, wu, wd, layer, rows, final_norm=None, hgrn=False):
    d = x[0].shape[1]
    tm = rows.tm
    if hgrn:
        od_p, od_l, gate, onorm = mix
        head = [od_p, od_l, gate, onorm]
        head_specs = [pl.BlockSpec((2, tm, d), lambda i: (0, jnp.minimum(i, rows.prompt_tiles - 1), 0)),
                      pl.BlockSpec((2, tm, d), lambda i: (0, jnp.maximum(i - rows.prompt_tiles, 0), 0)),
                      rows.row_spec(d), _whole(onorm.shape)]
    else:
        head = list(mix)
        head_specs = [rows.prompt_spec(mix[0].shape[1]), rows.latent_spec(mix[1].shape[1])]
    ins = head + [*x, mods, wo, gn, wg, wu, wd]
    in_specs = (head_specs + rows.x_specs(d) + [rows.mod_spec(d), _whole(wo.shape), _whole(gn.shape)]
                + [_layer_of(a, layer) for a in (wg, wu, wd)])
    if final_norm is not None:
        ins.append(final_norm)
        in_specs.append(_whole(final_norm.shape))
    return pl.pallas_call(
        functools.partial(_post_kernel, hgrn=hgrn, final=final_norm is not None, prompt_tiles=rows.prompt_tiles),
        grid=(rows.n_tiles,),
        in_specs=in_specs,
        out_specs=rows.x_specs(d),
        out_shape=[jax.ShapeDtypeStruct(x[0].shape, F32), jax.ShapeDtypeStruct(x[1].shape, F32)],
        compiler_params=_params("arbitrary"),
        name="post",
    )(*ins)


def _pick_tile(n_prompt_rows, dec_seq, want):
    tm = want
    while n_prompt_rows % tm or dec_seq % tm:
        tm //= 2
    return tm


def kernel(x_prompt, x_sample, cache_mla_ckv, cache_mla_krope, state_hgrn, cache_swa_k, cache_swa_v, c, c_ctx, ada_w, ada_b, norm_mix, norm_ffn, ffn_w_gate, ffn_w_up, ffn_w_down, final_norm, mla_w_dq, mla_q_norm, mla_w_uq, mla_w_dkv, mla_kv_norm, mla_w_uk, mla_w_uv, mla_w_o, hg_w_q, hg_w_f, hg_w_i, hg_w_g, hg_o_norm, hg_w_o, hg_lb_logits, swa_w_q, swa_w_k, swa_w_v, swa_w_o, swa_sink):
    batch, seq, d = x_prompt.shape
    dec_batch, dec_seq, _ = x_sample.shape
    past = cache_mla_ckv.shape[2]
    depth = ada_w.shape[0]
    n_prompt = batch * seq
    n_rows = n_prompt + dec_batch * dec_seq
    assert dec_batch + 1 <= COND_ROWS and seq % SWA_WINDOW == 0 and dec_seq % (2 * SWA_WINDOW) == 0
    assert n_prompt % dec_seq == 0

    pre_rows = _Rows(n_prompt, dec_seq, n_rows, _pick_tile(n_prompt, dec_seq, 256))
    post_rows = _Rows(n_prompt, dec_seq, n_rows, _pick_tile(n_prompt, dec_seq, 512))
    tq = _pick_tile(n_prompt, dec_seq, 256)
    cos64, sin64 = _rope_tables(dec_seq, pre_rows.tm)
    n_tab = cos64.shape[0]
    cos_mla = jnp.concatenate([cos64, jnp.ones((n_tab, ROPE_PERIOD), F32)], axis=1)
    sin_mla = jnp.concatenate([sin64, jnp.zeros((n_tab, ROPE_PERIOD), F32)], axis=1)
    cos_swa, sin_swa = jnp.tile(cos64, (1, 2)), jnp.tile(sin64, (1, 2))

    cond = jnp.concatenate([c_ctx[None, :], c, jnp.zeros((COND_ROWS - 1 - dec_batch, d), F32)], axis=0)
    mods = _adaln(cond, ada_w, ada_b).reshape(depth, COND_ROWS, 6, d)

    x = (x_prompt.reshape(n_prompt, d), x_sample.reshape(dec_batch * dec_seq, d))
    ffn_wg, ffn_wu, ffn_wd = ffn_w_gate.astype(BF16), ffn_w_up.astype(BF16), ffn_w_down.astype(BF16)
    row1 = lambda a: a.reshape(1, -1)
    new_ckv, new_krope, new_hg, new_k, new_v = [], [], [], [], []
    for i in range(depth):
        kind, j = i % N_MIXERS, i // N_MIXERS
        gn = row1(norm_mix[i])
        if kind == 0:
            uq = mla_w_uq[j].reshape(-1, MLA_HEADS, MLA_NOPE_DIM + MLA_ROPE_DIM)
            uq_rope = jnp.pad(uq[:, :, MLA_NOPE_DIM:], ((0, 0), (0, 0), (0, LANES - MLA_ROPE_DIM)))
            uq = jnp.concatenate([uq[:, :, :MLA_NOPE_DIM].reshape(uq.shape[0], -1),
                                  uq_rope.reshape(uq.shape[0], -1)], axis=1)
            w = {
                "dq": mla_w_dq[j].astype(BF16), "q_norm": row1(mla_q_norm[j]), "uq": uq.astype(BF16),
                "dkv": jnp.pad(mla_w_dkv[j], ((0, 0), (0, LANES - MLA_ROPE_DIM))).astype(BF16),
                "kv_norm": row1(mla_kv_norm[j]),
                "uk": mla_w_uk[j].astype(BF16), "uv": mla_w_uv[j].astype(BF16),
            }
            q, ckv, kr_raw, kcat, vt = _mla_pre(x, mods[i], gn, cos_mla, sin_mla, w, pre_rows)
            kr_ctx = jnp.pad(cache_mla_krope[:, j].reshape(dec_batch * past, -1), ((0, 0), (0, LANES - MLA_ROPE_DIM)))
            kcat_ctx, vt_ctx = _mla_expand(cache_mla_ckv[:, j].reshape(dec_batch * past, -1), kr_ctx, w["uk"], w["uv"])
            mix = (_mla_attn_prompt(q, kcat, vt, batch, seq),
                   _mla_attn_latent(q, kcat, vt, kcat_ctx, vt_ctx, n_prompt, dec_batch, dec_seq, past, tq))
            wo = mla_w_o[j].astype(BF16)
            new_ckv.append(ckv.reshape(batch, seq, -1))
            new_krope.append(kr_raw.reshape(batch, seq, -1))
        elif kind == 1:
            w5 = jnp.concatenate([hg_w_q[j], hg_w_i[j], hg_w_g[j], hg_w_f[j, 0], hg_w_f[j, 1]], axis=1).astype(BF16)
            q, v, gate, f2, q_max = _hgrn_pre(x, mods[i], gn, w5, hg_lb_logits, i, pre_rows)
            rt = _pick_tile(seq, dec_seq, 256)
            sm = jax.nn.softmax(hg_lb_logits.astype(F32), axis=1)
            lb_min = jnp.min(jnp.cumsum(sm, axis=1)[:, i] - sm[:, 0])
            worst = (HG_DIAG_BLOCK // 2) * -jnp.log(lb_min) + jnp.log(jnp.maximum(jnp.max(q_max), 1.0))
            fits = worst < HG_MAX_EXPONENT

            def scans(bounded):
                od_p, s_p = _hgrn_scan(q, v, f2, batch, seq, 0, rt, bounded=bounded)
                od_l, _ = _hgrn_scan(q, v, f2, dec_batch, dec_seq, n_prompt, rt, s0=state_hgrn[:, j], bounded=bounded)
                return od_p, od_l, s_p

            od_p, od_l, s_prompt = lax.cond(fits, lambda: scans(True), lambda: scans(False))
            mix = (od_p, od_l, gate, row1(hg_o_norm[j]))
            wo = hg_w_o[j].astype(BF16)
            new_hg.append(s_prompt)
        else:
            wqkv = jnp.concatenate([swa_w_q[j], swa_w_k[j], swa_w_v[j]], axis=1).astype(BF16)
            q, kx, vt, k_raw, v_raw = _swa_pre(x, mods[i], gn, cos_swa, sin_swa, wqkv, pre_rows)
            sink = swa_sink[j]
            k_ctx = cache_swa_k[:, j].reshape(dec_batch * past, SWA_KV_HEADS, 1, SWA_HEAD_DIM).transpose(1, 0, 2, 3)
            kx_ctx = jnp.broadcast_to(k_ctx, (SWA_KV_HEADS, dec_batch * past, SWA_GROUP, SWA_HEAD_DIM))
            kx_ctx = kx_ctx.reshape(SWA_KV_HEADS, dec_batch * past, SWA_GW).astype(BF16)
            vt_ctx = cache_swa_v[:, j].reshape(dec_batch * past, SWA_KV_HEADS, SWA_HEAD_DIM).transpose(1, 2, 0).astype(BF16)
            mix = (_swa_attn_prompt(q, kx, vt, sink, batch, seq),
                   _swa_attn_latent(q, kx, vt, kx_ctx, vt_ctx, sink, n_prompt, dec_batch, dec_seq, past, tq))
            wo = swa_w_o[j].astype(BF16)
            new_k.append(k_raw.reshape(batch, seq, SWA_KV_HEADS, SWA_HEAD_DIM))
            new_v.append(v_raw.reshape(batch, seq, SWA_KV_HEADS, SWA_HEAD_DIM))
        x = _post(mix, x, mods[i], wo, row1(norm_ffn[i]), ffn_wg, ffn_wu, ffn_wd, i,
                  pre_rows if kind == 1 else post_rows,
                  final_norm=row1(final_norm) if i == depth - 1 else None, hgrn=kind == 1)
    y_prompt = x[0].reshape(batch, seq, d)
    y_sample = x[1].reshape(dec_batch, dec_seq, d)
    return (y_prompt, y_sample, jnp.stack(new_ckv, axis=1), jnp.stack(new_krope, axis=1),
            jnp.stack(new_hg, axis=1), jnp.stack(new_k, axis=1), jnp.stack(new_v, axis=1))
```

```python
import functools

import jax
import jax.numpy as jnp
from jax import lax
from jax.experimental import pallas as pl
from jax.experimental.pallas import tpu as pltpu

F32 = jnp.float32
BF16 = jnp.bfloat16

GRID_W = 64
N_MIXERS = 3

MLA_HEADS = 8
MLA_KV_LORA = 256
MLA_NOPE_DIM = 128
MLA_ROPE_DIM = 64
MLA_V_DIM = 128
MLA_QK_PAD = 256
MLA_SCALE = (MLA_NOPE_DIM + MLA_ROPE_DIM) ** -0.5

HG_HEADS = 8
HG_DK = 128
HG_DV = 128
HG_DIAG_BLOCK = 32
HG_MAX_EXPONENT = 80.0

SWA_HEADS = 16
SWA_KV_HEADS = 4
SWA_GROUP = SWA_HEADS // SWA_KV_HEADS
SWA_HEAD_DIM = 64
SWA_GW = SWA_GROUP * SWA_HEAD_DIM
SWA_WINDOW = 128
SWA_SCALE = SWA_HEAD_DIM ** -0.5

ROPE_BASE = 10000.0
ROPE_PERIOD = 64
ROPE_QUARTER = 16
NORM_EPS = 1e-6
NEG_INF = -1e30
LOG2_E = 1.4426950408889634

LANES = 128
COND_ROWS = 8
VMEM_LIMIT = 56 * 1024 * 1024


def _sigmoid(x):
    return jax.nn.sigmoid(x)


def _silu(x):
    return x * jax.nn.sigmoid(x)


def _rms(x, g):
    return x * lax.rsqrt(jnp.mean(x * x, axis=-1, keepdims=True) + NORM_EPS) * g


def _modulate(x, g, shift, scale):
    return _rms(x, g) * (1.0 + scale) + shift


def _dot(a, b):
    return jnp.dot(a, b, preferred_element_type=F32)


def _dot_nt(a, b):
    return lax.dot_general(a, b, (((1,), (1,)), ((), ())), preferred_element_type=F32)


def _dot_tn(a, b):
    return lax.dot_general(a, b, (((0,), (0,)), ((), ())), preferred_element_type=F32)


def _swap_pairs(x):
    n = x.shape[1]
    lane = lax.broadcasted_iota(jnp.int32, x.shape, 1)
    ahead = pltpu.roll(x, n - ROPE_QUARTER, 1)
    behind = pltpu.roll(x, ROPE_QUARTER, 1)
    return jnp.where((lane & (2 * ROPE_QUARTER - 1)) < ROPE_QUARTER, ahead, behind)


def _rope(x, cos, sin):
    reps = x.shape[1] // cos.shape[1]
    if reps > 1:
        cos = jnp.concatenate([cos] * reps, axis=1)
        sin = jnp.concatenate([sin] * reps, axis=1)
    return x * cos + _swap_pairs(x) * sin


def _whole(shape):
    zeros = (0,) * len(shape)
    return pl.BlockSpec(shape, lambda *_: zeros, pipeline_mode=pl.Buffered(1))


def _params(*sem):
    return pltpu.CompilerParams(dimension_semantics=sem, vmem_limit_bytes=VMEM_LIMIT)


class _Rows:
    def __init__(self, n_prompt_rows, dec_seq, n_rows, tm):
        assert n_prompt_rows % tm == 0 and dec_seq % tm == 0
        self.tm = tm
        self.n_tiles = n_rows // tm
        self.prompt_tiles = n_prompt_rows // tm
        self.seq_tiles = dec_seq // tm

    def cond(self, i):
        return jnp.where(i < self.prompt_tiles, 0, 1 + jnp.maximum(i - self.prompt_tiles, 0) // self.seq_tiles)

    def rope_block(self, i):
        return jnp.where(i < self.prompt_tiles, 0, 1 + jnp.maximum(i - self.prompt_tiles, 0) % self.seq_tiles)

    def row_spec(self, width):
        return pl.BlockSpec((self.tm, width), lambda i: (i, 0))

    def col_spec(self, height):
        return pl.BlockSpec((height, self.tm), lambda i: (0, i))

    def mod_spec(self, d):
        return pl.BlockSpec((1, 6, d), lambda i: (self.cond(i), 0, 0))

    def rope_spec(self, width):
        return pl.BlockSpec((self.tm, width), lambda i: (self.rope_block(i), 0))

    def x_specs(self, width):
        return [self.prompt_spec(width), self.latent_spec(width)]

    def is_prompt(self):
        return pl.program_id(0) < self.prompt_tiles

    def prompt_spec(self, width):
        return pl.BlockSpec((self.tm, width), lambda i: (jnp.minimum(i, self.prompt_tiles - 1), 0))

    def latent_spec(self, width):
        return pl.BlockSpec((self.tm, width), lambda i: (jnp.maximum(i - self.prompt_tiles, 0), 0))


def _rope_tables(dec_seq, tm):
    pos = jnp.arange(dec_seq)
    row = (pos // GRID_W).astype(F32)
    col = (pos % GRID_W).astype(F32)
    inv_freq = ROPE_BASE ** (-jnp.arange(ROPE_QUARTER, dtype=F32) / ROPE_QUARTER)
    ang_r = row[:, None] * inv_freq[None, :]
    ang_c = col[:, None] * inv_freq[None, :]
    cos = jnp.concatenate([jnp.cos(ang_r), jnp.cos(ang_r), jnp.cos(ang_c), jnp.cos(ang_c)], axis=1)
    sin = jnp.concatenate([-jnp.sin(ang_r), jnp.sin(ang_r), -jnp.sin(ang_c), jnp.sin(ang_c)], axis=1)
    cos = jnp.concatenate([jnp.ones((tm, ROPE_PERIOD), F32), cos], axis=0)
    sin = jnp.concatenate([jnp.zeros((tm, ROPE_PERIOD), F32), sin], axis=0)
    return cos, sin


def _adaln_kernel(c_ref, w_ref, b_ref, o_ref):
    o_ref[0] = _dot(_silu(c_ref[...]), w_ref[0]) + b_ref[0]


def _adaln(cond, ada_w, ada_b, tn=1536):
    depth, d, n = ada_w.shape
    return pl.pallas_call(
        _adaln_kernel,
        grid=(depth, n // tn),
        in_specs=[pl.BlockSpec((COND_ROWS, d), lambda l, j: (0, 0)),
                  pl.BlockSpec((1, d, tn), lambda l, j: (l, 0, j)),
                  pl.BlockSpec((1, 1, tn), lambda l, j: (l, 0, j))],
        out_specs=pl.BlockSpec((1, COND_ROWS, tn), lambda l, j: (l, 0, j)),
        out_shape=jax.ShapeDtypeStruct((depth, COND_ROWS, n), F32),
        compiler_params=_params("arbitrary", "arbitrary"),
        name="adaln",
    )(cond, ada_w, ada_b.reshape(depth, 1, n))


def _mla_store_heads(q_ref, kcat_ref, vt_ref, qn, qr_pad, kn, kr_pad, v):
    vt = v.T.astype(BF16)
    for h in range(MLA_HEADS):
        nope = slice(h * MLA_NOPE_DIM, (h + 1) * MLA_NOPE_DIM)
        if q_ref is not None:
            q_ref[h] = jnp.concatenate([qn[:, nope], qr_pad[:, h * LANES:(h + 1) * LANES]], axis=1)
        kcat_ref[h] = jnp.concatenate([kn[:, nope], kr_pad], axis=1)
        vt_ref[h] = vt[h * MLA_V_DIM:(h + 1) * MLA_V_DIM, :]


def _read_rows(is_prompt, p_ref, l_ref):
    return jnp.where(is_prompt, p_ref[...], l_ref[...])


def _mla_pre_kernel(xp_ref, xl_ref, mod_ref, gn_ref, cos_ref, sin_ref, wdq_ref, qnorm_ref, wuq_ref,
                    wdkv_ref, kvnorm_ref, wuk_ref, wuv_ref,
                    q_ref, ckv_ref, krraw_ref, kcat_ref, vt_ref, *, prompt_tiles):
    is_prompt = pl.program_id(0) < prompt_tiles
    m = mod_ref[0]
    h = _modulate(_read_rows(is_prompt, xp_ref, xl_ref), gn_ref[...], m[0:1], m[1:2]).astype(BF16)
    q_lat = _rms(_dot(h, wdq_ref[...]), qnorm_ref[...]).astype(BF16)
    cos, sin = cos_ref[...], sin_ref[...]
    nn = MLA_HEADS * MLA_NOPE_DIM
    qn = _dot(q_lat, wuq_ref[:, :nn]).astype(BF16)
    qr_pad = _rope(_dot(q_lat, wuq_ref[:, nn:]), cos, sin).astype(BF16)
    kv = _dot(h, wdkv_ref[...])
    ckv = _rms(kv[:, :MLA_KV_LORA], kvnorm_ref[...])
    kr = kv[:, MLA_KV_LORA:]

    @pl.when(is_prompt)
    def _():
        ckv_ref[...] = ckv
        krraw_ref[...] = kr[:, :MLA_ROPE_DIM]

    kr_pad = _rope(kr, cos, sin).astype(BF16)
    cb = ckv.astype(BF16)
    _mla_store_heads(q_ref, kcat_ref, vt_ref, qn, qr_pad, _dot(cb, wuk_ref[...]).astype(BF16), kr_pad,
                     _dot(cb, wuv_ref[...]))


def _mla_pre(x, mods, gn, cos, sin, w, rows):
    n_prompt, d = x[0].shape
    t = n_prompt + x[1].shape[0]
    tm = rows.tm
    ins = [*x, mods, gn, cos, sin, w["dq"], w["q_norm"], w["uq"], w["dkv"], w["kv_norm"], w["uk"], w["uv"]]
    in_specs = rows.x_specs(d) + [rows.mod_spec(d), _whole(gn.shape),
                                  rows.rope_spec(cos.shape[1]), rows.rope_spec(sin.shape[1])]
    in_specs += [_whole(a.shape) for a in ins[6:]]
    heads_rows = pl.BlockSpec((MLA_HEADS, tm, MLA_QK_PAD), lambda i: (0, i, 0))
    return pl.pallas_call(
        functools.partial(_mla_pre_kernel, prompt_tiles=rows.prompt_tiles),
        grid=(rows.n_tiles,),
        in_specs=in_specs,
        out_specs=[heads_rows, rows.prompt_spec(MLA_KV_LORA), rows.prompt_spec(MLA_ROPE_DIM), heads_rows,
                   pl.BlockSpec((MLA_HEADS, MLA_V_DIM, tm), lambda i: (0, 0, i))],
        out_shape=[jax.ShapeDtypeStruct((MLA_HEADS, t, MLA_QK_PAD), BF16),
                   jax.ShapeDtypeStruct((n_prompt, MLA_KV_LORA), F32),
                   jax.ShapeDtypeStruct((n_prompt, MLA_ROPE_DIM), F32),
                   jax.ShapeDtypeStruct((MLA_HEADS, t, MLA_QK_PAD), BF16),
                   jax.ShapeDtypeStruct((MLA_HEADS, MLA_V_DIM, t), BF16)],
        compiler_params=_params("arbitrary"),
        name="mla_pre",
    )(*ins)


def _mla_expand_kernel(c_ref, kr_ref, wuk_ref, wuv_ref, kcat_ref, vt_ref):
    cb = c_ref[...].astype(BF16)
    _mla_store_heads(None, kcat_ref, vt_ref, None, None, _dot(cb, wuk_ref[...]).astype(BF16),
                     kr_ref[...].astype(BF16), _dot(cb, wuv_ref[...]))


def _mla_expand(ckv, kr_pad, wuk, wuv):
    n = ckv.shape[0]
    return pl.pallas_call(
        _mla_expand_kernel,
        out_shape=[jax.ShapeDtypeStruct((MLA_HEADS, n, MLA_QK_PAD), BF16),
                   jax.ShapeDtypeStruct((MLA_HEADS, MLA_V_DIM, n), BF16)],
        compiler_params=pltpu.CompilerParams(vmem_limit_bytes=VMEM_LIMIT),
        name="mla_expand",
    )(ckv, kr_pad, wuk, wuv)


def _mla_attn_kernel(*refs, n_src):
    q_ref = refs[0]
    srcs = [(refs[1 + 2 * i], refs[2 + 2 * i]) for i in range(n_src)]
    o_ref, o_buf = refs[1 + 2 * n_src:3 + 2 * n_src]
    s_bufs = refs[3 + 2 * n_src:]
    n_keys = [k_ref.shape[1] for k_ref, _ in srcs]
    starts = [sum(n_keys[:i]) for i in range(n_src)]

    def put_scores(h, s_buf):
        q = q_ref[h]
        for (k_ref, _), first, n in zip(srcs, starts, n_keys):
            s_buf[first:first + n, :] = _dot_nt(k_ref[h], q)

    def finish(h, s_buf):
        s = s_buf[...]
        mx = jnp.max(s, axis=0, keepdims=True)
        p = jnp.exp2((s - mx) * (MLA_SCALE * LOG2_E))
        den = jnp.sum(p, axis=0, keepdims=True)
        p = p.astype(BF16)
        acc = None
        for (_, vt_ref), first, n in zip(srcs, starts, n_keys):
            a = _dot(vt_ref[h], p[first:first + n, :])
            acc = a if acc is None else acc + a
        o_buf[h] = (acc / den).T.astype(BF16)

    _pipelined_heads(MLA_HEADS, put_scores, finish, s_bufs)
    for h in range(MLA_HEADS):
        o_ref[:, h * MLA_V_DIM:(h + 1) * MLA_V_DIM] = o_buf[h]


def _mla_attn_scratch(n_keys, tq):
    return [pltpu.VMEM((MLA_HEADS, tq, MLA_V_DIM), BF16)] + [pltpu.VMEM((n_keys, tq), F32)] * 4


def _pipelined_heads(n_heads, put_scores, finish, bufs):
    a, b, c, d = bufs
    assert n_heads % 4 == 0
    put_scores(0, a)
    put_scores(1, b)

    def quad(j, carry):
        h = 4 * j
        put_scores(h + 2, c)
        put_scores(h + 3, d)
        finish(h, a)
        finish(h + 1, b)
        put_scores(h + 4, a)
        put_scores(h + 5, b)
        finish(h + 2, c)
        finish(h + 3, d)
        return carry

    lax.fori_loop(0, n_heads // 4 - 1, quad, 0)
    h = n_heads - 4
    put_scores(h + 2, c)
    put_scores(h + 3, d)
    finish(h, a)
    finish(h + 1, b)
    finish(h + 2, c)
    finish(h + 3, d)


def _staged_attention(scores, values_t, scale, sinks=None):
    c = scale * LOG2_E
    mx = [jnp.max(s, axis=0, keepdims=True) for s in scores]
    if sinks is not None:
        mx = [jnp.maximum(m, z) for m, z in zip(mx, sinks)]
    p = [jnp.exp2((s - m) * c) for s, m in zip(scores, mx)]
    den = [jnp.sum(x, axis=0, keepdims=True) for x in p]
    if sinks is not None:
        den = [d + jnp.exp2((z - m) * c) for d, z, m in zip(den, sinks, mx)]
    acc = [_dot(v, x.astype(BF16)) for v, x in zip(values_t, p)]
    return [a / d for a, d in zip(acc, den)]


def _mla_attn_prompt_kernel(q_ref, k_ref, vt_ref, o_ref):
    scores = [_dot_nt(k_ref[h], q_ref[h]) for h in range(MLA_HEADS)]
    outs = _staged_attention(scores, [vt_ref[h] for h in range(MLA_HEADS)], MLA_SCALE)
    for h in range(MLA_HEADS):
        o_ref[:, h * MLA_V_DIM:(h + 1) * MLA_V_DIM] = outs[h].T.astype(BF16)


def _mla_attn_prompt(q, kcat, vt, batch, seq):
    hv = MLA_HEADS * MLA_V_DIM
    rows = pl.BlockSpec((MLA_HEADS, seq, MLA_QK_PAD), lambda b: (0, b, 0))
    return pl.pallas_call(
        _mla_attn_prompt_kernel,
        grid=(batch,),
        in_specs=[rows, rows, pl.BlockSpec((MLA_HEADS, MLA_V_DIM, seq), lambda b: (0, 0, b))],
        out_specs=pl.BlockSpec((seq, hv), lambda b: (b, 0)),
        out_shape=jax.ShapeDtypeStruct((batch * seq, hv), BF16),
        compiler_params=_params("arbitrary"),
        name="mla_attn_prompt",
    )(q, kcat, vt)


def _mla_attn_latent(q, kcat, vt, kcat_ctx, vt_ctx, n_prompt, dec_batch, dec_seq, past, tq):
    hv = MLA_HEADS * MLA_V_DIM
    nq = dec_seq // tq
    lat_blk = n_prompt // dec_seq
    return pl.pallas_call(
        functools.partial(_mla_attn_kernel, n_src=2),
        grid=(dec_batch, nq),
        in_specs=[pl.BlockSpec((MLA_HEADS, tq, MLA_QK_PAD), lambda b, i: (0, n_prompt // tq + b * nq + i, 0)),
                  pl.BlockSpec((MLA_HEADS, past, MLA_QK_PAD), lambda b, i: (0, b, 0)),
                  pl.BlockSpec((MLA_HEADS, MLA_V_DIM, past), lambda b, i: (0, 0, b)),
                  pl.BlockSpec((MLA_HEADS, dec_seq, MLA_QK_PAD), lambda b, i: (0, lat_blk + b, 0)),
                  pl.BlockSpec((MLA_HEADS, MLA_V_DIM, dec_seq), lambda b, i: (0, 0, lat_blk + b))],
        out_specs=pl.BlockSpec((tq, hv), lambda b, i: (b * nq + i, 0)),
        out_shape=jax.ShapeDtypeStruct((dec_batch * dec_seq, hv), BF16),
        scratch_shapes=_mla_attn_scratch(past + dec_seq, tq),
        compiler_params=_params("arbitrary", "arbitrary"),
        name="mla_attn_latent",
    )(q, kcat_ctx, vt_ctx, kcat, vt)


def _tile_heads(x):
    n = x.shape[1]
    block = lax.broadcasted_iota(jnp.int32, x.shape, 1) // SWA_HEAD_DIM
    rolled = [x] + [pltpu.roll(x, s * SWA_HEAD_DIM, 1) for s in range(1, SWA_KV_HEADS)]
    out = []
    for kvh in range(SWA_KV_HEADS):
        blk = rolled[(0 - kvh) % SWA_KV_HEADS]
        for g in range(1, n // SWA_HEAD_DIM):
            blk = jnp.where(block == g, rolled[(g - kvh) % SWA_KV_HEADS], blk)
        out.append(blk)
    return jnp.concatenate(out, axis=1)


def _swa_pre_kernel(xp_ref, xl_ref, mod_ref, gn_ref, cos_ref, sin_ref, wqkv_ref,
                    q_ref, kx_ref, vt_ref, kraw_ref, vraw_ref, *, prompt_tiles):
    is_prompt = pl.program_id(0) < prompt_tiles
    m = mod_ref[0]
    h = _modulate(_read_rows(is_prompt, xp_ref, xl_ref), gn_ref[...], m[0:1], m[1:2]).astype(BF16)
    qkv = _dot(h, wqkv_ref[...])
    nq, nk = SWA_HEADS * SWA_HEAD_DIM, SWA_KV_HEADS * SWA_HEAD_DIM
    cos, sin = cos_ref[...], sin_ref[...]
    q = _rope(qkv[:, :nq], cos, sin).astype(BF16)
    k = qkv[:, nq:nq + nk]
    v = qkv[:, nq + nk:]

    @pl.when(is_prompt)
    def _():
        kraw_ref[...] = k
        vraw_ref[...] = v

    kx = _tile_heads(_rope(k, cos, sin)).astype(BF16)
    vt = v.T.astype(BF16)
    for kvh in range(SWA_KV_HEADS):
        q_ref[kvh] = q[:, kvh * SWA_GW:(kvh + 1) * SWA_GW]
        kx_ref[kvh] = kx[:, kvh * SWA_GW:(kvh + 1) * SWA_GW]
        vt_ref[kvh] = vt[kvh * SWA_HEAD_DIM:(kvh + 1) * SWA_HEAD_DIM, :]


def _swa_pre(x, mods, gn, cos, sin, wqkv, rows):
    n_prompt, d = x[0].shape
    t = n_prompt + x[1].shape[0]
    tm = rows.tm
    nk = SWA_KV_HEADS * SWA_HEAD_DIM
    heads_rows = pl.BlockSpec((SWA_KV_HEADS, tm, SWA_GW), lambda i: (0, i, 0))
    return pl.pallas_call(
        functools.partial(_swa_pre_kernel, prompt_tiles=rows.prompt_tiles),
        grid=(rows.n_tiles,),
        in_specs=rows.x_specs(d) + [rows.mod_spec(d), _whole(gn.shape),
                                    rows.rope_spec(cos.shape[1]), rows.rope_spec(sin.shape[1]), _whole(wqkv.shape)],
        out_specs=[heads_rows, heads_rows, pl.BlockSpec((SWA_KV_HEADS, SWA_HEAD_DIM, tm), lambda i: (0, 0, i)),
                   rows.prompt_spec(nk), rows.prompt_spec(nk)],
        out_shape=[jax.ShapeDtypeStruct((SWA_KV_HEADS, t, SWA_GW), BF16),
                   jax.ShapeDtypeStruct((SWA_KV_HEADS, t, SWA_GW), BF16),
                   jax.ShapeDtypeStruct((SWA_KV_HEADS, SWA_HEAD_DIM, t), BF16),
                   jax.ShapeDtypeStruct((n_prompt, nk), F32), jax.ShapeDtypeStruct((n_prompt, nk), F32)],
        compiler_params=_params("arbitrary"),
        name="swa_pre",
    )(*x, mods, gn, cos, sin, wqkv)


def _swa_heads(q_ref, sink_ref, o_ref, srcs, bias_ref, ot_buf, s_bufs):
    tq = q_ref.shape[1]
    n_keys = [kx_ref.shape[1] for kx_ref, _ in srcs]
    starts = [sum(n_keys[:i]) for i in range(len(srcs))]
    group = lax.broadcasted_iota(jnp.int32, (tq, SWA_GW), 1) // SWA_HEAD_DIM

    def put_scores(hq, s_buf):
        kvh, g = hq // SWA_GROUP, hq % SWA_GROUP
        q = jnp.where(group == g, q_ref[kvh].astype(F32), 0.0).astype(BF16)
        for (kx_ref, _), first, n in zip(srcs, starts, n_keys):
            s_buf[first:first + n, :] = _dot_nt(kx_ref[kvh], q)

    def finish(hq, s_buf):
        kvh, g = hq // SWA_GROUP, hq % SWA_GROUP
        s = s_buf[...]
        if bias_ref is not None:
            s = s + bias_ref[...]
        sink = jnp.full((1, 1), sink_ref[hq] * (1.0 / SWA_SCALE), F32)
        mx = jnp.maximum(jnp.max(s, axis=0, keepdims=True), sink)
        p = jnp.exp2((s - mx) * (SWA_SCALE * LOG2_E))
        den = jnp.sum(p, axis=0, keepdims=True) + jnp.exp2((sink - mx) * (SWA_SCALE * LOG2_E))
        p = p.astype(BF16)
        acc = None
        for (_, vt_ref), first, n in zip(srcs, starts, n_keys):
            a = _dot(vt_ref[kvh], p[first:first + n, :])
            acc = a if acc is None else acc + a
        ot_buf[kvh, pl.ds(pl.multiple_of(g * SWA_HEAD_DIM, SWA_HEAD_DIM), SWA_HEAD_DIM), :] = acc / den

    _pipelined_heads(SWA_HEADS, put_scores, finish, s_bufs)
    for kvh in range(SWA_KV_HEADS):
        o_ref[:, kvh * SWA_GW:(kvh + 1) * SWA_GW] = ot_buf[kvh].T.astype(BF16)


def _swa_attn_scratch(n_keys, tq):
    return [pltpu.VMEM((SWA_KV_HEADS, SWA_GW, tq), F32)] + [pltpu.VMEM((n_keys, tq), F32)] * 5


def _swa_attn_prompt_kernel(q_ref, kx_ref, vt_ref, sink_ref, o_ref):
    group = lax.broadcasted_iota(jnp.int32, (q_ref.shape[1], SWA_GW), 1) // SWA_HEAD_DIM
    scores, values_t, sinks = [], [], []
    for kvh in range(SWA_KV_HEADS):
        q_all = q_ref[kvh].astype(F32)
        for g in range(SWA_GROUP):
            q = jnp.where(group == g, q_all, 0.0).astype(BF16)
            scores.append(_dot_nt(kx_ref[kvh], q))
            values_t.append(vt_ref[kvh])
            sinks.append(jnp.full((1, 1), sink_ref[kvh * SWA_GROUP + g] * (1.0 / SWA_SCALE), F32))
    outs = _staged_attention(scores, values_t, SWA_SCALE, sinks)
    for kvh in range(SWA_KV_HEADS):
        ot = jnp.concatenate(outs[kvh * SWA_GROUP:(kvh + 1) * SWA_GROUP], axis=0)
        o_ref[:, kvh * SWA_GW:(kvh + 1) * SWA_GW] = ot.T.astype(BF16)


def _swa_attn_prompt(q, kx, vt, sink, batch, seq):
    wq = SWA_HEADS * SWA_HEAD_DIM
    rows = pl.BlockSpec((SWA_KV_HEADS, seq, SWA_GW), lambda b: (0, b, 0))
    return pl.pallas_call(
        _swa_attn_prompt_kernel,
        grid=(batch,),
        in_specs=[rows, rows, pl.BlockSpec((SWA_KV_HEADS, SWA_HEAD_DIM, seq), lambda b: (0, 0, b)),
                  pl.BlockSpec(memory_space=pltpu.SMEM)],
        out_specs=pl.BlockSpec((seq, wq), lambda b: (b, 0)),
        out_shape=jax.ShapeDtypeStruct((batch * seq, wq), BF16),
        compiler_params=_params("arbitrary"),
        name="swa_attn_prompt",
    )(q, kx, vt, sink)


def _swa_attn_latent_kernel(q_ref, kc_ref, vc_ref, kp_ref, vp_ref, km_ref, vm_ref, kn_ref, vn_ref,
                            sink_ref, o_ref, ot_buf, s_a, s_b, s_c, s_d, bias_ref, *, tq, dec_seq, past):
    i = pl.program_id(1)
    qpos = i * tq + lax.broadcasted_iota(jnp.int32, (1, tq), 1)

    def band(first, n):
        kpos = first + lax.broadcasted_iota(jnp.int32, (n, 1), 0)
        valid = (jnp.abs(qpos - kpos) <= SWA_WINDOW) & (kpos >= 0) & (kpos < dec_seq)
        return jnp.where(valid, 0.0, NEG_INF)

    bias_ref[...] = jnp.concatenate(
        [jnp.zeros((past, tq), F32), band(i * tq - SWA_WINDOW, SWA_WINDOW), band(i * tq, tq),
         band((i + 1) * tq, SWA_WINDOW)], axis=0)
    srcs = [(kc_ref, vc_ref), (kp_ref, vp_ref), (km_ref, vm_ref), (kn_ref, vn_ref)]
    _swa_heads(q_ref, sink_ref, o_ref, srcs, bias_ref, ot_buf, (s_a, s_b, s_c, s_d))


def _swa_attn_latent(q, kx, vt, kx_ctx, vt_ctx, sink, n_prompt, dec_batch, dec_seq, past, tq):
    t = q.shape[1]
    wq = SWA_HEADS * SWA_HEAD_DIM
    nq = dec_seq // tq
    w = SWA_WINDOW
    n_keys = past + tq + 2 * w
    first = lambda b, i: n_prompt + b * dec_seq + i * tq
    prev = lambda b, i: first(b, i) // w - 1
    nxt = lambda b, i: jnp.minimum((first(b, i) + tq) // w, t // w - 1)
    rows = lambda n, blk: pl.BlockSpec((SWA_KV_HEADS, n, SWA_GW), lambda b, i: (0, blk(b, i), 0))
    cols = lambda n, blk: pl.BlockSpec((SWA_KV_HEADS, SWA_HEAD_DIM, n), lambda b, i: (0, 0, blk(b, i)))
    main = lambda b, i: first(b, i) // tq
    ctx = lambda b, i: b
    return pl.pallas_call(
        functools.partial(_swa_attn_latent_kernel, tq=tq, dec_seq=dec_seq, past=past),
        grid=(dec_batch, nq),
        in_specs=[rows(tq, main), rows(past, ctx), cols(past, ctx), rows(w, prev), cols(w, prev),
                  rows(tq, main), cols(tq, main), rows(w, nxt), cols(w, nxt),
                  pl.BlockSpec(memory_space=pltpu.SMEM)],
        out_specs=pl.BlockSpec((tq, wq), lambda b, i: (b * nq + i, 0)),
        out_shape=jax.ShapeDtypeStruct((dec_batch * dec_seq, wq), BF16),
        scratch_shapes=_swa_attn_scratch(n_keys, tq),
        compiler_params=_params("arbitrary", "arbitrary"),
        name="swa_attn_latent",
    )(q, kx_ctx, vt_ctx, kx, vt, kx, vt, kx, vt, sink)


def _hgrn_pre_kernel(xp_ref, xl_ref, mod_ref, gn_ref, w_ref, lbl_ref, q_ref, v_ref, g_ref, f_ref, qmax_ref, *,
                     layer, prompt_tiles):
    is_prompt = pl.program_id(0) < prompt_tiles
    m = mod_ref[0]
    h = _modulate(_read_rows(is_prompt, xp_ref, xl_ref), gn_ref[...], m[0:1], m[1:2]).astype(BF16)
    y = _dot(h, w_ref[...])
    n = HG_HEADS * HG_DK
    q = _silu(y[:, :n])
    q_ref[...] = q
    qmax_ref[0] = jnp.max(jnp.abs(q), axis=0, keepdims=True)
    v_ref[...] = y[:, n:2 * n].astype(BF16)
    g_ref[...] = _silu(y[:, 2 * n:3 * n])
    for d in range(2):
        logits = lbl_ref[d]
        e = jnp.exp(logits - jnp.max(logits, axis=0, keepdims=True))
        s = e / jnp.sum(e, axis=0, keepdims=True)
        cs = s[0:1]
        for r in range(1, layer + 1):
            cs = cs + s[r:r + 1]
        lb = cs - s[0:1]
        f_ref[d] = lb + (1.0 - lb) * _sigmoid(y[:, (3 + d) * n:(4 + d) * n])


def _hgrn_pre(x, mods, gn, w5, lb_logits, layer, rows):
    d = x[0].shape[1]
    t = x[0].shape[0] + x[1].shape[0]
    n = HG_HEADS * HG_DK
    return pl.pallas_call(
        functools.partial(_hgrn_pre_kernel, layer=layer, prompt_tiles=rows.prompt_tiles),
        grid=(rows.n_tiles,),
        in_specs=rows.x_specs(d) + [rows.mod_spec(d), _whole(gn.shape), _whole(w5.shape), _whole(lb_logits.shape)],
        out_specs=[rows.row_spec(n), rows.row_spec(n), rows.row_spec(n),
                   pl.BlockSpec((2, rows.tm, n), lambda i: (0, i, 0)), pl.BlockSpec((1, 1, n), lambda i: (i, 0, 0))],
        out_shape=[jax.ShapeDtypeStruct((t, n), F32), jax.ShapeDtypeStruct((t, n), BF16),
                   jax.ShapeDtypeStruct((t, n), F32), jax.ShapeDtypeStruct((2, t, n), F32),
                   jax.ShapeDtypeStruct((rows.n_tiles, 1, n), F32)],
        compiler_params=_params("arbitrary"),
        name="hgrn_pre",
    )(*x, mods, gn, w5, lb_logits)


def _tri_cumsum(tri, x):
    hi = x.astype(BF16)
    r1 = x - hi.astype(F32)
    mid = r1.astype(BF16)
    lo = (r1 - mid.astype(F32)).astype(BF16)
    return _dot(tri, hi) + _dot(tri, mid) + _dot(tri, lo)


def _hgrn_tile(q_ref, v_ref, f_ref, o_ref, st_ref, *, rows, reverse, bounded, slot=0, sfin_ref=None):
    r = rows
    diag = HG_DIAG_BLOCK if bounded else 1
    a = lax.broadcasted_iota(jnp.int32, (r, r), 0)
    b = lax.broadcasted_iota(jnp.int32, (r, r), 1)
    seen = (b >= a) if reverse else (b <= a)
    tri = jnp.where(seen, 1.0, 0.0).astype(BF16)

    q = q_ref[...]
    f = f_ref[slot]
    vb = v_ref[...].astype(BF16)
    lf = jnp.log(f)
    cum = _tri_cumsum(tri, lf)
    tot = cum[0:1, :] if reverse else cum[r - 1:r, :]
    kk = 1.0 - f
    q_in = (q * jnp.exp(cum)).astype(BF16)
    k_d = (kk * jnp.exp(tot - cum)).astype(BF16)
    e_tot = jnp.exp(tot)

    levels = []
    c = r // 2
    while c >= diag:
        q_half = (a % (2 * c) < c) if reverse else (a % (2 * c) >= c)
        k_half = (b % (2 * c) >= c) if reverse else (b % (2 * c) < c)
        levels.append((2 * c, c if reverse else c - 1, ((a // (2 * c)) == (b // (2 * c))) & q_half & k_half))
        c //= 2
    levels.append((diag, diag // 2, ((a // diag) == (b // diag)) & seen))

    factors = []
    for size, ref_row, own in levels:
        if size == 1:
            factors.append((q.astype(BF16), kk.astype(BF16), own))
            continue
        if size >= 8:
            ref = jnp.concatenate(
                [jnp.broadcast_to(cum[j * size + ref_row:j * size + ref_row + 1, :], (size, cum.shape[1]))
                 for j in range(r // size)], axis=0)
        else:
            ref = _tri_cumsum(jnp.where(b == (a // size) * size + ref_row, 1.0, 0.0).astype(BF16), cum)
        factors.append(((q * jnp.exp(cum - ref)).astype(BF16), (kk * jnp.exp(ref - cum)).astype(BF16), own))

    outs = []
    for h in range(HG_HEADS):
        sl = slice(h * HG_DK, (h + 1) * HG_DK)
        att = jnp.zeros((r, r), F32)
        for q_l, k_l, own in factors:
            att = jnp.where(own, _dot_nt(q_l[:, sl], k_l[:, sl]), att)
        intra = _dot(att.astype(BF16), vb[:, sl])
        if sfin_ref is not None:
            outs.append(intra)
            sfin_ref[0, slot, h] = _dot_tn(k_d[:, sl], vb[:, sl])
            continue
        st = st_ref[h]
        outs.append(_dot_nt(q_in[:, sl], st.astype(BF16)) + intra)
        st_ref[h] = st * e_tot[:, sl] + _dot_tn(vb[:, sl], k_d[:, sl])
    o_ref[slot] = jnp.concatenate(outs, axis=1)


def _hgrn_whole_kernel(q_ref, v_ref, f_ref, o_ref, sfin_ref, *, rows, bounded):
    for slot, reverse in enumerate((False, True)):
        _hgrn_tile(q_ref, v_ref, f_ref, o_ref, None, rows=rows, reverse=reverse, bounded=bounded, slot=slot,
                   sfin_ref=sfin_ref)


def _hgrn_scan_kernel(*refs, rows, has_init, bounded):
    if has_init:
        q_ref, v_ref, f_ref, s0_ref, o_ref, sfin_ref, st_ref = refs
    else:
        q_ref, v_ref, f_ref, o_ref, sfin_ref, st_ref = refs
    d = pl.program_id(1)
    t = pl.program_id(2)

    @pl.when(t == 0)
    def _():
        for h in range(HG_HEADS):
            st_ref[h] = s0_ref[0, 0, h].T if has_init else jnp.zeros((HG_DV, HG_DK), F32)

    for reverse in (False, True):
        @pl.when(d == int(reverse))
        def _():
            _hgrn_tile(q_ref, v_ref, f_ref, o_ref, st_ref, rows=rows, reverse=reverse, bounded=bounded)

    @pl.when(t == pl.num_programs(2) - 1)
    def _():
        for h in range(HG_HEADS):
            sfin_ref[0, 0, h] = st_ref[h].T


def _hgrn_scan(q, v, f2, batch, seq, first_row, rt, s0=None, bounded=True):
    n = q.shape[1]
    nt = seq // rt
    has_init = s0 is not None
    out_shape = [jax.ShapeDtypeStruct((2, batch * seq, n), F32),
                 jax.ShapeDtypeStruct((batch, 2, HG_HEADS, HG_DK, HG_DV), F32)]
    if nt == 1 and not has_init:
        rows_of = lambda b: (first_row // rt + b, 0)
        return pl.pallas_call(
            functools.partial(_hgrn_whole_kernel, rows=rt, bounded=bounded),
            grid=(batch,),
            in_specs=[pl.BlockSpec((rt, n), rows_of), pl.BlockSpec((rt, n), rows_of),
                      pl.BlockSpec((2, rt, n), lambda b: (0, first_row // rt + b, 0))],
            out_specs=[pl.BlockSpec((2, rt, n), lambda b: (0, b, 0)),
                       pl.BlockSpec((1, 2, HG_HEADS, HG_DK, HG_DV), lambda b: (b, 0, 0, 0, 0))],
            out_shape=out_shape,
            compiler_params=_params("arbitrary"),
            name="hgrn_scan_whole",
        )(q, v, f2)

    def local(b, d, i):
        return b * nt + jnp.where(d == 0, i, nt - 1 - i)

    def slab(b, d, i):
        return first_row // rt + local(b, d, i)

    row = pl.BlockSpec((rt, n), lambda b, d, i: (slab(b, d, i), 0))
    state = pl.BlockSpec((1, 1, HG_HEADS, HG_DK, HG_DV), lambda b, d, i: (b, d, 0, 0, 0))
    ins = [q, v, f2]
    in_specs = [row, row, pl.BlockSpec((1, rt, n), lambda b, d, i: (d, slab(b, d, i), 0))]
    if has_init:
        ins.append(s0)
        in_specs.append(state)
    return pl.pallas_call(
        functools.partial(_hgrn_scan_kernel, rows=rt, has_init=has_init, bounded=bounded),
        grid=(batch, 2, nt),
        in_specs=in_specs,
        out_specs=[pl.BlockSpec((1, rt, n), lambda b, d, i: (d, local(b, d, i), 0)), state],
        out_shape=[jax.ShapeDtypeStruct((2, batch * seq, n), F32),
                   jax.ShapeDtypeStruct((batch, 2, HG_HEADS, HG_DK, HG_DV), F32)],
        scratch_shapes=[pltpu.VMEM((HG_HEADS, HG_DV, HG_DK), F32)],
        compiler_params=_params("arbitrary", "arbitrary", "arbitrary"),
        name="hgrn_scan_latent" if has_init else "hgrn_scan_prompt",
    )(*ins)


def _post_kernel(*refs, hgrn, final, prompt_tiles):
    refs = list(refs)
    outl_ref = refs.pop()
    outp_ref = refs.pop()
    is_prompt = pl.program_id(0) < prompt_tiles
    if hgrn:
        odp_ref, odl_ref, g_ref, onorm_ref = refs[:4]
        refs = refs[4:]
        o2 = jnp.where(is_prompt, odp_ref[0] + odp_ref[1], odl_ref[0] + odl_ref[1])
        gate = g_ref[...]
        onorm = onorm_ref[...]
        parts = []
        for h in range(HG_HEADS):
            sl = slice(h * HG_DV, (h + 1) * HG_DV)
            parts.append(_rms(o2[:, sl], onorm) * gate[:, sl])
        o = jnp.concatenate(parts, axis=1).astype(BF16)
    else:
        o = jnp.where(is_prompt, refs[0][...], refs[1][...])
        refs = refs[2:]
    xp_ref, xl_ref, mod_ref, wo_ref, gn_ref, wg_ref, wu_ref, wd_ref = refs[:8]
    m = mod_ref[0]
    x1 = _read_rows(is_prompt, xp_ref, xl_ref) + m[2:3] * _dot(o, wo_ref[...])
    h2 = _modulate(x1, gn_ref[...], m[3:4], m[4:5]).astype(BF16)
    a = (_silu(_dot(h2, wg_ref[0])) * _dot(h2, wu_ref[0])).astype(BF16)
    x2 = x1 + m[5:6] * _dot(a, wd_ref[0])
    if final:
        x2 = _rms(x2, refs[8][...])

    @pl.when(is_prompt)
    def _():
        outp_ref[...] = x2

    @pl.when(jnp.logical_not(is_prompt))
    def _():
        outl_ref[...] = x2


def _layer_of(stack, layer):
    shape = (1,) + stack.shape[1:]
    return pl.BlockSpec(shape, lambda *_: (layer,) + (0,) * (len(shape) - 1), pipeline_mode=pl.Buffered(1))


def _post(mix, x, mods, wo, gn, wg, wu, wd, layer, rows, final_norm=None, hgrn=False):
    d = x[0].shape[1]
    tm = rows.tm
    if hgrn:
        od_p, od_l, gate, onorm = mix
        head = [od_p, od_l, gate, onorm]
        head_specs = [pl.BlockSpec((2, tm, d), lambda i: (0, jnp.minimum(i, rows.prompt_tiles - 1), 0)),
                      pl.BlockSpec((2, tm, d), lambda i: (0, jnp.maximum(i - rows.prompt_tiles, 0), 0)),
                      rows.row_spec(d), _whole(onorm.shape)]
    else:
        head = list(mix)
        head_specs = [rows.prompt_spec(mix[0].shape[1]), rows.latent_spec(mix[1].shape[1])]
    ins = head + [*x, mods, wo, gn, wg, wu, wd]
    in_specs = (head_specs + rows.x_specs(d) + [rows.mod_spec(d), _whole(wo.shape), _whole(gn.shape)]
                + [_layer_of(a, layer) for a in (wg, wu, wd)])
    if final_norm is not None:
        ins.append(final_norm)
        in_specs.append(_whole(final_norm.shape))
    return pl.pallas_call(
        functools.partial(_post_kernel, hgrn=hgrn, final=final_norm is not None, prompt_tiles=rows.prompt_tiles),
        grid=(rows.n_tiles,),
        in_specs=in_specs,
        out_specs=rows.x_specs(d),
        out_shape=[jax.ShapeDtypeStruct(x[0].shape, F32), jax.ShapeDtypeStruct(x[1].shape, F32)],
        compiler_params=_params("arbitrary"),
        name="post",
    )(*ins)


def _pick_tile(n_prompt_rows, dec_seq, want):
    tm = want
    while n_prompt_rows % tm or dec_seq % tm:
        tm //= 2
    return tm


def kernel(x_prompt, x_sample, cache_mla_ckv, cache_mla_krope, state_hgrn, cache_swa_k, cache_swa_v, c, c_ctx, ada_w, ada_b, norm_mix, norm_ffn, ffn_w_gate, ffn_w_up, ffn_w_down, final_norm, mla_w_dq, mla_q_norm, mla_w_uq, mla_w_dkv, mla_kv_norm, mla_w_uk, mla_w_uv, mla_w_o, hg_w_q, hg_w_f, hg_w_i, hg_w_g, hg_o_norm, hg_w_o, hg_lb_logits, swa_w_q, swa_w_k, swa_w_v, swa_w_o, swa_sink):
    batch, seq, d = x_prompt.shape
    dec_batch, dec_seq, _ = x_sample.shape
    past = cache_mla_ckv.shape[2]
    depth = ada_w.shape[0]
    n_prompt = batch * seq
    n_rows = n_prompt + dec_batch * dec_seq
    assert dec_batch + 1 <= COND_ROWS and seq % SWA_WINDOW == 0 and dec_seq % (2 * SWA_WINDOW) == 0
    assert n_prompt % dec_seq == 0

    pre_rows = _Rows(n_prompt, dec_seq, n_rows, _pick_tile(n_prompt, dec_seq, 512))
    post_rows = _Rows(n_prompt, dec_seq, n_rows, _pick_tile(n_prompt, dec_seq, 512))
    hg_pre_rows = _Rows(n_prompt, dec_seq, n_rows, _pick_tile(n_prompt, dec_seq, 256))
    hg_post_rows = _Rows(n_prompt, dec_seq, n_rows, _pick_tile(n_prompt, dec_seq, 256))
    tq = _pick_tile(n_prompt, dec_seq, 256)
    cos64, sin64 = _rope_tables(dec_seq, pre_rows.tm)
    n_tab = cos64.shape[0]
    cos_mla = jnp.concatenate([cos64, jnp.ones((n_tab, ROPE_PERIOD), F32)], axis=1)
    sin_mla = jnp.concatenate([sin64, jnp.zeros((n_tab, ROPE_PERIOD), F32)], axis=1)
    cos_swa, sin_swa = jnp.tile(cos64, (1, 2)), jnp.tile(sin64, (1, 2))

    cond = jnp.concatenate([c_ctx[None, :], c, jnp.zeros((COND_ROWS - 1 - dec_batch, d), F32)], axis=0)
    mods = _adaln(cond, ada_w, ada_b).reshape(depth, COND_ROWS, 6, d)

    x = (x_prompt.reshape(n_prompt, d), x_sample.reshape(dec_batch * dec_seq, d))
    ffn_wg, ffn_wu, ffn_wd = ffn_w_gate.astype(BF16), ffn_w_up.astype(BF16), ffn_w_down.astype(BF16)
    row1 = lambda a: a.reshape(1, -1)
    new_ckv, new_krope, new_hg, new_k, new_v = [], [], [], [], []
    for i in range(depth):
        kind, j = i % N_MIXERS, i // N_MIXERS
        gn = row1(norm_mix[i])
        if kind == 0:
            uq = mla_w_uq[j].reshape(-1, MLA_HEADS, MLA_NOPE_DIM + MLA_ROPE_DIM)
            uq_rope = jnp.pad(uq[:, :, MLA_NOPE_DIM:], ((0, 0), (0, 0), (0, LANES - MLA_ROPE_DIM)))
            uq = jnp.concatenate([uq[:, :, :MLA_NOPE_DIM].reshape(uq.shape[0], -1),
                                  uq_rope.reshape(uq.shape[0], -1)], axis=1)
            w = {
                "dq": mla_w_dq[j].astype(BF16), "q_norm": row1(mla_q_norm[j]), "uq": uq.astype(BF16),
                "dkv": jnp.pad(mla_w_dkv[j], ((0, 0), (0, LANES - MLA_ROPE_DIM))).astype(BF16),
                "kv_norm": row1(mla_kv_norm[j]),
                "uk": mla_w_uk[j].astype(BF16), "uv": mla_w_uv[j].astype(BF16),
            }
            q, ckv, kr_raw, kcat, vt = _mla_pre(x, mods[i], gn, cos_mla, sin_mla, w, pre_rows)
            kr_ctx = jnp.pad(cache_mla_krope[:, j].reshape(dec_batch * past, -1), ((0, 0), (0, LANES - MLA_ROPE_DIM)))
            kcat_ctx, vt_ctx = _mla_expand(cache_mla_ckv[:, j].reshape(dec_batch * past, -1), kr_ctx, w["uk"], w["uv"])
            mix = (_mla_attn_prompt(q, kcat, vt, batch, seq),
                   _mla_attn_latent(q, kcat, vt, kcat_ctx, vt_ctx, n_prompt, dec_batch, dec_seq, past, tq))
            wo = mla_w_o[j].astype(BF16)
            new_ckv.append(ckv.reshape(batch, seq, -1))
            new_krope.append(kr_raw.reshape(batch, seq, -1))
        elif kind == 1:
            w5 = jnp.concatenate([hg_w_q[j], hg_w_i[j], hg_w_g[j], hg_w_f[j, 0], hg_w_f[j, 1]], axis=1).astype(BF16)
            q, v, gate, f2, q_max = _hgrn_pre(x, mods[i], gn, w5, hg_lb_logits, i, hg_pre_rows)
            rt = _pick_tile(seq, dec_seq, 256)
            sm = jax.nn.softmax(hg_lb_logits.astype(F32), axis=1)
            lb_min = jnp.min(jnp.cumsum(sm, axis=1)[:, i] - sm[:, 0])
            worst = (HG_DIAG_BLOCK // 2) * -jnp.log(lb_min) + jnp.log(jnp.maximum(jnp.max(q_max), 1.0))
            fits = worst < HG_MAX_EXPONENT

            def scans(bounded):
                od_p, s_p = _hgrn_scan(q, v, f2, batch, seq, 0, rt, bounded=bounded)
                od_l, _ = _hgrn_scan(q, v, f2, dec_batch, dec_seq, n_prompt, rt, s0=state_hgrn[:, j], bounded=bounded)
                return od_p, od_l, s_p

            od_p, od_l, s_prompt = lax.cond(fits, lambda: scans(True), lambda: scans(False))
            mix = (od_p, od_l, gate, row1(hg_o_norm[j]))
            wo = hg_w_o[j].astype(BF16)
            new_hg.append(s_prompt)
        else:
            wqkv = jnp.concatenate([swa_w_q[j], swa_w_k[j], swa_w_v[j]], axis=1).astype(BF16)
            q, kx, vt, k_raw, v_raw = _swa_pre(x, mods[i], gn, cos_swa, sin_swa, wqkv, pre_rows)
            sink = swa_sink[j]
            k_ctx = cache_swa_k[:, j].reshape(dec_batch * past, SWA_KV_HEADS, 1, SWA_HEAD_DIM).transpose(1, 0, 2, 3)
            kx_ctx = jnp.broadcast_to(k_ctx, (SWA_KV_HEADS, dec_batch * past, SWA_GROUP, SWA_HEAD_DIM))
            kx_ctx = kx_ctx.reshape(SWA_KV_HEADS, dec_batch * past, SWA_GW).astype(BF16)
            vt_ctx = cache_swa_v[:, j].reshape(dec_batch * past, SWA_KV_HEADS, SWA_HEAD_DIM).transpose(1, 2, 0).astype(BF16)
            mix = (_swa_attn_prompt(q, kx, vt, sink, batch, seq),
                   _swa_attn_latent(q, kx, vt, kx_ctx, vt_ctx, sink, n_prompt, dec_batch, dec_seq, past, tq))
            wo = swa_w_o[j].astype(BF16)
            new_k.append(k_raw.reshape(batch, seq, SWA_KV_HEADS, SWA_HEAD_DIM))
            new_v.append(v_raw.reshape(batch, seq, SWA_KV_HEADS, SWA_HEAD_DIM))
        x = _post(mix, x, mods[i], wo, row1(norm_ffn[i]), ffn_wg, ffn_wu, ffn_wd, i,
                  hg_post_rows if kind == 1 else post_rows,
                  final_norm=row1(final_norm) if i == depth - 1 else None, hgrn=kind == 1)
    y_prompt = x[0].reshape(batch, seq, d)
    y_sample = x[1].reshape(dec_batch, dec_seq, d)
    return (y_prompt, y_sample, jnp.stack(new_ckv, axis=1), jnp.stack(new_krope, axis=1),
            jnp.stack(new_hg, axis=1), jnp.stack(new_k, axis=1), jnp.stack(new_v, axis=1))
```

```python
import functools

import jax
import jax.numpy as jnp
from jax import lax
from jax.experimental import pallas as pl
from jax.experimental.pallas import tpu as pltpu

F32 = jnp.float32
BF16 = jnp.bfloat16

GRID_W = 64
N_MIXERS = 3

MLA_HEADS = 8
MLA_KV_LORA = 256
MLA_NOPE_DIM = 128
MLA_ROPE_DIM = 64
MLA_V_DIM = 128
MLA_QK_PAD = 256
MLA_SCALE = (MLA_NOPE_DIM + MLA_ROPE_DIM) ** -0.5

HG_HEADS = 8
HG_DK = 128
HG_DV = 128
HG_DIAG_BLOCK = 32
HG_MAX_EXPONENT = 80.0

SWA_HEADS = 16
SWA_KV_HEADS = 4
SWA_GROUP = SWA_HEADS // SWA_KV_HEADS
SWA_HEAD_DIM = 64
SWA_GW = SWA_GROUP * SWA_HEAD_DIM
SWA_WINDOW = 128
SWA_SCALE = SWA_HEAD_DIM ** -0.5

ROPE_BASE = 10000.0
ROPE_PERIOD = 64
ROPE_QUARTER = 16
NORM_EPS = 1e-6
NEG_INF = -1e30
LOG2_E = 1.4426950408889634

LANES = 128
COND_ROWS = 8
VMEM_LIMIT = 56 * 1024 * 1024


def _sigmoid(x):
    return jax.nn.sigmoid(x)


def _silu(x):
    return x * jax.nn.sigmoid(x)


def _rms(x, g):
    return x * lax.rsqrt(jnp.mean(x * x, axis=-1, keepdims=True) + NORM_EPS) * g


def _modulate(x, g, shift, scale):
    return _rms(x, g) * (1.0 + scale) + shift


def _dot(a, b):
    return jnp.dot(a, b, preferred_element_type=F32)


def _dot_nt(a, b):
    return lax.dot_general(a, b, (((1,), (1,)), ((), ())), preferred_element_type=F32)


def _dot_tn(a, b):
    return lax.dot_general(a, b, (((0,), (0,)), ((), ())), preferred_element_type=F32)


def _swap_pairs(x):
    n = x.shape[1]
    lane = lax.broadcasted_iota(jnp.int32, x.shape, 1)
    ahead = pltpu.roll(x, n - ROPE_QUARTER, 1)
    behind = pltpu.roll(x, ROPE_QUARTER, 1)
    return jnp.where((lane & (2 * ROPE_QUARTER - 1)) < ROPE_QUARTER, ahead, behind)


def _rope(x, cos, sin):
    reps = x.shape[1] // cos.shape[1]
    if reps > 1:
        cos = jnp.concatenate([cos] * reps, axis=1)
        sin = jnp.concatenate([sin] * reps, axis=1)
    return x * cos + _swap_pairs(x) * sin


def _whole(shape):
    zeros = (0,) * len(shape)
    return pl.BlockSpec(shape, lambda *_: zeros, pipeline_mode=pl.Buffered(1))


def _params(*sem):
    return pltpu.CompilerParams(dimension_semantics=sem, vmem_limit_bytes=VMEM_LIMIT)


class _Rows:
    def __init__(self, n_prompt_rows, dec_seq, n_rows, tm):
        assert n_prompt_rows % tm == 0 and dec_seq % tm == 0
        self.tm = tm
        self.n_tiles = n_rows // tm
        self.prompt_tiles = n_prompt_rows // tm
        self.seq_tiles = dec_seq // tm

    def cond(self, i):
        return jnp.where(i < self.prompt_tiles, 0, 1 + jnp.maximum(i - self.prompt_tiles, 0) // self.seq_tiles)

    def rope_block(self, i):
        return jnp.where(i < self.prompt_tiles, 0, 1 + jnp.maximum(i - self.prompt_tiles, 0) % self.seq_tiles)

    def row_spec(self, width):
        return pl.BlockSpec((self.tm, width), lambda i: (i, 0))

    def col_spec(self, height):
        return pl.BlockSpec((height, self.tm), lambda i: (0, i))

    def mod_spec(self, d):
        return pl.BlockSpec((1, 6, d), lambda i: (self.cond(i), 0, 0))

    def rope_spec(self, width):
        return pl.BlockSpec((self.tm, width), lambda i: (self.rope_block(i), 0))

    def x_specs(self, width):
        return [self.prompt_spec(width), self.latent_spec(width)]

    def is_prompt(self):
        return pl.program_id(0) < self.prompt_tiles

    def prompt_spec(self, width):
        return pl.BlockSpec((self.tm, width), lambda i: (jnp.minimum(i, self.prompt_tiles - 1), 0))

    def latent_spec(self, width):
        return pl.BlockSpec((self.tm, width), lambda i: (jnp.maximum(i - self.prompt_tiles, 0), 0))


def _rope_tables(dec_seq, tm):
    pos = jnp.arange(dec_seq)
    row = (pos // GRID_W).astype(F32)
    col = (pos % GRID_W).astype(F32)
    inv_freq = ROPE_BASE ** (-jnp.arange(ROPE_QUARTER, dtype=F32) / ROPE_QUARTER)
    ang_r = row[:, None] * inv_freq[None, :]
    ang_c = col[:, None] * inv_freq[None, :]
    cos = jnp.concatenate([jnp.cos(ang_r), jnp.cos(ang_r), jnp.cos(ang_c), jnp.cos(ang_c)], axis=1)
    sin = jnp.concatenate([-jnp.sin(ang_r), jnp.sin(ang_r), -jnp.sin(ang_c), jnp.sin(ang_c)], axis=1)
    cos = jnp.concatenate([jnp.ones((tm, ROPE_PERIOD), F32), cos], axis=0)
    sin = jnp.concatenate([jnp.zeros((tm, ROPE_PERIOD), F32), sin], axis=0)
    return cos, sin


def _adaln_kernel(c_ref, w_ref, b_ref, o_ref):
    o_ref[0] = _dot(_silu(c_ref[...]), w_ref[0]) + b_ref[0]


def _adaln(cond, ada_w, ada_b, tn=1536):
    depth, d, n = ada_w.shape
    return pl.pallas_call(
        _adaln_kernel,
        grid=(depth, n // tn),
        in_specs=[pl.BlockSpec((COND_ROWS, d), lambda l, j: (0, 0)),
                  pl.BlockSpec((1, d, tn), lambda l, j: (l, 0, j)),
                  pl.BlockSpec((1, 1, tn), lambda l, j: (l, 0, j))],
        out_specs=pl.BlockSpec((1, COND_ROWS, tn), lambda l, j: (l, 0, j)),
        out_shape=jax.ShapeDtypeStruct((depth, COND_ROWS, n), F32),
        compiler_params=_params("arbitrary", "arbitrary"),
        name="adaln",
    )(cond, ada_w, ada_b.reshape(depth, 1, n))


def _mla_store_heads(q_ref, kcat_ref, vt_ref, qn, qr_pad, kn, kr_pad, v):
    vt = v.T.astype(BF16)
    for h in range(MLA_HEADS):
        nope = slice(h * MLA_NOPE_DIM, (h + 1) * MLA_NOPE_DIM)
        if q_ref is not None:
            q_ref[h] = jnp.concatenate([qn[:, nope], qr_pad[:, h * LANES:(h + 1) * LANES]], axis=1)
        kcat_ref[h] = jnp.concatenate([kn[:, nope], kr_pad], axis=1)
        vt_ref[h] = vt[h * MLA_V_DIM:(h + 1) * MLA_V_DIM, :]


def _read_rows(is_prompt, p_ref, l_ref):
    return jnp.where(is_prompt, p_ref[...], l_ref[...])


def _take_rows(refs, pair, prompt_tiles):
    is_prompt = pl.program_id(0) < prompt_tiles
    if pair:
        return _read_rows(is_prompt, refs[0], refs[1]), is_prompt, refs[2:]
    return refs[0][...], is_prompt, refs[1:]


def _rows_inputs(x, rows):
    if isinstance(x, tuple):
        d = x[0].shape[1]
        return list(x), rows.x_specs(d), True, x[0].shape[0] + x[1].shape[0], d
    return [x], [rows.row_spec(x.shape[1])], False, x.shape[0], x.shape[1]


def _mla_pre_kernel(*refs, pair, prompt_tiles):
    x, is_prompt, refs = _take_rows(refs, pair, prompt_tiles)
    (mod_ref, gn_ref, cos_ref, sin_ref, wdq_ref, qnorm_ref, wuq_ref, wdkv_ref, kvnorm_ref, wuk_ref, wuv_ref,
     q_ref, ckv_ref, krraw_ref, kcat_ref, vt_ref) = refs
    m = mod_ref[0]
    h = _modulate(x, gn_ref[...], m[0:1], m[1:2]).astype(BF16)
    q_lat = _rms(_dot(h, wdq_ref[...]), qnorm_ref[...]).astype(BF16)
    cos, sin = cos_ref[...], sin_ref[...]
    nn = MLA_HEADS * MLA_NOPE_DIM
    qn = _dot(q_lat, wuq_ref[:, :nn]).astype(BF16)
    qr_pad = _rope(_dot(q_lat, wuq_ref[:, nn:]), cos, sin).astype(BF16)
    kv = _dot(h, wdkv_ref[...])
    ckv = _rms(kv[:, :MLA_KV_LORA], kvnorm_ref[...])
    kr = kv[:, MLA_KV_LORA:]

    @pl.when(is_prompt)
    def _():
        ckv_ref[...] = ckv
        krraw_ref[...] = kr[:, :MLA_ROPE_DIM]

    kr_pad = _rope(kr, cos, sin).astype(BF16)
    cb = ckv.astype(BF16)
    _mla_store_heads(q_ref, kcat_ref, vt_ref, qn, qr_pad, _dot(cb, wuk_ref[...]).astype(BF16), kr_pad,
                     _dot(cb, wuv_ref[...]))


def _mla_pre(x, mods, gn, cos, sin, w, rows):
    xs, x_specs, pair, t, d = _rows_inputs(x, rows)
    tm = rows.tm
    n_prompt = rows.prompt_tiles * tm
    weights = [w["dq"], w["q_norm"], w["uq"], w["dkv"], w["kv_norm"], w["uk"], w["uv"]]
    ins = xs + [mods, gn, cos, sin] + weights
    in_specs = x_specs + [rows.mod_spec(d), _whole(gn.shape),
                          rows.rope_spec(cos.shape[1]), rows.rope_spec(sin.shape[1])]
    in_specs += [_whole(a.shape) for a in weights]
    heads_rows = pl.BlockSpec((MLA_HEADS, tm, MLA_QK_PAD), lambda i: (0, i, 0))
    return pl.pallas_call(
        functools.partial(_mla_pre_kernel, pair=pair, prompt_tiles=rows.prompt_tiles),
        grid=(rows.n_tiles,),
        in_specs=in_specs,
        out_specs=[heads_rows, rows.prompt_spec(MLA_KV_LORA), rows.prompt_spec(MLA_ROPE_DIM), heads_rows,
                   pl.BlockSpec((MLA_HEADS, MLA_V_DIM, tm), lambda i: (0, 0, i))],
        out_shape=[jax.ShapeDtypeStruct((MLA_HEADS, t, MLA_QK_PAD), BF16),
                   jax.ShapeDtypeStruct((n_prompt, MLA_KV_LORA), F32),
                   jax.ShapeDtypeStruct((n_prompt, MLA_ROPE_DIM), F32),
                   jax.ShapeDtypeStruct((MLA_HEADS, t, MLA_QK_PAD), BF16),
                   jax.ShapeDtypeStruct((MLA_HEADS, MLA_V_DIM, t), BF16)],
        compiler_params=_params("arbitrary"),
        name="mla_pre",
    )(*ins)


def _mla_expand_kernel(c_ref, kr_ref, wuk_ref, wuv_ref, kcat_ref, vt_ref):
    cb = c_ref[...].astype(BF16)
    _mla_store_heads(None, kcat_ref, vt_ref, None, None, _dot(cb, wuk_ref[...]).astype(BF16),
                     kr_ref[...].astype(BF16), _dot(cb, wuv_ref[...]))


def _mla_expand(ckv, kr_pad, wuk, wuv):
    n = ckv.shape[0]
    return pl.pallas_call(
        _mla_expand_kernel,
        out_shape=[jax.ShapeDtypeStruct((MLA_HEADS, n, MLA_QK_PAD), BF16),
                   jax.ShapeDtypeStruct((MLA_HEADS, MLA_V_DIM, n), BF16)],
        compiler_params=pltpu.CompilerParams(vmem_limit_bytes=VMEM_LIMIT),
        name="mla_expand",
    )(ckv, kr_pad, wuk, wuv)


def _mla_attn_kernel(*refs, n_src):
    q_ref = refs[0]
    srcs = [(refs[1 + 2 * i], refs[2 + 2 * i]) for i in range(n_src)]
    o_ref, o_buf = refs[1 + 2 * n_src:3 + 2 * n_src]
    s_bufs = refs[3 + 2 * n_src:]
    n_keys = [k_ref.shape[1] for k_ref, _ in srcs]
    starts = [sum(n_keys[:i]) for i in range(n_src)]

    def put_scores(h, s_buf):
        q = q_ref[h]
        for (k_ref, _), first, n in zip(srcs, starts, n_keys):
            s_buf[first:first + n, :] = _dot_nt(k_ref[h], q)

    def finish(h, s_buf):
        s = s_buf[...]
        mx = jnp.max(s, axis=0, keepdims=True)
        p = jnp.exp2((s - mx) * (MLA_SCALE * LOG2_E))
        den = jnp.sum(p, axis=0, keepdims=True)
        p = p.astype(BF16)
        acc = None
        for (_, vt_ref), first, n in zip(srcs, starts, n_keys):
            a = _dot(vt_ref[h], p[first:first + n, :])
            acc = a if acc is None else acc + a
        o_buf[h] = (acc / den).T.astype(BF16)

    _pipelined_heads(MLA_HEADS, put_scores, finish, s_bufs)
    for h in range(MLA_HEADS):
        o_ref[:, h * MLA_V_DIM:(h + 1) * MLA_V_DIM] = o_buf[h]


def _mla_attn_scratch(n_keys, tq):
    return [pltpu.VMEM((MLA_HEADS, tq, MLA_V_DIM), BF16)] + [pltpu.VMEM((n_keys, tq), F32)] * 4


def _pipelined_heads(n_heads, put_scores, finish, bufs):
    a, b, c, d = bufs
    assert n_heads % 4 == 0
    put_scores(0, a)
    put_scores(1, b)

    def quad(j, carry):
        h = 4 * j
        put_scores(h + 2, c)
        put_scores(h + 3, d)
        finish(h, a)
        finish(h + 1, b)
        put_scores(h + 4, a)
        put_scores(h + 5, b)
        finish(h + 2, c)
        finish(h + 3, d)
        return carry

    lax.fori_loop(0, n_heads // 4 - 1, quad, 0)
    h = n_heads - 4
    put_scores(h + 2, c)
    put_scores(h + 3, d)
    finish(h, a)
    finish(h + 1, b)
    finish(h + 2, c)
    finish(h + 3, d)


def _staged_attention(scores, values_t, scale, sinks=None):
    c = scale * LOG2_E
    mx = [jnp.max(s, axis=0, keepdims=True) for s in scores]
    if sinks is not None:
        mx = [jnp.maximum(m, z) for m, z in zip(mx, sinks)]
    p = [jnp.exp2((s - m) * c) for s, m in zip(scores, mx)]
    den = [jnp.sum(x, axis=0, keepdims=True) for x in p]
    if sinks is not None:
        den = [d + jnp.exp2((z - m) * c) for d, z, m in zip(den, sinks, mx)]
    acc = [_dot(v, x.astype(BF16)) for v, x in zip(values_t, p)]
    return [a / d for a, d in zip(acc, den)]


def _mla_attn_prompt_kernel(q_ref, k_ref, vt_ref, o_ref):
    scores = [_dot_nt(k_ref[h], q_ref[h]) for h in range(MLA_HEADS)]
    outs = _staged_attention(scores, [vt_ref[h] for h in range(MLA_HEADS)], MLA_SCALE)
    for h in range(MLA_HEADS):
        o_ref[:, h * MLA_V_DIM:(h + 1) * MLA_V_DIM] = outs[h].T.astype(BF16)


def _mla_attn_prompt(q, kcat, vt, batch, seq):
    hv = MLA_HEADS * MLA_V_DIM
    rows = pl.BlockSpec((MLA_HEADS, seq, MLA_QK_PAD), lambda b: (0, b, 0))
    return pl.pallas_call(
        _mla_attn_prompt_kernel,
        grid=(batch,),
        in_specs=[rows, rows, pl.BlockSpec((MLA_HEADS, MLA_V_DIM, seq), lambda b: (0, 0, b))],
        out_specs=pl.BlockSpec((seq, hv), lambda b: (b, 0)),
        out_shape=jax.ShapeDtypeStruct((batch * seq, hv), BF16),
        compiler_params=_params("arbitrary"),
        name="mla_attn_prompt",
    )(q, kcat, vt)


def _mla_attn_latent(q, kcat, vt, kcat_ctx, vt_ctx, n_prompt, dec_batch, dec_seq, past, tq):
    hv = MLA_HEADS * MLA_V_DIM
    nq = dec_seq // tq
    lat_blk = n_prompt // dec_seq
    return pl.pallas_call(
        functools.partial(_mla_attn_kernel, n_src=2),
        grid=(dec_batch, nq),
        in_specs=[pl.BlockSpec((MLA_HEADS, tq, MLA_QK_PAD), lambda b, i: (0, n_prompt // tq + b * nq + i, 0)),
                  pl.BlockSpec((MLA_HEADS, past, MLA_QK_PAD), lambda b, i: (0, b, 0)),
                  pl.BlockSpec((MLA_HEADS, MLA_V_DIM, past), lambda b, i: (0, 0, b)),
                  pl.BlockSpec((MLA_HEADS, dec_seq, MLA_QK_PAD), lambda b, i: (0, lat_blk + b, 0)),
                  pl.BlockSpec((MLA_HEADS, MLA_V_DIM, dec_seq), lambda b, i: (0, 0, lat_blk + b))],
        out_specs=pl.BlockSpec((tq, hv), lambda b, i: (b * nq + i, 0)),
        out_shape=jax.ShapeDtypeStruct((dec_batch * dec_seq, hv), BF16),
        scratch_shapes=_mla_attn_scratch(past + dec_seq, tq),
        compiler_params=_params("arbitrary", "arbitrary"),
        name="mla_attn_latent",
    )(q, kcat_ctx, vt_ctx, kcat, vt)


def _tile_heads(x):
    n = x.shape[1]
    block = lax.broadcasted_iota(jnp.int32, x.shape, 1) // SWA_HEAD_DIM
    rolled = [x] + [pltpu.roll(x, s * SWA_HEAD_DIM, 1) for s in range(1, SWA_KV_HEADS)]
    out = []
    for kvh in range(SWA_KV_HEADS):
        blk = rolled[(0 - kvh) % SWA_KV_HEADS]
        for g in range(1, n // SWA_HEAD_DIM):
            blk = jnp.where(block == g, rolled[(g - kvh) % SWA_KV_HEADS], blk)
        out.append(blk)
    return jnp.concatenate(out, axis=1)


def _swa_pre_kernel(*refs, pair, prompt_tiles):
    x, is_prompt, refs = _take_rows(refs, pair, prompt_tiles)
    mod_ref, gn_ref, cos_ref, sin_ref, wqkv_ref, q_ref, kx_ref, vt_ref, kraw_ref, vraw_ref = refs
    m = mod_ref[0]
    h = _modulate(x, gn_ref[...], m[0:1], m[1:2]).astype(BF16)
    qkv = _dot(h, wqkv_ref[...])
    nq, nk = SWA_HEADS * SWA_HEAD_DIM, SWA_KV_HEADS * SWA_HEAD_DIM
    cos, sin = cos_ref[...], sin_ref[...]
    q = _rope(qkv[:, :nq], cos, sin).astype(BF16)
    k = qkv[:, nq:nq + nk]
    v = qkv[:, nq + nk:]

    @pl.when(is_prompt)
    def _():
        kraw_ref[...] = k
        vraw_ref[...] = v

    kx = _tile_heads(_rope(k, cos, sin)).astype(BF16)
    vt = v.T.astype(BF16)
    for kvh in range(SWA_KV_HEADS):
        q_ref[kvh] = q[:, kvh * SWA_GW:(kvh + 1) * SWA_GW]
        kx_ref[kvh] = kx[:, kvh * SWA_GW:(kvh + 1) * SWA_GW]
        vt_ref[kvh] = vt[kvh * SWA_HEAD_DIM:(kvh + 1) * SWA_HEAD_DIM, :]


def _swa_pre(x, mods, gn, cos, sin, wqkv, rows):
    xs, x_specs, pair, t, d = _rows_inputs(x, rows)
    tm = rows.tm
    n_prompt = rows.prompt_tiles * tm
    nk = SWA_KV_HEADS * SWA_HEAD_DIM
    heads_rows = pl.BlockSpec((SWA_KV_HEADS, tm, SWA_GW), lambda i: (0, i, 0))
    return pl.pallas_call(
        functools.partial(_swa_pre_kernel, pair=pair, prompt_tiles=rows.prompt_tiles),
        grid=(rows.n_tiles,),
        in_specs=x_specs + [rows.mod_spec(d), _whole(gn.shape),
                            rows.rope_spec(cos.shape[1]), rows.rope_spec(sin.shape[1]), _whole(wqkv.shape)],
        out_specs=[heads_rows, heads_rows, pl.BlockSpec((SWA_KV_HEADS, SWA_HEAD_DIM, tm), lambda i: (0, 0, i)),
                   rows.prompt_spec(nk), rows.prompt_spec(nk)],
        out_shape=[jax.ShapeDtypeStruct((SWA_KV_HEADS, t, SWA_GW), BF16),
                   jax.ShapeDtypeStruct((SWA_KV_HEADS, t, SWA_GW), BF16),
                   jax.ShapeDtypeStruct((SWA_KV_HEADS, SWA_HEAD_DIM, t), BF16),
                   jax.ShapeDtypeStruct((n_prompt, nk), F32), jax.ShapeDtypeStruct((n_prompt, nk), F32)],
        compiler_params=_params("arbitrary"),
        name="swa_pre",
    )(*xs, mods, gn, cos, sin, wqkv)


def _swa_heads(q_ref, sink_ref, o_ref, srcs, bias_ref, ot_buf, s_bufs):
    tq = q_ref.shape[1]
    n_keys = [kx_ref.shape[1] for kx_ref, _ in srcs]
    starts = [sum(n_keys[:i]) for i in range(len(srcs))]
    group = lax.broadcasted_iota(jnp.int32, (tq, SWA_GW), 1) // SWA_HEAD_DIM

    def put_scores(hq, s_buf):
        kvh, g = hq // SWA_GROUP, hq % SWA_GROUP
        q = jnp.where(group == g, q_ref[kvh].astype(F32), 0.0).astype(BF16)
        for (kx_ref, _), first, n in zip(srcs, starts, n_keys):
            s_buf[first:first + n, :] = _dot_nt(kx_ref[kvh], q)

    def finish(hq, s_buf):
        kvh, g = hq // SWA_GROUP, hq % SWA_GROUP
        s = s_buf[...]
        if bias_ref is not None:
            s = s + bias_ref[...]
        sink = jnp.full((1, 1), sink_ref[hq] * (1.0 / SWA_SCALE), F32)
        mx = jnp.maximum(jnp.max(s, axis=0, keepdims=True), sink)
        p = jnp.exp2((s - mx) * (SWA_SCALE * LOG2_E))
        den = jnp.sum(p, axis=0, keepdims=True) + jnp.exp2((sink - mx) * (SWA_SCALE * LOG2_E))
        p = p.astype(BF16)
        acc = None
        for (_, vt_ref), first, n in zip(srcs, starts, n_keys):
            a = _dot(vt_ref[kvh], p[first:first + n, :])
            acc = a if acc is None else acc + a
        ot_buf[kvh, pl.ds(pl.multiple_of(g * SWA_HEAD_DIM, SWA_HEAD_DIM), SWA_HEAD_DIM), :] = acc / den

    _pipelined_heads(SWA_HEADS, put_scores, finish, s_bufs)
    for kvh in range(SWA_KV_HEADS):
        o_ref[:, kvh * SWA_GW:(kvh + 1) * SWA_GW] = ot_buf[kvh].T.astype(BF16)


def _swa_attn_scratch(n_keys, tq):
    return [pltpu.VMEM((SWA_KV_HEADS, SWA_GW, tq), F32)] + [pltpu.VMEM((n_keys, tq), F32)] * 5


def _swa_attn_prompt_kernel(q_ref, kx_ref, vt_ref, sink_ref, o_ref):
    group = lax.broadcasted_iota(jnp.int32, (q_ref.shape[1], SWA_GW), 1) // SWA_HEAD_DIM
    scores, values_t, sinks = [], [], []
    for kvh in range(SWA_KV_HEADS):
        q_all = q_ref[kvh].astype(F32)
        for g in range(SWA_GROUP):
            q = jnp.where(group == g, q_all, 0.0).astype(BF16)
            scores.append(_dot_nt(kx_ref[kvh], q))
            values_t.append(vt_ref[kvh])
            sinks.append(jnp.full((1, 1), sink_ref[kvh * SWA_GROUP + g] * (1.0 / SWA_SCALE), F32))
    outs = _staged_attention(scores, values_t, SWA_SCALE, sinks)
    for kvh in range(SWA_KV_HEADS):
        ot = jnp.concatenate(outs[kvh * SWA_GROUP:(kvh + 1) * SWA_GROUP], axis=0)
        o_ref[:, kvh * SWA_GW:(kvh + 1) * SWA_GW] = ot.T.astype(BF16)


def _swa_attn_prompt(q, kx, vt, sink, batch, seq):
    wq = SWA_HEADS * SWA_HEAD_DIM
    rows = pl.BlockSpec((SWA_KV_HEADS, seq, SWA_GW), lambda b: (0, b, 0))
    return pl.pallas_call(
        _swa_attn_prompt_kernel,
        grid=(batch,),
        in_specs=[rows, rows, pl.BlockSpec((SWA_KV_HEADS, SWA_HEAD_DIM, seq), lambda b: (0, 0, b)),
                  pl.BlockSpec(memory_space=pltpu.SMEM)],
        out_specs=pl.BlockSpec((seq, wq), lambda b: (b, 0)),
        out_shape=jax.ShapeDtypeStruct((batch * seq, wq), BF16),
        compiler_params=_params("arbitrary"),
        name="swa_attn_prompt",
    )(q, kx, vt, sink)


def _swa_attn_latent_kernel(q_ref, kc_ref, vc_ref, kp_ref, vp_ref, km_ref, vm_ref, kn_ref, vn_ref,
                            sink_ref, o_ref, ot_buf, s_a, s_b, s_c, s_d, bias_ref, *, tq, dec_seq, past):
    i = pl.program_id(1)
    qpos = i * tq + lax.broadcasted_iota(jnp.int32, (1, tq), 1)

    def band(first, n):
        kpos = first + lax.broadcasted_iota(jnp.int32, (n, 1), 0)
        valid = (jnp.abs(qpos - kpos) <= SWA_WINDOW) & (kpos >= 0) & (kpos < dec_seq)
        return jnp.where(valid, 0.0, NEG_INF)

    bias_ref[...] = jnp.concatenate(
        [jnp.zeros((past, tq), F32), band(i * tq - SWA_WINDOW, SWA_WINDOW), band(i * tq, tq),
         band((i + 1) * tq, SWA_WINDOW)], axis=0)
    srcs = [(kc_ref, vc_ref), (kp_ref, vp_ref), (km_ref, vm_ref), (kn_ref, vn_ref)]
    _swa_heads(q_ref, sink_ref, o_ref, srcs, bias_ref, ot_buf, (s_a, s_b, s_c, s_d))


def _swa_attn_latent(q, kx, vt, kx_ctx, vt_ctx, sink, n_prompt, dec_batch, dec_seq, past, tq):
    t = q.shape[1]
    wq = SWA_HEADS * SWA_HEAD_DIM
    nq = dec_seq // tq
    w = SWA_WINDOW
    n_keys = past + tq + 2 * w
    first = lambda b, i: n_prompt + b * dec_seq + i * tq
    prev = lambda b, i: first(b, i) // w - 1
    nxt = lambda b, i: jnp.minimum((first(b, i) + tq) // w, t // w - 1)
    rows = lambda n, blk: pl.BlockSpec((SWA_KV_HEADS, n, SWA_GW), lambda b, i: (0, blk(b, i), 0))
    cols = lambda n, blk: pl.BlockSpec((SWA_KV_HEADS, SWA_HEAD_DIM, n), lambda b, i: (0, 0, blk(b, i)))
    main = lambda b, i: first(b, i) // tq
    ctx = lambda b, i: b
    return pl.pallas_call(
        functools.partial(_swa_attn_latent_kernel, tq=tq, dec_seq=dec_seq, past=past),
        grid=(dec_batch, nq),
        in_specs=[rows(tq, main), rows(past, ctx), cols(past, ctx), rows(w, prev), cols(w, prev),
                  rows(tq, main), cols(tq, main), rows(w, nxt), cols(w, nxt),
                  pl.BlockSpec(memory_space=pltpu.SMEM)],
        out_specs=pl.BlockSpec((tq, wq), lambda b, i: (b * nq + i, 0)),
        out_shape=jax.ShapeDtypeStruct((dec_batch * dec_seq, wq), BF16),
        scratch_shapes=_swa_attn_scratch(n_keys, tq),
        compiler_params=_params("arbitrary", "arbitrary"),
        name="swa_attn_latent",
    )(q, kx_ctx, vt_ctx, kx, vt, kx, vt, kx, vt, sink)


def _hgrn_pre_kernel(*refs, layer, pair, prompt_tiles):
    x, _, refs = _take_rows(refs, pair, prompt_tiles)
    mod_ref, gn_ref, w_ref, lbl_ref, q_ref, v_ref, g_ref, f_ref, qmax_ref = refs
    m = mod_ref[0]
    h = _modulate(x, gn_ref[...], m[0:1], m[1:2]).astype(BF16)
    y = _dot(h, w_ref[...])
    n = HG_HEADS * HG_DK
    q = _silu(y[:, :n])
    q_ref[...] = q
    qmax_ref[0] = jnp.max(jnp.abs(q), axis=0, keepdims=True)
    v_ref[...] = y[:, n:2 * n].astype(BF16)
    g_ref[...] = _silu(y[:, 2 * n:3 * n])
    for d in range(2):
        logits = lbl_ref[d]
        e = jnp.exp(logits - jnp.max(logits, axis=0, keepdims=True))
        s = e / jnp.sum(e, axis=0, keepdims=True)
        cs = s[0:1]
        for r in range(1, layer + 1):
            cs = cs + s[r:r + 1]
        lb = cs - s[0:1]
        f_ref[d] = lb + (1.0 - lb) * _sigmoid(y[:, (3 + d) * n:(4 + d) * n])


def _hgrn_pre(x, mods, gn, w5, lb_logits, layer, rows):
    xs, x_specs, pair, t, d = _rows_inputs(x, rows)
    n = HG_HEADS * HG_DK
    return pl.pallas_call(
        functools.partial(_hgrn_pre_kernel, layer=layer, pair=pair, prompt_tiles=rows.prompt_tiles),
        grid=(rows.n_tiles,),
        in_specs=x_specs + [rows.mod_spec(d), _whole(gn.shape), _whole(w5.shape), _whole(lb_logits.shape)],
        out_specs=[rows.row_spec(n), rows.row_spec(n), rows.row_spec(n),
                   pl.BlockSpec((2, rows.tm, n), lambda i: (0, i, 0)), pl.BlockSpec((1, 1, n), lambda i: (i, 0, 0))],
        out_shape=[jax.ShapeDtypeStruct((t, n), F32), jax.ShapeDtypeStruct((t, n), BF16),
                   jax.ShapeDtypeStruct((t, n), F32), jax.ShapeDtypeStruct((2, t, n), F32),
                   jax.ShapeDtypeStruct((rows.n_tiles, 1, n), F32)],
        compiler_params=_params("arbitrary"),
        name="hgrn_pre",
    )(*xs, mods, gn, w5, lb_logits)


def _tri_cumsum(tri, x):
    hi = x.astype(BF16)
    r1 = x - hi.astype(F32)
    mid = r1.astype(BF16)
    lo = (r1 - mid.astype(F32)).astype(BF16)
    return _dot(tri, hi) + _dot(tri, mid) + _dot(tri, lo)


def _hgrn_tile(q_ref, v_ref, f_ref, o_ref, st_ref, *, rows, reverse, bounded, slot=0, sfin_ref=None):
    r = rows
    diag = HG_DIAG_BLOCK if bounded else 1
    a = lax.broadcasted_iota(jnp.int32, (r, r), 0)
    b = lax.broadcasted_iota(jnp.int32, (r, r), 1)
    seen = (b >= a) if reverse else (b <= a)
    tri = jnp.where(seen, 1.0, 0.0).astype(BF16)

    q = q_ref[...]
    f = f_ref[slot]
    vb = v_ref[...].astype(BF16)
    lf = jnp.log(f)
    cum = _tri_cumsum(tri, lf)
    tot = cum[0:1, :] if reverse else cum[r - 1:r, :]
    kk = 1.0 - f
    q_in = (q * jnp.exp(cum)).astype(BF16)
    k_d = (kk * jnp.exp(tot - cum)).astype(BF16)
    e_tot = jnp.exp(tot)

    levels = []
    c = r // 2
    while c >= diag:
        q_half = (a % (2 * c) < c) if reverse else (a % (2 * c) >= c)
        k_half = (b % (2 * c) >= c) if reverse else (b % (2 * c) < c)
        levels.append((2 * c, c if reverse else c - 1, ((a // (2 * c)) == (b // (2 * c))) & q_half & k_half))
        c //= 2
    levels.append((diag, diag // 2, ((a // diag) == (b // diag)) & seen))

    factors = []
    for size, ref_row, own in levels:
        if size == 1:
            factors.append((q.astype(BF16), kk.astype(BF16), own))
            continue
        if size >= 8:
            ref = jnp.concatenate(
                [jnp.broadcast_to(cum[j * size + ref_row:j * size + ref_row + 1, :], (size, cum.shape[1]))
                 for j in range(r // size)], axis=0)
        else:
            ref = _tri_cumsum(jnp.where(b == (a // size) * size + ref_row, 1.0, 0.0).astype(BF16), cum)
        factors.append(((q * jnp.exp(cum - ref)).astype(BF16), (kk * jnp.exp(ref - cum)).astype(BF16), own))

    outs = []
    for h in range(HG_HEADS):
        sl = slice(h * HG_DK, (h + 1) * HG_DK)
        att = jnp.zeros((r, r), F32)
        for q_l, k_l, own in factors:
            att = jnp.where(own, _dot_nt(q_l[:, sl], k_l[:, sl]), att)
        intra = _dot(att.astype(BF16), vb[:, sl])
        if sfin_ref is not None:
            outs.append(intra)
            sfin_ref[0, slot, h] = _dot_tn(k_d[:, sl], vb[:, sl])
            continue
        st = st_ref[h]
        outs.append(_dot_nt(q_in[:, sl], st.astype(BF16)) + intra)
        st_ref[h] = st * e_tot[:, sl] + _dot_tn(vb[:, sl], k_d[:, sl])
    o_ref[slot] = jnp.concatenate(outs, axis=1)


def _hgrn_whole_kernel(q_ref, v_ref, f_ref, o_ref, sfin_ref, *, rows, bounded):
    for slot, reverse in enumerate((False, True)):
        _hgrn_tile(q_ref, v_ref, f_ref, o_ref, None, rows=rows, reverse=reverse, bounded=bounded, slot=slot,
                   sfin_ref=sfin_ref)


def _hgrn_scan_kernel(*refs, rows, has_init, bounded):
    if has_init:
        q_ref, v_ref, f_ref, s0_ref, o_ref, sfin_ref, st_ref = refs
    else:
        q_ref, v_ref, f_ref, o_ref, sfin_ref, st_ref = refs
    d = pl.program_id(1)
    t = pl.program_id(2)

    @pl.when(t == 0)
    def _():
        for h in range(HG_HEADS):
            st_ref[h] = s0_ref[0, 0, h].T if has_init else jnp.zeros((HG_DV, HG_DK), F32)

    for reverse in (False, True):
        @pl.when(d == int(reverse))
        def _():
            _hgrn_tile(q_ref, v_ref, f_ref, o_ref, st_ref, rows=rows, reverse=reverse, bounded=bounded)

    @pl.when(t == pl.num_programs(2) - 1)
    def _():
        for h in range(HG_HEADS):
            sfin_ref[0, 0, h] = st_ref[h].T


def _hgrn_scan(q, v, f2, batch, seq, first_row, rt, s0=None, bounded=True):
    n = q.shape[1]
    nt = seq // rt
    has_init = s0 is not None
    out_shape = [jax.ShapeDtypeStruct((2, batch * seq, n), F32),
                 jax.ShapeDtypeStruct((batch, 2, HG_HEADS, HG_DK, HG_DV), F32)]
    if nt == 1 and not has_init:
        rows_of = lambda b: (first_row // rt + b, 0)
        return pl.pallas_call(
            functools.partial(_hgrn_whole_kernel, rows=rt, bounded=bounded),
            grid=(batch,),
            in_specs=[pl.BlockSpec((rt, n), rows_of), pl.BlockSpec((rt, n), rows_of),
                      pl.BlockSpec((2, rt, n), lambda b: (0, first_row // rt + b, 0))],
            out_specs=[pl.BlockSpec((2, rt, n), lambda b: (0, b, 0)),
                       pl.BlockSpec((1, 2, HG_HEADS, HG_DK, HG_DV), lambda b: (b, 0, 0, 0, 0))],
            out_shape=out_shape,
            compiler_params=_params("arbitrary"),
            name="hgrn_scan_whole",
        )(q, v, f2)

    def local(b, d, i):
        return b * nt + jnp.where(d == 0, i, nt - 1 - i)

    def slab(b, d, i):
        return first_row // rt + local(b, d, i)

    row = pl.BlockSpec((rt, n), lambda b, d, i: (slab(b, d, i), 0))
    state = pl.BlockSpec((1, 1, HG_HEADS, HG_DK, HG_DV), lambda b, d, i: (b, d, 0, 0, 0))
    ins = [q, v, f2]
    in_specs = [row, row, pl.BlockSpec((1, rt, n), lambda b, d, i: (d, slab(b, d, i), 0))]
    if has_init:
        ins.append(s0)
        in_specs.append(state)
    return pl.pallas_call(
        functools.partial(_hgrn_scan_kernel, rows=rt, has_init=has_init, bounded=bounded),
        grid=(batch, 2, nt),
        in_specs=in_specs,
        out_specs=[pl.BlockSpec((1, rt, n), lambda b, d, i: (d, local(b, d, i), 0)), state],
        out_shape=[jax.ShapeDtypeStruct((2, batch * seq, n), F32),
                   jax.ShapeDtypeStruct((batch, 2, HG_HEADS, HG_DK, HG_DV), F32)],
        scratch_shapes=[pltpu.VMEM((HG_HEADS, HG_DV, HG_DK), F32)],
        compiler_params=_params("arbitrary", "arbitrary", "arbitrary"),
        name="hgrn_scan_latent" if has_init else "hgrn_scan_prompt",
    )(*ins)


def _post_kernel(*refs, hgrn, final, pair_in, pair_out, prompt_tiles, n_tiles):
    refs = list(refs)
    n_mix = 4 if hgrn else 2
    mix_refs, refs = refs[:n_mix], refs[n_mix:]
    n_x = 2 if pair_in else 1
    x_refs, refs = refs[:n_x], refs[n_x:]
    moda_ref, modb_ref, wo_ref, gn_ref, wg_ref, wu_ref, wd_ref = refs[:7]
    refs = refs[7:]
    fn_ref = refs.pop(0) if final else None
    n_out = 2 if pair_out else 1
    out_refs, (x1_buf, h2_buf) = refs[:n_out], refs[n_out:]

    i = pl.program_id(0)
    a_is_prompt = jnp.minimum(i, n_tiles - 1) < prompt_tiles
    b_is_prompt = jnp.maximum(i - 1, 0) < prompt_tiles

    @pl.when(i == 0)
    def _():
        x1_buf[1] = jnp.zeros(x1_buf.shape[1:], x1_buf.dtype)
        h2_buf[1] = jnp.zeros(h2_buf.shape[1:], h2_buf.dtype)

    def stage_a(slot):
        if hgrn:
            odp_ref, odl_ref, g_ref, onorm_ref = mix_refs
            o2 = jnp.where(a_is_prompt, odp_ref[0] + odp_ref[1], odl_ref[0] + odl_ref[1])
            gate = g_ref[...]
            onorm = onorm_ref[...]
            parts = []
            for h in range(HG_HEADS):
                sl = slice(h * HG_DV, (h + 1) * HG_DV)
                parts.append(_rms(o2[:, sl], onorm) * gate[:, sl])
            o = jnp.concatenate(parts, axis=1).astype(BF16)
        else:
            o = jnp.where(a_is_prompt, mix_refs[0][...], mix_refs[1][...])
        x = _read_rows(a_is_prompt, *x_refs) if pair_in else x_refs[0][...]
        m = moda_ref[0]
        x1 = x + m[2:3] * _dot(o, wo_ref[...])
        x1_buf[slot] = x1
        h2_buf[slot] = _modulate(x1, gn_ref[...], m[3:4], m[4:5]).astype(BF16)

    def stage_b(slot):
        h2 = h2_buf[slot]
        a = (_silu(_dot(h2, wg_ref[0])) * _dot(h2, wu_ref[0])).astype(BF16)
        x2 = x1_buf[slot] + modb_ref[0][5:6] * _dot(a, wd_ref[0])
        if final:
            x2 = _rms(x2, fn_ref[...])
        if pair_out:
            @pl.when(b_is_prompt)
            def _():
                out_refs[0][...] = x2

            @pl.when(jnp.logical_not(b_is_prompt))
            def _():
                out_refs[1][...] = x2
        else:
            out_refs[0][...] = x2

    for parity in (0, 1):
        @pl.when(i % 2 == parity)
        def _():
            stage_a(parity)
            stage_b(1 - parity)


def _layer_of(stack, layer):
    shape = (1,) + stack.shape[1:]
    return pl.BlockSpec(shape, lambda *_: (layer,) + (0,) * (len(shape) - 1), pipeline_mode=pl.Buffered(1))


def _post(mix, x, mods, wo, gn, wg, wu, wd, layer, rows, final_norm=None, hgrn=False, pair_out=False):
    pair_in = isinstance(x, tuple)
    xs = list(x) if pair_in else [x]
    d = xs[0].shape[1]
    tm, n, pt = rows.tm, rows.n_tiles, rows.prompt_tiles
    tile_a = lambda i: jnp.minimum(i, n - 1)
    tile_b = lambda i: jnp.maximum(i - 1, 0)
    prompt_blk = lambda t: jnp.minimum(t, pt - 1)
    latent_blk = lambda t: jnp.maximum(t - pt, 0)

    def pair_specs(width, tile):
        return [pl.BlockSpec((tm, width), lambda i: (prompt_blk(tile(i)), 0)),
                pl.BlockSpec((tm, width), lambda i: (latent_blk(tile(i)), 0))]

    if hgrn:
        od_p, od_l, gate, onorm = mix
        head = [od_p, od_l, gate, onorm]
        head_specs = [pl.BlockSpec((2, tm, d), lambda i: (0, prompt_blk(tile_a(i)), 0)),
                      pl.BlockSpec((2, tm, d), lambda i: (0, latent_blk(tile_a(i)), 0)),
                      pl.BlockSpec((tm, d), lambda i: (tile_a(i), 0)), _whole(onorm.shape)]
    else:
        head = list(mix)
        head_specs = pair_specs(mix[0].shape[1], tile_a)
    x_specs = pair_specs(d, tile_a) if pair_in else [pl.BlockSpec((tm, d), lambda i: (tile_a(i), 0))]
    mod_spec = lambda tile: pl.BlockSpec((1, 6, d), lambda i: (rows.cond(tile(i)), 0, 0))
    ins = head + xs + [mods, mods, wo, gn, wg, wu, wd]
    in_specs = (head_specs + x_specs + [mod_spec(tile_a), mod_spec(tile_b), _whole(wo.shape), _whole(gn.shape)]
                + [_layer_of(a, layer) for a in (wg, wu, wd)])
    if final_norm is not None:
        ins.append(final_norm)
        in_specs.append(_whole(final_norm.shape))
    n_prompt = pt * tm
    if pair_out:
        out_specs = pair_specs(d, tile_b)
        out_shape = [jax.ShapeDtypeStruct((n_prompt, d), F32), jax.ShapeDtypeStruct((n * tm - n_prompt, d), F32)]
    else:
        out_specs = pl.BlockSpec((tm, d), lambda i: (tile_b(i), 0))
        out_shape = jax.ShapeDtypeStruct((n * tm, d), F32)
    out = pl.pallas_call(
        functools.partial(_post_kernel, hgrn=hgrn, final=final_norm is not None, pair_in=pair_in, pair_out=pair_out,
                          prompt_tiles=pt, n_tiles=n),
        grid=(n + 1,),
        in_specs=in_specs,
        out_specs=out_specs,
        out_shape=out_shape,
        scratch_shapes=[pltpu.VMEM((2, tm, d), F32), pltpu.VMEM((2, tm, d), BF16)],
        compiler_params=_params("arbitrary"),
        name="post",
    )(*ins)
    return tuple(out) if pair_out else out


def _pick_tile(n_prompt_rows, dec_seq, want):
    tm = want
    while n_prompt_rows % tm or dec_seq % tm:
        tm //= 2
    return tm


def kernel(x_prompt, x_sample, cache_mla_ckv, cache_mla_krope, state_hgrn, cache_swa_k, cache_swa_v, c, c_ctx, ada_w, ada_b, norm_mix, norm_ffn, ffn_w_gate, ffn_w_up, ffn_w_down, final_norm, mla_w_dq, mla_q_norm, mla_w_uq, mla_w_dkv, mla_kv_norm, mla_w_uk, mla_w_uv, mla_w_o, hg_w_q, hg_w_f, hg_w_i, hg_w_g, hg_o_norm, hg_w_o, hg_lb_logits, swa_w_q, swa_w_k, swa_w_v, swa_w_o, swa_sink):
    batch, seq, d = x_prompt.shape
    dec_batch, dec_seq, _ = x_sample.shape
    past = cache_mla_ckv.shape[2]
    depth = ada_w.shape[0]
    n_prompt = batch * seq
    n_rows = n_prompt + dec_batch * dec_seq
    assert dec_batch + 1 <= COND_ROWS and seq % SWA_WINDOW == 0 and dec_seq % (2 * SWA_WINDOW) == 0
    assert n_prompt % dec_seq == 0

    pre_rows = _Rows(n_prompt, dec_seq, n_rows, _pick_tile(n_prompt, dec_seq, 512))
    post_rows = _Rows(n_prompt, dec_seq, n_rows, _pick_tile(n_prompt, dec_seq, 512))
    hg_pre_rows = _Rows(n_prompt, dec_seq, n_rows, _pick_tile(n_prompt, dec_seq, 256))
    hg_post_rows = _Rows(n_prompt, dec_seq, n_rows, _pick_tile(n_prompt, dec_seq, 256))
    tq = _pick_tile(n_prompt, dec_seq, 256)
    cos64, sin64 = _rope_tables(dec_seq, pre_rows.tm)
    n_tab = cos64.shape[0]
    cos_mla = jnp.concatenate([cos64, jnp.ones((n_tab, ROPE_PERIOD), F32)], axis=1)
    sin_mla = jnp.concatenate([sin64, jnp.zeros((n_tab, ROPE_PERIOD), F32)], axis=1)
    cos_swa, sin_swa = jnp.tile(cos64, (1, 2)), jnp.tile(sin64, (1, 2))

    cond = jnp.concatenate([c_ctx[None, :], c, jnp.zeros((COND_ROWS - 1 - dec_batch, d), F32)], axis=0)
    mods = _adaln(cond, ada_w, ada_b).reshape(depth, COND_ROWS, 6, d)

    x = (x_prompt.reshape(n_prompt, d), x_sample.reshape(dec_batch * dec_seq, d))
    ffn_wg, ffn_wu, ffn_wd = ffn_w_gate.astype(BF16), ffn_w_up.astype(BF16), ffn_w_down.astype(BF16)
    row1 = lambda a: a.reshape(1, -1)
    new_ckv, new_krope, new_hg, new_k, new_v = [], [], [], [], []
    for i in range(depth):
        kind, j = i % N_MIXERS, i // N_MIXERS
        gn = row1(norm_mix[i])
        if kind == 0:
            uq = mla_w_uq[j].reshape(-1, MLA_HEADS, MLA_NOPE_DIM + MLA_ROPE_DIM)
            uq_rope = jnp.pad(uq[:, :, MLA_NOPE_DIM:], ((0, 0), (0, 0), (0, LANES - MLA_ROPE_DIM)))
            uq = jnp.concatenate([uq[:, :, :MLA_NOPE_DIM].reshape(uq.shape[0], -1),
                                  uq_rope.reshape(uq.shape[0], -1)], axis=1)
            w = {
                "dq": mla_w_dq[j].astype(BF16), "q_norm": row1(mla_q_norm[j]), "uq": uq.astype(BF16),
                "dkv": jnp.pad(mla_w_dkv[j], ((0, 0), (0, LANES - MLA_ROPE_DIM))).astype(BF16),
                "kv_norm": row1(mla_kv_norm[j]),
                "uk": mla_w_uk[j].astype(BF16), "uv": mla_w_uv[j].astype(BF16),
            }
            q, ckv, kr_raw, kcat, vt = _mla_pre(x, mods[i], gn, cos_mla, sin_mla, w, pre_rows)
            kr_ctx = jnp.pad(cache_mla_krope[:, j].reshape(dec_batch * past, -1), ((0, 0), (0, LANES - MLA_ROPE_DIM)))
            kcat_ctx, vt_ctx = _mla_expand(cache_mla_ckv[:, j].reshape(dec_batch * past, -1), kr_ctx, w["uk"], w["uv"])
            mix = (_mla_attn_prompt(q, kcat, vt, batch, seq),
                   _mla_attn_latent(q, kcat, vt, kcat_ctx, vt_ctx, n_prompt, dec_batch, dec_seq, past, tq))
            wo = mla_w_o[j].astype(BF16)
            new_ckv.append(ckv.reshape(batch, seq, -1))
            new_krope.append(kr_raw.reshape(batch, seq, -1))
        elif kind == 1:
            w5 = jnp.concatenate([hg_w_q[j], hg_w_i[j], hg_w_g[j], hg_w_f[j, 0], hg_w_f[j, 1]], axis=1).astype(BF16)
            q, v, gate, f2, q_max = _hgrn_pre(x, mods[i], gn, w5, hg_lb_logits, i, hg_pre_rows)
            rt = _pick_tile(seq, dec_seq, 256)
            sm = jax.nn.softmax(hg_lb_logits.astype(F32), axis=1)
            lb_min = jnp.min(jnp.cumsum(sm, axis=1)[:, i] - sm[:, 0])
            worst = (HG_DIAG_BLOCK // 2) * -jnp.log(lb_min) + jnp.log(jnp.maximum(jnp.max(q_max), 1.0))
            fits = worst < HG_MAX_EXPONENT

            def scans(bounded):
                od_p, s_p = _hgrn_scan(q, v, f2, batch, seq, 0, rt, bounded=bounded)
                od_l, _ = _hgrn_scan(q, v, f2, dec_batch, dec_seq, n_prompt, rt, s0=state_hgrn[:, j], bounded=bounded)
                return od_p, od_l, s_p

            od_p, od_l, s_prompt = lax.cond(fits, lambda: scans(True), lambda: scans(False))
            mix = (od_p, od_l, gate, row1(hg_o_norm[j]))
            wo = hg_w_o[j].astype(BF16)
            new_hg.append(s_prompt)
        else:
            wqkv = jnp.concatenate([swa_w_q[j], swa_w_k[j], swa_w_v[j]], axis=1).astype(BF16)
            q, kx, vt, k_raw, v_raw = _swa_pre(x, mods[i], gn, cos_swa, sin_swa, wqkv, pre_rows)
            sink = swa_sink[j]
            k_ctx = cache_swa_k[:, j].reshape(dec_batch * past, SWA_KV_HEADS, 1, SWA_HEAD_DIM).transpose(1, 0, 2, 3)
            kx_ctx = jnp.broadcast_to(k_ctx, (SWA_KV_HEADS, dec_batch * past, SWA_GROUP, SWA_HEAD_DIM))
            kx_ctx = kx_ctx.reshape(SWA_KV_HEADS, dec_batch * past, SWA_GW).astype(BF16)
            vt_ctx = cache_swa_v[:, j].reshape(dec_batch * past, SWA_KV_HEADS, SWA_HEAD_DIM).transpose(1, 2, 0).astype(BF16)
            mix = (_swa_attn_prompt(q, kx, vt, sink, batch, seq),
                   _swa_attn_latent(q, kx, vt, kx_ctx, vt_ctx, sink, n_prompt, dec_batch, dec_seq, past, tq))
            wo = swa_w_o[j].astype(BF16)
            new_k.append(k_raw.reshape(batch, seq, SWA_KV_HEADS, SWA_HEAD_DIM))
            new_v.append(v_raw.reshape(batch, seq, SWA_KV_HEADS, SWA_HEAD_DIM))
        x = _post(mix, x, mods[i], wo, row1(norm_ffn[i]), ffn_wg, ffn_wu, ffn_wd, i,
                  hg_post_rows if kind == 1 else post_rows,
                  final_norm=row1(final_norm) if i == depth - 1 else None, hgrn=kind == 1,
                  pair_out=i == depth - 1)
    y_prompt = x[0].reshape(batch, seq, d)
    y_sample = x[1].reshape(dec_batch, dec_seq, d)
    return (y_prompt, y_sample, jnp.stack(new_ckv, axis=1), jnp.stack(new_krope, axis=1),
            jnp.stack(new_hg, axis=1), jnp.stack(new_k, axis=1), jnp.stack(new_v, axis=1))
```

```python
import functools

import jax
import jax.numpy as jnp
from jax import lax
from jax.experimental import pallas as pl
from jax.experimental.pallas import tpu as pltpu

F32 = jnp.float32
BF16 = jnp.bfloat16

GRID_W = 64
N_MIXERS = 3

MLA_HEADS = 8
MLA_KV_LORA = 256
MLA_NOPE_DIM = 128
MLA_ROPE_DIM = 64
MLA_V_DIM = 128
MLA_QK_PAD = 256
MLA_SCALE = (MLA_NOPE_DIM + MLA_ROPE_DIM) ** -0.5

HG_HEADS = 8
HG_DK = 128
HG_DV = 128
HG_DIAG_BLOCK = 32
HG_MAX_EXPONENT = 80.0

SWA_HEADS = 16
SWA_KV_HEADS = 4
SWA_GROUP = SWA_HEADS // SWA_KV_HEADS
SWA_HEAD_DIM = 64
SWA_GW = SWA_GROUP * SWA_HEAD_DIM
SWA_WINDOW = 128
SWA_SCALE = SWA_HEAD_DIM ** -0.5

ROPE_BASE = 10000.0
ROPE_PERIOD = 64
ROPE_QUARTER = 16
NORM_EPS = 1e-6
NEG_INF = -1e30
LOG2_E = 1.4426950408889634

LANES = 128
COND_ROWS = 8
VMEM_LIMIT = 56 * 1024 * 1024


def _sigmoid(x):
    return jax.nn.sigmoid(x)


def _silu(x):
    return x * jax.nn.sigmoid(x)


def _rms(x, g):
    return x * lax.rsqrt(jnp.mean(x * x, axis=-1, keepdims=True) + NORM_EPS) * g


def _modulate(x, g, shift, scale):
    return _rms(x, g) * (1.0 + scale) + shift


def _dot(a, b):
    return jnp.dot(a, b, preferred_element_type=F32)


def _dot_nt(a, b):
    return lax.dot_general(a, b, (((1,), (1,)), ((), ())), preferred_element_type=F32)


def _dot_tn(a, b):
    return lax.dot_general(a, b, (((0,), (0,)), ((), ())), preferred_element_type=F32)


def _swap_pairs(x):
    n = x.shape[1]
    lane = lax.broadcasted_iota(jnp.int32, x.shape, 1)
    ahead = pltpu.roll(x, n - ROPE_QUARTER, 1)
    behind = pltpu.roll(x, ROPE_QUARTER, 1)
    return jnp.where((lane & (2 * ROPE_QUARTER - 1)) < ROPE_QUARTER, ahead, behind)


def _rope(x, cos, sin):
    reps = x.shape[1] // cos.shape[1]
    if reps > 1:
        cos = jnp.concatenate([cos] * reps, axis=1)
        sin = jnp.concatenate([sin] * reps, axis=1)
    return x * cos + _swap_pairs(x) * sin


def _whole(shape):
    zeros = (0,) * len(shape)
    return pl.BlockSpec(shape, lambda *_: zeros, pipeline_mode=pl.Buffered(1))


def _params(*sem):
    return pltpu.CompilerParams(dimension_semantics=sem, vmem_limit_bytes=VMEM_LIMIT)


class _Rows:
    def __init__(self, n_prompt_rows, dec_seq, n_rows, tm):
        assert n_prompt_rows % tm == 0 and dec_seq % tm == 0
        self.tm = tm
        self.n_tiles = n_rows // tm
        self.prompt_tiles = n_prompt_rows // tm
        self.seq_tiles = dec_seq // tm

    def cond(self, i):
        return jnp.where(i < self.prompt_tiles, 0, 1 + jnp.maximum(i - self.prompt_tiles, 0) // self.seq_tiles)

    def rope_block(self, i):
        return jnp.where(i < self.prompt_tiles, 0, 1 + jnp.maximum(i - self.prompt_tiles, 0) % self.seq_tiles)

    def row_spec(self, width):
        return pl.BlockSpec((self.tm, width), lambda i: (i, 0))

    def col_spec(self, height):
        return pl.BlockSpec((height, self.tm), lambda i: (0, i))

    def mod_spec(self, d):
        return pl.BlockSpec((1, 6, d), lambda i: (self.cond(i), 0, 0))

    def rope_spec(self, width):
        return pl.BlockSpec((self.tm, width), lambda i: (self.rope_block(i), 0))

    def x_specs(self, width):
        return [self.prompt_spec(width), self.latent_spec(width)]

    def is_prompt(self):
        return pl.program_id(0) < self.prompt_tiles

    def prompt_spec(self, width):
        return pl.BlockSpec((self.tm, width), lambda i: (jnp.minimum(i, self.prompt_tiles - 1), 0))

    def latent_spec(self, width):
        return pl.BlockSpec((self.tm, width), lambda i: (jnp.maximum(i - self.prompt_tiles, 0), 0))


def _rope_tables(dec_seq, tm):
    pos = jnp.arange(dec_seq)
    row = (pos // GRID_W).astype(F32)
    col = (pos % GRID_W).astype(F32)
    inv_freq = ROPE_BASE ** (-jnp.arange(ROPE_QUARTER, dtype=F32) / ROPE_QUARTER)
    ang_r = row[:, None] * inv_freq[None, :]
    ang_c = col[:, None] * inv_freq[None, :]
    cos = jnp.concatenate([jnp.cos(ang_r), jnp.cos(ang_r), jnp.cos(ang_c), jnp.cos(ang_c)], axis=1)
    sin = jnp.concatenate([-jnp.sin(ang_r), jnp.sin(ang_r), -jnp.sin(ang_c), jnp.sin(ang_c)], axis=1)
    cos = jnp.concatenate([jnp.ones((tm, ROPE_PERIOD), F32), cos], axis=0)
    sin = jnp.concatenate([jnp.zeros((tm, ROPE_PERIOD), F32), sin], axis=0)
    return cos, sin


def _adaln_kernel(c_ref, w_ref, b_ref, o_ref):
    o_ref[0] = _dot(_silu(c_ref[...]), w_ref[0]) + b_ref[0]


def _adaln(cond, ada_w, ada_b, tn=1536):
    depth, d, n = ada_w.shape
    return pl.pallas_call(
        _adaln_kernel,
        grid=(depth, n // tn),
        in_specs=[pl.BlockSpec((COND_ROWS, d), lambda l, j: (0, 0)),
                  pl.BlockSpec((1, d, tn), lambda l, j: (l, 0, j)),
                  pl.BlockSpec((1, 1, tn), lambda l, j: (l, 0, j))],
        out_specs=pl.BlockSpec((1, COND_ROWS, tn), lambda l, j: (l, 0, j)),
        out_shape=jax.ShapeDtypeStruct((depth, COND_ROWS, n), F32),
        compiler_params=_params("arbitrary", "arbitrary"),
        name="adaln",
    )(cond, ada_w, ada_b.reshape(depth, 1, n))


def _mla_store_heads(q_ref, kcat_ref, vt_ref, qn, qr_pad, kn, kr_pad, v):
    vt = v.T.astype(BF16)
    for h in range(MLA_HEADS):
        nope = slice(h * MLA_NOPE_DIM, (h + 1) * MLA_NOPE_DIM)
        if q_ref is not None:
            q_ref[h] = jnp.concatenate([qn[:, nope], qr_pad[:, h * LANES:(h + 1) * LANES]], axis=1)
        kcat_ref[h] = jnp.concatenate([kn[:, nope], kr_pad], axis=1)
        vt_ref[h] = vt[h * MLA_V_DIM:(h + 1) * MLA_V_DIM, :]


def _read_rows(is_prompt, p_ref, l_ref):
    return jnp.where(is_prompt, p_ref[...], l_ref[...])


def _take_rows(refs, pair, prompt_tiles):
    is_prompt = pl.program_id(0) < prompt_tiles
    if pair:
        return _read_rows(is_prompt, refs[0], refs[1]), is_prompt, refs[2:]
    return refs[0][...], is_prompt, refs[1:]


def _rows_inputs(x, rows):
    if isinstance(x, tuple):
        d = x[0].shape[1]
        return list(x), rows.x_specs(d), True, x[0].shape[0] + x[1].shape[0], d
    return [x], [rows.row_spec(x.shape[1])], False, x.shape[0], x.shape[1]


def _mla_pre_kernel(*refs, pair, prompt_tiles):
    x, is_prompt, refs = _take_rows(refs, pair, prompt_tiles)
    (mod_ref, gn_ref, cos_ref, sin_ref, wdq_ref, qnorm_ref, wuq_ref, wdkv_ref, kvnorm_ref, wuk_ref, wuv_ref,
     q_ref, ckv_ref, krraw_ref, kcat_ref, vt_ref) = refs
    m = mod_ref[0]
    h = _modulate(x, gn_ref[...], m[0:1], m[1:2]).astype(BF16)
    q_lat = _rms(_dot(h, wdq_ref[...]), qnorm_ref[...]).astype(BF16)
    cos, sin = cos_ref[...], sin_ref[...]
    nn = MLA_HEADS * MLA_NOPE_DIM
    qn = _dot(q_lat, wuq_ref[:, :nn]).astype(BF16)
    qr_pad = _rope(_dot(q_lat, wuq_ref[:, nn:]), cos, sin).astype(BF16)
    kv = _dot(h, wdkv_ref[...])
    ckv = _rms(kv[:, :MLA_KV_LORA], kvnorm_ref[...])
    kr = kv[:, MLA_KV_LORA:]

    @pl.when(is_prompt)
    def _():
        ckv_ref[...] = ckv
        krraw_ref[...] = kr[:, :MLA_ROPE_DIM]

    kr_pad = _rope(kr, cos, sin).astype(BF16)
    cb = ckv.astype(BF16)
    _mla_store_heads(q_ref, kcat_ref, vt_ref, qn, qr_pad, _dot(cb, wuk_ref[...]).astype(BF16), kr_pad,
                     _dot(cb, wuv_ref[...]))


def _mla_pre(x, mods, gn, cos, sin, w, rows):
    xs, x_specs, pair, t, d = _rows_inputs(x, rows)
    tm = rows.tm
    n_prompt = rows.prompt_tiles * tm
    weights = [w["dq"], w["q_norm"], w["uq"], w["dkv"], w["kv_norm"], w["uk"], w["uv"]]
    ins = xs + [mods, gn, cos, sin] + weights
    in_specs = x_specs + [rows.mod_spec(d), _whole(gn.shape),
                          rows.rope_spec(cos.shape[1]), rows.rope_spec(sin.shape[1])]
    in_specs += [_whole(a.shape) for a in weights]
    heads_rows = pl.BlockSpec((MLA_HEADS, tm, MLA_QK_PAD), lambda i: (0, i, 0))
    return pl.pallas_call(
        functools.partial(_mla_pre_kernel, pair=pair, prompt_tiles=rows.prompt_tiles),
        grid=(rows.n_tiles,),
        in_specs=in_specs,
        out_specs=[heads_rows, rows.prompt_spec(MLA_KV_LORA), rows.prompt_spec(MLA_ROPE_DIM), heads_rows,
                   pl.BlockSpec((MLA_HEADS, MLA_V_DIM, tm), lambda i: (0, 0, i))],
        out_shape=[jax.ShapeDtypeStruct((MLA_HEADS, t, MLA_QK_PAD), BF16),
                   jax.ShapeDtypeStruct((n_prompt, MLA_KV_LORA), F32),
                   jax.ShapeDtypeStruct((n_prompt, MLA_ROPE_DIM), F32),
                   jax.ShapeDtypeStruct((MLA_HEADS, t, MLA_QK_PAD), BF16),
                   jax.ShapeDtypeStruct((MLA_HEADS, MLA_V_DIM, t), BF16)],
        compiler_params=_params("arbitrary"),
        name="mla_pre",
    )(*ins)


def _mla_expand_kernel(c_ref, kr_ref, wuk_ref, wuv_ref, kcat_ref, vt_ref):
    cb = c_ref[...].astype(BF16)
    _mla_store_heads(None, kcat_ref, vt_ref, None, None, _dot(cb, wuk_ref[...]).astype(BF16),
                     kr_ref[...].astype(BF16), _dot(cb, wuv_ref[...]))


def _mla_expand(ckv, kr_pad, wuk, wuv):
    n = ckv.shape[0]
    return pl.pallas_call(
        _mla_expand_kernel,
        out_shape=[jax.ShapeDtypeStruct((MLA_HEADS, n, MLA_QK_PAD), BF16),
                   jax.ShapeDtypeStruct((MLA_HEADS, MLA_V_DIM, n), BF16)],
        compiler_params=pltpu.CompilerParams(vmem_limit_bytes=VMEM_LIMIT),
        name="mla_expand",
    )(ckv, kr_pad, wuk, wuv)


def _mla_attn_kernel(*refs, n_src):
    q_ref = refs[0]
    srcs = [(refs[1 + 2 * i], refs[2 + 2 * i]) for i in range(n_src)]
    o_ref, o_buf = refs[1 + 2 * n_src:3 + 2 * n_src]
    s_bufs = refs[3 + 2 * n_src:]
    n_keys = [k_ref.shape[1] for k_ref, _ in srcs]
    starts = [sum(n_keys[:i]) for i in range(n_src)]

    def put_scores(h, s_buf):
        q = q_ref[h]
        for (k_ref, _), first, n in zip(srcs, starts, n_keys):
            s_buf[first:first + n, :] = _dot_nt(k_ref[h], q)

    def finish(h, s_buf):
        s = s_buf[...]
        mx = jnp.max(s, axis=0, keepdims=True)
        p = jnp.exp2((s - mx) * (MLA_SCALE * LOG2_E))
        den = jnp.sum(p, axis=0, keepdims=True)
        p = p.astype(BF16)
        acc = None
        for (_, vt_ref), first, n in zip(srcs, starts, n_keys):
            a = _dot(vt_ref[h], p[first:first + n, :])
            acc = a if acc is None else acc + a
        o_buf[h] = (acc / den).T.astype(BF16)

    _pipelined_heads(MLA_HEADS, put_scores, finish, s_bufs)
    for h in range(MLA_HEADS):
        o_ref[:, h * MLA_V_DIM:(h + 1) * MLA_V_DIM] = o_buf[h]


def _mla_attn_scratch(n_keys, tq):
    return [pltpu.VMEM((MLA_HEADS, tq, MLA_V_DIM), BF16)] + [pltpu.VMEM((n_keys, tq), F32)] * 4


def _pipelined_heads(n_heads, put_scores, finish, bufs):
    a, b, c, d = bufs
    assert n_heads % 4 == 0
    put_scores(0, a)
    put_scores(1, b)

    def quad(j, carry):
        h = 4 * j
        put_scores(h + 2, c)
        put_scores(h + 3, d)
        finish(h, a)
        finish(h + 1, b)
        put_scores(h + 4, a)
        put_scores(h + 5, b)
        finish(h + 2, c)
        finish(h + 3, d)
        return carry

    lax.fori_loop(0, n_heads // 4 - 1, quad, 0)
    h = n_heads - 4
    put_scores(h + 2, c)
    put_scores(h + 3, d)
    finish(h, a)
    finish(h + 1, b)
    finish(h + 2, c)
    finish(h + 3, d)


def _staged_attention(scores, values_t, scale, sinks=None):
    c = scale * LOG2_E
    mx = [jnp.max(s, axis=0, keepdims=True) for s in scores]
    if sinks is not None:
        mx = [jnp.maximum(m, z) for m, z in zip(mx, sinks)]
    p = [jnp.exp2((s - m) * c) for s, m in zip(scores, mx)]
    den = [jnp.sum(x, axis=0, keepdims=True) for x in p]
    if sinks is not None:
        den = [d + jnp.exp2((z - m) * c) for d, z, m in zip(den, sinks, mx)]
    acc = [_dot(v, x.astype(BF16)) for v, x in zip(values_t, p)]
    return [a / d for a, d in zip(acc, den)]


def _mla_attn_prompt_kernel(q_ref, k_ref, vt_ref, o_ref):
    scores = [_dot_nt(k_ref[h], q_ref[h]) for h in range(MLA_HEADS)]
    outs = _staged_attention(scores, [vt_ref[h] for h in range(MLA_HEADS)], MLA_SCALE)
    for h in range(MLA_HEADS):
        o_ref[:, h * MLA_V_DIM:(h + 1) * MLA_V_DIM] = outs[h].T.astype(BF16)


def _mla_attn_prompt(q, kcat, vt, batch, seq):
    hv = MLA_HEADS * MLA_V_DIM
    rows = pl.BlockSpec((MLA_HEADS, seq, MLA_QK_PAD), lambda b: (0, b, 0))
    return pl.pallas_call(
        _mla_attn_prompt_kernel,
        grid=(batch,),
        in_specs=[rows, rows, pl.BlockSpec((MLA_HEADS, MLA_V_DIM, seq), lambda b: (0, 0, b))],
        out_specs=pl.BlockSpec((seq, hv), lambda b: (b, 0)),
        out_shape=jax.ShapeDtypeStruct((batch * seq, hv), BF16),
        compiler_params=_params("arbitrary"),
        name="mla_attn_prompt",
    )(q, kcat, vt)


def _mla_attn_latent(q, kcat, vt, kcat_ctx, vt_ctx, n_prompt, dec_batch, dec_seq, past, tq):
    hv = MLA_HEADS * MLA_V_DIM
    nq = dec_seq // tq
    lat_blk = n_prompt // dec_seq
    return pl.pallas_call(
        functools.partial(_mla_attn_kernel, n_src=2),
        grid=(dec_batch, nq),
        in_specs=[pl.BlockSpec((MLA_HEADS, tq, MLA_QK_PAD), lambda b, i: (0, n_prompt // tq + b * nq + i, 0)),
                  pl.BlockSpec((MLA_HEADS, past, MLA_QK_PAD), lambda b, i: (0, b, 0)),
                  pl.BlockSpec((MLA_HEADS, MLA_V_DIM, past), lambda b, i: (0, 0, b)),
                  pl.BlockSpec((MLA_HEADS, dec_seq, MLA_QK_PAD), lambda b, i: (0, lat_blk + b, 0)),
                  pl.BlockSpec((MLA_HEADS, MLA_V_DIM, dec_seq), lambda b, i: (0, 0, lat_blk + b))],
        out_specs=pl.BlockSpec((tq, hv), lambda b, i: (b * nq + i, 0)),
        out_shape=jax.ShapeDtypeStruct((dec_batch * dec_seq, hv), BF16),
        scratch_shapes=_mla_attn_scratch(past + dec_seq, tq),
        compiler_params=_params("arbitrary", "arbitrary"),
        name="mla_attn_latent",
    )(q, kcat_ctx, vt_ctx, kcat, vt)


def _tile_heads(x):
    n = x.shape[1]
    block = lax.broadcasted_iota(jnp.int32, x.shape, 1) // SWA_HEAD_DIM
    rolled = [x] + [pltpu.roll(x, s * SWA_HEAD_DIM, 1) for s in range(1, SWA_KV_HEADS)]
    out = []
    for kvh in range(SWA_KV_HEADS):
        blk = rolled[(0 - kvh) % SWA_KV_HEADS]
        for g in range(1, n // SWA_HEAD_DIM):
            blk = jnp.where(block == g, rolled[(g - kvh) % SWA_KV_HEADS], blk)
        out.append(blk)
    return jnp.concatenate(out, axis=1)


def _swa_pre_kernel(*refs, pair, prompt_tiles):
    x, is_prompt, refs = _take_rows(refs, pair, prompt_tiles)
    mod_ref, gn_ref, cos_ref, sin_ref, wqkv_ref, q_ref, kx_ref, vt_ref, kraw_ref, vraw_ref = refs
    m = mod_ref[0]
    h = _modulate(x, gn_ref[...], m[0:1], m[1:2]).astype(BF16)
    qkv = _dot(h, wqkv_ref[...])
    nq, nk = SWA_HEADS * SWA_HEAD_DIM, SWA_KV_HEADS * SWA_HEAD_DIM
    cos, sin = cos_ref[...], sin_ref[...]
    q = _rope(qkv[:, :nq], cos, sin).astype(BF16)
    k = qkv[:, nq:nq + nk]
    v = qkv[:, nq + nk:]

    @pl.when(is_prompt)
    def _():
        kraw_ref[...] = k
        vraw_ref[...] = v

    kx = _tile_heads(_rope(k, cos, sin)).astype(BF16)
    vt = v.T.astype(BF16)
    for kvh in range(SWA_KV_HEADS):
        q_ref[kvh] = q[:, kvh * SWA_GW:(kvh + 1) * SWA_GW]
        kx_ref[kvh] = kx[:, kvh * SWA_GW:(kvh + 1) * SWA_GW]
        vt_ref[kvh] = vt[kvh * SWA_HEAD_DIM:(kvh + 1) * SWA_HEAD_DIM, :]


def _swa_pre(x, mods, gn, cos, sin, wqkv, rows):
    xs, x_specs, pair, t, d = _rows_inputs(x, rows)
    tm = rows.tm
    n_prompt = rows.prompt_tiles * tm
    nk = SWA_KV_HEADS * SWA_HEAD_DIM
    heads_rows = pl.BlockSpec((SWA_KV_HEADS, tm, SWA_GW), lambda i: (0, i, 0))
    return pl.pallas_call(
        functools.partial(_swa_pre_kernel, pair=pair, prompt_tiles=rows.prompt_tiles),
        grid=(rows.n_tiles,),
        in_specs=x_specs + [rows.mod_spec(d), _whole(gn.shape),
                            rows.rope_spec(cos.shape[1]), rows.rope_spec(sin.shape[1]), _whole(wqkv.shape)],
        out_specs=[heads_rows, heads_rows, pl.BlockSpec((SWA_KV_HEADS, SWA_HEAD_DIM, tm), lambda i: (0, 0, i)),
                   rows.prompt_spec(nk), rows.prompt_spec(nk)],
        out_shape=[jax.ShapeDtypeStruct((SWA_KV_HEADS, t, SWA_GW), BF16),
                   jax.ShapeDtypeStruct((SWA_KV_HEADS, t, SWA_GW), BF16),
                   jax.ShapeDtypeStruct((SWA_KV_HEADS, SWA_HEAD_DIM, t), BF16),
                   jax.ShapeDtypeStruct((n_prompt, nk), F32), jax.ShapeDtypeStruct((n_prompt, nk), F32)],
        compiler_params=_params("arbitrary"),
        name="swa_pre",
    )(*xs, mods, gn, cos, sin, wqkv)


def _swa_heads(q_ref, sink_ref, o_ref, srcs, bias_ref, ot_buf, s_bufs):
    tq = q_ref.shape[1]
    n_keys = [kx_ref.shape[1] for kx_ref, _ in srcs]
    starts = [sum(n_keys[:i]) for i in range(len(srcs))]
    group = lax.broadcasted_iota(jnp.int32, (tq, SWA_GW), 1) // SWA_HEAD_DIM

    def put_scores(hq, s_buf):
        kvh, g = hq // SWA_GROUP, hq % SWA_GROUP
        q = jnp.where(group == g, q_ref[kvh].astype(F32), 0.0).astype(BF16)
        for (kx_ref, _), first, n in zip(srcs, starts, n_keys):
            s_buf[first:first + n, :] = _dot_nt(kx_ref[kvh], q)

    def finish(hq, s_buf):
        kvh, g = hq // SWA_GROUP, hq % SWA_GROUP
        s = s_buf[...]
        if bias_ref is not None:
            s = s + bias_ref[...]
        sink = jnp.full((1, 1), sink_ref[hq] * (1.0 / SWA_SCALE), F32)
        mx = jnp.maximum(jnp.max(s, axis=0, keepdims=True), sink)
        p = jnp.exp2((s - mx) * (SWA_SCALE * LOG2_E))
        den = jnp.sum(p, axis=0, keepdims=True) + jnp.exp2((sink - mx) * (SWA_SCALE * LOG2_E))
        p = p.astype(BF16)
        acc = None
        for (_, vt_ref), first, n in zip(srcs, starts, n_keys):
            a = _dot(vt_ref[kvh], p[first:first + n, :])
            acc = a if acc is None else acc + a
        ot_buf[kvh, pl.ds(pl.multiple_of(g * SWA_HEAD_DIM, SWA_HEAD_DIM), SWA_HEAD_DIM), :] = acc / den

    _pipelined_heads(SWA_HEADS, put_scores, finish, s_bufs)
    for kvh in range(SWA_KV_HEADS):
        o_ref[:, kvh * SWA_GW:(kvh + 1) * SWA_GW] = ot_buf[kvh].T.astype(BF16)


def _swa_attn_scratch(n_keys, tq):
    return [pltpu.VMEM((SWA_KV_HEADS, SWA_GW, tq), F32)] + [pltpu.VMEM((n_keys, tq), F32)] * 5


def _swa_attn_prompt_kernel(q_ref, kx_ref, vt_ref, sink_ref, o_ref):
    group = lax.broadcasted_iota(jnp.int32, (q_ref.shape[1], SWA_GW), 1) // SWA_HEAD_DIM
    scores, values_t, sinks = [], [], []
    for kvh in range(SWA_KV_HEADS):
        q_all = q_ref[kvh].astype(F32)
        for g in range(SWA_GROUP):
            q = jnp.where(group == g, q_all, 0.0).astype(BF16)
            scores.append(_dot_nt(kx_ref[kvh], q))
            values_t.append(vt_ref[kvh])
            sinks.append(jnp.full((1, 1), sink_ref[kvh * SWA_GROUP + g] * (1.0 / SWA_SCALE), F32))
    outs = _staged_attention(scores, values_t, SWA_SCALE, sinks)
    for kvh in range(SWA_KV_HEADS):
        ot = jnp.concatenate(outs[kvh * SWA_GROUP:(kvh + 1) * SWA_GROUP], axis=0)
        o_ref[:, kvh * SWA_GW:(kvh + 1) * SWA_GW] = ot.T.astype(BF16)


def _swa_attn_prompt(q, kx, vt, sink, batch, seq):
    wq = SWA_HEADS * SWA_HEAD_DIM
    rows = pl.BlockSpec((SWA_KV_HEADS, seq, SWA_GW), lambda b: (0, b, 0))
    return pl.pallas_call(
        _swa_attn_prompt_kernel,
        grid=(batch,),
        in_specs=[rows, rows, pl.BlockSpec((SWA_KV_HEADS, SWA_HEAD_DIM, seq), lambda b: (0, 0, b)),
                  pl.BlockSpec(memory_space=pltpu.SMEM)],
        out_specs=pl.BlockSpec((seq, wq), lambda b: (b, 0)),
        out_shape=jax.ShapeDtypeStruct((batch * seq, wq), BF16),
        compiler_params=_params("arbitrary"),
        name="swa_attn_prompt",
    )(q, kx, vt, sink)


def _swa_attn_latent_kernel(q_ref, kc_ref, vc_ref, kp_ref, vp_ref, km_ref, vm_ref, kn_ref, vn_ref,
                            sink_ref, o_ref, ot_buf, s_a, s_b, s_c, s_d, bias_ref, *, tq, dec_seq, past):
    i = pl.program_id(1)
    qpos = i * tq + lax.broadcasted_iota(jnp.int32, (1, tq), 1)

    def band(first, n):
        kpos = first + lax.broadcasted_iota(jnp.int32, (n, 1), 0)
        valid = (jnp.abs(qpos - kpos) <= SWA_WINDOW) & (kpos >= 0) & (kpos < dec_seq)
        return jnp.where(valid, 0.0, NEG_INF)

    bias_ref[...] = jnp.concatenate(
        [jnp.zeros((past, tq), F32), band(i * tq - SWA_WINDOW, SWA_WINDOW), band(i * tq, tq),
         band((i + 1) * tq, SWA_WINDOW)], axis=0)
    srcs = [(kc_ref, vc_ref), (kp_ref, vp_ref), (km_ref, vm_ref), (kn_ref, vn_ref)]
    _swa_heads(q_ref, sink_ref, o_ref, srcs, bias_ref, ot_buf, (s_a, s_b, s_c, s_d))


def _swa_attn_latent(q, kx, vt, kx_ctx, vt_ctx, sink, n_prompt, dec_batch, dec_seq, past, tq):
    t = q.shape[1]
    wq = SWA_HEADS * SWA_HEAD_DIM
    nq = dec_seq // tq
    w = SWA_WINDOW
    n_keys = past + tq + 2 * w
    first = lambda b, i: n_prompt + b * dec_seq + i * tq
    prev = lambda b, i: first(b, i) // w - 1
    nxt = lambda b, i: jnp.minimum((first(b, i) + tq) // w, t // w - 1)
    rows = lambda n, blk: pl.BlockSpec((SWA_KV_HEADS, n, SWA_GW), lambda b, i: (0, blk(b, i), 0))
    cols = lambda n, blk: pl.BlockSpec((SWA_KV_HEADS, SWA_HEAD_DIM, n), lambda b, i: (0, 0, blk(b, i)))
    main = lambda b, i: first(b, i) // tq
    ctx = lambda b, i: b
    return pl.pallas_call(
        functools.partial(_swa_attn_latent_kernel, tq=tq, dec_seq=dec_seq, past=past),
        grid=(dec_batch, nq),
        in_specs=[rows(tq, main), rows(past, ctx), cols(past, ctx), rows(w, prev), cols(w, prev),
                  rows(tq, main), cols(tq, main), rows(w, nxt), cols(w, nxt),
                  pl.BlockSpec(memory_space=pltpu.SMEM)],
        out_specs=pl.BlockSpec((tq, wq), lambda b, i: (b * nq + i, 0)),
        out_shape=jax.ShapeDtypeStruct((dec_batch * dec_seq, wq), BF16),
        scratch_shapes=_swa_attn_scratch(n_keys, tq),
        compiler_params=_params("arbitrary", "arbitrary"),
        name="swa_attn_latent",
    )(q, kx_ctx, vt_ctx, kx, vt, kx, vt, kx, vt, sink)


def _hgrn_pre_kernel(*refs, layer, pair, prompt_tiles):
    x, _, refs = _take_rows(refs, pair, prompt_tiles)
    mod_ref, gn_ref, w_ref, lbl_ref, q_ref, v_ref, g_ref, f_ref, qmax_ref = refs
    m = mod_ref[0]
    h = _modulate(x, gn_ref[...], m[0:1], m[1:2]).astype(BF16)
    y = _dot(h, w_ref[...])
    n = HG_HEADS * HG_DK
    q = _silu(y[:, :n])
    q_ref[...] = q
    qmax_ref[0] = jnp.max(jnp.abs(q), axis=0, keepdims=True)
    v_ref[...] = y[:, n:2 * n].astype(BF16)
    g_ref[...] = _silu(y[:, 2 * n:3 * n])
    for d in range(2):
        logits = lbl_ref[d]
        e = jnp.exp(logits - jnp.max(logits, axis=0, keepdims=True))
        s = e / jnp.sum(e, axis=0, keepdims=True)
        cs = s[0:1]
        for r in range(1, layer + 1):
            cs = cs + s[r:r + 1]
        lb = cs - s[0:1]
        f_ref[d] = lb + (1.0 - lb) * _sigmoid(y[:, (3 + d) * n:(4 + d) * n])


def _hgrn_pre(x, mods, gn, w5, lb_logits, layer, rows):
    xs, x_specs, pair, t, d = _rows_inputs(x, rows)
    n = HG_HEADS * HG_DK
    return pl.pallas_call(
        functools.partial(_hgrn_pre_kernel, layer=layer, pair=pair, prompt_tiles=rows.prompt_tiles),
        grid=(rows.n_tiles,),
        in_specs=x_specs + [rows.mod_spec(d), _whole(gn.shape), _whole(w5.shape), _whole(lb_logits.shape)],
        out_specs=[rows.row_spec(n), rows.row_spec(n), rows.row_spec(n),
                   pl.BlockSpec((2, rows.tm, n), lambda i: (0, i, 0)), pl.BlockSpec((1, 1, n), lambda i: (i, 0, 0))],
        out_shape=[jax.ShapeDtypeStruct((t, n), F32), jax.ShapeDtypeStruct((t, n), BF16),
                   jax.ShapeDtypeStruct((t, n), F32), jax.ShapeDtypeStruct((2, t, n), F32),
                   jax.ShapeDtypeStruct((rows.n_tiles, 1, n), F32)],
        compiler_params=_params("arbitrary"),
        name="hgrn_pre",
    )(*xs, mods, gn, w5, lb_logits)


def _tri_cumsum(tri, x):
    hi = x.astype(BF16)
    r1 = x - hi.astype(F32)
    mid = r1.astype(BF16)
    lo = (r1 - mid.astype(F32)).astype(BF16)
    return _dot(tri, hi) + _dot(tri, mid) + _dot(tri, lo)


def _hgrn_tile(q_ref, v_ref, f_ref, o_ref, st_ref, *, rows, reverse, bounded, slot=0, sfin_ref=None):
    r = rows
    diag = HG_DIAG_BLOCK if bounded else 1
    a = lax.broadcasted_iota(jnp.int32, (r, r), 0)
    b = lax.broadcasted_iota(jnp.int32, (r, r), 1)
    seen = (b >= a) if reverse else (b <= a)
    tri = jnp.where(seen, 1.0, 0.0).astype(BF16)

    q = q_ref[...]
    f = f_ref[slot]
    vb = v_ref[...].astype(BF16)
    lf = jnp.log(f)
    cum = _tri_cumsum(tri, lf)
    tot = cum[0:1, :] if reverse else cum[r - 1:r, :]
    kk = 1.0 - f
    q_in = (q * jnp.exp(cum)).astype(BF16)
    k_d = (kk * jnp.exp(tot - cum)).astype(BF16)
    e_tot = jnp.exp(tot)

    levels = []
    c = r // 2
    while c >= diag:
        q_half = (a % (2 * c) < c) if reverse else (a % (2 * c) >= c)
        k_half = (b % (2 * c) >= c) if reverse else (b % (2 * c) < c)
        levels.append((2 * c, c if reverse else c - 1, ((a // (2 * c)) == (b // (2 * c))) & q_half & k_half))
        c //= 2
    levels.append((diag, diag // 2, ((a // diag) == (b // diag)) & seen))

    factors = []
    for size, ref_row, own in levels:
        if size == 1:
            factors.append((q.astype(BF16), kk.astype(BF16), own))
            continue
        if size >= 8:
            ref = jnp.concatenate(
                [jnp.broadcast_to(cum[j * size + ref_row:j * size + ref_row + 1, :], (size, cum.shape[1]))
                 for j in range(r // size)], axis=0)
        else:
            ref = _tri_cumsum(jnp.where(b == (a // size) * size + ref_row, 1.0, 0.0).astype(BF16), cum)
        factors.append(((q * jnp.exp(cum - ref)).astype(BF16), (kk * jnp.exp(ref - cum)).astype(BF16), own))

    outs = []
    for h in range(HG_HEADS):
        sl = slice(h * HG_DK, (h + 1) * HG_DK)
        att = jnp.zeros((r, r), F32)
        for q_l, k_l, own in factors:
            att = jnp.where(own, _dot_nt(q_l[:, sl], k_l[:, sl]), att)
        intra = _dot(att.astype(BF16), vb[:, sl])
        if sfin_ref is not None:
            outs.append(intra)
            sfin_ref[0, slot, h] = _dot_tn(k_d[:, sl], vb[:, sl])
            continue
        st = st_ref[h]
        outs.append(_dot_nt(q_in[:, sl], st.astype(BF16)) + intra)
        st_ref[h] = st * e_tot[:, sl] + _dot_tn(vb[:, sl], k_d[:, sl])
    o_ref[slot] = jnp.concatenate(outs, axis=1)


def _hgrn_whole_kernel(q_ref, v_ref, f_ref, o_ref, sfin_ref, *, rows, bounded):
    for slot, reverse in enumerate((False, True)):
        _hgrn_tile(q_ref, v_ref, f_ref, o_ref, None, rows=rows, reverse=reverse, bounded=bounded, slot=slot,
                   sfin_ref=sfin_ref)


def _hgrn_scan_kernel(*refs, rows, has_init, bounded):
    if has_init:
        q_ref, v_ref, f_ref, s0_ref, o_ref, sfin_ref, st_ref = refs
    else:
        q_ref, v_ref, f_ref, o_ref, sfin_ref, st_ref = refs
    d = pl.program_id(1)
    t = pl.program_id(2)

    @pl.when(t == 0)
    def _():
        for h in range(HG_HEADS):
            st_ref[h] = s0_ref[0, 0, h].T if has_init else jnp.zeros((HG_DV, HG_DK), F32)

    for reverse in (False, True):
        @pl.when(d == int(reverse))
        def _():
            _hgrn_tile(q_ref, v_ref, f_ref, o_ref, st_ref, rows=rows, reverse=reverse, bounded=bounded)

    @pl.when(t == pl.num_programs(2) - 1)
    def _():
        for h in range(HG_HEADS):
            sfin_ref[0, 0, h] = st_ref[h].T


def _hgrn_scan(q, v, f2, batch, seq, first_row, rt, s0=None, bounded=True):
    n = q.shape[1]
    nt = seq // rt
    has_init = s0 is not None
    out_shape = [jax.ShapeDtypeStruct((2, batch * seq, n), F32),
                 jax.ShapeDtypeStruct((batch, 2, HG_HEADS, HG_DK, HG_DV), F32)]
    if nt == 1 and not has_init:
        rows_of = lambda b: (first_row // rt + b, 0)
        return pl.pallas_call(
            functools.partial(_hgrn_whole_kernel, rows=rt, bounded=bounded),
            grid=(batch,),
            in_specs=[pl.BlockSpec((rt, n), rows_of), pl.BlockSpec((rt, n), rows_of),
                      pl.BlockSpec((2, rt, n), lambda b: (0, first_row // rt + b, 0))],
            out_specs=[pl.BlockSpec((2, rt, n), lambda b: (0, b, 0)),
                       pl.BlockSpec((1, 2, HG_HEADS, HG_DK, HG_DV), lambda b: (b, 0, 0, 0, 0))],
            out_shape=out_shape,
            compiler_params=_params("arbitrary"),
            name="hgrn_scan_whole",
        )(q, v, f2)

    def local(b, d, i):
        return b * nt + jnp.where(d == 0, i, nt - 1 - i)

    def slab(b, d, i):
        return first_row // rt + local(b, d, i)

    row = pl.BlockSpec((rt, n), lambda b, d, i: (slab(b, d, i), 0))
    state = pl.BlockSpec((1, 1, HG_HEADS, HG_DK, HG_DV), lambda b, d, i: (b, d, 0, 0, 0))
    ins = [q, v, f2]
    in_specs = [row, row, pl.BlockSpec((1, rt, n), lambda b, d, i: (d, slab(b, d, i), 0))]
    if has_init:
        ins.append(s0)
        in_specs.append(state)
    return pl.pallas_call(
        functools.partial(_hgrn_scan_kernel, rows=rt, has_init=has_init, bounded=bounded),
        grid=(batch, 2, nt),
        in_specs=in_specs,
        out_specs=[pl.BlockSpec((1, rt, n), lambda b, d, i: (d, local(b, d, i), 0)), state],
        out_shape=[jax.ShapeDtypeStruct((2, batch * seq, n), F32),
                   jax.ShapeDtypeStruct((batch, 2, HG_HEADS, HG_DK, HG_DV), F32)],
        scratch_shapes=[pltpu.VMEM((HG_HEADS, HG_DV, HG_DK), F32)],
        compiler_params=_params("arbitrary", "arbitrary", "arbitrary"),
        name="hgrn_scan_latent" if has_init else "hgrn_scan_prompt",
    )(*ins)


def _post_kernel(*refs, hgrn, final, pair_in, pair_out, prompt_tiles):
    refs = list(refs)
    n_mix = 4 if hgrn else 2
    mix_refs, refs = refs[:n_mix], refs[n_mix:]
    x, is_prompt, refs = _take_rows(refs, pair_in, prompt_tiles)
    mod_ref, wo_ref, gn_ref, wg_ref, wu_ref, wd_ref = refs[:6]
    refs = list(refs[6:])
    fn_ref = refs.pop(0) if final else None
    if hgrn:
        odp_ref, odl_ref, g_ref, onorm_ref = mix_refs
        o2 = jnp.where(is_prompt, odp_ref[0] + odp_ref[1], odl_ref[0] + odl_ref[1])
        gate = g_ref[...]
        onorm = onorm_ref[...]
        parts = []
        for h in range(HG_HEADS):
            sl = slice(h * HG_DV, (h + 1) * HG_DV)
            parts.append(_rms(o2[:, sl], onorm) * gate[:, sl])
        o = jnp.concatenate(parts, axis=1).astype(BF16)
    else:
        o = jnp.where(is_prompt, mix_refs[0][...], mix_refs[1][...])
    m = mod_ref[0]
    x1 = x + m[2:3] * _dot(o, wo_ref[...])
    h2 = _modulate(x1, gn_ref[...], m[3:4], m[4:5]).astype(BF16)
    a = (_silu(_dot(h2, wg_ref[0])) * _dot(h2, wu_ref[0])).astype(BF16)
    x2 = x1 + m[5:6] * _dot(a, wd_ref[0])
    if final:
        x2 = _rms(x2, fn_ref[...])
    if pair_out:
        @pl.when(is_prompt)
        def _():
            refs[0][...] = x2

        @pl.when(jnp.logical_not(is_prompt))
        def _():
            refs[1][...] = x2
    else:
        refs[0][...] = x2


def _layer_of(stack, layer):
    shape = (1,) + stack.shape[1:]
    return pl.BlockSpec(shape, lambda *_: (layer,) + (0,) * (len(shape) - 1), pipeline_mode=pl.Buffered(1))


def _post(mix, x, mods, wo, gn, wg, wu, wd, layer, rows, final_norm=None, hgrn=False, pair_out=False):
    xs, x_specs, pair_in, t, d = _rows_inputs(x, rows)
    tm, pt = rows.tm, rows.prompt_tiles
    if hgrn:
        od_p, od_l, gate, onorm = mix
        head = [od_p, od_l, gate, onorm]
        head_specs = [pl.BlockSpec((2, tm, d), lambda i: (0, jnp.minimum(i, pt - 1), 0)),
                      pl.BlockSpec((2, tm, d), lambda i: (0, jnp.maximum(i - pt, 0), 0)),
                      rows.row_spec(d), _whole(onorm.shape)]
    else:
        head = list(mix)
        head_specs = [rows.prompt_spec(mix[0].shape[1]), rows.latent_spec(mix[1].shape[1])]
    ins = head + xs + [mods, wo, gn, wg, wu, wd]
    in_specs = (head_specs + x_specs + [rows.mod_spec(d), _whole(wo.shape), _whole(gn.shape)]
                + [_layer_of(a, layer) for a in (wg, wu, wd)])
    if final_norm is not None:
        ins.append(final_norm)
        in_specs.append(_whole(final_norm.shape))
    if pair_out:
        out_specs = rows.x_specs(d)
        out_shape = [jax.ShapeDtypeStruct((pt * tm, d), F32), jax.ShapeDtypeStruct((t - pt * tm, d), F32)]
    else:
        out_specs = rows.row_spec(d)
        out_shape = jax.ShapeDtypeStruct((t, d), F32)
    out = pl.pallas_call(
        functools.partial(_post_kernel, hgrn=hgrn, final=final_norm is not None, pair_in=pair_in, pair_out=pair_out,
                          prompt_tiles=pt),
        grid=(rows.n_tiles,),
        in_specs=in_specs,
        out_specs=out_specs,
        out_shape=out_shape,
        compiler_params=_params("arbitrary"),
        name="post",
    )(*ins)
    return tuple(out) if pair_out else out


def _pick_tile(n_prompt_rows, dec_seq, want):
    tm = want
    while n_prompt_rows % tm or dec_seq % tm:
        tm //= 2
    return tm


def kernel(x_prompt, x_sample, cache_mla_ckv, cache_mla_krope, state_hgrn, cache_swa_k, cache_swa_v, c, c_ctx, ada_w, ada_b, norm_mix, norm_ffn, ffn_w_gate, ffn_w_up, ffn_w_down, final_norm, mla_w_dq, mla_q_norm, mla_w_uq, mla_w_dkv, mla_kv_norm, mla_w_uk, mla_w_uv, mla_w_o, hg_w_q, hg_w_f, hg_w_i, hg_w_g, hg_o_norm, hg_w_o, hg_lb_logits, swa_w_q, swa_w_k, swa_w_v, swa_w_o, swa_sink):
    batch, seq, d = x_prompt.shape
    dec_batch, dec_seq, _ = x_sample.shape
    past = cache_mla_ckv.shape[2]
    depth = ada_w.shape[0]
    n_prompt = batch * seq
    n_rows = n_prompt + dec_batch * dec_seq
    assert dec_batch + 1 <= COND_ROWS and seq % SWA_WINDOW == 0 and dec_seq % (2 * SWA_WINDOW) == 0
    assert n_prompt % dec_seq == 0

    pre_rows = _Rows(n_prompt, dec_seq, n_rows, _pick_tile(n_prompt, dec_seq, 512))
    post_rows = _Rows(n_prompt, dec_seq, n_rows, _pick_tile(n_prompt, dec_seq, 512))
    hg_pre_rows = _Rows(n_prompt, dec_seq, n_rows, _pick_tile(n_prompt, dec_seq, 256))
    hg_post_rows = _Rows(n_prompt, dec_seq, n_rows, _pick_tile(n_prompt, dec_seq, 256))
    tq = _pick_tile(n_prompt, dec_seq, 256)
    cos64, sin64 = _rope_tables(dec_seq, pre_rows.tm)
    n_tab = cos64.shape[0]
    cos_mla = jnp.concatenate([cos64, jnp.ones((n_tab, ROPE_PERIOD), F32)], axis=1)
    sin_mla = jnp.concatenate([sin64, jnp.zeros((n_tab, ROPE_PERIOD), F32)], axis=1)
    cos_swa, sin_swa = jnp.tile(cos64, (1, 2)), jnp.tile(sin64, (1, 2))

    cond = jnp.concatenate([c_ctx[None, :], c, jnp.zeros((COND_ROWS - 1 - dec_batch, d), F32)], axis=0)
    mods = _adaln(cond, ada_w, ada_b).reshape(depth, COND_ROWS, 6, d)

    x = (x_prompt.reshape(n_prompt, d), x_sample.reshape(dec_batch * dec_seq, d))
    ffn_wg, ffn_wu, ffn_wd = ffn_w_gate.astype(BF16), ffn_w_up.astype(BF16), ffn_w_down.astype(BF16)
    row1 = lambda a: a.reshape(1, -1)
    new_ckv, new_krope, new_hg, new_k, new_v = [], [], [], [], []
    for i in range(depth):
        kind, j = i % N_MIXERS, i // N_MIXERS
        gn = row1(norm_mix[i])
        if kind == 0:
            uq = mla_w_uq[j].reshape(-1, MLA_HEADS, MLA_NOPE_DIM + MLA_ROPE_DIM)
            uq_rope = jnp.pad(uq[:, :, MLA_NOPE_DIM:], ((0, 0), (0, 0), (0, LANES - MLA_ROPE_DIM)))
            uq = jnp.concatenate([uq[:, :, :MLA_NOPE_DIM].reshape(uq.shape[0], -1),
                                  uq_rope.reshape(uq.shape[0], -1)], axis=1)
            w = {
                "dq": mla_w_dq[j].astype(BF16), "q_norm": row1(mla_q_norm[j]), "uq": uq.astype(BF16),
                "dkv": jnp.pad(mla_w_dkv[j], ((0, 0), (0, LANES - MLA_ROPE_DIM))).astype(BF16),
                "kv_norm": row1(mla_kv_norm[j]),
                "uk": mla_w_uk[j].astype(BF16), "uv": mla_w_uv[j].astype(BF16),
            }
            q, ckv, kr_raw, kcat, vt = _mla_pre(x, mods[i], gn, cos_mla, sin_mla, w, pre_rows)
            kr_ctx = jnp.pad(cache_mla_krope[:, j].reshape(dec_batch * past, -1), ((0, 0), (0, LANES - MLA_ROPE_DIM)))
            kcat_ctx, vt_ctx = _mla_expand(cache_mla_ckv[:, j].reshape(dec_batch * past, -1), kr_ctx, w["uk"], w["uv"])
            mix = (_mla_attn_prompt(q, kcat, vt, batch, seq),
                   _mla_attn_latent(q, kcat, vt, kcat_ctx, vt_ctx, n_prompt, dec_batch, dec_seq, past, tq))
            wo = mla_w_o[j].astype(BF16)
            new_ckv.append(ckv.reshape(batch, seq, -1))
            new_krope.append(kr_raw.reshape(batch, seq, -1))
        elif kind == 1:
            w5 = jnp.concatenate([hg_w_q[j], hg_w_i[j], hg_w_g[j], hg_w_f[j, 0], hg_w_f[j, 1]], axis=1).astype(BF16)
            q, v, gate, f2, q_max = _hgrn_pre(x, mods[i], gn, w5, hg_lb_logits, i, hg_pre_rows)
            rt = _pick_tile(seq, dec_seq, 256)
            sm = jax.nn.softmax(hg_lb_logits.astype(F32), axis=1)
            lb_min = jnp.min(jnp.cumsum(sm, axis=1)[:, i] - sm[:, 0])
            worst = (HG_DIAG_BLOCK // 2) * -jnp.log(lb_min) + jnp.log(jnp.maximum(jnp.max(q_max), 1.0))
            fits = worst < HG_MAX_EXPONENT

            def scans(bounded):
                od_p, s_p = _hgrn_scan(q, v, f2, batch, seq, 0, rt, bounded=bounded)
                od_l, _ = _hgrn_scan(q, v, f2, dec_batch, dec_seq, n_prompt, rt, s0=state_hgrn[:, j], bounded=bounded)
                return od_p, od_l, s_p

            od_p, od_l, s_prompt = lax.cond(fits, lambda: scans(True), lambda: scans(False))
            mix = (od_p, od_l, gate, row1(hg_o_norm[j]))
            wo = hg_w_o[j].astype(BF16)
            new_hg.append(s_prompt)
        else:
            wqkv = jnp.concatenate([swa_w_q[j], swa_w_k[j], swa_w_v[j]], axis=1).astype(BF16)
            q, kx, vt, k_raw, v_raw = _swa_pre(x, mods[i], gn, cos_swa, sin_swa, wqkv, pre_rows)
            sink = swa_sink[j]
            k_ctx = cache_swa_k[:, j].reshape(dec_batch * past, SWA_KV_HEADS, 1, SWA_HEAD_DIM).transpose(1, 0, 2, 3)
            kx_ctx = jnp.broadcast_to(k_ctx, (SWA_KV_HEADS, dec_batch * past, SWA_GROUP, SWA_HEAD_DIM))
            kx_ctx = kx_ctx.reshape(SWA_KV_HEADS, dec_batch * past, SWA_GW).astype(BF16)
            vt_ctx = cache_swa_v[:, j].reshape(dec_batch * past, SWA_KV_HEADS, SWA_HEAD_DIM).transpose(1, 2, 0).astype(BF16)
            mix = (_swa_attn_prompt(q, kx, vt, sink, batch, seq),
                   _swa_attn_latent(q, kx, vt, kx_ctx, vt_ctx, sink, n_prompt, dec_batch, dec_seq, past, tq))
            wo = swa_w_o[j].astype(BF16)
            new_k.append(k_raw.reshape(batch, seq, SWA_KV_HEADS, SWA_HEAD_DIM))
            new_v.append(v_raw.reshape(batch, seq, SWA_KV_HEADS, SWA_HEAD_DIM))
        x = _post(mix, x, mods[i], wo, row1(norm_ffn[i]), ffn_wg, ffn_wu, ffn_wd, i,
                  hg_post_rows if kind == 1 else post_rows,
                  final_norm=row1(final_norm) if i == depth - 1 else None, hgrn=kind == 1,
                  pair_out=i == depth - 1)
    y_prompt = x[0].reshape(batch, seq, d)
    y_sample = x[1].reshape(dec_batch, dec_seq, d)
    return (y_prompt, y_sample, jnp.stack(new_ckv, axis=1), jnp.stack(new_krope, axis=1),
            jnp.stack(new_hg, axis=1), jnp.stack(new_k, axis=1), jnp.stack(new_v, axis=1))
```

```python
import functools

import jax
import jax.numpy as jnp
from jax import lax
from jax.experimental import pallas as pl
from jax.experimental.pallas import tpu as pltpu

F32 = jnp.float32
BF16 = jnp.bfloat16

GRID_W = 64
N_MIXERS = 3

MLA_HEADS = 8
MLA_KV_LORA = 256
MLA_NOPE_DIM = 128
MLA_ROPE_DIM = 64
MLA_V_DIM = 128
MLA_QK_PAD = 256
MLA_SCALE = (MLA_NOPE_DIM + MLA_ROPE_DIM) ** -0.5

HG_HEADS = 8
HG_DK = 128
HG_DV = 128
HG_DIAG_BLOCK = 32
HG_MAX_EXPONENT = 80.0

SWA_HEADS = 16
SWA_KV_HEADS = 4
SWA_GROUP = SWA_HEADS // SWA_KV_HEADS
SWA_HEAD_DIM = 64
SWA_GW = SWA_GROUP * SWA_HEAD_DIM
SWA_WINDOW = 128
SWA_SCALE = SWA_HEAD_DIM ** -0.5

ROPE_BASE = 10000.0
ROPE_PERIOD = 64
ROPE_QUARTER = 16
NORM_EPS = 1e-6
NEG_INF = -1e30
LOG2_E = 1.4426950408889634

LANES = 128
COND_ROWS = 8
VMEM_LIMIT = 56 * 1024 * 1024


def _sigmoid(x):
    return jax.nn.sigmoid(x)


def _silu(x):
    return x * jax.nn.sigmoid(x)


def _rms(x, g):
    return x * lax.rsqrt(jnp.mean(x * x, axis=-1, keepdims=True) + NORM_EPS) * g


def _modulate(x, g, shift, scale):
    return _rms(x, g) * (1.0 + scale) + shift


def _dot(a, b):
    return jnp.dot(a, b, preferred_element_type=F32)


def _dot_nt(a, b):
    return lax.dot_general(a, b, (((1,), (1,)), ((), ())), preferred_element_type=F32)


def _dot_tn(a, b):
    return lax.dot_general(a, b, (((0,), (0,)), ((), ())), preferred_element_type=F32)


def _swap_pairs(x):
    n = x.shape[1]
    lane = lax.broadcasted_iota(jnp.int32, x.shape, 1)
    ahead = pltpu.roll(x, n - ROPE_QUARTER, 1)
    behind = pltpu.roll(x, ROPE_QUARTER, 1)
    return jnp.where((lane & (2 * ROPE_QUARTER - 1)) < ROPE_QUARTER, ahead, behind)


def _rope(x, cos, sin):
    reps = x.shape[1] // cos.shape[1]
    if reps > 1:
        cos = jnp.concatenate([cos] * reps, axis=1)
        sin = jnp.concatenate([sin] * reps, axis=1)
    return x * cos + _swap_pairs(x) * sin


def _whole(shape):
    zeros = (0,) * len(shape)
    return pl.BlockSpec(shape, lambda *_: zeros, pipeline_mode=pl.Buffered(1))


def _params(*sem):
    return pltpu.CompilerParams(dimension_semantics=sem, vmem_limit_bytes=VMEM_LIMIT)


class _Rows:
    def __init__(self, n_prompt_rows, dec_seq, n_rows, tm):
        assert n_prompt_rows % tm == 0 and dec_seq % tm == 0
        self.tm = tm
        self.n_tiles = n_rows // tm
        self.prompt_tiles = n_prompt_rows // tm
        self.seq_tiles = dec_seq // tm

    def cond(self, i):
        return jnp.where(i < self.prompt_tiles, 0, 1 + jnp.maximum(i - self.prompt_tiles, 0) // self.seq_tiles)

    def rope_block(self, i):
        return jnp.where(i < self.prompt_tiles, 0, 1 + jnp.maximum(i - self.prompt_tiles, 0) % self.seq_tiles)

    def row_spec(self, width):
        return pl.BlockSpec((self.tm, width), lambda i: (i, 0))

    def col_spec(self, height):
        return pl.BlockSpec((height, self.tm), lambda i: (0, i))

    def mod_spec(self, d):
        return pl.BlockSpec((1, 6, d), lambda i: (self.cond(i), 0, 0))

    def rope_spec(self, width):
        return pl.BlockSpec((self.tm, width), lambda i: (self.rope_block(i), 0))

    def x_specs(self, width):
        return [self.prompt_spec(width), self.latent_spec(width)]

    def is_prompt(self):
        return pl.program_id(0) < self.prompt_tiles

    def prompt_spec(self, width):
        return pl.BlockSpec((self.tm, width), lambda i: (jnp.minimum(i, self.prompt_tiles - 1), 0))

    def latent_spec(self, width):
        return pl.BlockSpec((self.tm, width), lambda i: (jnp.maximum(i - self.prompt_tiles, 0), 0))


def _rope_tables(dec_seq, tm):
    pos = jnp.arange(dec_seq)
    row = (pos // GRID_W).astype(F32)
    col = (pos % GRID_W).astype(F32)
    inv_freq = ROPE_BASE ** (-jnp.arange(ROPE_QUARTER, dtype=F32) / ROPE_QUARTER)
    ang_r = row[:, None] * inv_freq[None, :]
    ang_c = col[:, None] * inv_freq[None, :]
    cos = jnp.concatenate([jnp.cos(ang_r), jnp.cos(ang_r), jnp.cos(ang_c), jnp.cos(ang_c)], axis=1)
    sin = jnp.concatenate([-jnp.sin(ang_r), jnp.sin(ang_r), -jnp.sin(ang_c), jnp.sin(ang_c)], axis=1)
    cos = jnp.concatenate([jnp.ones((tm, ROPE_PERIOD), F32), cos], axis=0)
    sin = jnp.concatenate([jnp.zeros((tm, ROPE_PERIOD), F32), sin], axis=0)
    return cos, sin


def _adaln_kernel(c_ref, w_ref, b_ref, o_ref):
    o_ref[0] = _dot(_silu(c_ref[...]), w_ref[0]) + b_ref[0]


def _adaln(cond, ada_w, ada_b, tn=1536):
    depth, d, n = ada_w.shape
    return pl.pallas_call(
        _adaln_kernel,
        grid=(depth, n // tn),
        in_specs=[pl.BlockSpec((COND_ROWS, d), lambda l, j: (0, 0)),
                  pl.BlockSpec((1, d, tn), lambda l, j: (l, 0, j)),
                  pl.BlockSpec((1, 1, tn), lambda l, j: (l, 0, j))],
        out_specs=pl.BlockSpec((1, COND_ROWS, tn), lambda l, j: (l, 0, j)),
        out_shape=jax.ShapeDtypeStruct((depth, COND_ROWS, n), F32),
        compiler_params=_params("arbitrary", "arbitrary"),
        name="adaln",
    )(cond, ada_w, ada_b.reshape(depth, 1, n))


def _mla_store_heads(q_ref, kcat_ref, vt_ref, qn, qr_pad, kn, kr_pad, vt):
    for h in range(MLA_HEADS):
        nope = slice(h * MLA_NOPE_DIM, (h + 1) * MLA_NOPE_DIM)
        if q_ref is not None:
            q_ref[h] = jnp.concatenate([qn[:, nope], qr_pad[:, h * LANES:(h + 1) * LANES]], axis=1)
        kcat_ref[h] = jnp.concatenate([kn[:, nope], kr_pad], axis=1)
        vt_ref[h] = vt[h * MLA_V_DIM:(h + 1) * MLA_V_DIM, :]


def _read_rows(is_prompt, p_ref, l_ref):
    return jnp.where(is_prompt, p_ref[...], l_ref[...])


def _take_rows(refs, pair, prompt_tiles):
    is_prompt = pl.program_id(0) < prompt_tiles
    if pair:
        return _read_rows(is_prompt, refs[0], refs[1]), is_prompt, refs[2:]
    return refs[0][...], is_prompt, refs[1:]


def _rows_inputs(x, rows):
    if isinstance(x, tuple):
        d = x[0].shape[1]
        return list(x), rows.x_specs(d), True, x[0].shape[0] + x[1].shape[0], d
    return [x], [rows.row_spec(x.shape[1])], False, x.shape[0], x.shape[1]


def _mla_pre_kernel(*refs, pair, prompt_tiles):
    x, is_prompt, refs = _take_rows(refs, pair, prompt_tiles)
    (mod_ref, gn_ref, cos_ref, sin_ref, wdq_ref, qnorm_ref, wuq_ref, wdkv_ref, kvnorm_ref, wuk_ref, wuvt_ref,
     q_ref, ckv_ref, krraw_ref, kcat_ref, vt_ref) = refs
    m = mod_ref[0]
    h = _modulate(x, gn_ref[...], m[0:1], m[1:2]).astype(BF16)
    q_lat = _rms(_dot(h, wdq_ref[...]), qnorm_ref[...]).astype(BF16)
    cos, sin = cos_ref[...], sin_ref[...]
    nn = MLA_HEADS * MLA_NOPE_DIM
    qn = _dot(q_lat, wuq_ref[:, :nn]).astype(BF16)
    qr_pad = _rope(_dot(q_lat, wuq_ref[:, nn:]), cos, sin).astype(BF16)
    kv = _dot(h, wdkv_ref[...])
    ckv = _rms(kv[:, :MLA_KV_LORA], kvnorm_ref[...])
    kr = kv[:, MLA_KV_LORA:]

    @pl.when(is_prompt)
    def _():
        ckv_ref[...] = ckv
        krraw_ref[...] = kr[:, :MLA_ROPE_DIM]

    kr_pad = _rope(kr, cos, sin).astype(BF16)
    cb = ckv.astype(BF16)
    _mla_store_heads(q_ref, kcat_ref, vt_ref, qn, qr_pad, _dot(cb, wuk_ref[...]).astype(BF16), kr_pad,
                     _dot_nt(wuvt_ref[...], cb).astype(BF16))


def _mla_pre(x, mods, gn, cos, sin, w, rows):
    xs, x_specs, pair, t, d = _rows_inputs(x, rows)
    tm = rows.tm
    n_prompt = rows.prompt_tiles * tm
    weights = [w["dq"], w["q_norm"], w["uq"], w["dkv"], w["kv_norm"], w["uk"], w["uv_t"]]
    ins = xs + [mods, gn, cos, sin] + weights
    in_specs = x_specs + [rows.mod_spec(d), _whole(gn.shape),
                          rows.rope_spec(cos.shape[1]), rows.rope_spec(sin.shape[1])]
    in_specs += [_whole(a.shape) for a in weights]
    heads_rows = pl.BlockSpec((MLA_HEADS, tm, MLA_QK_PAD), lambda i: (0, i, 0))
    return pl.pallas_call(
        functools.partial(_mla_pre_kernel, pair=pair, prompt_tiles=rows.prompt_tiles),
        grid=(rows.n_tiles,),
        in_specs=in_specs,
        out_specs=[heads_rows, rows.prompt_spec(MLA_KV_LORA), rows.prompt_spec(MLA_ROPE_DIM), heads_rows,
                   pl.BlockSpec((MLA_HEADS, MLA_V_DIM, tm), lambda i: (0, 0, i))],
        out_shape=[jax.ShapeDtypeStruct((MLA_HEADS, t, MLA_QK_PAD), BF16),
                   jax.ShapeDtypeStruct((n_prompt, MLA_KV_LORA), F32),
                   jax.ShapeDtypeStruct((n_prompt, MLA_ROPE_DIM), F32),
                   jax.ShapeDtypeStruct((MLA_HEADS, t, MLA_QK_PAD), BF16),
                   jax.ShapeDtypeStruct((MLA_HEADS, MLA_V_DIM, t), BF16)],
        compiler_params=_params("arbitrary"),
        name="mla_pre",
    )(*ins)


def _mla_expand_kernel(c_ref, kr_ref, wuk_ref, wuvt_ref, kcat_ref, vt_ref):
    cb = c_ref[...].astype(BF16)
    _mla_store_heads(None, kcat_ref, vt_ref, None, None, _dot(cb, wuk_ref[...]).astype(BF16),
                     kr_ref[...].astype(BF16), _dot_nt(wuvt_ref[...], cb).astype(BF16))


def _mla_expand(ckv, kr_pad, wuk, wuvt):
    n = ckv.shape[0]
    return pl.pallas_call(
        _mla_expand_kernel,
        out_shape=[jax.ShapeDtypeStruct((MLA_HEADS, n, MLA_QK_PAD), BF16),
                   jax.ShapeDtypeStruct((MLA_HEADS, MLA_V_DIM, n), BF16)],
        compiler_params=pltpu.CompilerParams(vmem_limit_bytes=VMEM_LIMIT),
        name="mla_expand",
    )(ckv, kr_pad, wuk, wuvt)


def _mla_attn_kernel(*refs, n_src):
    q_ref = refs[0]
    srcs = [(refs[1 + 2 * i], refs[2 + 2 * i]) for i in range(n_src)]
    o_ref, o_buf = refs[1 + 2 * n_src:3 + 2 * n_src]
    s_bufs = refs[3 + 2 * n_src:]
    n_keys = [k_ref.shape[1] for k_ref, _ in srcs]
    starts = [sum(n_keys[:i]) for i in range(n_src)]

    def put_scores(h, s_buf):
        q = q_ref[h]
        for (k_ref, _), first, n in zip(srcs, starts, n_keys):
            s_buf[first:first + n, :] = _dot_nt(k_ref[h], q)

    def finish(h, s_buf):
        s = s_buf[...]
        mx = jnp.max(s, axis=0, keepdims=True)
        p = jnp.exp2((s - mx) * (MLA_SCALE * LOG2_E))
        den = jnp.sum(p, axis=0, keepdims=True)
        p = p.astype(BF16)
        acc = None
        for (_, vt_ref), first, n in zip(srcs, starts, n_keys):
            a = _dot(vt_ref[h], p[first:first + n, :])
            acc = a if acc is None else acc + a
        o_buf[h] = (acc / den).T.astype(BF16)

    _pipelined_heads(MLA_HEADS, put_scores, finish, s_bufs)
    for h in range(MLA_HEADS):
        o_ref[:, h * MLA_V_DIM:(h + 1) * MLA_V_DIM] = o_buf[h]


def _mla_attn_scratch(n_keys, tq):
    return [pltpu.VMEM((MLA_HEADS, tq, MLA_V_DIM), BF16)] + [pltpu.VMEM((n_keys, tq), F32)] * 4


def _pipelined_heads(n_heads, put_scores, finish, bufs):
    a, b, c, d = bufs
    assert n_heads % 4 == 0
    put_scores(0, a)
    put_scores(1, b)

    def quad(j, carry):
        h = 4 * j
        put_scores(h + 2, c)
        put_scores(h + 3, d)
        finish(h, a)
        finish(h + 1, b)
        put_scores(h + 4, a)
        put_scores(h + 5, b)
        finish(h + 2, c)
        finish(h + 3, d)
        return carry

    lax.fori_loop(0, n_heads // 4 - 1, quad, 0)
    h = n_heads - 4
    put_scores(h + 2, c)
    put_scores(h + 3, d)
    finish(h, a)
    finish(h + 1, b)
    finish(h + 2, c)
    finish(h + 3, d)


def _staged_attention(scores, values_t, scale, sinks=None):
    c = scale * LOG2_E
    mx = [jnp.max(s, axis=0, keepdims=True) for s in scores]
    if sinks is not None:
        mx = [jnp.maximum(m, z) for m, z in zip(mx, sinks)]
    p = [jnp.exp2((s - m) * c) for s, m in zip(scores, mx)]
    den = [jnp.sum(x, axis=0, keepdims=True) for x in p]
    if sinks is not None:
        den = [d + jnp.exp2((z - m) * c) for d, z, m in zip(den, sinks, mx)]
    acc = [_dot(v, x.astype(BF16)) for v, x in zip(values_t, p)]
    return [a / d for a, d in zip(acc, den)]


def _mla_attn_prompt_kernel(q_ref, k_ref, vt_ref, o_ref, *, seq):
    per_step = q_ref.shape[1] // seq
    spans = [slice(s * seq, (s + 1) * seq) for s in range(per_step)]
    scores = [_dot_nt(k_ref[h, sp, :], q_ref[h, sp, :]) for sp in spans for h in range(MLA_HEADS)]
    values = [vt_ref[h, :, sp] for sp in spans for h in range(MLA_HEADS)]
    outs = _staged_attention(scores, values, MLA_SCALE)
    for s, sp in enumerate(spans):
        for h in range(MLA_HEADS):
            o_ref[sp, h * MLA_V_DIM:(h + 1) * MLA_V_DIM] = outs[s * MLA_HEADS + h].T.astype(BF16)


def _mla_attn_prompt(q, kcat, vt, batch, seq):
    hv = MLA_HEADS * MLA_V_DIM
    per_step = 2 if batch % 2 == 0 else 1
    seq_rows = seq * per_step
    rows = pl.BlockSpec((MLA_HEADS, seq_rows, MLA_QK_PAD), lambda b: (0, b, 0))
    return pl.pallas_call(
        functools.partial(_mla_attn_prompt_kernel, seq=seq),
        grid=(batch // per_step,),
        in_specs=[rows, rows, pl.BlockSpec((MLA_HEADS, MLA_V_DIM, seq_rows), lambda b: (0, 0, b))],
        out_specs=pl.BlockSpec((seq_rows, hv), lambda b: (b, 0)),
        out_shape=jax.ShapeDtypeStruct((batch * seq, hv), BF16),
        compiler_params=_params("arbitrary"),
        name="mla_attn_prompt",
    )(q, kcat, vt)


def _mla_attn_latent(q, kcat, vt, kcat_ctx, vt_ctx, n_prompt, dec_batch, dec_seq, past, tq):
    hv = MLA_HEADS * MLA_V_DIM
    nq = dec_seq // tq
    lat_blk = n_prompt // dec_seq
    return pl.pallas_call(
        functools.partial(_mla_attn_kernel, n_src=2),
        grid=(dec_batch, nq),
        in_specs=[pl.BlockSpec((MLA_HEADS, tq, MLA_QK_PAD), lambda b, i: (0, n_prompt // tq + b * nq + i, 0)),
                  pl.BlockSpec((MLA_HEADS, past, MLA_QK_PAD), lambda b, i: (0, b, 0)),
                  pl.BlockSpec((MLA_HEADS, MLA_V_DIM, past), lambda b, i: (0, 0, b)),
                  pl.BlockSpec((MLA_HEADS, dec_seq, MLA_QK_PAD), lambda b, i: (0, lat_blk + b, 0)),
                  pl.BlockSpec((MLA_HEADS, MLA_V_DIM, dec_seq), lambda b, i: (0, 0, lat_blk + b))],
        out_specs=pl.BlockSpec((tq, hv), lambda b, i: (b * nq + i, 0)),
        out_shape=jax.ShapeDtypeStruct((dec_batch * dec_seq, hv), BF16),
        scratch_shapes=_mla_attn_scratch(past + dec_seq, tq),
        compiler_params=_params("arbitrary", "arbitrary"),
        name="mla_attn_latent",
    )(q, kcat_ctx, vt_ctx, kcat, vt)


def _tile_heads(x):
    n = x.shape[1]
    block = lax.broadcasted_iota(jnp.int32, x.shape, 1) // SWA_HEAD_DIM
    rolled = [x] + [pltpu.roll(x, s * SWA_HEAD_DIM, 1) for s in range(1, SWA_KV_HEADS)]
    out = []
    for kvh in range(SWA_KV_HEADS):
        blk = rolled[(0 - kvh) % SWA_KV_HEADS]
        for g in range(1, n // SWA_HEAD_DIM):
            blk = jnp.where(block == g, rolled[(g - kvh) % SWA_KV_HEADS], blk)
        out.append(blk)
    return jnp.concatenate(out, axis=1)


def _swa_pre_kernel(*refs, pair, prompt_tiles):
    x, is_prompt, refs = _take_rows(refs, pair, prompt_tiles)
    mod_ref, gn_ref, cos_ref, sin_ref, wqkv_ref, q_ref, kx_ref, vt_ref, kraw_ref, vraw_ref = refs
    m = mod_ref[0]
    h = _modulate(x, gn_ref[...], m[0:1], m[1:2]).astype(BF16)
    qkv = _dot(h, wqkv_ref[...])
    nq, nk = SWA_HEADS * SWA_HEAD_DIM, SWA_KV_HEADS * SWA_HEAD_DIM
    cos, sin = cos_ref[...], sin_ref[...]
    q = _rope(qkv[:, :nq], cos, sin).astype(BF16)
    k = qkv[:, nq:nq + nk]
    v = qkv[:, nq + nk:]

    @pl.when(is_prompt)
    def _():
        kraw_ref[...] = k
        vraw_ref[...] = v

    kx = _tile_heads(_rope(k, cos, sin)).astype(BF16)
    vt = v.T.astype(BF16)
    for kvh in range(SWA_KV_HEADS):
        q_ref[kvh] = q[:, kvh * SWA_GW:(kvh + 1) * SWA_GW]
        kx_ref[kvh] = kx[:, kvh * SWA_GW:(kvh + 1) * SWA_GW]
        vt_ref[kvh] = vt[kvh * SWA_HEAD_DIM:(kvh + 1) * SWA_HEAD_DIM, :]


def _swa_pre(x, mods, gn, cos, sin, wqkv, rows):
    xs, x_specs, pair, t, d = _rows_inputs(x, rows)
    tm = rows.tm
    n_prompt = rows.prompt_tiles * tm
    nk = SWA_KV_HEADS * SWA_HEAD_DIM
    heads_rows = pl.BlockSpec((SWA_KV_HEADS, tm, SWA_GW), lambda i: (0, i, 0))
    return pl.pallas_call(
        functools.partial(_swa_pre_kernel, pair=pair, prompt_tiles=rows.prompt_tiles),
        grid=(rows.n_tiles,),
        in_specs=x_specs + [rows.mod_spec(d), _whole(gn.shape),
                            rows.rope_spec(cos.shape[1]), rows.rope_spec(sin.shape[1]), _whole(wqkv.shape)],
        out_specs=[heads_rows, heads_rows, pl.BlockSpec((SWA_KV_HEADS, SWA_HEAD_DIM, tm), lambda i: (0, 0, i)),
                   rows.prompt_spec(nk), rows.prompt_spec(nk)],
        out_shape=[jax.ShapeDtypeStruct((SWA_KV_HEADS, t, SWA_GW), BF16),
                   jax.ShapeDtypeStruct((SWA_KV_HEADS, t, SWA_GW), BF16),
                   jax.ShapeDtypeStruct((SWA_KV_HEADS, SWA_HEAD_DIM, t), BF16),
                   jax.ShapeDtypeStruct((n_prompt, nk), F32), jax.ShapeDtypeStruct((n_prompt, nk), F32)],
        compiler_params=_params("arbitrary"),
        name="swa_pre",
    )(*xs, mods, gn, cos, sin, wqkv)


def _swa_heads(q_ref, sink_ref, o_ref, srcs, bias_ref, ot_buf, s_bufs):
    tq = q_ref.shape[1]
    n_keys = [kx_ref.shape[1] for kx_ref, _ in srcs]
    starts = [sum(n_keys[:i]) for i in range(len(srcs))]
    group = lax.broadcasted_iota(jnp.int32, (tq, SWA_GW), 1) // SWA_HEAD_DIM

    def put_scores(hq, s_buf):
        kvh, g = hq // SWA_GROUP, hq % SWA_GROUP
        q = jnp.where(group == g, q_ref[kvh].astype(F32), 0.0).astype(BF16)
        for (kx_ref, _), first, n in zip(srcs, starts, n_keys):
            s_buf[first:first + n, :] = _dot_nt(kx_ref[kvh], q)

    def finish(hq, s_buf):
        kvh, g = hq // SWA_GROUP, hq % SWA_GROUP
        s = s_buf[...]
        if bias_ref is not None:
            s = s + bias_ref[...]
        sink = jnp.full((1, 1), sink_ref[hq] * (1.0 / SWA_SCALE), F32)
        mx = jnp.maximum(jnp.max(s, axis=0, keepdims=True), sink)
        p = jnp.exp2((s - mx) * (SWA_SCALE * LOG2_E))
        den = jnp.sum(p, axis=0, keepdims=True) + jnp.exp2((sink - mx) * (SWA_SCALE * LOG2_E))
        p = p.astype(BF16)
        acc = None
        for (_, vt_ref), first, n in zip(srcs, starts, n_keys):
            a = _dot(vt_ref[kvh], p[first:first + n, :])
            acc = a if acc is None else acc + a
        ot_buf[kvh, pl.ds(pl.multiple_of(g * SWA_HEAD_DIM, SWA_HEAD_DIM), SWA_HEAD_DIM), :] = acc / den

    _pipelined_heads(SWA_HEADS, put_scores, finish, s_bufs)
    for kvh in range(SWA_KV_HEADS):
        o_ref[:, kvh * SWA_GW:(kvh + 1) * SWA_GW] = ot_buf[kvh].T.astype(BF16)


def _swa_attn_scratch(n_keys, tq):
    return [pltpu.VMEM((SWA_KV_HEADS, SWA_GW, tq), F32)] + [pltpu.VMEM((n_keys, tq), F32)] * 5


def _swa_attn_prompt_kernel(q_ref, kx_ref, vt_ref, sink_ref, o_ref, *, seq):
    per_step = q_ref.shape[1] // seq
    spans = [slice(s * seq, (s + 1) * seq) for s in range(per_step)]
    group = lax.broadcasted_iota(jnp.int32, (seq, SWA_GW), 1) // SWA_HEAD_DIM
    scores, values_t, sinks = [], [], []
    for sp in spans:
        for kvh in range(SWA_KV_HEADS):
            q_all = q_ref[kvh, sp, :].astype(F32)
            for g in range(SWA_GROUP):
                q = jnp.where(group == g, q_all, 0.0).astype(BF16)
                scores.append(_dot_nt(kx_ref[kvh, sp, :], q))
                values_t.append(vt_ref[kvh, :, sp])
                sinks.append(jnp.full((1, 1), sink_ref[kvh * SWA_GROUP + g] * (1.0 / SWA_SCALE), F32))
    outs = _staged_attention(scores, values_t, SWA_SCALE, sinks)
    for s, sp in enumerate(spans):
        for kvh in range(SWA_KV_HEADS):
            first = s * SWA_HEADS + kvh * SWA_GROUP
            ot = jnp.concatenate(outs[first:first + SWA_GROUP], axis=0)
            o_ref[sp, kvh * SWA_GW:(kvh + 1) * SWA_GW] = ot.T.astype(BF16)


def _swa_attn_prompt(q, kx, vt, sink, batch, seq):
    wq = SWA_HEADS * SWA_HEAD_DIM
    per_step = 2 if batch % 2 == 0 else 1
    seq_rows = seq * per_step
    rows = pl.BlockSpec((SWA_KV_HEADS, seq_rows, SWA_GW), lambda b: (0, b, 0))
    return pl.pallas_call(
        functools.partial(_swa_attn_prompt_kernel, seq=seq),
        grid=(batch // per_step,),
        in_specs=[rows, rows, pl.BlockSpec((SWA_KV_HEADS, SWA_HEAD_DIM, seq_rows), lambda b: (0, 0, b)),
                  pl.BlockSpec(memory_space=pltpu.SMEM)],
        out_specs=pl.BlockSpec((seq_rows, wq), lambda b: (b, 0)),
        out_shape=jax.ShapeDtypeStruct((batch * seq, wq), BF16),
        compiler_params=_params("arbitrary"),
        name="swa_attn_prompt",
    )(q, kx, vt, sink)


def _swa_attn_latent_kernel(q_ref, kc_ref, vc_ref, kp_ref, vp_ref, km_ref, vm_ref, kn_ref, vn_ref,
                            sink_ref, o_ref, ot_buf, s_a, s_b, s_c, s_d, bias_ref, *, tq, dec_seq, past):
    i = pl.program_id(1)
    qpos = i * tq + lax.broadcasted_iota(jnp.int32, (1, tq), 1)

    def band(first, n):
        kpos = first + lax.broadcasted_iota(jnp.int32, (n, 1), 0)
        valid = (jnp.abs(qpos - kpos) <= SWA_WINDOW) & (kpos >= 0) & (kpos < dec_seq)
        return jnp.where(valid, 0.0, NEG_INF)

    bias_ref[...] = jnp.concatenate(
        [jnp.zeros((past, tq), F32), band(i * tq - SWA_WINDOW, SWA_WINDOW), band(i * tq, tq),
         band((i + 1) * tq, SWA_WINDOW)], axis=0)
    srcs = [(kc_ref, vc_ref), (kp_ref, vp_ref), (km_ref, vm_ref), (kn_ref, vn_ref)]
    _swa_heads(q_ref, sink_ref, o_ref, srcs, bias_ref, ot_buf, (s_a, s_b, s_c, s_d))


def _swa_attn_latent(q, kx, vt, kx_ctx, vt_ctx, sink, n_prompt, dec_batch, dec_seq, past, tq):
    t = q.shape[1]
    wq = SWA_HEADS * SWA_HEAD_DIM
    nq = dec_seq // tq
    w = SWA_WINDOW
    n_keys = past + tq + 2 * w
    first = lambda b, i: n_prompt + b * dec_seq + i * tq
    prev = lambda b, i: first(b, i) // w - 1
    nxt = lambda b, i: jnp.minimum((first(b, i) + tq) // w, t // w - 1)
    rows = lambda n, blk: pl.BlockSpec((SWA_KV_HEADS, n, SWA_GW), lambda b, i: (0, blk(b, i), 0))
    cols = lambda n, blk: pl.BlockSpec((SWA_KV_HEADS, SWA_HEAD_DIM, n), lambda b, i: (0, 0, blk(b, i)))
    main = lambda b, i: first(b, i) // tq
    ctx = lambda b, i: b
    return pl.pallas_call(
        functools.partial(_swa_attn_latent_kernel, tq=tq, dec_seq=dec_seq, past=past),
        grid=(dec_batch, nq),
        in_specs=[rows(tq, main), rows(past, ctx), cols(past, ctx), rows(w, prev), cols(w, prev),
                  rows(tq, main), cols(tq, main), rows(w, nxt), cols(w, nxt),
                  pl.BlockSpec(memory_space=pltpu.SMEM)],
        out_specs=pl.BlockSpec((tq, wq), lambda b, i: (b * nq + i, 0)),
        out_shape=jax.ShapeDtypeStruct((dec_batch * dec_seq, wq), BF16),
        scratch_shapes=_swa_attn_scratch(n_keys, tq),
        compiler_params=_params("arbitrary", "arbitrary"),
        name="swa_attn_latent",
    )(q, kx_ctx, vt_ctx, kx, vt, kx, vt, kx, vt, sink)


def _hgrn_pre_kernel(*refs, layer, pair, prompt_tiles):
    x, _, refs = _take_rows(refs, pair, prompt_tiles)
    mod_ref, gn_ref, w_ref, lbl_ref, q_ref, v_ref, g_ref, f_ref, qmax_ref = refs
    m = mod_ref[0]
    h = _modulate(x, gn_ref[...], m[0:1], m[1:2]).astype(BF16)
    y = _dot(h, w_ref[...])
    n = HG_HEADS * HG_DK
    q = _silu(y[:, :n])
    q_ref[...] = q
    qmax_ref[0] = jnp.max(jnp.abs(q), axis=0, keepdims=True)
    v_ref[...] = y[:, n:2 * n].astype(BF16)
    g_ref[...] = _silu(y[:, 2 * n:3 * n])
    for d in range(2):
        logits = lbl_ref[d]
        e = jnp.exp(logits - jnp.max(logits, axis=0, keepdims=True))
        s = e / jnp.sum(e, axis=0, keepdims=True)
        cs = s[0:1]
        for r in range(1, layer + 1):
            cs = cs + s[r:r + 1]
        lb = cs - s[0:1]
        f_ref[d] = lb + (1.0 - lb) * _sigmoid(y[:, (3 + d) * n:(4 + d) * n])


def _hgrn_pre(x, mods, gn, w5, lb_logits, layer, rows):
    xs, x_specs, pair, t, d = _rows_inputs(x, rows)
    n = HG_HEADS * HG_DK
    return pl.pallas_call(
        functools.partial(_hgrn_pre_kernel, layer=layer, pair=pair, prompt_tiles=rows.prompt_tiles),
        grid=(rows.n_tiles,),
        in_specs=x_specs + [rows.mod_spec(d), _whole(gn.shape), _whole(w5.shape), _whole(lb_logits.shape)],
        out_specs=[rows.row_spec(n), rows.row_spec(n), rows.row_spec(n),
                   pl.BlockSpec((2, rows.tm, n), lambda i: (0, i, 0)), pl.BlockSpec((1, 1, n), lambda i: (i, 0, 0))],
        out_shape=[jax.ShapeDtypeStruct((t, n), F32), jax.ShapeDtypeStruct((t, n), BF16),
                   jax.ShapeDtypeStruct((t, n), F32), jax.ShapeDtypeStruct((2, t, n), F32),
                   jax.ShapeDtypeStruct((rows.n_tiles, 1, n), F32)],
        compiler_params=_params("arbitrary"),
        name="hgrn_pre",
    )(*xs, mods, gn, w5, lb_logits)


def _tri_cumsum(tri, x):
    hi = x.astype(BF16)
    r1 = x - hi.astype(F32)
    mid = r1.astype(BF16)
    lo = (r1 - mid.astype(F32)).astype(BF16)
    return _dot(tri, hi) + _dot(tri, mid) + _dot(tri, lo)


def _hgrn_tile(q_ref, v_ref, f_ref, o_ref, st_ref, *, rows, reverse, bounded, slot=0, sfin_ref=None):
    r = rows
    diag = HG_DIAG_BLOCK if bounded else 1
    a = lax.broadcasted_iota(jnp.int32, (r, r), 0)
    b = lax.broadcasted_iota(jnp.int32, (r, r), 1)
    seen = (b >= a) if reverse else (b <= a)
    tri = jnp.where(seen, 1.0, 0.0).astype(BF16)

    q = q_ref[...]
    f = f_ref[slot]
    vb = v_ref[...].astype(BF16)
    lf = jnp.log(f)
    cum = _tri_cumsum(tri, lf)
    tot = cum[0:1, :] if reverse else cum[r - 1:r, :]
    kk = 1.0 - f
    q_in = (q * jnp.exp(cum)).astype(BF16)
    k_d = (kk * jnp.exp(tot - cum)).astype(BF16)
    e_tot = jnp.exp(tot)

    levels = []
    c = r // 2
    while c >= diag:
        q_half = (a % (2 * c) < c) if reverse else (a % (2 * c) >= c)
        k_half = (b % (2 * c) >= c) if reverse else (b % (2 * c) < c)
        levels.append((2 * c, c if reverse else c - 1, ((a // (2 * c)) == (b // (2 * c))) & q_half & k_half))
        c //= 2
    levels.append((diag, diag // 2, ((a // diag) == (b // diag)) & seen))

    def spread(rows_of_block, size):
        return jnp.concatenate([jnp.broadcast_to(row, (size, row.shape[1])) for row in rows_of_block], axis=0)

    factors = []
    for size, ref_row, own in levels:
        if size == 1:
            factors.append((q.astype(BF16), kk.astype(BF16), own))
            continue
        if size >= 8:
            ref = spread([cum[j * size + ref_row:j * size + ref_row + 1, :] for j in range(r // size)], size)
        else:
            ref = _tri_cumsum(jnp.where(b == (a // size) * size + ref_row, 1.0, 0.0).astype(BF16), cum)
        factors.append(((q * jnp.exp(cum - ref)).astype(BF16), (kk * jnp.exp(ref - cum)).astype(BF16), own))

    outs = []
    for h in range(HG_HEADS):
        sl = slice(h * HG_DK, (h + 1) * HG_DK)
        att = jnp.zeros((r, r), F32)
        for q_l, k_l, own in factors:
            att = jnp.where(own, _dot_nt(q_l[:, sl], k_l[:, sl]), att)
        intra = _dot(att.astype(BF16), vb[:, sl])
        if sfin_ref is not None:
            outs.append(intra)
            sfin_ref[0, slot, h] = _dot_tn(k_d[:, sl], vb[:, sl])
            continue
        st = st_ref[h]
        outs.append(_dot_nt(q_in[:, sl], st.astype(BF16)) + intra)
        st_ref[h] = st * e_tot[:, sl] + _dot_tn(vb[:, sl], k_d[:, sl])
    o_ref[slot] = jnp.concatenate(outs, axis=1)


def _hgrn_whole_kernel(q_ref, v_ref, f_ref, o_ref, sfin_ref, *, rows, bounded):
    for slot, reverse in enumerate((False, True)):
        _hgrn_tile(q_ref, v_ref, f_ref, o_ref, None, rows=rows, reverse=reverse, bounded=bounded, slot=slot,
                   sfin_ref=sfin_ref)


def _hgrn_scan_kernel(*refs, rows, has_init, bounded):
    if has_init:
        q_ref, v_ref, f_ref, s0_ref, o_ref, sfin_ref, st_ref = refs
    else:
        q_ref, v_ref, f_ref, o_ref, sfin_ref, st_ref = refs
    d = pl.program_id(1)
    t = pl.program_id(2)

    @pl.when(t == 0)
    def _():
        for h in range(HG_HEADS):
            st_ref[h] = s0_ref[0, 0, h].T if has_init else jnp.zeros((HG_DV, HG_DK), F32)

    for reverse in (False, True):
        @pl.when(d == int(reverse))
        def _():
            _hgrn_tile(q_ref, v_ref, f_ref, o_ref, st_ref, rows=rows, reverse=reverse, bounded=bounded)

    @pl.when(t == pl.num_programs(2) - 1)
    def _():
        for h in range(HG_HEADS):
            sfin_ref[0, 0, h] = st_ref[h].T


def _hgrn_scan(q, v, f2, batch, seq, first_row, rt, s0=None, bounded=True):
    n = q.shape[1]
    nt = seq // rt
    has_init = s0 is not None
    out_shape = [jax.ShapeDtypeStruct((2, batch * seq, n), F32),
                 jax.ShapeDtypeStruct((batch, 2, HG_HEADS, HG_DK, HG_DV), F32)]
    if nt == 1 and not has_init:
        rows_of = lambda b: (first_row // rt + b, 0)
        return pl.pallas_call(
            functools.partial(_hgrn_whole_kernel, rows=rt, bounded=bounded),
            grid=(batch,),
            in_specs=[pl.BlockSpec((rt, n), rows_of), pl.BlockSpec((rt, n), rows_of),
                      pl.BlockSpec((2, rt, n), lambda b: (0, first_row // rt + b, 0))],
            out_specs=[pl.BlockSpec((2, rt, n), lambda b: (0, b, 0)),
                       pl.BlockSpec((1, 2, HG_HEADS, HG_DK, HG_DV), lambda b: (b, 0, 0, 0, 0))],
            out_shape=out_shape,
            compiler_params=_params("arbitrary"),
            name="hgrn_scan_whole",
        )(q, v, f2)

    def local(b, d, i):
        return b * nt + jnp.where(d == 0, i, nt - 1 - i)

    def slab(b, d, i):
        return first_row // rt + local(b, d, i)

    row = pl.BlockSpec((rt, n), lambda b, d, i: (slab(b, d, i), 0))
    state = pl.BlockSpec((1, 1, HG_HEADS, HG_DK, HG_DV), lambda b, d, i: (b, d, 0, 0, 0))
    ins = [q, v, f2]
    in_specs = [row, row, pl.BlockSpec((1, rt, n), lambda b, d, i: (d, slab(b, d, i), 0))]
    if has_init:
        ins.append(s0)
        in_specs.append(state)
    return pl.pallas_call(
        functools.partial(_hgrn_scan_kernel, rows=rt, has_init=has_init, bounded=bounded),
        grid=(batch, 2, nt),
        in_specs=in_specs,
        out_specs=[pl.BlockSpec((1, rt, n), lambda b, d, i: (d, local(b, d, i), 0)), state],
        out_shape=[jax.ShapeDtypeStruct((2, batch * seq, n), F32),
                   jax.ShapeDtypeStruct((batch, 2, HG_HEADS, HG_DK, HG_DV), F32)],
        scratch_shapes=[pltpu.VMEM((HG_HEADS, HG_DV, HG_DK), F32)],
        compiler_params=_params("arbitrary", "arbitrary", "arbitrary"),
        name="hgrn_scan_latent" if has_init else "hgrn_scan_prompt",
    )(*ins)


def _post_kernel(*refs, hgrn, final, pair_in, pair_out, prompt_tiles):
    refs = list(refs)
    n_mix = 4 if hgrn else 2
    mix_refs, refs = refs[:n_mix], refs[n_mix:]
    x, is_prompt, refs = _take_rows(refs, pair_in, prompt_tiles)
    mod_ref, wo_ref, gn_ref, wg_ref, wu_ref, wd_ref = refs[:6]
    refs = list(refs[6:])
    fn_ref = refs.pop(0) if final else None
    if hgrn:
        odp_ref, odl_ref, g_ref, onorm_ref = mix_refs
        o2 = jnp.where(is_prompt, odp_ref[0] + odp_ref[1], odl_ref[0] + odl_ref[1])
        gate = g_ref[...]
        onorm = onorm_ref[...]
        parts = []
        for h in range(HG_HEADS):
            sl = slice(h * HG_DV, (h + 1) * HG_DV)
            parts.append(_rms(o2[:, sl], onorm) * gate[:, sl])
        o = jnp.concatenate(parts, axis=1).astype(BF16)
    else:
        o = jnp.where(is_prompt, mix_refs[0][...], mix_refs[1][...])
    m = mod_ref[0]
    x1 = x + m[2:3] * _dot(o, wo_ref[...])
    h2 = _modulate(x1, gn_ref[...], m[3:4], m[4:5]).astype(BF16)
    a = (_silu(_dot(h2, wg_ref[0])) * _dot(h2, wu_ref[0])).astype(BF16)
    x2 = x1 + m[5:6] * _dot(a, wd_ref[0])
    if final:
        x2 = _rms(x2, fn_ref[...])
    if pair_out:
        @pl.when(is_prompt)
        def _():
            refs[0][...] = x2

        @pl.when(jnp.logical_not(is_prompt))
        def _():
            refs[1][...] = x2
    else:
        refs[0][...] = x2


def _layer_of(stack, layer):
    shape = (1,) + stack.shape[1:]
    return pl.BlockSpec(shape, lambda *_: (layer,) + (0,) * (len(shape) - 1), pipeline_mode=pl.Buffered(1))


def _post(mix, x, mods, wo, gn, wg, wu, wd, layer, rows, final_norm=None, hgrn=False, pair_out=False):
    xs, x_specs, pair_in, t, d = _rows_inputs(x, rows)
    tm, pt = rows.tm, rows.prompt_tiles
    if hgrn:
        od_p, od_l, gate, onorm = mix
        head = [od_p, od_l, gate, onorm]
        head_specs = [pl.BlockSpec((2, tm, d), lambda i: (0, jnp.minimum(i, pt - 1), 0)),
                      pl.BlockSpec((2, tm, d), lambda i: (0, jnp.maximum(i - pt, 0), 0)),
                      rows.row_spec(d), _whole(onorm.shape)]
    else:
        head = list(mix)
        head_specs = [rows.prompt_spec(mix[0].shape[1]), rows.latent_spec(mix[1].shape[1])]
    ins = head + xs + [mods, wo, gn, wg, wu, wd]
    in_specs = (head_specs + x_specs + [rows.mod_spec(d), _whole(wo.shape), _whole(gn.shape)]
                + [_layer_of(a, layer) for a in (wg, wu, wd)])
    if final_norm is not None:
        ins.append(final_norm)
        in_specs.append(_whole(final_norm.shape))
    if pair_out:
        out_specs = rows.x_specs(d)
        out_shape = [jax.ShapeDtypeStruct((pt * tm, d), F32), jax.ShapeDtypeStruct((t - pt * tm, d), F32)]
    else:
        out_specs = rows.row_spec(d)
        out_shape = jax.ShapeDtypeStruct((t, d), F32)
    out = pl.pallas_call(
        functools.partial(_post_kernel, hgrn=hgrn, final=final_norm is not None, pair_in=pair_in, pair_out=pair_out,
                          prompt_tiles=pt),
        grid=(rows.n_tiles,),
        in_specs=in_specs,
        out_specs=out_specs,
        out_shape=out_shape,
        compiler_params=_params("arbitrary"),
        name="post",
    )(*ins)
    return tuple(out) if pair_out else out


def _pick_tile(n_prompt_rows, dec_seq, want):
    tm = want
    while n_prompt_rows % tm or dec_seq % tm:
        tm //= 2
    return tm


def kernel(x_prompt, x_sample, cache_mla_ckv, cache_mla_krope, state_hgrn, cache_swa_k, cache_swa_v, c, c_ctx, ada_w, ada_b, norm_mix, norm_ffn, ffn_w_gate, ffn_w_up, ffn_w_down, final_norm, mla_w_dq, mla_q_norm, mla_w_uq, mla_w_dkv, mla_kv_norm, mla_w_uk, mla_w_uv, mla_w_o, hg_w_q, hg_w_f, hg_w_i, hg_w_g, hg_o_norm, hg_w_o, hg_lb_logits, swa_w_q, swa_w_k, swa_w_v, swa_w_o, swa_sink):
    batch, seq, d = x_prompt.shape
    dec_batch, dec_seq, _ = x_sample.shape
    past = cache_mla_ckv.shape[2]
    depth = ada_w.shape[0]
    n_prompt = batch * seq
    n_rows = n_prompt + dec_batch * dec_seq
    assert dec_batch + 1 <= COND_ROWS and seq % SWA_WINDOW == 0 and dec_seq % (2 * SWA_WINDOW) == 0
    assert n_prompt % dec_seq == 0

    pre_rows = _Rows(n_prompt, dec_seq, n_rows, _pick_tile(n_prompt, dec_seq, 512))
    post_rows = _Rows(n_prompt, dec_seq, n_rows, _pick_tile(n_prompt, dec_seq, 512))
    hg_pre_rows = _Rows(n_prompt, dec_seq, n_rows, _pick_tile(n_prompt, dec_seq, 256))
    hg_post_rows = _Rows(n_prompt, dec_seq, n_rows, _pick_tile(n_prompt, dec_seq, 256))
    tq = _pick_tile(n_prompt, dec_seq, 256)
    cos64, sin64 = _rope_tables(dec_seq, pre_rows.tm)
    n_tab = cos64.shape[0]
    cos_mla = jnp.concatenate([cos64, jnp.ones((n_tab, ROPE_PERIOD), F32)], axis=1)
    sin_mla = jnp.concatenate([sin64, jnp.zeros((n_tab, ROPE_PERIOD), F32)], axis=1)
    cos_swa, sin_swa = jnp.tile(cos64, (1, 2)), jnp.tile(sin64, (1, 2))

    cond = jnp.concatenate([c_ctx[None, :], c, jnp.zeros((COND_ROWS - 1 - dec_batch, d), F32)], axis=0)
    mods = _adaln(cond, ada_w, ada_b).reshape(depth, COND_ROWS, 6, d)

    x = (x_prompt.reshape(n_prompt, d), x_sample.reshape(dec_batch * dec_seq, d))
    ffn_wg, ffn_wu, ffn_wd = ffn_w_gate.astype(BF16), ffn_w_up.astype(BF16), ffn_w_down.astype(BF16)
    row1 = lambda a: a.reshape(1, -1)
    new_ckv, new_krope, new_hg, new_k, new_v = [], [], [], [], []
    for i in range(depth):
        kind, j = i % N_MIXERS, i // N_MIXERS
        gn = row1(norm_mix[i])
        if kind == 0:
            uq = mla_w_uq[j].reshape(-1, MLA_HEADS, MLA_NOPE_DIM + MLA_ROPE_DIM)
            uq_rope = jnp.pad(uq[:, :, MLA_NOPE_DIM:], ((0, 0), (0, 0), (0, LANES - MLA_ROPE_DIM)))
            uq = jnp.concatenate([uq[:, :, :MLA_NOPE_DIM].reshape(uq.shape[0], -1),
                                  uq_rope.reshape(uq.shape[0], -1)], axis=1)
            w = {
                "dq": mla_w_dq[j].astype(BF16), "q_norm": row1(mla_q_norm[j]), "uq": uq.astype(BF16),
                "dkv": jnp.pad(mla_w_dkv[j], ((0, 0), (0, LANES - MLA_ROPE_DIM))).astype(BF16),
                "kv_norm": row1(mla_kv_norm[j]),
                "uk": mla_w_uk[j].astype(BF16), "uv_t": mla_w_uv[j].T.astype(BF16),
            }
            q, ckv, kr_raw, kcat, vt = _mla_pre(x, mods[i], gn, cos_mla, sin_mla, w, pre_rows)
            kr_ctx = jnp.pad(cache_mla_krope[:, j].reshape(dec_batch * past, -1), ((0, 0), (0, LANES - MLA_ROPE_DIM)))
            kcat_ctx, vt_ctx = _mla_expand(cache_mla_ckv[:, j].reshape(dec_batch * past, -1), kr_ctx, w["uk"], w["uv_t"])
            mix = (_mla_attn_prompt(q, kcat, vt, batch, seq),
                   _mla_attn_latent(q, kcat, vt, kcat_ctx, vt_ctx, n_prompt, dec_batch, dec_seq, past, tq))
            wo = mla_w_o[j].astype(BF16)
            new_ckv.append(ckv.reshape(batch, seq, -1))
            new_krope.append(kr_raw.reshape(batch, seq, -1))
        elif kind == 1:
            w5 = jnp.concatenate([hg_w_q[j], hg_w_i[j], hg_w_g[j], hg_w_f[j, 0], hg_w_f[j, 1]], axis=1).astype(BF16)
            q, v, gate, f2, q_max = _hgrn_pre(x, mods[i], gn, w5, hg_lb_logits, i, hg_pre_rows)
            rt = _pick_tile(seq, dec_seq, 256)
            sm = jax.nn.softmax(hg_lb_logits.astype(F32), axis=1)
            lb_min = jnp.min(jnp.cumsum(sm, axis=1)[:, i] - sm[:, 0])
            worst = (HG_DIAG_BLOCK // 2) * -jnp.log(lb_min) + jnp.log(jnp.maximum(jnp.max(q_max), 1.0))
            fits = worst < HG_MAX_EXPONENT

            def scans(bounded):
                od_p, s_p = _hgrn_scan(q, v, f2, batch, seq, 0, rt, bounded=bounded)
                od_l, _ = _hgrn_scan(q, v, f2, dec_batch, dec_seq, n_prompt, rt, s0=state_hgrn[:, j], bounded=bounded)
                return od_p, od_l, s_p

            od_p, od_l, s_prompt = lax.cond(fits, lambda: scans(True), lambda: scans(False))
            mix = (od_p, od_l, gate, row1(hg_o_norm[j]))
            wo = hg_w_o[j].astype(BF16)
            new_hg.append(s_prompt)
        else:
            wqkv = jnp.concatenate([swa_w_q[j], swa_w_k[j], swa_w_v[j]], axis=1).astype(BF16)
            q, kx, vt, k_raw, v_raw = _swa_pre(x, mods[i], gn, cos_swa, sin_swa, wqkv, pre_rows)
            sink = swa_sink[j]
            k_ctx = cache_swa_k[:, j].reshape(dec_batch * past, SWA_KV_HEADS, 1, SWA_HEAD_DIM).transpose(1, 0, 2, 3)
            kx_ctx = jnp.broadcast_to(k_ctx, (SWA_KV_HEADS, dec_batch * past, SWA_GROUP, SWA_HEAD_DIM))
            kx_ctx = kx_ctx.reshape(SWA_KV_HEADS, dec_batch * past, SWA_GW).astype(BF16)
            vt_ctx = cache_swa_v[:, j].reshape(dec_batch * past, SWA_KV_HEADS, SWA_HEAD_DIM).transpose(1, 2, 0).astype(BF16)
            mix = (_swa_attn_prompt(q, kx, vt, sink, batch, seq),
                   _swa_attn_latent(q, kx, vt, kx_ctx, vt_ctx, sink, n_prompt, dec_batch, dec_seq, past, tq))
            wo = swa_w_o[j].astype(BF16)
            new_k.append(k_raw.reshape(batch, seq, SWA_KV_HEADS, SWA_HEAD_DIM))
            new_v.append(v_raw.reshape(batch, seq, SWA_KV_HEADS, SWA_HEAD_DIM))
        x = _post(mix, x, mods[i], wo, row1(norm_ffn[i]), ffn_wg, ffn_wu, ffn_wd, i,
                  hg_post_rows if kind == 1 else post_rows,
                  final_norm=row1(final_norm) if i == depth - 1 else None, hgrn=kind == 1,
                  pair_out=i == depth - 1)
    y_prompt = x[0].reshape(batch, seq, d)
    y_sample = x[1].reshape(dec_batch, dec_seq, d)
    return (y_prompt, y_sample, jnp.stack(new_ckv, axis=1), jnp.stack(new_krope, axis=1),
            jnp.stack(new_hg, axis=1), jnp.stack(new_k, axis=1), jnp.stack(new_v, axis=1))
```

```python
import functools

import numpy as np
import jax
import jax.numpy as jnp
from jax import lax
from jax.experimental import pallas as pl
from jax.experimental.pallas import tpu as pltpu

F32 = jnp.float32
BF16 = jnp.bfloat16

GRID_W = 64
N_MIXERS = 3

MLA_HEADS = 8
MLA_KV_LORA = 256
MLA_NOPE_DIM = 128
MLA_ROPE_DIM = 64
MLA_V_DIM = 128
MLA_QK_PAD = 256
MLA_SCALE = (MLA_NOPE_DIM + MLA_ROPE_DIM) ** -0.5

HG_HEADS = 8
HG_DK = 128
HG_DV = 128
HG_DIAG_BLOCK = 32
HG_MAX_EXPONENT = 80.0

SWA_HEADS = 16
SWA_KV_HEADS = 4
SWA_GROUP = SWA_HEADS // SWA_KV_HEADS
SWA_HEAD_DIM = 64
SWA_GW = SWA_GROUP * SWA_HEAD_DIM
SWA_WINDOW = 128
SWA_SCALE = SWA_HEAD_DIM ** -0.5

ROPE_BASE = 10000.0
ROPE_PERIOD = 64
ROPE_QUARTER = 16
NORM_EPS = 1e-6
NEG_INF = -1e30
LOG2_E = 1.4426950408889634

LANES = 128
COND_ROWS = 8
VMEM_LIMIT = 56 * 1024 * 1024


def _sigmoid(x):
    return jax.nn.sigmoid(x)


def _silu(x):
    return x * jax.nn.sigmoid(x)


def _rms(x, g):
    return x * lax.rsqrt(jnp.mean(x * x, axis=-1, keepdims=True) + NORM_EPS) * g


def _modulate(x, g, shift, scale):
    return _rms(x, g) * (1.0 + scale) + shift


def _dot(a, b):
    return jnp.dot(a, b, preferred_element_type=F32)


def _dot_nt(a, b):
    return lax.dot_general(a, b, (((1,), (1,)), ((), ())), preferred_element_type=F32)


def _dot_tn(a, b):
    return lax.dot_general(a, b, (((0,), (0,)), ((), ())), preferred_element_type=F32)


def _swap_pairs(x):
    n = x.shape[1]
    lane = lax.broadcasted_iota(jnp.int32, x.shape, 1)
    ahead = pltpu.roll(x, n - ROPE_QUARTER, 1)
    behind = pltpu.roll(x, ROPE_QUARTER, 1)
    return jnp.where((lane & (2 * ROPE_QUARTER - 1)) < ROPE_QUARTER, ahead, behind)


def _rope(x, cos, sin):
    reps = x.shape[1] // cos.shape[1]
    if reps > 1:
        cos = jnp.concatenate([cos] * reps, axis=1)
        sin = jnp.concatenate([sin] * reps, axis=1)
    return x * cos + _swap_pairs(x) * sin


def _whole(shape):
    zeros = (0,) * len(shape)
    return pl.BlockSpec(shape, lambda *_: zeros, pipeline_mode=pl.Buffered(1))


def _params(*sem):
    return pltpu.CompilerParams(dimension_semantics=sem, vmem_limit_bytes=VMEM_LIMIT)


class _Rows:
    def __init__(self, n_prompt_rows, dec_seq, n_rows, tm):
        assert n_prompt_rows % tm == 0 and dec_seq % tm == 0
        self.tm = tm
        self.n_tiles = n_rows // tm
        self.prompt_tiles = n_prompt_rows // tm
        self.seq_tiles = dec_seq // tm

    def cond(self, i):
        return jnp.where(i < self.prompt_tiles, 0, 1 + jnp.maximum(i - self.prompt_tiles, 0) // self.seq_tiles)

    def rope_block(self, i):
        return jnp.where(i < self.prompt_tiles, 0, 1 + jnp.maximum(i - self.prompt_tiles, 0) % self.seq_tiles)

    def row_spec(self, width):
        return pl.BlockSpec((self.tm, width), lambda i: (i, 0))

    def col_spec(self, height):
        return pl.BlockSpec((height, self.tm), lambda i: (0, i))

    def mod_spec(self, d):
        return pl.BlockSpec((1, 6, d), lambda i: (self.cond(i), 0, 0))

    def rope_spec(self, width):
        return pl.BlockSpec((self.tm, width), lambda i: (self.rope_block(i), 0))

    def x_specs(self, width):
        return [self.prompt_spec(width), self.latent_spec(width)]

    def is_prompt(self):
        return pl.program_id(0) < self.prompt_tiles

    def prompt_spec(self, width):
        return pl.BlockSpec((self.tm, width), lambda i: (jnp.minimum(i, self.prompt_tiles - 1), 0))

    def latent_spec(self, width):
        return pl.BlockSpec((self.tm, width), lambda i: (jnp.maximum(i - self.prompt_tiles, 0), 0))


def _rope_tables(dec_seq, tm):
    pos = np.arange(dec_seq)
    row = (pos // GRID_W).astype(np.float32)
    col = (pos % GRID_W).astype(np.float32)
    inv_freq = (ROPE_BASE ** (-np.arange(ROPE_QUARTER, dtype=np.float32) / ROPE_QUARTER)).astype(np.float32)
    ang_r = row[:, None] * inv_freq[None, :]
    ang_c = col[:, None] * inv_freq[None, :]
    cos = np.concatenate([np.cos(ang_r), np.cos(ang_r), np.cos(ang_c), np.cos(ang_c)], axis=1)
    sin = np.concatenate([-np.sin(ang_r), np.sin(ang_r), -np.sin(ang_c), np.sin(ang_c)], axis=1)
    cos = np.concatenate([np.ones((tm, ROPE_PERIOD), np.float32), cos], axis=0)
    sin = np.concatenate([np.zeros((tm, ROPE_PERIOD), np.float32), sin], axis=0)
    return cos.astype(np.float32), sin.astype(np.float32)


def _adaln_kernel(c_ref, w_ref, b_ref, o_ref):
    o_ref[0] = _dot(_silu(c_ref[...]), w_ref[0]) + b_ref[0]


def _adaln(cond, ada_w, ada_b, tn=1536):
    depth, d, n = ada_w.shape
    return pl.pallas_call(
        _adaln_kernel,
        grid=(depth, n // tn),
        in_specs=[pl.BlockSpec((COND_ROWS, d), lambda l, j: (0, 0)),
                  pl.BlockSpec((1, d, tn), lambda l, j: (l, 0, j)),
                  pl.BlockSpec((1, 1, tn), lambda l, j: (l, 0, j))],
        out_specs=pl.BlockSpec((1, COND_ROWS, tn), lambda l, j: (l, 0, j)),
        out_shape=jax.ShapeDtypeStruct((depth, COND_ROWS, n), F32),
        compiler_params=_params("arbitrary", "arbitrary"),
        name="adaln",
    )(cond, ada_w, ada_b.reshape(depth, 1, n))


def _mla_store_heads(q_ref, kcat_ref, vt_ref, qn, qr_pad, kn, kr_pad, vt):
    for h in range(MLA_HEADS):
        nope = slice(h * MLA_NOPE_DIM, (h + 1) * MLA_NOPE_DIM)
        if q_ref is not None:
            q_ref[h] = jnp.concatenate([qn[:, nope], qr_pad[:, h * LANES:(h + 1) * LANES]], axis=1)
        kcat_ref[h] = jnp.concatenate([kn[:, nope], kr_pad], axis=1)
        vt_ref[h] = vt[h * MLA_V_DIM:(h + 1) * MLA_V_DIM, :]


def _read_rows(is_prompt, p_ref, l_ref):
    return jnp.where(is_prompt, p_ref[...], l_ref[...])


def _take_rows(refs, pair, prompt_tiles):
    is_prompt = pl.program_id(0) < prompt_tiles
    if pair:
        return _read_rows(is_prompt, refs[0], refs[1]), is_prompt, refs[2:]
    return refs[0][...], is_prompt, refs[1:]


def _rows_inputs(x, rows):
    if isinstance(x, tuple):
        d = x[0].shape[1]
        return list(x), rows.x_specs(d), True, x[0].shape[0] + x[1].shape[0], d
    return [x], [rows.row_spec(x.shape[1])], False, x.shape[0], x.shape[1]


def _mla_pre_kernel(*refs, pair, prompt_tiles):
    x, is_prompt, refs = _take_rows(refs, pair, prompt_tiles)
    (mod_ref, gn_ref, cos_ref, sin_ref, wdq_ref, qnorm_ref, wuq_ref, wdkv_ref, kvnorm_ref, wuk_ref, wuvt_ref,
     q_ref, ckv_ref, krraw_ref, kcat_ref, vt_ref) = refs
    m = mod_ref[0]
    h = _modulate(x, gn_ref[...], m[0:1], m[1:2]).astype(BF16)
    q_lat = _rms(_dot(h, wdq_ref[...]), qnorm_ref[...]).astype(BF16)
    cos, sin = cos_ref[...], sin_ref[...]
    nn = MLA_HEADS * MLA_NOPE_DIM
    qn = _dot(q_lat, wuq_ref[:, :nn]).astype(BF16)
    qr_pad = _rope(_dot(q_lat, wuq_ref[:, nn:]), cos, sin).astype(BF16)
    kv = _dot(h, wdkv_ref[...])
    ckv = _rms(kv[:, :MLA_KV_LORA], kvnorm_ref[...])
    kr = kv[:, MLA_KV_LORA:]

    @pl.when(is_prompt)
    def _():
        ckv_ref[...] = ckv
        krraw_ref[...] = kr[:, :MLA_ROPE_DIM]

    kr_pad = _rope(kr, cos, sin).astype(BF16)
    cb = ckv.astype(BF16)
    _mla_store_heads(q_ref, kcat_ref, vt_ref, qn, qr_pad, _dot(cb, wuk_ref[...]).astype(BF16), kr_pad,
                     _dot_nt(wuvt_ref[...], cb).astype(BF16))


def _mla_pre(x, mods, gn, cos, sin, w, rows):
    xs, x_specs, pair, t, d = _rows_inputs(x, rows)
    tm = rows.tm
    n_prompt = rows.prompt_tiles * tm
    weights = [w["dq"], w["q_norm"], w["uq"], w["dkv"], w["kv_norm"], w["uk"], w["uv_t"]]
    ins = xs + [mods, gn, cos, sin] + weights
    in_specs = x_specs + [rows.mod_spec(d), _whole(gn.shape),
                          rows.rope_spec(cos.shape[1]), rows.rope_spec(sin.shape[1])]
    in_specs += [_whole(a.shape) for a in weights]
    heads_rows = pl.BlockSpec((MLA_HEADS, tm, MLA_QK_PAD), lambda i: (0, i, 0))
    return pl.pallas_call(
        functools.partial(_mla_pre_kernel, pair=pair, prompt_tiles=rows.prompt_tiles),
        grid=(rows.n_tiles,),
        in_specs=in_specs,
        out_specs=[heads_rows, rows.prompt_spec(MLA_KV_LORA), rows.prompt_spec(MLA_ROPE_DIM), heads_rows,
                   pl.BlockSpec((MLA_HEADS, MLA_V_DIM, tm), lambda i: (0, 0, i))],
        out_shape=[jax.ShapeDtypeStruct((MLA_HEADS, t, MLA_QK_PAD), BF16),
                   jax.ShapeDtypeStruct((n_prompt, MLA_KV_LORA), F32),
                   jax.ShapeDtypeStruct((n_prompt, MLA_ROPE_DIM), F32),
                   jax.ShapeDtypeStruct((MLA_HEADS, t, MLA_QK_PAD), BF16),
                   jax.ShapeDtypeStruct((MLA_HEADS, MLA_V_DIM, t), BF16)],
        compiler_params=_params("arbitrary"),
        name="mla_pre",
    )(*ins)


def _mla_expand_kernel(c_ref, kr_ref, wuk_ref, wuvt_ref, kcat_ref, vt_ref):
    cb = c_ref[...].astype(BF16)
    _mla_store_heads(None, kcat_ref, vt_ref, None, None, _dot(cb, wuk_ref[...]).astype(BF16),
                     kr_ref[...].astype(BF16), _dot_nt(wuvt_ref[...], cb).astype(BF16))


def _mla_expand(ckv, kr_pad, wuk, wuvt):
    n = ckv.shape[0]
    return pl.pallas_call(
        _mla_expand_kernel,
        out_shape=[jax.ShapeDtypeStruct((MLA_HEADS, n, MLA_QK_PAD), BF16),
                   jax.ShapeDtypeStruct((MLA_HEADS, MLA_V_DIM, n), BF16)],
        compiler_params=pltpu.CompilerParams(vmem_limit_bytes=VMEM_LIMIT),
        name="mla_expand",
    )(ckv, kr_pad, wuk, wuvt)


def _mla_attn_kernel(*refs, n_src):
    q_ref = refs[0]
    srcs = [(refs[1 + 2 * i], refs[2 + 2 * i]) for i in range(n_src)]
    o_ref, o_buf = refs[1 + 2 * n_src:3 + 2 * n_src]
    s_bufs = refs[3 + 2 * n_src:]
    n_keys = [k_ref.shape[1] for k_ref, _ in srcs]
    starts = [sum(n_keys[:i]) for i in range(n_src)]

    def put_scores(h, s_buf):
        q = q_ref[h]
        for (k_ref, _), first, n in zip(srcs, starts, n_keys):
            s_buf[first:first + n, :] = _dot_nt(k_ref[h], q)

    def finish(h, s_buf):
        s = s_buf[...]
        mx = jnp.max(s, axis=0, keepdims=True)
        p = jnp.exp2((s - mx) * (MLA_SCALE * LOG2_E))
        den = jnp.sum(p, axis=0, keepdims=True)
        p = p.astype(BF16)
        acc = None
        for (_, vt_ref), first, n in zip(srcs, starts, n_keys):
            a = _dot(vt_ref[h], p[first:first + n, :])
            acc = a if acc is None else acc + a
        o_buf[h] = (acc / den).T.astype(BF16)

    _pipelined_heads(MLA_HEADS, put_scores, finish, s_bufs)
    for h in range(MLA_HEADS):
        o_ref[:, h * MLA_V_DIM:(h + 1) * MLA_V_DIM] = o_buf[h]


def _mla_attn_scratch(n_keys, tq):
    return [pltpu.VMEM((MLA_HEADS, tq, MLA_V_DIM), BF16)] + [pltpu.VMEM((n_keys, tq), F32)] * 4


def _pipelined_heads(n_heads, put_scores, finish, bufs):
    a, b, c, d = bufs
    assert n_heads % 4 == 0
    put_scores(0, a)
    put_scores(1, b)

    def quad(j, carry):
        h = 4 * j
        put_scores(h + 2, c)
        put_scores(h + 3, d)
        finish(h, a)
        finish(h + 1, b)
        put_scores(h + 4, a)
        put_scores(h + 5, b)
        finish(h + 2, c)
        finish(h + 3, d)
        return carry

    lax.fori_loop(0, n_heads // 4 - 1, quad, 0)
    h = n_heads - 4
    put_scores(h + 2, c)
    put_scores(h + 3, d)
    finish(h, a)
    finish(h + 1, b)
    finish(h + 2, c)
    finish(h + 3, d)


def _staged_attention(scores, values_t, scale, sinks=None):
    c = scale * LOG2_E
    mx = [jnp.max(s, axis=0, keepdims=True) for s in scores]
    if sinks is not None:
        mx = [jnp.maximum(m, z) for m, z in zip(mx, sinks)]
    p = [jnp.exp2((s - m) * c) for s, m in zip(scores, mx)]
    den = [jnp.sum(x, axis=0, keepdims=True) for x in p]
    if sinks is not None:
        den = [d + jnp.exp2((z - m) * c) for d, z, m in zip(den, sinks, mx)]
    acc = [_dot(v, x.astype(BF16)) for v, x in zip(values_t, p)]
    return [a / d for a, d in zip(acc, den)]


def _mla_attn_prompt_kernel(q_ref, k_ref, vt_ref, o_ref, *, seq):
    per_step = q_ref.shape[1] // seq
    spans = [slice(s * seq, (s + 1) * seq) for s in range(per_step)]
    scores = [_dot_nt(k_ref[h, sp, :], q_ref[h, sp, :]) for sp in spans for h in range(MLA_HEADS)]
    values = [vt_ref[h, :, sp] for sp in spans for h in range(MLA_HEADS)]
    outs = _staged_attention(scores, values, MLA_SCALE)
    for s, sp in enumerate(spans):
        for h in range(MLA_HEADS):
            o_ref[sp, h * MLA_V_DIM:(h + 1) * MLA_V_DIM] = outs[s * MLA_HEADS + h].T.astype(BF16)


def _mla_attn_prompt(q, kcat, vt, batch, seq):
    hv = MLA_HEADS * MLA_V_DIM
    per_step = next(n for n in (4, 2, 1) if batch % n == 0)
    seq_rows = seq * per_step
    rows = pl.BlockSpec((MLA_HEADS, seq_rows, MLA_QK_PAD), lambda b: (0, b, 0))
    return pl.pallas_call(
        functools.partial(_mla_attn_prompt_kernel, seq=seq),
        grid=(batch // per_step,),
        in_specs=[rows, rows, pl.BlockSpec((MLA_HEADS, MLA_V_DIM, seq_rows), lambda b: (0, 0, b))],
        out_specs=pl.BlockSpec((seq_rows, hv), lambda b: (b, 0)),
        out_shape=jax.ShapeDtypeStruct((batch * seq, hv), BF16),
        compiler_params=_params("arbitrary"),
        name="mla_attn_prompt",
    )(q, kcat, vt)


def _mla_attn_latent(q, kcat, vt, kcat_ctx, vt_ctx, n_prompt, dec_batch, dec_seq, past, tq):
    hv = MLA_HEADS * MLA_V_DIM
    nq = dec_seq // tq
    lat_blk = n_prompt // dec_seq
    return pl.pallas_call(
        functools.partial(_mla_attn_kernel, n_src=2),
        grid=(dec_batch, nq),
        in_specs=[pl.BlockSpec((MLA_HEADS, tq, MLA_QK_PAD), lambda b, i: (0, n_prompt // tq + b * nq + i, 0)),
                  pl.BlockSpec((MLA_HEADS, past, MLA_QK_PAD), lambda b, i: (0, b, 0)),
                  pl.BlockSpec((MLA_HEADS, MLA_V_DIM, past), lambda b, i: (0, 0, b)),
                  pl.BlockSpec((MLA_HEADS, dec_seq, MLA_QK_PAD), lambda b, i: (0, lat_blk + b, 0)),
                  pl.BlockSpec((MLA_HEADS, MLA_V_DIM, dec_seq), lambda b, i: (0, 0, lat_blk + b))],
        out_specs=pl.BlockSpec((tq, hv), lambda b, i: (b * nq + i, 0)),
        out_shape=jax.ShapeDtypeStruct((dec_batch * dec_seq, hv), BF16),
        scratch_shapes=_mla_attn_scratch(past + dec_seq, tq),
        compiler_params=_params("arbitrary", "arbitrary"),
        name="mla_attn_latent",
    )(q, kcat_ctx, vt_ctx, kcat, vt)


def _tile_heads(x):
    n = x.shape[1]
    block = lax.broadcasted_iota(jnp.int32, x.shape, 1) // SWA_HEAD_DIM
    rolled = [x] + [pltpu.roll(x, s * SWA_HEAD_DIM, 1) for s in range(1, SWA_KV_HEADS)]
    out = []
    for kvh in range(SWA_KV_HEADS):
        blk = rolled[(0 - kvh) % SWA_KV_HEADS]
        for g in range(1, n // SWA_HEAD_DIM):
            blk = jnp.where(block == g, rolled[(g - kvh) % SWA_KV_HEADS], blk)
        out.append(blk)
    return jnp.concatenate(out, axis=1)


def _swa_pre_kernel(*refs, pair, prompt_tiles):
    x, is_prompt, refs = _take_rows(refs, pair, prompt_tiles)
    mod_ref, gn_ref, cos_ref, sin_ref, wqkv_ref, q_ref, kx_ref, vt_ref, kraw_ref, vraw_ref = refs
    m = mod_ref[0]
    h = _modulate(x, gn_ref[...], m[0:1], m[1:2]).astype(BF16)
    qkv = _dot(h, wqkv_ref[...])
    nq, nk = SWA_HEADS * SWA_HEAD_DIM, SWA_KV_HEADS * SWA_HEAD_DIM
    cos, sin = cos_ref[...], sin_ref[...]
    q = _rope(qkv[:, :nq], cos, sin).astype(BF16)
    k = qkv[:, nq:nq + nk]
    v = qkv[:, nq + nk:]

    @pl.when(is_prompt)
    def _():
        kraw_ref[...] = k.reshape(kraw_ref.shape)
        vraw_ref[...] = v.reshape(vraw_ref.shape)

    kx = _tile_heads(_rope(k, cos, sin)).astype(BF16)
    vt = v.T.astype(BF16)
    for kvh in range(SWA_KV_HEADS):
        q_ref[kvh] = q[:, kvh * SWA_GW:(kvh + 1) * SWA_GW]
        kx_ref[kvh] = kx[:, kvh * SWA_GW:(kvh + 1) * SWA_GW]
        vt_ref[kvh] = vt[kvh * SWA_HEAD_DIM:(kvh + 1) * SWA_HEAD_DIM, :]


def _swa_pre(x, mods, gn, cos, sin, wqkv, rows):
    xs, x_specs, pair, t, d = _rows_inputs(x, rows)
    tm = rows.tm
    n_prompt = rows.prompt_tiles * tm
    nk = SWA_KV_HEADS * SWA_HEAD_DIM
    heads_rows = pl.BlockSpec((SWA_KV_HEADS, tm, SWA_GW), lambda i: (0, i, 0))
    raw_spec = pl.BlockSpec((tm, SWA_KV_HEADS, SWA_HEAD_DIM), lambda i: (jnp.minimum(i, rows.prompt_tiles - 1), 0, 0))
    return pl.pallas_call(
        functools.partial(_swa_pre_kernel, pair=pair, prompt_tiles=rows.prompt_tiles),
        grid=(rows.n_tiles,),
        in_specs=x_specs + [rows.mod_spec(d), _whole(gn.shape),
                            rows.rope_spec(cos.shape[1]), rows.rope_spec(sin.shape[1]), _whole(wqkv.shape)],
        out_specs=[heads_rows, heads_rows, pl.BlockSpec((SWA_KV_HEADS, SWA_HEAD_DIM, tm), lambda i: (0, 0, i)),
                   raw_spec, raw_spec],
        out_shape=[jax.ShapeDtypeStruct((SWA_KV_HEADS, t, SWA_GW), BF16),
                   jax.ShapeDtypeStruct((SWA_KV_HEADS, t, SWA_GW), BF16),
                   jax.ShapeDtypeStruct((SWA_KV_HEADS, SWA_HEAD_DIM, t), BF16),
                   jax.ShapeDtypeStruct((n_prompt, SWA_KV_HEADS, SWA_HEAD_DIM), F32),
                   jax.ShapeDtypeStruct((n_prompt, SWA_KV_HEADS, SWA_HEAD_DIM), F32)],
        compiler_params=_params("arbitrary"),
        name="swa_pre",
    )(*xs, mods, gn, cos, sin, wqkv)


def _swa_heads(q_ref, sink_ref, o_ref, srcs, bias_ref, ot_buf, s_bufs):
    tq = q_ref.shape[1]
    n_keys = [kx_ref.shape[1] for kx_ref, _ in srcs]
    starts = [sum(n_keys[:i]) for i in range(len(srcs))]
    group = lax.broadcasted_iota(jnp.int32, (tq, SWA_GW), 1) // SWA_HEAD_DIM

    def put_scores(hq, s_buf):
        kvh, g = hq // SWA_GROUP, hq % SWA_GROUP
        q = jnp.where(group == g, q_ref[kvh].astype(F32), 0.0).astype(BF16)
        for (kx_ref, _), first, n in zip(srcs, starts, n_keys):
            s_buf[first:first + n, :] = _dot_nt(kx_ref[kvh], q)

    def finish(hq, s_buf):
        kvh, g = hq // SWA_GROUP, hq % SWA_GROUP
        s = s_buf[...]
        if bias_ref is not None:
            s = s + bias_ref[...]
        sink = jnp.full((1, 1), sink_ref[hq] * (1.0 / SWA_SCALE), F32)
        mx = jnp.maximum(jnp.max(s, axis=0, keepdims=True), sink)
        p = jnp.exp2((s - mx) * (SWA_SCALE * LOG2_E))
        den = jnp.sum(p, axis=0, keepdims=True) + jnp.exp2((sink - mx) * (SWA_SCALE * LOG2_E))
        p = p.astype(BF16)
        acc = None
        for (_, vt_ref), first, n in zip(srcs, starts, n_keys):
            a = _dot(vt_ref[kvh], p[first:first + n, :])
            acc = a if acc is None else acc + a
        ot_buf[kvh, pl.ds(pl.multiple_of(g * SWA_HEAD_DIM, SWA_HEAD_DIM), SWA_HEAD_DIM), :] = acc / den

    _pipelined_heads(SWA_HEADS, put_scores, finish, s_bufs)
    for kvh in range(SWA_KV_HEADS):
        o_ref[:, kvh * SWA_GW:(kvh + 1) * SWA_GW] = ot_buf[kvh].T.astype(BF16)


def _swa_attn_scratch(n_keys, tq):
    return [pltpu.VMEM((SWA_KV_HEADS, SWA_GW, tq), F32)] + [pltpu.VMEM((n_keys, tq), F32)] * 5


def _swa_attn_prompt_kernel(q_ref, kx_ref, vt_ref, sink_ref, o_ref, *, seq):
    per_step = q_ref.shape[1] // seq
    spans = [slice(s * seq, (s + 1) * seq) for s in range(per_step)]
    group = lax.broadcasted_iota(jnp.int32, (seq, SWA_GW), 1) // SWA_HEAD_DIM
    scores, values_t, sinks = [], [], []
    for sp in spans:
        for kvh in range(SWA_KV_HEADS):
            q_all = q_ref[kvh, sp, :].astype(F32)
            for g in range(SWA_GROUP):
                q = jnp.where(group == g, q_all, 0.0).astype(BF16)
                scores.append(_dot_nt(kx_ref[kvh, sp, :], q))
                values_t.append(vt_ref[kvh, :, sp])
                sinks.append(jnp.full((1, 1), sink_ref[kvh * SWA_GROUP + g] * (1.0 / SWA_SCALE), F32))
    outs = _staged_attention(scores, values_t, SWA_SCALE, sinks)
    for s, sp in enumerate(spans):
        for kvh in range(SWA_KV_HEADS):
            first = s * SWA_HEADS + kvh * SWA_GROUP
            ot = jnp.concatenate(outs[first:first + SWA_GROUP], axis=0)
            o_ref[sp, kvh * SWA_GW:(kvh + 1) * SWA_GW] = ot.T.astype(BF16)


def _swa_attn_prompt(q, kx, vt, sink, batch, seq):
    wq = SWA_HEADS * SWA_HEAD_DIM
    per_step = next(n for n in (4, 2, 1) if batch % n == 0)
    seq_rows = seq * per_step
    rows = pl.BlockSpec((SWA_KV_HEADS, seq_rows, SWA_GW), lambda b: (0, b, 0))
    return pl.pallas_call(
        functools.partial(_swa_attn_prompt_kernel, seq=seq),
        grid=(batch // per_step,),
        in_specs=[rows, rows, pl.BlockSpec((SWA_KV_HEADS, SWA_HEAD_DIM, seq_rows), lambda b: (0, 0, b)),
                  pl.BlockSpec(memory_space=pltpu.SMEM)],
        out_specs=pl.BlockSpec((seq_rows, wq), lambda b: (b, 0)),
        out_shape=jax.ShapeDtypeStruct((batch * seq, wq), BF16),
        compiler_params=_params("arbitrary"),
        name="swa_attn_prompt",
    )(q, kx, vt, sink)


def _swa_attn_latent_kernel(q_ref, kc_ref, vc_ref, kp_ref, vp_ref, km_ref, vm_ref, kn_ref, vn_ref,
                            sink_ref, o_ref, ot_buf, s_a, s_b, s_c, s_d, bias_ref, *, tq, dec_seq, past):
    i = pl.program_id(1)
    qpos = i * tq + lax.broadcasted_iota(jnp.int32, (1, tq), 1)

    def band(first, n):
        kpos = first + lax.broadcasted_iota(jnp.int32, (n, 1), 0)
        valid = (jnp.abs(qpos - kpos) <= SWA_WINDOW) & (kpos >= 0) & (kpos < dec_seq)
        return jnp.where(valid, 0.0, NEG_INF)

    bias_ref[...] = jnp.concatenate(
        [jnp.zeros((past, tq), F32), band(i * tq - SWA_WINDOW, SWA_WINDOW), band(i * tq, tq),
         band((i + 1) * tq, SWA_WINDOW)], axis=0)
    srcs = [(kc_ref, vc_ref), (kp_ref, vp_ref), (km_ref, vm_ref), (kn_ref, vn_ref)]
    _swa_heads(q_ref, sink_ref, o_ref, srcs, bias_ref, ot_buf, (s_a, s_b, s_c, s_d))


def _swa_attn_latent(q, kx, vt, kx_ctx, vt_ctx, sink, n_prompt, dec_batch, dec_seq, past, tq):
    t = q.shape[1]
    wq = SWA_HEADS * SWA_HEAD_DIM
    nq = dec_seq // tq
    w = SWA_WINDOW
    n_keys = past + tq + 2 * w
    first = lambda b, i: n_prompt + b * dec_seq + i * tq
    prev = lambda b, i: first(b, i) // w - 1
    nxt = lambda b, i: jnp.minimum((first(b, i) + tq) // w, t // w - 1)
    rows = lambda n, blk: pl.BlockSpec((SWA_KV_HEADS, n, SWA_GW), lambda b, i: (0, blk(b, i), 0))
    cols = lambda n, blk: pl.BlockSpec((SWA_KV_HEADS, SWA_HEAD_DIM, n), lambda b, i: (0, 0, blk(b, i)))
    main = lambda b, i: first(b, i) // tq
    ctx = lambda b, i: b
    return pl.pallas_call(
        functools.partial(_swa_attn_latent_kernel, tq=tq, dec_seq=dec_seq, past=past),
        grid=(dec_batch, nq),
        in_specs=[rows(tq, main), rows(past, ctx), cols(past, ctx), rows(w, prev), cols(w, prev),
                  rows(tq, main), cols(tq, main), rows(w, nxt), cols(w, nxt),
                  pl.BlockSpec(memory_space=pltpu.SMEM)],
        out_specs=pl.BlockSpec((tq, wq), lambda b, i: (b * nq + i, 0)),
        out_shape=jax.ShapeDtypeStruct((dec_batch * dec_seq, wq), BF16),
        scratch_shapes=_swa_attn_scratch(n_keys, tq),
        compiler_params=_params("arbitrary", "arbitrary"),
        name="swa_attn_latent",
    )(q, kx_ctx, vt_ctx, kx, vt, kx, vt, kx, vt, sink)


def _hgrn_pre_kernel(*refs, layer, pair, prompt_tiles):
    x, _, refs = _take_rows(refs, pair, prompt_tiles)
    mod_ref, gn_ref, w_ref, lbl_ref, q_ref, v_ref, g_ref, f_ref, qmax_ref = refs
    m = mod_ref[0]
    h = _modulate(x, gn_ref[...], m[0:1], m[1:2]).astype(BF16)
    y = _dot(h, w_ref[...])
    n = HG_HEADS * HG_DK
    q = _silu(y[:, :n])
    q_ref[...] = q
    qmax_ref[0] = jnp.max(jnp.abs(q), axis=0, keepdims=True)
    v_ref[...] = y[:, n:2 * n].astype(BF16)
    g_ref[...] = _silu(y[:, 2 * n:3 * n])
    for d in range(2):
        logits = lbl_ref[d]
        e = jnp.exp(logits - jnp.max(logits, axis=0, keepdims=True))
        s = e / jnp.sum(e, axis=0, keepdims=True)
        cs = s[0:1]
        for r in range(1, layer + 1):
            cs = cs + s[r:r + 1]
        lb = cs - s[0:1]
        f_ref[d] = lb + (1.0 - lb) * _sigmoid(y[:, (3 + d) * n:(4 + d) * n])


def _hgrn_pre(x, mods, gn, w5, lb_logits, layer, rows):
    xs, x_specs, pair, t, d = _rows_inputs(x, rows)
    n = HG_HEADS * HG_DK
    return pl.pallas_call(
        functools.partial(_hgrn_pre_kernel, layer=layer, pair=pair, prompt_tiles=rows.prompt_tiles),
        grid=(rows.n_tiles,),
        in_specs=x_specs + [rows.mod_spec(d), _whole(gn.shape), _whole(w5.shape), _whole(lb_logits.shape)],
        out_specs=[rows.row_spec(n), rows.row_spec(n), rows.row_spec(n),
                   pl.BlockSpec((2, rows.tm, n), lambda i: (0, i, 0)), pl.BlockSpec((1, 1, n), lambda i: (i, 0, 0))],
        out_shape=[jax.ShapeDtypeStruct((t, n), F32), jax.ShapeDtypeStruct((t, n), BF16),
                   jax.ShapeDtypeStruct((t, n), F32), jax.ShapeDtypeStruct((2, t, n), F32),
                   jax.ShapeDtypeStruct((rows.n_tiles, 1, n), F32)],
        compiler_params=_params("arbitrary"),
        name="hgrn_pre",
    )(*xs, mods, gn, w5, lb_logits)


def _tri_cumsum(tri, x):
    hi = x.astype(BF16)
    r1 = x - hi.astype(F32)
    mid = r1.astype(BF16)
    lo = (r1 - mid.astype(F32)).astype(BF16)
    return _dot(tri, hi) + _dot(tri, mid) + _dot(tri, lo)


def _hgrn_tile(q_ref, v_ref, f_ref, o_ref, st_ref, *, rows, reverse, bounded, slot=0, sfin_ref=None):
    r = rows
    diag = HG_DIAG_BLOCK if bounded else 1
    a = lax.broadcasted_iota(jnp.int32, (r, r), 0)
    b = lax.broadcasted_iota(jnp.int32, (r, r), 1)
    seen = (b >= a) if reverse else (b <= a)
    tri = jnp.where(seen, 1.0, 0.0).astype(BF16)

    q = q_ref[...]
    f = f_ref[slot]
    vb = v_ref[...].astype(BF16)
    lf = jnp.log(f)
    cum = _tri_cumsum(tri, lf)
    tot = cum[0:1, :] if reverse else cum[r - 1:r, :]
    kk = 1.0 - f
    q_in = (q * jnp.exp(cum)).astype(BF16)
    k_d = (kk * jnp.exp(tot - cum)).astype(BF16)
    e_tot = jnp.exp(tot)

    levels = []
    c = r // 2
    while c >= diag:
        q_half = (a % (2 * c) < c) if reverse else (a % (2 * c) >= c)
        k_half = (b % (2 * c) >= c) if reverse else (b % (2 * c) < c)
        levels.append((2 * c, c if reverse else c - 1, ((a // (2 * c)) == (b // (2 * c))) & q_half & k_half))
        c //= 2
    levels.append((diag, diag // 2, ((a // diag) == (b // diag)) & seen))

    def spread(rows_of_block, size):
        return jnp.concatenate([jnp.broadcast_to(row, (size, row.shape[1])) for row in rows_of_block], axis=0)

    factors = []
    for size, ref_row, own in levels:
        if size == 1:
            factors.append((q.astype(BF16), kk.astype(BF16), own))
            continue
        if size >= 8:
            ref = spread([cum[j * size + ref_row:j * size + ref_row + 1, :] for j in range(r // size)], size)
        else:
            ref = _tri_cumsum(jnp.where(b == (a // size) * size + ref_row, 1.0, 0.0).astype(BF16), cum)
        factors.append(((q * jnp.exp(cum - ref)).astype(BF16), (kk * jnp.exp(ref - cum)).astype(BF16), own))

    outs = []
    for h in range(HG_HEADS):
        sl = slice(h * HG_DK, (h + 1) * HG_DK)
        att = jnp.zeros((r, r), F32)
        for q_l, k_l, own in factors:
            att = jnp.where(own, _dot_nt(q_l[:, sl], k_l[:, sl]), att)
        intra = _dot(att.astype(BF16), vb[:, sl])
        if sfin_ref is not None:
            outs.append(intra)
            sfin_ref[0, slot, h] = _dot_tn(k_d[:, sl], vb[:, sl])
            continue
        st = st_ref[h]
        outs.append(_dot_nt(q_in[:, sl], st.astype(BF16)) + intra)
        st_ref[h] = st * e_tot[:, sl] + _dot_tn(vb[:, sl], k_d[:, sl])
    o_ref[slot] = jnp.concatenate(outs, axis=1)


def _hgrn_whole_kernel(q_ref, v_ref, f_ref, o_ref, sfin_ref, *, rows, bounded):
    for slot, reverse in enumerate((False, True)):
        _hgrn_tile(q_ref, v_ref, f_ref, o_ref, None, rows=rows, reverse=reverse, bounded=bounded, slot=slot,
                   sfin_ref=sfin_ref)


def _hgrn_scan_kernel(*refs, rows, has_init, bounded):
    if has_init:
        q_ref, v_ref, f_ref, s0_ref, o_ref, sfin_ref, st_ref = refs
    else:
        q_ref, v_ref, f_ref, o_ref, sfin_ref, st_ref = refs
    d = pl.program_id(1)
    t = pl.program_id(2)

    @pl.when(t == 0)
    def _():
        for h in range(HG_HEADS):
            st_ref[h] = s0_ref[0, 0, h].T if has_init else jnp.zeros((HG_DV, HG_DK), F32)

    for reverse in (False, True):
        @pl.when(d == int(reverse))
        def _():
            _hgrn_tile(q_ref, v_ref, f_ref, o_ref, st_ref, rows=rows, reverse=reverse, bounded=bounded)

    @pl.when(t == pl.num_programs(2) - 1)
    def _():
        for h in range(HG_HEADS):
            sfin_ref[0, 0, h] = st_ref[h].T


def _hgrn_scan(q, v, f2, batch, seq, first_row, rt, s0=None, bounded=True):
    n = q.shape[1]
    nt = seq // rt
    has_init = s0 is not None
    out_shape = [jax.ShapeDtypeStruct((2, batch * seq, n), F32),
                 jax.ShapeDtypeStruct((batch, 2, HG_HEADS, HG_DK, HG_DV), F32)]
    if nt == 1 and not has_init:
        rows_of = lambda b: (first_row // rt + b, 0)
        return pl.pallas_call(
            functools.partial(_hgrn_whole_kernel, rows=rt, bounded=bounded),
            grid=(batch,),
            in_specs=[pl.BlockSpec((rt, n), rows_of), pl.BlockSpec((rt, n), rows_of),
                      pl.BlockSpec((2, rt, n), lambda b: (0, first_row // rt + b, 0))],
            out_specs=[pl.BlockSpec((2, rt, n), lambda b: (0, b, 0)),
                       pl.BlockSpec((1, 2, HG_HEADS, HG_DK, HG_DV), lambda b: (b, 0, 0, 0, 0))],
            out_shape=out_shape,
            compiler_params=_params("arbitrary"),
            name="hgrn_scan_whole",
        )(q, v, f2)

    def local(b, d, i):
        return b * nt + jnp.where(d == 0, i, nt - 1 - i)

    def slab(b, d, i):
        return first_row // rt + local(b, d, i)

    row = pl.BlockSpec((rt, n), lambda b, d, i: (slab(b, d, i), 0))
    state = pl.BlockSpec((1, 1, HG_HEADS, HG_DK, HG_DV), lambda b, d, i: (b, d, 0, 0, 0))
    ins = [q, v, f2]
    in_specs = [row, row, pl.BlockSpec((1, rt, n), lambda b, d, i: (d, slab(b, d, i), 0))]
    if has_init:
        ins.append(s0)
        in_specs.append(state)
    return pl.pallas_call(
        functools.partial(_hgrn_scan_kernel, rows=rt, has_init=has_init, bounded=bounded),
        grid=(batch, 2, nt),
        in_specs=in_specs,
        out_specs=[pl.BlockSpec((1, rt, n), lambda b, d, i: (d, local(b, d, i), 0)), state],
        out_shape=[jax.ShapeDtypeStruct((2, batch * seq, n), F32),
                   jax.ShapeDtypeStruct((batch, 2, HG_HEADS, HG_DK, HG_DV), F32)],
        scratch_shapes=[pltpu.VMEM((HG_HEADS, HG_DV, HG_DK), F32)],
        compiler_params=_params("arbitrary", "arbitrary", "arbitrary"),
        name="hgrn_scan_latent" if has_init else "hgrn_scan_prompt",
    )(*ins)


def _post_kernel(*refs, hgrn, final, pair_in, pair_out, prompt_tiles):
    refs = list(refs)
    n_mix = 4 if hgrn else 2
    mix_refs, refs = refs[:n_mix], refs[n_mix:]
    x, is_prompt, refs = _take_rows(refs, pair_in, prompt_tiles)
    mod_ref, wo_ref, gn_ref, wg_ref, wu_ref, wd_ref = refs[:6]
    refs = list(refs[6:])
    fn_ref = refs.pop(0) if final else None
    if hgrn:
        odp_ref, odl_ref, g_ref, onorm_ref = mix_refs
        o2 = jnp.where(is_prompt, odp_ref[0] + odp_ref[1], odl_ref[0] + odl_ref[1])
        gate = g_ref[...]
        onorm = onorm_ref[...]
        parts = []
        for h in range(HG_HEADS):
            sl = slice(h * HG_DV, (h + 1) * HG_DV)
            parts.append(_rms(o2[:, sl], onorm) * gate[:, sl])
        o = jnp.concatenate(parts, axis=1).astype(BF16)
    else:
        o = jnp.where(is_prompt, mix_refs[0][...], mix_refs[1][...])
    m = mod_ref[0]
    x1 = x + m[2:3] * _dot(o, wo_ref[...])
    h2 = _modulate(x1, gn_ref[...], m[3:4], m[4:5]).astype(BF16)
    a = (_silu(_dot(h2, wg_ref[0])) * _dot(h2, wu_ref[0])).astype(BF16)
    x2 = x1 + m[5:6] * _dot(a, wd_ref[0])
    if final:
        x2 = _rms(x2, fn_ref[...])
    if pair_out:
        @pl.when(is_prompt)
        def _():
            refs[0][...] = x2

        @pl.when(jnp.logical_not(is_prompt))
        def _():
            refs[1][...] = x2
    else:
        refs[0][...] = x2


def _layer_of(stack, layer):
    shape = (1,) + stack.shape[1:]
    return pl.BlockSpec(shape, lambda *_: (layer,) + (0,) * (len(shape) - 1), pipeline_mode=pl.Buffered(1))


def _post(mix, x, mods, wo, gn, wg, wu, wd, layer, rows, final_norm=None, hgrn=False, pair_out=False):
    xs, x_specs, pair_in, t, d = _rows_inputs(x, rows)
    tm, pt = rows.tm, rows.prompt_tiles
    if hgrn:
        od_p, od_l, gate, onorm = mix
        head = [od_p, od_l, gate, onorm]
        head_specs = [pl.BlockSpec((2, tm, d), lambda i: (0, jnp.minimum(i, pt - 1), 0)),
                      pl.BlockSpec((2, tm, d), lambda i: (0, jnp.maximum(i - pt, 0), 0)),
                      rows.row_spec(d), _whole(onorm.shape)]
    else:
        head = list(mix)
        head_specs = [rows.prompt_spec(mix[0].shape[1]), rows.latent_spec(mix[1].shape[1])]
    ins = head + xs + [mods, wo, gn, wg, wu, wd]
    in_specs = (head_specs + x_specs + [rows.mod_spec(d), _whole(wo.shape), _whole(gn.shape)]
                + [_layer_of(a, layer) for a in (wg, wu, wd)])
    if final_norm is not None:
        ins.append(final_norm)
        in_specs.append(_whole(final_norm.shape))
    if pair_out:
        out_specs = rows.x_specs(d)
        out_shape = [jax.ShapeDtypeStruct((pt * tm, d), F32), jax.ShapeDtypeStruct((t - pt * tm, d), F32)]
    else:
        out_specs = rows.row_spec(d)
        out_shape = jax.ShapeDtypeStruct((t, d), F32)
    out = pl.pallas_call(
        functools.partial(_post_kernel, hgrn=hgrn, final=final_norm is not None, pair_in=pair_in, pair_out=pair_out,
                          prompt_tiles=pt),
        grid=(rows.n_tiles,),
        in_specs=in_specs,
        out_specs=out_specs,
        out_shape=out_shape,
        compiler_params=_params("arbitrary"),
        name="post",
    )(*ins)
    return tuple(out) if pair_out else out


def _pick_tile(n_prompt_rows, dec_seq, want):
    tm = want
    while n_prompt_rows % tm or dec_seq % tm:
        tm //= 2
    return tm


def kernel(x_prompt, x_sample, cache_mla_ckv, cache_mla_krope, state_hgrn, cache_swa_k, cache_swa_v, c, c_ctx, ada_w, ada_b, norm_mix, norm_ffn, ffn_w_gate, ffn_w_up, ffn_w_down, final_norm, mla_w_dq, mla_q_norm, mla_w_uq, mla_w_dkv, mla_kv_norm, mla_w_uk, mla_w_uv, mla_w_o, hg_w_q, hg_w_f, hg_w_i, hg_w_g, hg_o_norm, hg_w_o, hg_lb_logits, swa_w_q, swa_w_k, swa_w_v, swa_w_o, swa_sink):
    batch, seq, d = x_prompt.shape
    dec_batch, dec_seq, _ = x_sample.shape
    past = cache_mla_ckv.shape[2]
    depth = ada_w.shape[0]
    n_prompt = batch * seq
    n_rows = n_prompt + dec_batch * dec_seq
    assert dec_batch + 1 <= COND_ROWS and seq % SWA_WINDOW == 0 and dec_seq % (2 * SWA_WINDOW) == 0
    assert n_prompt % dec_seq == 0

    pre_rows = _Rows(n_prompt, dec_seq, n_rows, _pick_tile(n_prompt, dec_seq, 512))
    post_rows = _Rows(n_prompt, dec_seq, n_rows, _pick_tile(n_prompt, dec_seq, 512))
    hg_pre_rows = _Rows(n_prompt, dec_seq, n_rows, _pick_tile(n_prompt, dec_seq, 256))
    hg_post_rows = _Rows(n_prompt, dec_seq, n_rows, _pick_tile(n_prompt, dec_seq, 256))
    tq = _pick_tile(n_prompt, dec_seq, 256)
    cos64, sin64 = _rope_tables(dec_seq, pre_rows.tm)
    n_tab = cos64.shape[0]
    cos_mla = jnp.asarray(np.concatenate([cos64, np.ones((n_tab, ROPE_PERIOD), np.float32)], axis=1))
    sin_mla = jnp.asarray(np.concatenate([sin64, np.zeros((n_tab, ROPE_PERIOD), np.float32)], axis=1))
    cos_swa, sin_swa = jnp.asarray(np.tile(cos64, (1, 2))), jnp.asarray(np.tile(sin64, (1, 2)))

    cond = jnp.concatenate([c_ctx[None, :], c, jnp.zeros((COND_ROWS - 1 - dec_batch, d), F32)], axis=0)
    mods = _adaln(cond, ada_w, ada_b).reshape(depth, COND_ROWS, 6, d)

    x = (x_prompt.reshape(n_prompt, d), x_sample.reshape(dec_batch * dec_seq, d))
    ffn_wg, ffn_wu, ffn_wd = ffn_w_gate.astype(BF16), ffn_w_up.astype(BF16), ffn_w_down.astype(BF16)
    row1 = lambda a: a.reshape(1, -1)
    new_ckv, new_krope, new_hg, new_k, new_v = [], [], [], [], []
    for i in range(depth):
        kind, j = i % N_MIXERS, i // N_MIXERS
        gn = row1(norm_mix[i])
        if kind == 0:
            uq = mla_w_uq[j].reshape(-1, MLA_HEADS, MLA_NOPE_DIM + MLA_ROPE_DIM)
            uq_rope = jnp.pad(uq[:, :, MLA_NOPE_DIM:], ((0, 0), (0, 0), (0, LANES - MLA_ROPE_DIM)))
            uq = jnp.concatenate([uq[:, :, :MLA_NOPE_DIM].reshape(uq.shape[0], -1),
                                  uq_rope.reshape(uq.shape[0], -1)], axis=1)
            w = {
                "dq": mla_w_dq[j].astype(BF16), "q_norm": row1(mla_q_norm[j]), "uq": uq.astype(BF16),
                "dkv": jnp.pad(mla_w_dkv[j], ((0, 0), (0, LANES - MLA_ROPE_DIM))).astype(BF16),
                "kv_norm": row1(mla_kv_norm[j]),
                "uk": mla_w_uk[j].astype(BF16), "uv_t": mla_w_uv[j].T.astype(BF16),
            }
            q, ckv, kr_raw, kcat, vt = _mla_pre(x, mods[i], gn, cos_mla, sin_mla, w, pre_rows)
            kr_ctx = jnp.pad(cache_mla_krope[:, j].reshape(dec_batch * past, -1), ((0, 0), (0, LANES - MLA_ROPE_DIM)))
            kcat_ctx, vt_ctx = _mla_expand(cache_mla_ckv[:, j].reshape(dec_batch * past, -1), kr_ctx, w["uk"], w["uv_t"])
            mix = (_mla_attn_prompt(q, kcat, vt, batch, seq),
                   _mla_attn_latent(q, kcat, vt, kcat_ctx, vt_ctx, n_prompt, dec_batch, dec_seq, past, tq))
            wo = mla_w_o[j].astype(BF16)
            new_ckv.append(ckv.reshape(batch, seq, -1))
            new_krope.append(kr_raw.reshape(batch, seq, -1))
        elif kind == 1:
            w5 = jnp.concatenate([hg_w_q[j], hg_w_i[j], hg_w_g[j], hg_w_f[j, 0], hg_w_f[j, 1]], axis=1).astype(BF16)
            q, v, gate, f2, q_max = _hgrn_pre(x, mods[i], gn, w5, hg_lb_logits, i, hg_pre_rows)
            rt = _pick_tile(seq, dec_seq, 256)
            sm = jax.nn.softmax(hg_lb_logits.astype(F32), axis=1)
            lb_min = jnp.min(jnp.cumsum(sm, axis=1)[:, i] - sm[:, 0])
            worst = (HG_DIAG_BLOCK // 2) * -jnp.log(lb_min) + jnp.log(jnp.maximum(jnp.max(q_max), 1.0))
            fits = worst < HG_MAX_EXPONENT

            def scans(bounded):
                od_p, s_p = _hgrn_scan(q, v, f2, batch, seq, 0, rt, bounded=bounded)
                od_l, _ = _hgrn_scan(q, v, f2, dec_batch, dec_seq, n_prompt, rt, s0=state_hgrn[:, j], bounded=bounded)
                return od_p, od_l, s_p

            od_p, od_l, s_prompt = lax.cond(fits, lambda: scans(True), lambda: scans(False))
            mix = (od_p, od_l, gate, row1(hg_o_norm[j]))
            wo = hg_w_o[j].astype(BF16)
            new_hg.append(s_prompt)
        else:
            wqkv = jnp.concatenate([swa_w_q[j], swa_w_k[j], swa_w_v[j]], axis=1).astype(BF16)
            q, kx, vt, k_raw, v_raw = _swa_pre(x, mods[i], gn, cos_swa, sin_swa, wqkv, pre_rows)
            sink = swa_sink[j]
            k_ctx = cache_swa_k[:, j].reshape(dec_batch * past, SWA_KV_HEADS, 1, SWA_HEAD_DIM).transpose(1, 0, 2, 3)
            kx_ctx = jnp.broadcast_to(k_ctx, (SWA_KV_HEADS, dec_batch * past, SWA_GROUP, SWA_HEAD_DIM))
            kx_ctx = kx_ctx.reshape(SWA_KV_HEADS, dec_batch * past, SWA_GW).astype(BF16)
            vt_ctx = cache_swa_v[:, j].reshape(dec_batch * past, SWA_KV_HEADS, SWA_HEAD_DIM).transpose(1, 2, 0).astype(BF16)
            mix = (_swa_attn_prompt(q, kx, vt, sink, batch, seq),
                   _swa_attn_latent(q, kx, vt, kx_ctx, vt_ctx, sink, n_prompt, dec_batch, dec_seq, past, tq))
            wo = swa_w_o[j].astype(BF16)
            new_k.append(k_raw.reshape(batch, seq, SWA_KV_HEADS, SWA_HEAD_DIM))
            new_v.append(v_raw.reshape(batch, seq, SWA_KV_HEADS, SWA_HEAD_DIM))
        x = _post(mix, x, mods[i], wo, row1(norm_ffn[i]), ffn_wg, ffn_wu, ffn_wd, i,
                  hg_post_rows if kind == 1 else post_rows,
                  final_norm=row1(final_norm) if i == depth - 1 else None, hgrn=kind == 1,
                  pair_out=i == depth - 1)
    y_prompt = x[0].reshape(batch, seq, d)
    y_sample = x[1].reshape(dec_batch, dec_seq, d)
    return (y_prompt, y_sample, jnp.stack(new_ckv, axis=1), jnp.stack(new_krope, axis=1),
            jnp.stack(new_hg, axis=1), jnp.stack(new_k, axis=1), jnp.stack(new_v, axis=1))
```

```python
import functools

import numpy as np
import jax
import jax.numpy as jnp
from jax import lax
from jax.experimental import pallas as pl
from jax.experimental.pallas import tpu as pltpu

F32 = jnp.float32
BF16 = jnp.bfloat16

GRID_W = 64
N_MIXERS = 3

MLA_HEADS = 8
MLA_KV_LORA = 256
MLA_NOPE_DIM = 128
MLA_ROPE_DIM = 64
MLA_V_DIM = 128
MLA_QK_PAD = 256
MLA_SCALE = (MLA_NOPE_DIM + MLA_ROPE_DIM) ** -0.5

HG_HEADS = 8
HG_DK = 128
HG_DV = 128
HG_DIAG_BLOCK = 32
HG_MAX_EXPONENT = 80.0

SWA_HEADS = 16
SWA_KV_HEADS = 4
SWA_GROUP = SWA_HEADS // SWA_KV_HEADS
SWA_HEAD_DIM = 64
SWA_GW = SWA_GROUP * SWA_HEAD_DIM
SWA_WINDOW = 128
SWA_SCALE = SWA_HEAD_DIM ** -0.5

ROPE_BASE = 10000.0
ROPE_PERIOD = 64
ROPE_QUARTER = 16
NORM_EPS = 1e-6
NEG_INF = -1e30
LOG2_E = 1.4426950408889634

LANES = 128
COND_ROWS = 8
VMEM_LIMIT = 56 * 1024 * 1024


def _sigmoid(x):
    return jax.nn.sigmoid(x)


def _silu(x):
    return x * jax.nn.sigmoid(x)


def _rms(x, g):
    return x * lax.rsqrt(jnp.mean(x * x, axis=-1, keepdims=True) + NORM_EPS) * g


def _modulate(x, g, shift, scale):
    return _rms(x, g) * (1.0 + scale) + shift


def _dot(a, b):
    return jnp.dot(a, b, preferred_element_type=F32)


def _dot_nt(a, b):
    return lax.dot_general(a, b, (((1,), (1,)), ((), ())), preferred_element_type=F32)


def _dot_tn(a, b):
    return lax.dot_general(a, b, (((0,), (0,)), ((), ())), preferred_element_type=F32)


def _swap_pairs(x):
    n = x.shape[1]
    lane = lax.broadcasted_iota(jnp.int32, x.shape, 1)
    ahead = pltpu.roll(x, n - ROPE_QUARTER, 1)
    behind = pltpu.roll(x, ROPE_QUARTER, 1)
    return jnp.where((lane & (2 * ROPE_QUARTER - 1)) < ROPE_QUARTER, ahead, behind)


def _rope(x, cos, sin):
    reps = x.shape[1] // cos.shape[1]
    if reps > 1:
        cos = jnp.concatenate([cos] * reps, axis=1)
        sin = jnp.concatenate([sin] * reps, axis=1)
    return x * cos + _swap_pairs(x) * sin


def _whole(shape):
    zeros = (0,) * len(shape)
    return pl.BlockSpec(shape, lambda *_: zeros, pipeline_mode=pl.Buffered(1))


def _params(*sem):
    return pltpu.CompilerParams(dimension_semantics=sem, vmem_limit_bytes=VMEM_LIMIT)


class _Rows:
    def __init__(self, n_prompt_rows, dec_seq, n_rows, tm):
        assert n_prompt_rows % tm == 0 and dec_seq % tm == 0
        self.tm = tm
        self.n_tiles = n_rows // tm
        self.prompt_tiles = n_prompt_rows // tm
        self.seq_tiles = dec_seq // tm

    def cond(self, i):
        return jnp.where(i < self.prompt_tiles, 0, 1 + jnp.maximum(i - self.prompt_tiles, 0) // self.seq_tiles)

    def rope_block(self, i):
        return jnp.where(i < self.prompt_tiles, 0, 1 + jnp.maximum(i - self.prompt_tiles, 0) % self.seq_tiles)

    def row_spec(self, width):
        return pl.BlockSpec((self.tm, width), lambda i: (i, 0))

    def col_spec(self, height):
        return pl.BlockSpec((height, self.tm), lambda i: (0, i))

    def mod_spec(self, d):
        return pl.BlockSpec((1, 6, d), lambda i: (self.cond(i), 0, 0))

    def rope_spec(self, width):
        return pl.BlockSpec((self.tm, width), lambda i: (self.rope_block(i), 0))

    def x_specs(self, width):
        return [self.prompt_spec(width), self.latent_spec(width)]

    def is_prompt(self):
        return pl.program_id(0) < self.prompt_tiles

    def prompt_spec(self, width):
        return pl.BlockSpec((self.tm, width), lambda i: (jnp.minimum(i, self.prompt_tiles - 1), 0))

    def latent_spec(self, width):
        return pl.BlockSpec((self.tm, width), lambda i: (jnp.maximum(i - self.prompt_tiles, 0), 0))


def _rope_tables(dec_seq, tm):
    pos = np.arange(dec_seq)
    row = (pos // GRID_W).astype(np.float32)
    col = (pos % GRID_W).astype(np.float32)
    inv_freq = (ROPE_BASE ** (-np.arange(ROPE_QUARTER, dtype=np.float32) / ROPE_QUARTER)).astype(np.float32)
    ang_r = row[:, None] * inv_freq[None, :]
    ang_c = col[:, None] * inv_freq[None, :]
    cos = np.concatenate([np.cos(ang_r), np.cos(ang_r), np.cos(ang_c), np.cos(ang_c)], axis=1)
    sin = np.concatenate([-np.sin(ang_r), np.sin(ang_r), -np.sin(ang_c), np.sin(ang_c)], axis=1)
    cos = np.concatenate([np.ones((tm, ROPE_PERIOD), np.float32), cos], axis=0)
    sin = np.concatenate([np.zeros((tm, ROPE_PERIOD), np.float32), sin], axis=0)
    return cos.astype(np.float32), sin.astype(np.float32)


def _adaln_kernel(c_ref, w_ref, b_ref, o_ref):
    o_ref[0] = _dot(_silu(c_ref[...]), w_ref[0]) + b_ref[0]


def _adaln(cond, ada_w, ada_b, tn=1536):
    depth, d, n = ada_w.shape
    return pl.pallas_call(
        _adaln_kernel,
        grid=(depth, n // tn),
        in_specs=[pl.BlockSpec((COND_ROWS, d), lambda l, j: (0, 0)),
                  pl.BlockSpec((1, d, tn), lambda l, j: (l, 0, j)),
                  pl.BlockSpec((1, 1, tn), lambda l, j: (l, 0, j))],
        out_specs=pl.BlockSpec((1, COND_ROWS, tn), lambda l, j: (l, 0, j)),
        out_shape=jax.ShapeDtypeStruct((depth, COND_ROWS, n), F32),
        compiler_params=_params("arbitrary", "arbitrary"),
        name="adaln",
    )(cond, ada_w, ada_b.reshape(depth, 1, n))


def _mla_store_heads(q_ref, kcat_ref, vt_ref, qn, qr_pad, kn, kr_pad, vt):
    for h in range(MLA_HEADS):
        nope = slice(h * MLA_NOPE_DIM, (h + 1) * MLA_NOPE_DIM)
        if q_ref is not None:
            q_ref[h] = jnp.concatenate([qn[:, nope], qr_pad[:, h * LANES:(h + 1) * LANES]], axis=1)
        kcat_ref[h] = jnp.concatenate([kn[:, nope], kr_pad], axis=1)
        vt_ref[h] = vt[h * MLA_V_DIM:(h + 1) * MLA_V_DIM, :]


def _read_rows(is_prompt, p_ref, l_ref):
    return jnp.where(is_prompt, p_ref[...], l_ref[...])


def _take_rows(refs, pair, prompt_tiles):
    is_prompt = pl.program_id(0) < prompt_tiles
    if pair:
        return _read_rows(is_prompt, refs[0], refs[1]), is_prompt, refs[2:]
    return refs[0][...], is_prompt, refs[1:]


def _rows_inputs(x, rows):
    if isinstance(x, tuple):
        d = x[0].shape[1]
        return list(x), rows.x_specs(d), True, x[0].shape[0] + x[1].shape[0], d
    return [x], [rows.row_spec(x.shape[1])], False, x.shape[0], x.shape[1]


class _Cast:
    def __init__(self, stacks, layer, n_steps):
        self.stacks, self.layer = stacks, layer
        self.chunks = n_steps // len(stacks)
        self.ok = (n_steps % len(stacks) == 0
                   and all(a.shape[1] % self.chunks == 0 and (a.shape[1] // self.chunks) % 16 == 0 for a in stacks))

    def _spec(self, k, a, with_layer):
        rows = a.shape[1] // self.chunks
        chunk = lambda i: jnp.clip(i - k * self.chunks, 0, self.chunks - 1)
        if with_layer:
            return pl.BlockSpec((1, rows, a.shape[2]), lambda i: (self.layer, chunk(i), 0))
        return pl.BlockSpec((rows, a.shape[2]), lambda i: (chunk(i), 0))

    def in_specs(self):
        return [self._spec(k, a, True) for k, a in enumerate(self.stacks)]

    def out_specs(self):
        return [self._spec(k, a, False) for k, a in enumerate(self.stacks)]

    def out_shape(self):
        return [jax.ShapeDtypeStruct(a.shape[1:], BF16) for a in self.stacks]

    def run(self, src_refs, dst_refs):
        turn = pl.program_id(0) // self.chunks
        for k, (src, dst) in enumerate(zip(src_refs, dst_refs)):
            @pl.when(turn == k)
            def _():
                dst[...] = src[0].astype(BF16)


def _split_cast(refs, n_in, cast):
    if cast is None:
        return refs, (), ()
    n = len(cast.stacks)
    return refs[:n_in] + refs[n_in + n:len(refs) - n], refs[n_in:n_in + n], refs[len(refs) - n:]


def _mla_pre_kernel(*refs, pair, prompt_tiles, cast):
    x, is_prompt, refs = _take_rows(refs, pair, prompt_tiles)
    refs, cast_src, cast_dst = _split_cast(refs, 11, cast)
    (mod_ref, gn_ref, cos_ref, sin_ref, wdq_ref, qnorm_ref, wuq_ref, wdkv_ref, kvnorm_ref, wuk_ref, wuvt_ref,
     q_ref, ckv_ref, krraw_ref, kcat_ref, vt_ref) = refs
    if cast is not None:
        cast.run(cast_src, cast_dst)
    m = mod_ref[0]
    h = _modulate(x, gn_ref[...], m[0:1], m[1:2]).astype(BF16)
    q_lat = _rms(_dot(h, wdq_ref[...]), qnorm_ref[...]).astype(BF16)
    cos, sin = cos_ref[...], sin_ref[...]
    nn = MLA_HEADS * MLA_NOPE_DIM
    qn = _dot(q_lat, wuq_ref[:, :nn]).astype(BF16)
    qr_pad = _rope(_dot(q_lat, wuq_ref[:, nn:]), cos, sin).astype(BF16)
    kv = _dot(h, wdkv_ref[...])
    ckv = _rms(kv[:, :MLA_KV_LORA], kvnorm_ref[...])
    kr = kv[:, MLA_KV_LORA:]

    @pl.when(is_prompt)
    def _():
        ckv_ref[...] = ckv
        krraw_ref[...] = kr[:, :MLA_ROPE_DIM]

    kr_pad = _rope(kr, cos, sin).astype(BF16)
    cb = ckv.astype(BF16)
    _mla_store_heads(q_ref, kcat_ref, vt_ref, qn, qr_pad, _dot(cb, wuk_ref[...]).astype(BF16), kr_pad,
                     _dot_nt(wuvt_ref[...], cb).astype(BF16))


def _pre_call(kernel, cast, rows, ins, in_specs, out_specs, out_shape, name):
    if cast is not None and cast.ok:
        ins, in_specs = ins + list(cast.stacks), in_specs + cast.in_specs()
        out_specs, out_shape = out_specs + cast.out_specs(), out_shape + cast.out_shape()
    else:
        cast = None
    outs = pl.pallas_call(
        functools.partial(kernel, cast=cast),
        grid=(rows.n_tiles,),
        in_specs=in_specs,
        out_specs=out_specs,
        out_shape=out_shape,
        compiler_params=_params("arbitrary"),
        name=name,
    )(*ins)
    if cast is None:
        return outs, None
    n = len(cast.stacks)
    return outs[:-n], outs[-n:]


def _mla_pre(x, mods, gn, cos, sin, w, rows, cast):
    xs, x_specs, pair, t, d = _rows_inputs(x, rows)
    tm = rows.tm
    n_prompt = rows.prompt_tiles * tm
    weights = [w["dq"], w["q_norm"], w["uq"], w["dkv"], w["kv_norm"], w["uk"], w["uv_t"]]
    ins = xs + [mods, gn, cos, sin] + weights
    in_specs = x_specs + [rows.mod_spec(d), _whole(gn.shape),
                          rows.rope_spec(cos.shape[1]), rows.rope_spec(sin.shape[1])]
    in_specs += [_whole(a.shape) for a in weights]
    heads_rows = pl.BlockSpec((MLA_HEADS, tm, MLA_QK_PAD), lambda i: (0, i, 0))
    return _pre_call(
        functools.partial(_mla_pre_kernel, pair=pair, prompt_tiles=rows.prompt_tiles), cast, rows, ins, in_specs,
        [heads_rows, rows.prompt_spec(MLA_KV_LORA), rows.prompt_spec(MLA_ROPE_DIM), heads_rows,
         pl.BlockSpec((MLA_HEADS, MLA_V_DIM, tm), lambda i: (0, 0, i))],
        [jax.ShapeDtypeStruct((MLA_HEADS, t, MLA_QK_PAD), BF16),
         jax.ShapeDtypeStruct((n_prompt, MLA_KV_LORA), F32),
         jax.ShapeDtypeStruct((n_prompt, MLA_ROPE_DIM), F32),
         jax.ShapeDtypeStruct((MLA_HEADS, t, MLA_QK_PAD), BF16),
         jax.ShapeDtypeStruct((MLA_HEADS, MLA_V_DIM, t), BF16)],
        "mla_pre")


def _mla_expand_kernel(c_ref, kr_ref, wuk_ref, wuvt_ref, kcat_ref, vt_ref):
    cb = c_ref[...].astype(BF16)
    _mla_store_heads(None, kcat_ref, vt_ref, None, None, _dot(cb, wuk_ref[...]).astype(BF16),
                     kr_ref[...].astype(BF16), _dot_nt(wuvt_ref[...], cb).astype(BF16))


def _mla_expand(ckv, kr_pad, wuk, wuvt):
    n = ckv.shape[0]
    return pl.pallas_call(
        _mla_expand_kernel,
        out_shape=[jax.ShapeDtypeStruct((MLA_HEADS, n, MLA_QK_PAD), BF16),
                   jax.ShapeDtypeStruct((MLA_HEADS, MLA_V_DIM, n), BF16)],
        compiler_params=pltpu.CompilerParams(vmem_limit_bytes=VMEM_LIMIT),
        name="mla_expand",
    )(ckv, kr_pad, wuk, wuvt)


def _mla_attn_kernel(*refs, n_src):
    q_ref = refs[0]
    srcs = [(refs[1 + 2 * i], refs[2 + 2 * i]) for i in range(n_src)]
    o_ref, o_buf = refs[1 + 2 * n_src:3 + 2 * n_src]
    s_bufs = refs[3 + 2 * n_src:]
    n_keys = [k_ref.shape[1] for k_ref, _ in srcs]
    starts = [sum(n_keys[:i]) for i in range(n_src)]

    def put_scores(h, s_buf):
        q = q_ref[h]
        for (k_ref, _), first, n in zip(srcs, starts, n_keys):
            s_buf[first:first + n, :] = _dot_nt(k_ref[h], q)

    def finish(h, s_buf):
        s = s_buf[...]
        mx = jnp.max(s, axis=0, keepdims=True)
        p = jnp.exp2((s - mx) * (MLA_SCALE * LOG2_E))
        den = jnp.sum(p, axis=0, keepdims=True)
        p = p.astype(BF16)
        acc = None
        for (_, vt_ref), first, n in zip(srcs, starts, n_keys):
            a = _dot(vt_ref[h], p[first:first + n, :])
            acc = a if acc is None else acc + a
        o_buf[h] = (acc / den).T.astype(BF16)

    _pipelined_heads(MLA_HEADS, put_scores, finish, s_bufs)
    for h in range(MLA_HEADS):
        o_ref[:, h * MLA_V_DIM:(h + 1) * MLA_V_DIM] = o_buf[h]


def _mla_attn_scratch(n_keys, tq):
    return [pltpu.VMEM((MLA_HEADS, tq, MLA_V_DIM), BF16)] + [pltpu.VMEM((n_keys, tq), F32)] * 4


def _pipelined_heads(n_heads, put_scores, finish, bufs):
    a, b, c, d = bufs
    assert n_heads % 4 == 0
    put_scores(0, a)
    put_scores(1, b)

    def quad(j, carry):
        h = 4 * j
        put_scores(h + 2, c)
        put_scores(h + 3, d)
        finish(h, a)
        finish(h + 1, b)
        put_scores(h + 4, a)
        put_scores(h + 5, b)
        finish(h + 2, c)
        finish(h + 3, d)
        return carry

    lax.fori_loop(0, n_heads // 4 - 1, quad, 0)
    h = n_heads - 4
    put_scores(h + 2, c)
    put_scores(h + 3, d)
    finish(h, a)
    finish(h + 1, b)
    finish(h + 2, c)
    finish(h + 3, d)


def _staged_attention(scores, values_t, scale, sinks=None):
    c = scale * LOG2_E
    mx = [jnp.max(s, axis=0, keepdims=True) for s in scores]
    if sinks is not None:
        mx = [jnp.maximum(m, z) for m, z in zip(mx, sinks)]
    p = [jnp.exp2((s - m) * c) for s, m in zip(scores, mx)]
    den = [jnp.sum(x, axis=0, keepdims=True) for x in p]
    if sinks is not None:
        den = [d + jnp.exp2((z - m) * c) for d, z, m in zip(den, sinks, mx)]
    acc = [_dot(v, x.astype(BF16)) for v, x in zip(values_t, p)]
    return [a / d for a, d in zip(acc, den)]


def _mla_attn_prompt_kernel(q_ref, k_ref, vt_ref, o_ref, *, seq):
    per_step = q_ref.shape[1] // seq
    spans = [slice(s * seq, (s + 1) * seq) for s in range(per_step)]
    scores = [_dot_nt(k_ref[h, sp, :], q_ref[h, sp, :]) for sp in spans for h in range(MLA_HEADS)]
    values = [vt_ref[h, :, sp] for sp in spans for h in range(MLA_HEADS)]
    outs = _staged_attention(scores, values, MLA_SCALE)
    for s, sp in enumerate(spans):
        for h in range(MLA_HEADS):
            o_ref[sp, h * MLA_V_DIM:(h + 1) * MLA_V_DIM] = outs[s * MLA_HEADS + h].T.astype(BF16)


def _mla_attn_prompt(q, kcat, vt, batch, seq):
    hv = MLA_HEADS * MLA_V_DIM
    per_step = next(n for n in (4, 2, 1) if batch % n == 0)
    seq_rows = seq * per_step
    rows = pl.BlockSpec((MLA_HEADS, seq_rows, MLA_QK_PAD), lambda b: (0, b, 0))
    return pl.pallas_call(
        functools.partial(_mla_attn_prompt_kernel, seq=seq),
        grid=(batch // per_step,),
        in_specs=[rows, rows, pl.BlockSpec((MLA_HEADS, MLA_V_DIM, seq_rows), lambda b: (0, 0, b))],
        out_specs=pl.BlockSpec((seq_rows, hv), lambda b: (b, 0)),
        out_shape=jax.ShapeDtypeStruct((batch * seq, hv), BF16),
        compiler_params=_params("arbitrary"),
        name="mla_attn_prompt",
    )(q, kcat, vt)


def _mla_attn_latent(q, kcat, vt, kcat_ctx, vt_ctx, n_prompt, dec_batch, dec_seq, past, tq):
    hv = MLA_HEADS * MLA_V_DIM
    nq = dec_seq // tq
    lat_blk = n_prompt // dec_seq
    return pl.pallas_call(
        functools.partial(_mla_attn_kernel, n_src=2),
        grid=(dec_batch, nq),
        in_specs=[pl.BlockSpec((MLA_HEADS, tq, MLA_QK_PAD), lambda b, i: (0, n_prompt // tq + b * nq + i, 0)),
                  pl.BlockSpec((MLA_HEADS, past, MLA_QK_PAD), lambda b, i: (0, b, 0)),
                  pl.BlockSpec((MLA_HEADS, MLA_V_DIM, past), lambda b, i: (0, 0, b)),
                  pl.BlockSpec((MLA_HEADS, dec_seq, MLA_QK_PAD), lambda b, i: (0, lat_blk + b, 0)),
                  pl.BlockSpec((MLA_HEADS, MLA_V_DIM, dec_seq), lambda b, i: (0, 0, lat_blk + b))],
        out_specs=pl.BlockSpec((tq, hv), lambda b, i: (b * nq + i, 0)),
        out_shape=jax.ShapeDtypeStruct((dec_batch * dec_seq, hv), BF16),
        scratch_shapes=_mla_attn_scratch(past + dec_seq, tq),
        compiler_params=_params("arbitrary", "arbitrary"),
        name="mla_attn_latent",
    )(q, kcat_ctx, vt_ctx, kcat, vt)


def _tile_heads(x):
    n = x.shape[1]
    block = lax.broadcasted_iota(jnp.int32, x.shape, 1) // SWA_HEAD_DIM
    rolled = [x] + [pltpu.roll(x, s * SWA_HEAD_DIM, 1) for s in range(1, SWA_KV_HEADS)]
    out = []
    for kvh in range(SWA_KV_HEADS):
        blk = rolled[(0 - kvh) % SWA_KV_HEADS]
        for g in range(1, n // SWA_HEAD_DIM):
            blk = jnp.where(block == g, rolled[(g - kvh) % SWA_KV_HEADS], blk)
        out.append(blk)
    return jnp.concatenate(out, axis=1)


def _swa_pre_kernel(*refs, pair, prompt_tiles, cast):
    x, is_prompt, refs = _take_rows(refs, pair, prompt_tiles)
    refs, cast_src, cast_dst = _split_cast(refs, 5, cast)
    mod_ref, gn_ref, cos_ref, sin_ref, wqkv_ref, q_ref, kx_ref, vt_ref, kraw_ref, vraw_ref = refs
    if cast is not None:
        cast.run(cast_src, cast_dst)
    m = mod_ref[0]
    h = _modulate(x, gn_ref[...], m[0:1], m[1:2]).astype(BF16)
    qkv = _dot(h, wqkv_ref[...])
    nq, nk = SWA_HEADS * SWA_HEAD_DIM, SWA_KV_HEADS * SWA_HEAD_DIM
    cos, sin = cos_ref[...], sin_ref[...]
    q = _rope(qkv[:, :nq], cos, sin).astype(BF16)
    k = qkv[:, nq:nq + nk]
    v = qkv[:, nq + nk:]

    @pl.when(is_prompt)
    def _():
        kraw_ref[...] = k.reshape(kraw_ref.shape)
        vraw_ref[...] = v.reshape(vraw_ref.shape)

    kx = _tile_heads(_rope(k, cos, sin)).astype(BF16)
    vt = v.T.astype(BF16)
    for kvh in range(SWA_KV_HEADS):
        q_ref[kvh] = q[:, kvh * SWA_GW:(kvh + 1) * SWA_GW]
        kx_ref[kvh] = kx[:, kvh * SWA_GW:(kvh + 1) * SWA_GW]
        vt_ref[kvh] = vt[kvh * SWA_HEAD_DIM:(kvh + 1) * SWA_HEAD_DIM, :]


def _swa_pre(x, mods, gn, cos, sin, wqkv, rows, cast):
    xs, x_specs, pair, t, d = _rows_inputs(x, rows)
    tm = rows.tm
    n_prompt = rows.prompt_tiles * tm
    nk = SWA_KV_HEADS * SWA_HEAD_DIM
    heads_rows = pl.BlockSpec((SWA_KV_HEADS, tm, SWA_GW), lambda i: (0, i, 0))
    raw_spec = pl.BlockSpec((tm, SWA_KV_HEADS, SWA_HEAD_DIM), lambda i: (jnp.minimum(i, rows.prompt_tiles - 1), 0, 0))
    return _pre_call(
        functools.partial(_swa_pre_kernel, pair=pair, prompt_tiles=rows.prompt_tiles), cast, rows,
        xs + [mods, gn, cos, sin, wqkv],
        x_specs + [rows.mod_spec(d), _whole(gn.shape),
                   rows.rope_spec(cos.shape[1]), rows.rope_spec(sin.shape[1]), _whole(wqkv.shape)],
        [heads_rows, heads_rows, pl.BlockSpec((SWA_KV_HEADS, SWA_HEAD_DIM, tm), lambda i: (0, 0, i)),
         raw_spec, raw_spec],
        [jax.ShapeDtypeStruct((SWA_KV_HEADS, t, SWA_GW), BF16),
         jax.ShapeDtypeStruct((SWA_KV_HEADS, t, SWA_GW), BF16),
         jax.ShapeDtypeStruct((SWA_KV_HEADS, SWA_HEAD_DIM, t), BF16),
         jax.ShapeDtypeStruct((n_prompt, SWA_KV_HEADS, SWA_HEAD_DIM), F32),
         jax.ShapeDtypeStruct((n_prompt, SWA_KV_HEADS, SWA_HEAD_DIM), F32)],
        "swa_pre")


def _swa_heads(q_ref, sink_ref, o_ref, srcs, bias_ref, ot_buf, s_bufs):
    tq = q_ref.shape[1]
    n_keys = [kx_ref.shape[1] for kx_ref, _ in srcs]
    starts = [sum(n_keys[:i]) for i in range(len(srcs))]
    group = lax.broadcasted_iota(jnp.int32, (tq, SWA_GW), 1) // SWA_HEAD_DIM

    def put_scores(hq, s_buf):
        kvh, g = hq // SWA_GROUP, hq % SWA_GROUP
        q = jnp.where(group == g, q_ref[kvh].astype(F32), 0.0).astype(BF16)
        for (kx_ref, _), first, n in zip(srcs, starts, n_keys):
            s_buf[first:first + n, :] = _dot_nt(kx_ref[kvh], q)

    def finish(hq, s_buf):
        kvh, g = hq // SWA_GROUP, hq % SWA_GROUP
        s = s_buf[...]
        if bias_ref is not None:
            s = s + bias_ref[...]
        sink = jnp.full((1, 1), sink_ref[hq] * (1.0 / SWA_SCALE), F32)
        mx = jnp.maximum(jnp.max(s, axis=0, keepdims=True), sink)
        p = jnp.exp2((s - mx) * (SWA_SCALE * LOG2_E))
        den = jnp.sum(p, axis=0, keepdims=True) + jnp.exp2((sink - mx) * (SWA_SCALE * LOG2_E))
        p = p.astype(BF16)
        acc = None
        for (_, vt_ref), first, n in zip(srcs, starts, n_keys):
            a = _dot(vt_ref[kvh], p[first:first + n, :])
            acc = a if acc is None else acc + a
        ot_buf[kvh, pl.ds(pl.multiple_of(g * SWA_HEAD_DIM, SWA_HEAD_DIM), SWA_HEAD_DIM), :] = acc / den

    _pipelined_heads(SWA_HEADS, put_scores, finish, s_bufs)
    for kvh in range(SWA_KV_HEADS):
        o_ref[:, kvh * SWA_GW:(kvh + 1) * SWA_GW] = ot_buf[kvh].T.astype(BF16)


def _swa_attn_scratch(n_keys, tq):
    return [pltpu.VMEM((SWA_KV_HEADS, SWA_GW, tq), F32)] + [pltpu.VMEM((n_keys, tq), F32)] * 5


def _swa_attn_prompt_kernel(q_ref, kx_ref, vt_ref, sink_ref, o_ref, *, seq):
    per_step = q_ref.shape[1] // seq
    spans = [slice(s * seq, (s + 1) * seq) for s in range(per_step)]
    group = lax.broadcasted_iota(jnp.int32, (seq, SWA_GW), 1) // SWA_HEAD_DIM
    scores, values_t, sinks = [], [], []
    for sp in spans:
        for kvh in range(SWA_KV_HEADS):
            q_all = q_ref[kvh, sp, :].astype(F32)
            for g in range(SWA_GROUP):
                q = jnp.where(group == g, q_all, 0.0).astype(BF16)
                scores.append(_dot_nt(kx_ref[kvh, sp, :], q))
                values_t.append(vt_ref[kvh, :, sp])
                sinks.append(jnp.full((1, 1), sink_ref[kvh * SWA_GROUP + g] * (1.0 / SWA_SCALE), F32))
    outs = _staged_attention(scores, values_t, SWA_SCALE, sinks)
    for s, sp in enumerate(spans):
        for kvh in range(SWA_KV_HEADS):
            first = s * SWA_HEADS + kvh * SWA_GROUP
            ot = jnp.concatenate(outs[first:first + SWA_GROUP], axis=0)
            o_ref[sp, kvh * SWA_GW:(kvh + 1) * SWA_GW] = ot.T.astype(BF16)


def _swa_attn_prompt(q, kx, vt, sink, batch, seq):
    wq = SWA_HEADS * SWA_HEAD_DIM
    per_step = next(n for n in (4, 2, 1) if batch % n == 0)
    seq_rows = seq * per_step
    rows = pl.BlockSpec((SWA_KV_HEADS, seq_rows, SWA_GW), lambda b: (0, b, 0))
    return pl.pallas_call(
        functools.partial(_swa_attn_prompt_kernel, seq=seq),
        grid=(batch // per_step,),
        in_specs=[rows, rows, pl.BlockSpec((SWA_KV_HEADS, SWA_HEAD_DIM, seq_rows), lambda b: (0, 0, b)),
                  pl.BlockSpec(memory_space=pltpu.SMEM)],
        out_specs=pl.BlockSpec((seq_rows, wq), lambda b: (b, 0)),
        out_shape=jax.ShapeDtypeStruct((batch * seq, wq), BF16),
        compiler_params=_params("arbitrary"),
        name="swa_attn_prompt",
    )(q, kx, vt, sink)


def _swa_attn_latent_kernel(q_ref, kc_ref, vc_ref, kp_ref, vp_ref, km_ref, vm_ref, kn_ref, vn_ref,
                            sink_ref, o_ref, ot_buf, s_a, s_b, s_c, s_d, bias_ref, *, tq, dec_seq, past):
    i = pl.program_id(1)
    qpos = i * tq + lax.broadcasted_iota(jnp.int32, (1, tq), 1)

    def band(first, n):
        kpos = first + lax.broadcasted_iota(jnp.int32, (n, 1), 0)
        valid = (jnp.abs(qpos - kpos) <= SWA_WINDOW) & (kpos >= 0) & (kpos < dec_seq)
        return jnp.where(valid, 0.0, NEG_INF)

    bias_ref[...] = jnp.concatenate(
        [jnp.zeros((past, tq), F32), band(i * tq - SWA_WINDOW, SWA_WINDOW), band(i * tq, tq),
         band((i + 1) * tq, SWA_WINDOW)], axis=0)
    srcs = [(kc_ref, vc_ref), (kp_ref, vp_ref), (km_ref, vm_ref), (kn_ref, vn_ref)]
    _swa_heads(q_ref, sink_ref, o_ref, srcs, bias_ref, ot_buf, (s_a, s_b, s_c, s_d))


def _swa_attn_latent(q, kx, vt, kx_ctx, vt_ctx, sink, n_prompt, dec_batch, dec_seq, past, tq):
    t = q.shape[1]
    wq = SWA_HEADS * SWA_HEAD_DIM
    nq = dec_seq // tq
    w = SWA_WINDOW
    n_keys = past + tq + 2 * w
    first = lambda b, i: n_prompt + b * dec_seq + i * tq
    prev = lambda b, i: first(b, i) // w - 1
    nxt = lambda b, i: jnp.minimum((first(b, i) + tq) // w, t // w - 1)
    rows = lambda n, blk: pl.BlockSpec((SWA_KV_HEADS, n, SWA_GW), lambda b, i: (0, blk(b, i), 0))
    cols = lambda n, blk: pl.BlockSpec((SWA_KV_HEADS, SWA_HEAD_DIM, n), lambda b, i: (0, 0, blk(b, i)))
    main = lambda b, i: first(b, i) // tq
    ctx = lambda b, i: b
    return pl.pallas_call(
        functools.partial(_swa_attn_latent_kernel, tq=tq, dec_seq=dec_seq, past=past),
        grid=(dec_batch, nq),
        in_specs=[rows(tq, main), rows(past, ctx), cols(past, ctx), rows(w, prev), cols(w, prev),
                  rows(tq, main), cols(tq, main), rows(w, nxt), cols(w, nxt),
                  pl.BlockSpec(memory_space=pltpu.SMEM)],
        out_specs=pl.BlockSpec((tq, wq), lambda b, i: (b * nq + i, 0)),
        out_shape=jax.ShapeDtypeStruct((dec_batch * dec_seq, wq), BF16),
        scratch_shapes=_swa_attn_scratch(n_keys, tq),
        compiler_params=_params("arbitrary", "arbitrary"),
        name="swa_attn_latent",
    )(q, kx_ctx, vt_ctx, kx, vt, kx, vt, kx, vt, sink)


def _hgrn_pre_kernel(*refs, layer, pair, prompt_tiles, cast):
    x, _, refs = _take_rows(refs, pair, prompt_tiles)
    refs, cast_src, cast_dst = _split_cast(refs, 4, cast)
    mod_ref, gn_ref, w_ref, lbl_ref, q_ref, v_ref, g_ref, f_ref, qmax_ref = refs
    if cast is not None:
        cast.run(cast_src, cast_dst)
    m = mod_ref[0]
    h = _modulate(x, gn_ref[...], m[0:1], m[1:2]).astype(BF16)
    y = _dot(h, w_ref[...])
    n = HG_HEADS * HG_DK
    q = _silu(y[:, :n])
    q_ref[...] = q
    qmax_ref[0] = jnp.max(jnp.abs(q), axis=0, keepdims=True)
    v_ref[...] = y[:, n:2 * n].astype(BF16)
    g_ref[...] = _silu(y[:, 2 * n:3 * n])
    for d in range(2):
        logits = lbl_ref[d]
        e = jnp.exp(logits - jnp.max(logits, axis=0, keepdims=True))
        s = e / jnp.sum(e, axis=0, keepdims=True)
        cs = s[0:1]
        for r in range(1, layer + 1):
            cs = cs + s[r:r + 1]
        lb = cs - s[0:1]
        f_ref[d] = lb + (1.0 - lb) * _sigmoid(y[:, (3 + d) * n:(4 + d) * n])


def _hgrn_pre(x, mods, gn, w5, lb_logits, layer, rows, cast):
    xs, x_specs, pair, t, d = _rows_inputs(x, rows)
    n = HG_HEADS * HG_DK
    return _pre_call(
        functools.partial(_hgrn_pre_kernel, layer=layer, pair=pair, prompt_tiles=rows.prompt_tiles), cast, rows,
        xs + [mods, gn, w5, lb_logits],
        x_specs + [rows.mod_spec(d), _whole(gn.shape), _whole(w5.shape), _whole(lb_logits.shape)],
        [rows.row_spec(n), rows.row_spec(n), rows.row_spec(n),
         pl.BlockSpec((2, rows.tm, n), lambda i: (0, i, 0)), pl.BlockSpec((1, 1, n), lambda i: (i, 0, 0))],
        [jax.ShapeDtypeStruct((t, n), F32), jax.ShapeDtypeStruct((t, n), BF16),
         jax.ShapeDtypeStruct((t, n), F32), jax.ShapeDtypeStruct((2, t, n), F32),
         jax.ShapeDtypeStruct((rows.n_tiles, 1, n), F32)],
        "hgrn_pre")


def _tri_cumsum(tri, x):
    hi = x.astype(BF16)
    r1 = x - hi.astype(F32)
    mid = r1.astype(BF16)
    lo = (r1 - mid.astype(F32)).astype(BF16)
    return _dot(tri, hi) + _dot(tri, mid) + _dot(tri, lo)


def _hgrn_tile(q_ref, v_ref, f_ref, o_ref, st_ref, *, rows, reverse, bounded, slot=0, sfin_ref=None):
    r = rows
    diag = HG_DIAG_BLOCK if bounded else 1
    a = lax.broadcasted_iota(jnp.int32, (r, r), 0)
    b = lax.broadcasted_iota(jnp.int32, (r, r), 1)
    seen = (b >= a) if reverse else (b <= a)
    tri = jnp.where(seen, 1.0, 0.0).astype(BF16)

    q = q_ref[...]
    f = f_ref[slot]
    vb = v_ref[...].astype(BF16)
    lf = jnp.log(f)
    cum = _tri_cumsum(tri, lf)
    tot = cum[0:1, :] if reverse else cum[r - 1:r, :]
    kk = 1.0 - f
    q_in = (q * jnp.exp(cum)).astype(BF16)
    k_d = (kk * jnp.exp(tot - cum)).astype(BF16)
    e_tot = jnp.exp(tot)

    levels = []
    c = r // 2
    while c >= diag:
        q_half = (a % (2 * c) < c) if reverse else (a % (2 * c) >= c)
        k_half = (b % (2 * c) >= c) if reverse else (b % (2 * c) < c)
        levels.append((2 * c, c if reverse else c - 1, ((a // (2 * c)) == (b // (2 * c))) & q_half & k_half))
        c //= 2
    levels.append((diag, diag // 2, ((a // diag) == (b // diag)) & seen))

    def spread(rows_of_block, size):
        return jnp.concatenate([jnp.broadcast_to(row, (size, row.shape[1])) for row in rows_of_block], axis=0)

    factors = []
    for size, ref_row, own in levels:
        if size == 1:
            factors.append((q.astype(BF16), kk.astype(BF16), own))
            continue
        if size >= 8:
            ref = spread([cum[j * size + ref_row:j * size + ref_row + 1, :] for j in range(r // size)], size)
        else:
            ref = _tri_cumsum(jnp.where(b == (a // size) * size + ref_row, 1.0, 0.0).astype(BF16), cum)
        factors.append(((q * jnp.exp(cum - ref)).astype(BF16), (kk * jnp.exp(ref - cum)).astype(BF16), own))

    outs = []
    for h in range(HG_HEADS):
        sl = slice(h * HG_DK, (h + 1) * HG_DK)
        att = jnp.zeros((r, r), F32)
        for q_l, k_l, own in factors:
            att = jnp.where(own, _dot_nt(q_l[:, sl], k_l[:, sl]), att)
        intra = _dot(att.astype(BF16), vb[:, sl])
        if sfin_ref is not None:
            outs.append(intra)
            sfin_ref[0, slot, h] = _dot_tn(k_d[:, sl], vb[:, sl])
            continue
        st = st_ref[h]
        outs.append(_dot_nt(q_in[:, sl], st.astype(BF16)) + intra)
        st_ref[h] = st * e_tot[:, sl] + _dot_tn(vb[:, sl], k_d[:, sl])
    o_ref[slot] = jnp.concatenate(outs, axis=1)


def _hgrn_whole_kernel(q_ref, v_ref, f_ref, o_ref, sfin_ref, *, rows, bounded):
    for slot, reverse in enumerate((False, True)):
        _hgrn_tile(q_ref, v_ref, f_ref, o_ref, None, rows=rows, reverse=reverse, bounded=bounded, slot=slot,
                   sfin_ref=sfin_ref)


def _hgrn_scan_kernel(*refs, rows, has_init, bounded):
    if has_init:
        q_ref, v_ref, f_ref, s0_ref, o_ref, sfin_ref, st_ref = refs
    else:
        q_ref, v_ref, f_ref, o_ref, sfin_ref, st_ref = refs
    d = pl.program_id(1)
    t = pl.program_id(2)

    @pl.when(t == 0)
    def _():
        for h in range(HG_HEADS):
            st_ref[h] = s0_ref[0, 0, h].T if has_init else jnp.zeros((HG_DV, HG_DK), F32)

    for reverse in (False, True):
        @pl.when(d == int(reverse))
        def _():
            _hgrn_tile(q_ref, v_ref, f_ref, o_ref, st_ref, rows=rows, reverse=reverse, bounded=bounded)

    @pl.when(t == pl.num_programs(2) - 1)
    def _():
        for h in range(HG_HEADS):
            sfin_ref[0, 0, h] = st_ref[h].T


def _hgrn_scan(q, v, f2, batch, seq, first_row, rt, s0=None, bounded=True):
    n = q.shape[1]
    nt = seq // rt
    has_init = s0 is not None
    out_shape = [jax.ShapeDtypeStruct((2, batch * seq, n), F32),
                 jax.ShapeDtypeStruct((batch, 2, HG_HEADS, HG_DK, HG_DV), F32)]
    if nt == 1 and not has_init:
        rows_of = lambda b: (first_row // rt + b, 0)
        return pl.pallas_call(
            functools.partial(_hgrn_whole_kernel, rows=rt, bounded=bounded),
            grid=(batch,),
            in_specs=[pl.BlockSpec((rt, n), rows_of), pl.BlockSpec((rt, n), rows_of),
                      pl.BlockSpec((2, rt, n), lambda b: (0, first_row // rt + b, 0))],
            out_specs=[pl.BlockSpec((2, rt, n), lambda b: (0, b, 0)),
                       pl.BlockSpec((1, 2, HG_HEADS, HG_DK, HG_DV), lambda b: (b, 0, 0, 0, 0))],
            out_shape=out_shape,
            compiler_params=_params("arbitrary"),
            name="hgrn_scan_whole",
        )(q, v, f2)

    def local(b, d, i):
        return b * nt + jnp.where(d == 0, i, nt - 1 - i)

    def slab(b, d, i):
        return first_row // rt + local(b, d, i)

    row = pl.BlockSpec((rt, n), lambda b, d, i: (slab(b, d, i), 0))
    state = pl.BlockSpec((1, 1, HG_HEADS, HG_DK, HG_DV), lambda b, d, i: (b, d, 0, 0, 0))
    ins = [q, v, f2]
    in_specs = [row, row, pl.BlockSpec((1, rt, n), lambda b, d, i: (d, slab(b, d, i), 0))]
    if has_init:
        ins.append(s0)
        in_specs.append(state)
    return pl.pallas_call(
        functools.partial(_hgrn_scan_kernel, rows=rt, has_init=has_init, bounded=bounded),
        grid=(batch, 2, nt),
        in_specs=in_specs,
        out_specs=[pl.BlockSpec((1, rt, n), lambda b, d, i: (d, local(b, d, i), 0)), state],
        out_shape=[jax.ShapeDtypeStruct((2, batch * seq, n), F32),
                   jax.ShapeDtypeStruct((batch, 2, HG_HEADS, HG_DK, HG_DV), F32)],
        scratch_shapes=[pltpu.VMEM((HG_HEADS, HG_DV, HG_DK), F32)],
        compiler_params=_params("arbitrary", "arbitrary", "arbitrary"),
        name="hgrn_scan_latent" if has_init else "hgrn_scan_prompt",
    )(*ins)


def _post_kernel(*refs, hgrn, final, pair_in, pair_out, prompt_tiles):
    refs = list(refs)
    n_mix = 4 if hgrn else 2
    mix_refs, refs = refs[:n_mix], refs[n_mix:]
    x, is_prompt, refs = _take_rows(refs, pair_in, prompt_tiles)
    mod_ref, wo_ref, gn_ref, wg_ref, wu_ref, wd_ref = refs[:6]
    refs = list(refs[6:])
    fn_ref = refs.pop(0) if final else None
    if hgrn:
        odp_ref, odl_ref, g_ref, onorm_ref = mix_refs
        o2 = jnp.where(is_prompt, odp_ref[0] + odp_ref[1], odl_ref[0] + odl_ref[1])
        gate = g_ref[...]
        onorm = onorm_ref[...]
        parts = []
        for h in range(HG_HEADS):
            sl = slice(h * HG_DV, (h + 1) * HG_DV)
            parts.append(_rms(o2[:, sl], onorm) * gate[:, sl])
        o = jnp.concatenate(parts, axis=1).astype(BF16)
    else:
        o = jnp.where(is_prompt, mix_refs[0][...], mix_refs[1][...])
    m = mod_ref[0]
    x1 = x + m[2:3] * _dot(o, wo_ref[...])
    h2 = _modulate(x1, gn_ref[...], m[3:4], m[4:5]).astype(BF16)
    a = (_silu(_dot(h2, wg_ref[...])) * _dot(h2, wu_ref[...])).astype(BF16)
    x2 = x1 + m[5:6] * _dot(a, wd_ref[...])
    if final:
        x2 = _rms(x2, fn_ref[...])
    if pair_out:
        @pl.when(is_prompt)
        def _():
            refs[0][...] = x2

        @pl.when(jnp.logical_not(is_prompt))
        def _():
            refs[1][...] = x2
    else:
        refs[0][...] = x2


def _post(mix, x, mods, wo, gn, wg, wu, wd, rows, final_norm=None, hgrn=False, pair_out=False):
    xs, x_specs, pair_in, t, d = _rows_inputs(x, rows)
    tm, pt = rows.tm, rows.prompt_tiles
    if hgrn:
        od_p, od_l, gate, onorm = mix
        head = [od_p, od_l, gate, onorm]
        head_specs = [pl.BlockSpec((2, tm, d), lambda i: (0, jnp.minimum(i, pt - 1), 0)),
                      pl.BlockSpec((2, tm, d), lambda i: (0, jnp.maximum(i - pt, 0), 0)),
                      rows.row_spec(d), _whole(onorm.shape)]
    else:
        head = list(mix)
        head_specs = [rows.prompt_spec(mix[0].shape[1]), rows.latent_spec(mix[1].shape[1])]
    ins = head + xs + [mods, wo, gn, wg, wu, wd]
    in_specs = head_specs + x_specs + [rows.mod_spec(d)] + [_whole(a.shape) for a in (wo, gn, wg, wu, wd)]
    if final_norm is not None:
        ins.append(final_norm)
        in_specs.append(_whole(final_norm.shape))
    if pair_out:
        out_specs = rows.x_specs(d)
        out_shape = [jax.ShapeDtypeStruct((pt * tm, d), F32), jax.ShapeDtypeStruct((t - pt * tm, d), F32)]
    else:
        out_specs = rows.row_spec(d)
        out_shape = jax.ShapeDtypeStruct((t, d), F32)
    out = pl.pallas_call(
        functools.partial(_post_kernel, hgrn=hgrn, final=final_norm is not None, pair_in=pair_in, pair_out=pair_out,
                          prompt_tiles=pt),
        grid=(rows.n_tiles,),
        in_specs=in_specs,
        out_specs=out_specs,
        out_shape=out_shape,
        compiler_params=_params("arbitrary"),
        name="post",
    )(*ins)
    return tuple(out) if pair_out else out


def _pick_tile(n_prompt_rows, dec_seq, want):
    tm = want
    while n_prompt_rows % tm or dec_seq % tm:
        tm //= 2
    return tm


def kernel(x_prompt, x_sample, cache_mla_ckv, cache_mla_krope, state_hgrn, cache_swa_k, cache_swa_v, c, c_ctx, ada_w, ada_b, norm_mix, norm_ffn, ffn_w_gate, ffn_w_up, ffn_w_down, final_norm, mla_w_dq, mla_q_norm, mla_w_uq, mla_w_dkv, mla_kv_norm, mla_w_uk, mla_w_uv, mla_w_o, hg_w_q, hg_w_f, hg_w_i, hg_w_g, hg_o_norm, hg_w_o, hg_lb_logits, swa_w_q, swa_w_k, swa_w_v, swa_w_o, swa_sink):
    batch, seq, d = x_prompt.shape
    dec_batch, dec_seq, _ = x_sample.shape
    past = cache_mla_ckv.shape[2]
    depth = ada_w.shape[0]
    n_prompt = batch * seq
    n_rows = n_prompt + dec_batch * dec_seq
    assert dec_batch + 1 <= COND_ROWS and seq % SWA_WINDOW == 0 and dec_seq % (2 * SWA_WINDOW) == 0
    assert n_prompt % dec_seq == 0

    pre_rows = _Rows(n_prompt, dec_seq, n_rows, _pick_tile(n_prompt, dec_seq, 512))
    post_rows = _Rows(n_prompt, dec_seq, n_rows, _pick_tile(n_prompt, dec_seq, 512))
    hg_pre_rows = _Rows(n_prompt, dec_seq, n_rows, _pick_tile(n_prompt, dec_seq, 256))
    hg_post_rows = _Rows(n_prompt, dec_seq, n_rows, _pick_tile(n_prompt, dec_seq, 256))
    tq = _pick_tile(n_prompt, dec_seq, 256)
    cos64, sin64 = _rope_tables(dec_seq, pre_rows.tm)
    n_tab = cos64.shape[0]
    cos_mla = jnp.asarray(np.concatenate([cos64, np.ones((n_tab, ROPE_PERIOD), np.float32)], axis=1))
    sin_mla = jnp.asarray(np.concatenate([sin64, np.zeros((n_tab, ROPE_PERIOD), np.float32)], axis=1))
    cos_swa, sin_swa = jnp.asarray(np.tile(cos64, (1, 2))), jnp.asarray(np.tile(sin64, (1, 2)))

    cond = jnp.concatenate([c_ctx[None, :], c, jnp.zeros((COND_ROWS - 1 - dec_batch, d), F32)], axis=0)
    mods = _adaln(cond, ada_w, ada_b).reshape(depth, COND_ROWS, 6, d)

    x = (x_prompt.reshape(n_prompt, d), x_sample.reshape(dec_batch * dec_seq, d))
    ffn_stacks = (ffn_w_gate, ffn_w_up, ffn_w_down)
    row1 = lambda a: a.reshape(1, -1)
    new_ckv, new_krope, new_hg, new_k, new_v = [], [], [], [], []
    for i in range(depth):
        kind, j = i % N_MIXERS, i // N_MIXERS
        gn = row1(norm_mix[i])
        rows_i = hg_pre_rows if kind == 1 else pre_rows
        cast = _Cast(ffn_stacks, i, rows_i.n_tiles)
        if kind == 0:
            uq = mla_w_uq[j].reshape(-1, MLA_HEADS, MLA_NOPE_DIM + MLA_ROPE_DIM)
            uq_rope = jnp.pad(uq[:, :, MLA_NOPE_DIM:], ((0, 0), (0, 0), (0, LANES - MLA_ROPE_DIM)))
            uq = jnp.concatenate([uq[:, :, :MLA_NOPE_DIM].reshape(uq.shape[0], -1),
                                  uq_rope.reshape(uq.shape[0], -1)], axis=1)
            w = {
                "dq": mla_w_dq[j].astype(BF16), "q_norm": row1(mla_q_norm[j]), "uq": uq.astype(BF16),
                "dkv": jnp.pad(mla_w_dkv[j], ((0, 0), (0, LANES - MLA_ROPE_DIM))).astype(BF16),
                "kv_norm": row1(mla_kv_norm[j]),
                "uk": mla_w_uk[j].astype(BF16), "uv_t": mla_w_uv[j].T.astype(BF16),
            }
            (q, ckv, kr_raw, kcat, vt), ffn = _mla_pre(x, mods[i], gn, cos_mla, sin_mla, w, rows_i, cast)
            kr_ctx = jnp.pad(cache_mla_krope[:, j].reshape(dec_batch * past, -1), ((0, 0), (0, LANES - MLA_ROPE_DIM)))
            kcat_ctx, vt_ctx = _mla_expand(cache_mla_ckv[:, j].reshape(dec_batch * past, -1), kr_ctx, w["uk"], w["uv_t"])
            mix = (_mla_attn_prompt(q, kcat, vt, batch, seq),
                   _mla_attn_latent(q, kcat, vt, kcat_ctx, vt_ctx, n_prompt, dec_batch, dec_seq, past, tq))
            wo = mla_w_o[j].astype(BF16)
            new_ckv.append(ckv.reshape(batch, seq, -1))
            new_krope.append(kr_raw.reshape(batch, seq, -1))
        elif kind == 1:
            w5 = jnp.concatenate([hg_w_q[j], hg_w_i[j], hg_w_g[j], hg_w_f[j, 0], hg_w_f[j, 1]], axis=1).astype(BF16)
            (q, v, gate, f2, q_max), ffn = _hgrn_pre(x, mods[i], gn, w5, hg_lb_logits, i, rows_i, cast)
            rt = _pick_tile(seq, dec_seq, 256)
            sm = jax.nn.softmax(hg_lb_logits.astype(F32), axis=1)
            lb_min = jnp.min(jnp.cumsum(sm, axis=1)[:, i] - sm[:, 0])
            worst = (HG_DIAG_BLOCK // 2) * -jnp.log(lb_min) + jnp.log(jnp.maximum(jnp.max(q_max), 1.0))
            fits = worst < HG_MAX_EXPONENT

            def scans(bounded):
                od_p, s_p = _hgrn_scan(q, v, f2, batch, seq, 0, rt, bounded=bounded)
                od_l, _ = _hgrn_scan(q, v, f2, dec_batch, dec_seq, n_prompt, rt, s0=state_hgrn[:, j], bounded=bounded)
                return od_p, od_l, s_p

            od_p, od_l, s_prompt = lax.cond(fits, lambda: scans(True), lambda: scans(False))
            mix = (od_p, od_l, gate, row1(hg_o_norm[j]))
            wo = hg_w_o[j].astype(BF16)
            new_hg.append(s_prompt)
        else:
            wqkv = jnp.concatenate([swa_w_q[j], swa_w_k[j], swa_w_v[j]], axis=1).astype(BF16)
            (q, kx, vt, k_raw, v_raw), ffn = _swa_pre(x, mods[i], gn, cos_swa, sin_swa, wqkv, rows_i, cast)
            sink = swa_sink[j]
            k_ctx = cache_swa_k[:, j].reshape(dec_batch * past, SWA_KV_HEADS, 1, SWA_HEAD_DIM).transpose(1, 0, 2, 3)
            kx_ctx = jnp.broadcast_to(k_ctx, (SWA_KV_HEADS, dec_batch * past, SWA_GROUP, SWA_HEAD_DIM))
            kx_ctx = kx_ctx.reshape(SWA_KV_HEADS, dec_batch * past, SWA_GW).astype(BF16)
            vt_ctx = cache_swa_v[:, j].reshape(dec_batch * past, SWA_KV_HEADS, SWA_HEAD_DIM).transpose(1, 2, 0).astype(BF16)
            mix = (_swa_attn_prompt(q, kx, vt, sink, batch, seq),
                   _swa_attn_latent(q, kx, vt, kx_ctx, vt_ctx, sink, n_prompt, dec_batch, dec_seq, past, tq))
            wo = swa_w_o[j].astype(BF16)
            new_k.append(k_raw.reshape(batch, seq, SWA_KV_HEADS, SWA_HEAD_DIM))
            new_v.append(v_raw.reshape(batch, seq, SWA_KV_HEADS, SWA_HEAD_DIM))
        if ffn is None:
            ffn = [a[i].astype(BF16) for a in ffn_stacks]
        x = _post(mix, x, mods[i], wo, row1(norm_ffn[i]), *ffn,
                  hg_post_rows if kind == 1 else post_rows,
                  final_norm=row1(final_norm) if i == depth - 1 else None, hgrn=kind == 1,
                  pair_out=i == depth - 1)
    y_prompt = x[0].reshape(batch, seq, d)
    y_sample = x[1].reshape(dec_batch, dec_seq, d)
    return (y_prompt, y_sample, jnp.stack(new_ckv, axis=1), jnp.stack(new_krope, axis=1),
            jnp.stack(new_hg, axis=1), jnp.stack(new_k, axis=1), jnp.stack(new_v, axis=1))
```

```python
import functools

import numpy as np
import jax
import jax.numpy as jnp
from jax import lax
from jax.experimental import pallas as pl
from jax.experimental.pallas import tpu as pltpu

F32 = jnp.float32
BF16 = jnp.bfloat16

GRID_W = 64
N_MIXERS = 3

MLA_HEADS = 8
MLA_KV_LORA = 256
MLA_NOPE_DIM = 128
MLA_ROPE_DIM = 64
MLA_V_DIM = 128
MLA_QK_PAD = 256
MLA_SCALE = (MLA_NOPE_DIM + MLA_ROPE_DIM) ** -0.5

HG_HEADS = 8
HG_DK = 128
HG_DV = 128
HG_DIAG_BLOCK = 32
HG_MAX_EXPONENT = 80.0

SWA_HEADS = 16
SWA_KV_HEADS = 4
SWA_GROUP = SWA_HEADS // SWA_KV_HEADS
SWA_HEAD_DIM = 64
SWA_GW = SWA_GROUP * SWA_HEAD_DIM
SWA_WINDOW = 128
SWA_SCALE = SWA_HEAD_DIM ** -0.5

ROPE_BASE = 10000.0
ROPE_PERIOD = 64
ROPE_QUARTER = 16
NORM_EPS = 1e-6
NEG_INF = -1e30
LOG2_E = 1.4426950408889634

LANES = 128
COND_ROWS = 8
VMEM_LIMIT = 56 * 1024 * 1024


def _sigmoid(x):
    return jax.nn.sigmoid(x)


def _silu(x):
    return x * jax.nn.sigmoid(x)


def _rms(x, g):
    return x * lax.rsqrt(jnp.mean(x * x, axis=-1, keepdims=True) + NORM_EPS) * g


def _modulate(x, g, shift, scale):
    return _rms(x, g) * (1.0 + scale) + shift


def _dot(a, b):
    return jnp.dot(a, b, preferred_element_type=F32)


def _dot_nt(a, b):
    return lax.dot_general(a, b, (((1,), (1,)), ((), ())), preferred_element_type=F32)


def _dot_tn(a, b):
    return lax.dot_general(a, b, (((0,), (0,)), ((), ())), preferred_element_type=F32)


def _swap_pairs(x):
    n = x.shape[1]
    lane = lax.broadcasted_iota(jnp.int32, x.shape, 1)
    ahead = pltpu.roll(x, n - ROPE_QUARTER, 1)
    behind = pltpu.roll(x, ROPE_QUARTER, 1)
    return jnp.where((lane & (2 * ROPE_QUARTER - 1)) < ROPE_QUARTER, ahead, behind)


def _rope(x, cos, sin):
    reps = x.shape[1] // cos.shape[1]
    if reps > 1:
        cos = jnp.concatenate([cos] * reps, axis=1)
        sin = jnp.concatenate([sin] * reps, axis=1)
    return x * cos + _swap_pairs(x) * sin


def _whole(shape):
    zeros = (0,) * len(shape)
    return pl.BlockSpec(shape, lambda *_: zeros, pipeline_mode=pl.Buffered(1))


def _params(*sem):
    return pltpu.CompilerParams(dimension_semantics=sem, vmem_limit_bytes=VMEM_LIMIT)


class _Rows:
    def __init__(self, n_prompt_rows, dec_seq, n_rows, tm):
        assert n_prompt_rows % tm == 0 and dec_seq % tm == 0
        self.tm = tm
        self.n_tiles = n_rows // tm
        self.prompt_tiles = n_prompt_rows // tm
        self.seq_tiles = dec_seq // tm

    def cond(self, i):
        return jnp.where(i < self.prompt_tiles, 0, 1 + jnp.maximum(i - self.prompt_tiles, 0) // self.seq_tiles)

    def rope_block(self, i):
        return jnp.where(i < self.prompt_tiles, 0, 1 + jnp.maximum(i - self.prompt_tiles, 0) % self.seq_tiles)

    def row_spec(self, width):
        return pl.BlockSpec((self.tm, width), lambda i: (i, 0))

    def col_spec(self, height):
        return pl.BlockSpec((height, self.tm), lambda i: (0, i))

    def mod_spec(self, d):
        return pl.BlockSpec((1, 6, d), lambda i: (self.cond(i), 0, 0))

    def rope_spec(self, width):
        return pl.BlockSpec((self.tm, width), lambda i: (self.rope_block(i), 0))

    def x_specs(self, width):
        return [self.prompt_spec(width), self.latent_spec(width)]

    def is_prompt(self):
        return pl.program_id(0) < self.prompt_tiles

    def prompt_spec(self, width):
        return pl.BlockSpec((self.tm, width), lambda i: (jnp.minimum(i, self.prompt_tiles - 1), 0))

    def latent_spec(self, width):
        return pl.BlockSpec((self.tm, width), lambda i: (jnp.maximum(i - self.prompt_tiles, 0), 0))


def _rope_tables(dec_seq, tm):
    pos = np.arange(dec_seq)
    row = (pos // GRID_W).astype(np.float32)
    col = (pos % GRID_W).astype(np.float32)
    inv_freq = (ROPE_BASE ** (-np.arange(ROPE_QUARTER, dtype=np.float32) / ROPE_QUARTER)).astype(np.float32)
    ang_r = row[:, None] * inv_freq[None, :]
    ang_c = col[:, None] * inv_freq[None, :]
    cos = np.concatenate([np.cos(ang_r), np.cos(ang_r), np.cos(ang_c), np.cos(ang_c)], axis=1)
    sin = np.concatenate([-np.sin(ang_r), np.sin(ang_r), -np.sin(ang_c), np.sin(ang_c)], axis=1)
    cos = np.concatenate([np.ones((tm, ROPE_PERIOD), np.float32), cos], axis=0)
    sin = np.concatenate([np.zeros((tm, ROPE_PERIOD), np.float32), sin], axis=0)
    return cos.astype(np.float32), sin.astype(np.float32)


def _adaln_kernel(c_ref, w_ref, b_ref, o_ref):
    o_ref[0] = _dot(_silu(c_ref[...]), w_ref[0]) + b_ref[0]


def _adaln(cond, ada_w, ada_b, tn=1536):
    depth, d, n = ada_w.shape
    return pl.pallas_call(
        _adaln_kernel,
        grid=(depth, n // tn),
        in_specs=[pl.BlockSpec((COND_ROWS, d), lambda l, j: (0, 0)),
                  pl.BlockSpec((1, d, tn), lambda l, j: (l, 0, j)),
                  pl.BlockSpec((1, 1, tn), lambda l, j: (l, 0, j))],
        out_specs=pl.BlockSpec((1, COND_ROWS, tn), lambda l, j: (l, 0, j)),
        out_shape=jax.ShapeDtypeStruct((depth, COND_ROWS, n), F32),
        compiler_params=_params("arbitrary", "arbitrary"),
        name="adaln",
    )(cond, ada_w, ada_b.reshape(depth, 1, n))


def _mla_store_heads(q_ref, kcat_ref, vt_ref, qn, qr_pad, kn, kr_pad, vt):
    for h in range(MLA_HEADS):
        nope = slice(h * MLA_NOPE_DIM, (h + 1) * MLA_NOPE_DIM)
        if q_ref is not None:
            q_ref[h] = jnp.concatenate([qn[:, nope], qr_pad[:, h * LANES:(h + 1) * LANES]], axis=1)
        kcat_ref[h] = jnp.concatenate([kn[:, nope], kr_pad], axis=1)
        vt_ref[h] = vt[h * MLA_V_DIM:(h + 1) * MLA_V_DIM, :]


def _read_rows(is_prompt, p_ref, l_ref):
    return jnp.where(is_prompt, p_ref[...], l_ref[...])


def _take_rows(refs, pair, prompt_tiles):
    is_prompt = pl.program_id(0) < prompt_tiles
    if pair:
        return _read_rows(is_prompt, refs[0], refs[1]), is_prompt, refs[2:]
    return refs[0][...], is_prompt, refs[1:]


def _rows_inputs(x, rows):
    if isinstance(x, tuple):
        d = x[0].shape[1]
        return list(x), rows.x_specs(d), True, x[0].shape[0] + x[1].shape[0], d
    return [x], [rows.row_spec(x.shape[1])], False, x.shape[0], x.shape[1]


class _Cast:
    def __init__(self, stacks, layer, n_steps):
        self.stacks, self.layer = stacks, layer
        self.chunks = n_steps // len(stacks)
        self.ok = (n_steps % len(stacks) == 0
                   and all(a.shape[1] % self.chunks == 0 and (a.shape[1] // self.chunks) % 16 == 0 for a in stacks))

    def _spec(self, k, a, with_layer):
        rows = a.shape[1] // self.chunks
        chunk = lambda i: jnp.clip(i - k * self.chunks, 0, self.chunks - 1)
        if with_layer:
            return pl.BlockSpec((1, rows, a.shape[2]), lambda i: (self.layer, chunk(i), 0))
        return pl.BlockSpec((rows, a.shape[2]), lambda i: (chunk(i), 0))

    def in_specs(self):
        return [self._spec(k, a, True) for k, a in enumerate(self.stacks)]

    def out_specs(self):
        return [self._spec(k, a, False) for k, a in enumerate(self.stacks)]

    def out_shape(self):
        return [jax.ShapeDtypeStruct(a.shape[1:], BF16) for a in self.stacks]

    def run(self, src_refs, dst_refs):
        turn = pl.program_id(0) // self.chunks
        for k, (src, dst) in enumerate(zip(src_refs, dst_refs)):
            @pl.when(turn == k)
            def _():
                dst[...] = src[0].astype(BF16)


def _split_cast(refs, n_in, cast):
    if cast is None:
        return refs, (), ()
    n = len(cast.stacks)
    return refs[:n_in] + refs[n_in + n:len(refs) - n], refs[n_in:n_in + n], refs[len(refs) - n:]


def _mla_pre_kernel(*refs, pair, prompt_tiles, cast):
    x, is_prompt, refs = _take_rows(refs, pair, prompt_tiles)
    refs, cast_src, cast_dst = _split_cast(refs, 11, cast)
    (mod_ref, gn_ref, cos_ref, sin_ref, wdq_ref, qnorm_ref, wuq_ref, wdkv_ref, kvnorm_ref, wuk_ref, wuvt_ref,
     q_ref, ckv_ref, krraw_ref, kcat_ref, vt_ref) = refs
    if cast is not None:
        cast.run(cast_src, cast_dst)
    m = mod_ref[0]
    h = _modulate(x, gn_ref[...], m[0:1], m[1:2]).astype(BF16)
    q_lat = _rms(_dot(h, wdq_ref[...]), qnorm_ref[...]).astype(BF16)
    cos, sin = cos_ref[...], sin_ref[...]
    nn = MLA_HEADS * MLA_NOPE_DIM
    qn = _dot(q_lat, wuq_ref[:, :nn]).astype(BF16)
    qr_pad = _rope(_dot(q_lat, wuq_ref[:, nn:]), cos, sin).astype(BF16)
    kv = _dot(h, wdkv_ref[...])
    ckv = _rms(kv[:, :MLA_KV_LORA], kvnorm_ref[...])
    kr = kv[:, MLA_KV_LORA:]

    @pl.when(is_prompt)
    def _():
        ckv_ref[...] = ckv
        krraw_ref[...] = kr[:, :MLA_ROPE_DIM]

    kr_pad = _rope(kr, cos, sin).astype(BF16)
    cb = ckv.astype(BF16)
    _mla_store_heads(q_ref, kcat_ref, vt_ref, qn, qr_pad, _dot(cb, wuk_ref[...]).astype(BF16), kr_pad,
                     _dot_nt(wuvt_ref[...], cb).astype(BF16))


def _pre_call(kernel, cast, rows, ins, in_specs, out_specs, out_shape, name):
    if cast is not None and cast.ok:
        ins, in_specs = ins + list(cast.stacks), in_specs + cast.in_specs()
        out_specs, out_shape = out_specs + cast.out_specs(), out_shape + cast.out_shape()
    else:
        cast = None
    outs = pl.pallas_call(
        functools.partial(kernel, cast=cast),
        grid=(rows.n_tiles,),
        in_specs=in_specs,
        out_specs=out_specs,
        out_shape=out_shape,
        compiler_params=_params("arbitrary"),
        name=name,
    )(*ins)
    if cast is None:
        return outs, None
    n = len(cast.stacks)
    return outs[:-n], outs[-n:]


def _mla_pre(x, mods, gn, cos, sin, w, rows, cast):
    xs, x_specs, pair, t, d = _rows_inputs(x, rows)
    tm = rows.tm
    n_prompt = rows.prompt_tiles * tm
    weights = [w["dq"], w["q_norm"], w["uq"], w["dkv"], w["kv_norm"], w["uk"], w["uv_t"]]
    ins = xs + [mods, gn, cos, sin] + weights
    in_specs = x_specs + [rows.mod_spec(d), _whole(gn.shape),
                          rows.rope_spec(cos.shape[1]), rows.rope_spec(sin.shape[1])]
    in_specs += [_whole(a.shape) for a in weights]
    heads_rows = pl.BlockSpec((MLA_HEADS, tm, MLA_QK_PAD), lambda i: (0, i, 0))
    return _pre_call(
        functools.partial(_mla_pre_kernel, pair=pair, prompt_tiles=rows.prompt_tiles), cast, rows, ins, in_specs,
        [heads_rows, rows.prompt_spec(MLA_KV_LORA), rows.prompt_spec(MLA_ROPE_DIM), heads_rows,
         pl.BlockSpec((MLA_HEADS, MLA_V_DIM, tm), lambda i: (0, 0, i))],
        [jax.ShapeDtypeStruct((MLA_HEADS, t, MLA_QK_PAD), BF16),
         jax.ShapeDtypeStruct((n_prompt, MLA_KV_LORA), F32),
         jax.ShapeDtypeStruct((n_prompt, MLA_ROPE_DIM), F32),
         jax.ShapeDtypeStruct((MLA_HEADS, t, MLA_QK_PAD), BF16),
         jax.ShapeDtypeStruct((MLA_HEADS, MLA_V_DIM, t), BF16)],
        "mla_pre")


def _mla_expand_kernel(c_ref, kr_ref, wuk_ref, wuvt_ref, kcat_ref, vt_ref):
    cb = c_ref[...].astype(BF16)
    _mla_store_heads(None, kcat_ref, vt_ref, None, None, _dot(cb, wuk_ref[...]).astype(BF16),
                     kr_ref[...].astype(BF16), _dot_nt(wuvt_ref[...], cb).astype(BF16))


def _mla_expand(ckv, kr_pad, wuk, wuvt):
    n = ckv.shape[0]
    return pl.pallas_call(
        _mla_expand_kernel,
        out_shape=[jax.ShapeDtypeStruct((MLA_HEADS, n, MLA_QK_PAD), BF16),
                   jax.ShapeDtypeStruct((MLA_HEADS, MLA_V_DIM, n), BF16)],
        compiler_params=pltpu.CompilerParams(vmem_limit_bytes=VMEM_LIMIT),
        name="mla_expand",
    )(ckv, kr_pad, wuk, wuvt)


def _mla_attn_kernel(*refs, n_src):
    q_ref = refs[0]
    srcs = [(refs[1 + 2 * i], refs[2 + 2 * i]) for i in range(n_src)]
    o_ref, o_buf = refs[1 + 2 * n_src:3 + 2 * n_src]
    s_bufs = refs[3 + 2 * n_src:]
    n_keys = [k_ref.shape[1] for k_ref, _ in srcs]
    starts = [sum(n_keys[:i]) for i in range(n_src)]

    def put_scores(h, s_buf):
        q = q_ref[h]
        for (k_ref, _), first, n in zip(srcs, starts, n_keys):
            s_buf[first:first + n, :] = _dot_nt(k_ref[h], q)

    def finish(h, s_buf):
        s = s_buf[...]
        mx = jnp.max(s, axis=0, keepdims=True)
        p = jnp.exp2((s - mx) * (MLA_SCALE * LOG2_E))
        den = jnp.sum(p, axis=0, keepdims=True)
        p = p.astype(BF16)
        acc = None
        for (_, vt_ref), first, n in zip(srcs, starts, n_keys):
            a = _dot(vt_ref[h], p[first:first + n, :])
            acc = a if acc is None else acc + a
        o_buf[h] = (acc / den).T.astype(BF16)

    _pipelined_heads(MLA_HEADS, put_scores, finish, s_bufs)
    for h in range(MLA_HEADS):
        o_ref[:, h * MLA_V_DIM:(h + 1) * MLA_V_DIM] = o_buf[h]


def _mla_attn_scratch(n_keys, tq):
    return [pltpu.VMEM((MLA_HEADS, tq, MLA_V_DIM), BF16)] + [pltpu.VMEM((n_keys, tq), F32)] * 4


def _pipelined_heads(n_heads, put_scores, finish, bufs):
    a, b, c, d = bufs
    assert n_heads % 4 == 0
    put_scores(0, a)
    put_scores(1, b)

    def quad(j, carry):
        h = 4 * j
        put_scores(h + 2, c)
        put_scores(h + 3, d)
        finish(h, a)
        finish(h + 1, b)
        put_scores(h + 4, a)
        put_scores(h + 5, b)
        finish(h + 2, c)
        finish(h + 3, d)
        return carry

    lax.fori_loop(0, n_heads // 4 - 1, quad, 0)
    h = n_heads - 4
    put_scores(h + 2, c)
    put_scores(h + 3, d)
    finish(h, a)
    finish(h + 1, b)
    finish(h + 2, c)
    finish(h + 3, d)


def _staged_attention(scores, values_t, scale, sinks=None):
    c = scale * LOG2_E
    mx = [jnp.max(s, axis=0, keepdims=True) for s in scores]
    if sinks is not None:
        mx = [jnp.maximum(m, z) for m, z in zip(mx, sinks)]
    p = [jnp.exp2((s - m) * c) for s, m in zip(scores, mx)]
    den = [jnp.sum(x, axis=0, keepdims=True) for x in p]
    if sinks is not None:
        den = [d + jnp.exp2((z - m) * c) for d, z, m in zip(den, sinks, mx)]
    acc = [_dot(v, x.astype(BF16)) for v, x in zip(values_t, p)]
    return [a / d for a, d in zip(acc, den)]


def _mla_attn_prompt_kernel(q_ref, k_ref, vt_ref, o_ref, *, seq):
    per_step = q_ref.shape[1] // seq
    spans = [slice(s * seq, (s + 1) * seq) for s in range(per_step)]
    scores = [_dot_nt(k_ref[h, sp, :], q_ref[h, sp, :]) for sp in spans for h in range(MLA_HEADS)]
    values = [vt_ref[h, :, sp] for sp in spans for h in range(MLA_HEADS)]
    outs = _staged_attention(scores, values, MLA_SCALE)
    for s, sp in enumerate(spans):
        for h in range(MLA_HEADS):
            o_ref[sp, h * MLA_V_DIM:(h + 1) * MLA_V_DIM] = outs[s * MLA_HEADS + h].T.astype(BF16)


def _mla_attn_prompt(q, kcat, vt, batch, seq):
    hv = MLA_HEADS * MLA_V_DIM
    per_step = next(n for n in (4, 2, 1) if batch % n == 0)
    seq_rows = seq * per_step
    rows = pl.BlockSpec((MLA_HEADS, seq_rows, MLA_QK_PAD), lambda b: (0, b, 0))
    return pl.pallas_call(
        functools.partial(_mla_attn_prompt_kernel, seq=seq),
        grid=(batch // per_step,),
        in_specs=[rows, rows, pl.BlockSpec((MLA_HEADS, MLA_V_DIM, seq_rows), lambda b: (0, 0, b))],
        out_specs=pl.BlockSpec((seq_rows, hv), lambda b: (b, 0)),
        out_shape=jax.ShapeDtypeStruct((batch * seq, hv), BF16),
        compiler_params=_params("arbitrary"),
        name="mla_attn_prompt",
    )(q, kcat, vt)


def _mla_attn_latent(q, kcat, vt, kcat_ctx, vt_ctx, n_prompt, dec_batch, dec_seq, past, tq):
    hv = MLA_HEADS * MLA_V_DIM
    nq = dec_seq // tq
    lat_blk = n_prompt // dec_seq
    return pl.pallas_call(
        functools.partial(_mla_attn_kernel, n_src=2),
        grid=(dec_batch, nq),
        in_specs=[pl.BlockSpec((MLA_HEADS, tq, MLA_QK_PAD), lambda b, i: (0, n_prompt // tq + b * nq + i, 0)),
                  pl.BlockSpec((MLA_HEADS, past, MLA_QK_PAD), lambda b, i: (0, b, 0)),
                  pl.BlockSpec((MLA_HEADS, MLA_V_DIM, past), lambda b, i: (0, 0, b)),
                  pl.BlockSpec((MLA_HEADS, dec_seq, MLA_QK_PAD), lambda b, i: (0, lat_blk + b, 0)),
                  pl.BlockSpec((MLA_HEADS, MLA_V_DIM, dec_seq), lambda b, i: (0, 0, lat_blk + b))],
        out_specs=pl.BlockSpec((tq, hv), lambda b, i: (b * nq + i, 0)),
        out_shape=jax.ShapeDtypeStruct((dec_batch * dec_seq, hv), BF16),
        scratch_shapes=_mla_attn_scratch(past + dec_seq, tq),
        compiler_params=_params("arbitrary", "arbitrary"),
        name="mla_attn_latent",
    )(q, kcat_ctx, vt_ctx, kcat, vt)


def _tile_heads(x):
    n = x.shape[1]
    block = lax.broadcasted_iota(jnp.int32, x.shape, 1) // SWA_HEAD_DIM
    rolled = [x] + [pltpu.roll(x, s * SWA_HEAD_DIM, 1) for s in range(1, SWA_KV_HEADS)]
    out = []
    for kvh in range(SWA_KV_HEADS):
        blk = rolled[(0 - kvh) % SWA_KV_HEADS]
        for g in range(1, n // SWA_HEAD_DIM):
            blk = jnp.where(block == g, rolled[(g - kvh) % SWA_KV_HEADS], blk)
        out.append(blk)
    return jnp.concatenate(out, axis=1)


def _swa_pre_kernel(*refs, pair, prompt_tiles, cast):
    x, is_prompt, refs = _take_rows(refs, pair, prompt_tiles)
    refs, cast_src, cast_dst = _split_cast(refs, 5, cast)
    mod_ref, gn_ref, cos_ref, sin_ref, wqkv_ref, q_ref, kx_ref, vt_ref, kraw_ref, vraw_ref = refs
    if cast is not None:
        cast.run(cast_src, cast_dst)
    m = mod_ref[0]
    h = _modulate(x, gn_ref[...], m[0:1], m[1:2]).astype(BF16)
    qkv = _dot(h, wqkv_ref[...])
    nq, nk = SWA_HEADS * SWA_HEAD_DIM, SWA_KV_HEADS * SWA_HEAD_DIM
    cos, sin = cos_ref[...], sin_ref[...]
    q = _rope(qkv[:, :nq], cos, sin).astype(BF16)
    k = qkv[:, nq:nq + nk]
    v = qkv[:, nq + nk:]

    @pl.when(is_prompt)
    def _():
        kraw_ref[...] = k.reshape(kraw_ref.shape)
        vraw_ref[...] = v.reshape(vraw_ref.shape)

    kx = _tile_heads(_rope(k, cos, sin)).astype(BF16)
    vt = v.T.astype(BF16)
    for kvh in range(SWA_KV_HEADS):
        q_ref[kvh] = q[:, kvh * SWA_GW:(kvh + 1) * SWA_GW]
        kx_ref[kvh] = kx[:, kvh * SWA_GW:(kvh + 1) * SWA_GW]
        vt_ref[kvh] = vt[kvh * SWA_HEAD_DIM:(kvh + 1) * SWA_HEAD_DIM, :]


def _swa_pre(x, mods, gn, cos, sin, wqkv, rows, cast):
    xs, x_specs, pair, t, d = _rows_inputs(x, rows)
    tm = rows.tm
    n_prompt = rows.prompt_tiles * tm
    nk = SWA_KV_HEADS * SWA_HEAD_DIM
    heads_rows = pl.BlockSpec((SWA_KV_HEADS, tm, SWA_GW), lambda i: (0, i, 0))
    raw_spec = pl.BlockSpec((tm, SWA_KV_HEADS, SWA_HEAD_DIM), lambda i: (jnp.minimum(i, rows.prompt_tiles - 1), 0, 0))
    return _pre_call(
        functools.partial(_swa_pre_kernel, pair=pair, prompt_tiles=rows.prompt_tiles), cast, rows,
        xs + [mods, gn, cos, sin, wqkv],
        x_specs + [rows.mod_spec(d), _whole(gn.shape),
                   rows.rope_spec(cos.shape[1]), rows.rope_spec(sin.shape[1]), _whole(wqkv.shape)],
        [heads_rows, heads_rows, pl.BlockSpec((SWA_KV_HEADS, SWA_HEAD_DIM, tm), lambda i: (0, 0, i)),
         raw_spec, raw_spec],
        [jax.ShapeDtypeStruct((SWA_KV_HEADS, t, SWA_GW), BF16),
         jax.ShapeDtypeStruct((SWA_KV_HEADS, t, SWA_GW), BF16),
         jax.ShapeDtypeStruct((SWA_KV_HEADS, SWA_HEAD_DIM, t), BF16),
         jax.ShapeDtypeStruct((n_prompt, SWA_KV_HEADS, SWA_HEAD_DIM), F32),
         jax.ShapeDtypeStruct((n_prompt, SWA_KV_HEADS, SWA_HEAD_DIM), F32)],
        "swa_pre")


def _swa_heads(q_ref, sink_ref, o_ref, srcs, bias_ref, ot_buf, s_bufs):
    tq = q_ref.shape[1]
    n_keys = [kx_ref.shape[1] for kx_ref, _ in srcs]
    starts = [sum(n_keys[:i]) for i in range(len(srcs))]
    group = lax.broadcasted_iota(jnp.int32, (tq, SWA_GW), 1) // SWA_HEAD_DIM

    def put_scores(hq, s_buf):
        kvh, g = hq // SWA_GROUP, hq % SWA_GROUP
        q = jnp.where(group == g, q_ref[kvh].astype(F32), 0.0).astype(BF16)
        for (kx_ref, _), first, n in zip(srcs, starts, n_keys):
            s_buf[first:first + n, :] = _dot_nt(kx_ref[kvh], q)

    def finish(hq, s_buf):
        kvh, g = hq // SWA_GROUP, hq % SWA_GROUP
        s = s_buf[...]
        if bias_ref is not None:
            s = s + bias_ref[...]
        sink = jnp.full((1, 1), sink_ref[hq] * (1.0 / SWA_SCALE), F32)
        mx = jnp.maximum(jnp.max(s, axis=0, keepdims=True), sink)
        p = jnp.exp2((s - mx) * (SWA_SCALE * LOG2_E))
        den = jnp.sum(p, axis=0, keepdims=True) + jnp.exp2((sink - mx) * (SWA_SCALE * LOG2_E))
        p = p.astype(BF16)
        acc = None
        for (_, vt_ref), first, n in zip(srcs, starts, n_keys):
            a = _dot(vt_ref[kvh], p[first:first + n, :])
            acc = a if acc is None else acc + a
        ot_buf[kvh, pl.ds(pl.multiple_of(g * SWA_HEAD_DIM, SWA_HEAD_DIM), SWA_HEAD_DIM), :] = acc / den

    _pipelined_heads(SWA_HEADS, put_scores, finish, s_bufs)
    for kvh in range(SWA_KV_HEADS):
        o_ref[:, kvh * SWA_GW:(kvh + 1) * SWA_GW] = ot_buf[kvh].T.astype(BF16)


def _swa_attn_scratch(n_keys, tq):
    return [pltpu.VMEM((SWA_KV_HEADS, SWA_GW, tq), F32)] + [pltpu.VMEM((n_keys, tq), F32)] * 5


def _swa_attn_prompt_kernel(q_ref, kx_ref, vt_ref, sink_ref, o_ref, *, seq):
    per_step = q_ref.shape[1] // seq
    spans = [slice(s * seq, (s + 1) * seq) for s in range(per_step)]
    group = lax.broadcasted_iota(jnp.int32, (seq, SWA_GW), 1) // SWA_HEAD_DIM
    scores, values_t, sinks = [], [], []
    for sp in spans:
        for kvh in range(SWA_KV_HEADS):
            q_all = q_ref[kvh, sp, :].astype(F32)
            for g in range(SWA_GROUP):
                q = jnp.where(group == g, q_all, 0.0).astype(BF16)
                scores.append(_dot_nt(kx_ref[kvh, sp, :], q))
                values_t.append(vt_ref[kvh, :, sp])
                sinks.append(jnp.full((1, 1), sink_ref[kvh * SWA_GROUP + g] * (1.0 / SWA_SCALE), F32))
    outs = _staged_attention(scores, values_t, SWA_SCALE, sinks)
    for s, sp in enumerate(spans):
        for kvh in range(SWA_KV_HEADS):
            first = s * SWA_HEADS + kvh * SWA_GROUP
            ot = jnp.concatenate(outs[first:first + SWA_GROUP], axis=0)
            o_ref[sp, kvh * SWA_GW:(kvh + 1) * SWA_GW] = ot.T.astype(BF16)


def _swa_attn_prompt(q, kx, vt, sink, batch, seq):
    wq = SWA_HEADS * SWA_HEAD_DIM
    per_step = next(n for n in (4, 2, 1) if batch % n == 0)
    seq_rows = seq * per_step
    rows = pl.BlockSpec((SWA_KV_HEADS, seq_rows, SWA_GW), lambda b: (0, b, 0))
    return pl.pallas_call(
        functools.partial(_swa_attn_prompt_kernel, seq=seq),
        grid=(batch // per_step,),
        in_specs=[rows, rows, pl.BlockSpec((SWA_KV_HEADS, SWA_HEAD_DIM, seq_rows), lambda b: (0, 0, b)),
                  pl.BlockSpec(memory_space=pltpu.SMEM)],
        out_specs=pl.BlockSpec((seq_rows, wq), lambda b: (b, 0)),
        out_shape=jax.ShapeDtypeStruct((batch * seq, wq), BF16),
        compiler_params=_params("arbitrary"),
        name="swa_attn_prompt",
    )(q, kx, vt, sink)


def _swa_attn_latent_kernel(q_ref, kc_ref, vc_ref, kp_ref, vp_ref, km_ref, vm_ref, kn_ref, vn_ref,
                            sink_ref, o_ref, ot_buf, s_a, s_b, s_c, s_d, bias_ref, *, tq, dec_seq, past):
    i = pl.program_id(1)
    qpos = i * tq + lax.broadcasted_iota(jnp.int32, (1, tq), 1)

    def band(first, n):
        kpos = first + lax.broadcasted_iota(jnp.int32, (n, 1), 0)
        valid = (jnp.abs(qpos - kpos) <= SWA_WINDOW) & (kpos >= 0) & (kpos < dec_seq)
        return jnp.where(valid, 0.0, NEG_INF)

    bias_ref[...] = jnp.concatenate(
        [jnp.zeros((past, tq), F32), band(i * tq - SWA_WINDOW, SWA_WINDOW), band(i * tq, tq),
         band((i + 1) * tq, SWA_WINDOW)], axis=0)
    srcs = [(kc_ref, vc_ref), (kp_ref, vp_ref), (km_ref, vm_ref), (kn_ref, vn_ref)]
    _swa_heads(q_ref, sink_ref, o_ref, srcs, bias_ref, ot_buf, (s_a, s_b, s_c, s_d))


def _swa_attn_latent(q, kx, vt, kx_ctx, vt_ctx, sink, n_prompt, dec_batch, dec_seq, past, tq):
    t = q.shape[1]
    wq = SWA_HEADS * SWA_HEAD_DIM
    nq = dec_seq // tq
    w = SWA_WINDOW
    n_keys = past + tq + 2 * w
    first = lambda b, i: n_prompt + b * dec_seq + i * tq
    prev = lambda b, i: first(b, i) // w - 1
    nxt = lambda b, i: jnp.minimum((first(b, i) + tq) // w, t // w - 1)
    rows = lambda n, blk: pl.BlockSpec((SWA_KV_HEADS, n, SWA_GW), lambda b, i: (0, blk(b, i), 0))
    cols = lambda n, blk: pl.BlockSpec((SWA_KV_HEADS, SWA_HEAD_DIM, n), lambda b, i: (0, 0, blk(b, i)))
    main = lambda b, i: first(b, i) // tq
    ctx = lambda b, i: b
    return pl.pallas_call(
        functools.partial(_swa_attn_latent_kernel, tq=tq, dec_seq=dec_seq, past=past),
        grid=(dec_batch, nq),
        in_specs=[rows(tq, main), rows(past, ctx), cols(past, ctx), rows(w, prev), cols(w, prev),
                  rows(tq, main), cols(tq, main), rows(w, nxt), cols(w, nxt),
                  pl.BlockSpec(memory_space=pltpu.SMEM)],
        out_specs=pl.BlockSpec((tq, wq), lambda b, i: (b * nq + i, 0)),
        out_shape=jax.ShapeDtypeStruct((dec_batch * dec_seq, wq), BF16),
        scratch_shapes=_swa_attn_scratch(n_keys, tq),
        compiler_params=_params("arbitrary", "arbitrary"),
        name="swa_attn_latent",
    )(q, kx_ctx, vt_ctx, kx, vt, kx, vt, kx, vt, sink)


def _hgrn_pre_kernel(*refs, layer, pair, prompt_tiles, cast):
    x, _, refs = _take_rows(refs, pair, prompt_tiles)
    refs, cast_src, cast_dst = _split_cast(refs, 4, cast)
    mod_ref, gn_ref, w_ref, lbl_ref, q_ref, v_ref, g_ref, f_ref, qmax_ref = refs
    if cast is not None:
        cast.run(cast_src, cast_dst)
    m = mod_ref[0]
    h = _modulate(x, gn_ref[...], m[0:1], m[1:2]).astype(BF16)
    y = _dot(h, w_ref[...])
    n = HG_HEADS * HG_DK
    q = _silu(y[:, :n])
    q_ref[...] = q
    qmax_ref[0] = jnp.max(jnp.abs(q), axis=0, keepdims=True)
    v_ref[...] = y[:, n:2 * n].astype(BF16)
    g_ref[...] = _silu(y[:, 2 * n:3 * n])
    for d in range(2):
        logits = lbl_ref[d]
        e = jnp.exp(logits - jnp.max(logits, axis=0, keepdims=True))
        s = e / jnp.sum(e, axis=0, keepdims=True)
        cs = s[0:1]
        for r in range(1, layer + 1):
            cs = cs + s[r:r + 1]
        lb = cs - s[0:1]
        f_ref[d] = lb + (1.0 - lb) * _sigmoid(y[:, (3 + d) * n:(4 + d) * n])


def _hgrn_pre(x, mods, gn, w5, lb_logits, layer, rows, cast):
    xs, x_specs, pair, t, d = _rows_inputs(x, rows)
    n = HG_HEADS * HG_DK
    return _pre_call(
        functools.partial(_hgrn_pre_kernel, layer=layer, pair=pair, prompt_tiles=rows.prompt_tiles), cast, rows,
        xs + [mods, gn, w5, lb_logits],
        x_specs + [rows.mod_spec(d), _whole(gn.shape), _whole(w5.shape), _whole(lb_logits.shape)],
        [rows.row_spec(n), rows.row_spec(n), rows.row_spec(n),
         pl.BlockSpec((2, rows.tm, n), lambda i: (0, i, 0)), pl.BlockSpec((1, 1, n), lambda i: (i, 0, 0))],
        [jax.ShapeDtypeStruct((t, n), F32), jax.ShapeDtypeStruct((t, n), BF16),
         jax.ShapeDtypeStruct((t, n), F32), jax.ShapeDtypeStruct((2, t, n), F32),
         jax.ShapeDtypeStruct((rows.n_tiles, 1, n), F32)],
        "hgrn_pre")


def _tri_cumsum(tri, x):
    hi = x.astype(BF16)
    r1 = x - hi.astype(F32)
    mid = r1.astype(BF16)
    lo = (r1 - mid.astype(F32)).astype(BF16)
    return _dot(tri, hi) + _dot(tri, mid) + _dot(tri, lo)


def _hgrn_tile(q_ref, v_ref, f_ref, o_ref, st_ref, *, rows, reverse, bounded, slot=0, sfin_ref=None):
    r = rows
    diag = HG_DIAG_BLOCK if bounded else 1
    a = lax.broadcasted_iota(jnp.int32, (r, r), 0)
    b = lax.broadcasted_iota(jnp.int32, (r, r), 1)
    seen = (b >= a) if reverse else (b <= a)
    tri = jnp.where(seen, 1.0, 0.0).astype(BF16)

    q = q_ref[...]
    f = f_ref[slot]
    vb = v_ref[...].astype(BF16)
    lf = jnp.log(f)
    cum = _tri_cumsum(tri, lf)
    tot = cum[0:1, :] if reverse else cum[r - 1:r, :]
    kk = 1.0 - f
    q_in = (q * jnp.exp(cum)).astype(BF16)
    k_d = (kk * jnp.exp(tot - cum)).astype(BF16)
    e_tot = jnp.exp(tot)

    levels = []
    c = r // 2
    while c >= diag:
        q_half = (a % (2 * c) < c) if reverse else (a % (2 * c) >= c)
        k_half = (b % (2 * c) >= c) if reverse else (b % (2 * c) < c)
        levels.append((2 * c, c if reverse else c - 1, ((a // (2 * c)) == (b // (2 * c))) & q_half & k_half))
        c //= 2
    levels.append((diag, diag // 2, ((a // diag) == (b // diag)) & seen))

    def spread(rows_of_block, size):
        return jnp.concatenate([jnp.broadcast_to(row, (size, row.shape[1])) for row in rows_of_block], axis=0)

    factors = []
    for size, ref_row, own in levels:
        if size == 1:
            factors.append((q.astype(BF16), kk.astype(BF16), own))
            continue
        if size >= 8:
            ref = spread([cum[j * size + ref_row:j * size + ref_row + 1, :] for j in range(r // size)], size)
        else:
            ref = _tri_cumsum(jnp.where(b == (a // size) * size + ref_row, 1.0, 0.0).astype(BF16), cum)
        factors.append(((q * jnp.exp(cum - ref)).astype(BF16), (kk * jnp.exp(ref - cum)).astype(BF16), own))

    outs = []
    for h in range(HG_HEADS):
        sl = slice(h * HG_DK, (h + 1) * HG_DK)
        att = jnp.zeros((r, r), F32)
        for q_l, k_l, own in factors:
            att = jnp.where(own, _dot_nt(q_l[:, sl], k_l[:, sl]), att)
        intra = _dot(att.astype(BF16), vb[:, sl])
        if sfin_ref is not None:
            outs.append(intra)
            sfin_ref[0, slot, h] = _dot_tn(k_d[:, sl], vb[:, sl])
            continue
        st = st_ref[h]
        outs.append(_dot_nt(q_in[:, sl], st.astype(BF16)) + intra)
        st_ref[h] = st * e_tot[:, sl] + _dot_tn(vb[:, sl], k_d[:, sl])
    o_ref[slot] = jnp.concatenate(outs, axis=1)


def _hgrn_whole_kernel(fits_ref, q_ref, v_ref, f_ref, o_ref, sfin_ref, *, rows):
    for bounded in (True, False):
        @pl.when((fits_ref[0] != 0) == bounded)
        def _():
            for slot, reverse in enumerate((False, True)):
                _hgrn_tile(q_ref, v_ref, f_ref, o_ref, None, rows=rows, reverse=reverse, bounded=bounded,
                           slot=slot, sfin_ref=sfin_ref)


def _hgrn_scan_kernel(*refs, rows, has_init):
    if has_init:
        fits_ref, q_ref, v_ref, f_ref, s0_ref, o_ref, sfin_ref, st_ref = refs
    else:
        fits_ref, q_ref, v_ref, f_ref, o_ref, sfin_ref, st_ref = refs
    d = pl.program_id(1)
    t = pl.program_id(2)

    @pl.when(t == 0)
    def _():
        for h in range(HG_HEADS):
            st_ref[h] = s0_ref[0, 0, h].T if has_init else jnp.zeros((HG_DV, HG_DK), F32)

    for reverse in (False, True):
        for bounded in (True, False):
            @pl.when((d == int(reverse)) & ((fits_ref[0] != 0) == bounded))
            def _():
                _hgrn_tile(q_ref, v_ref, f_ref, o_ref, st_ref, rows=rows, reverse=reverse, bounded=bounded)

    @pl.when(t == pl.num_programs(2) - 1)
    def _():
        for h in range(HG_HEADS):
            sfin_ref[0, 0, h] = st_ref[h].T


def _hgrn_scan(fits, q, v, f2, batch, seq, first_row, rt, s0=None):
    n = q.shape[1]
    nt = seq // rt
    has_init = s0 is not None
    flag = pl.BlockSpec(memory_space=pltpu.SMEM)
    out_shape = [jax.ShapeDtypeStruct((2, batch * seq, n), F32),
                 jax.ShapeDtypeStruct((batch, 2, HG_HEADS, HG_DK, HG_DV), F32)]
    if nt == 1 and not has_init:
        rows_of = lambda b: (first_row // rt + b, 0)
        return pl.pallas_call(
            functools.partial(_hgrn_whole_kernel, rows=rt),
            grid=(batch,),
            in_specs=[flag, pl.BlockSpec((rt, n), rows_of), pl.BlockSpec((rt, n), rows_of),
                      pl.BlockSpec((2, rt, n), lambda b: (0, first_row // rt + b, 0))],
            out_specs=[pl.BlockSpec((2, rt, n), lambda b: (0, b, 0)),
                       pl.BlockSpec((1, 2, HG_HEADS, HG_DK, HG_DV), lambda b: (b, 0, 0, 0, 0))],
            out_shape=out_shape,
            compiler_params=_params("arbitrary"),
            name="hgrn_scan_whole",
        )(fits, q, v, f2)

    def local(b, d, i):
        return b * nt + jnp.where(d == 0, i, nt - 1 - i)

    def slab(b, d, i):
        return first_row // rt + local(b, d, i)

    row = pl.BlockSpec((rt, n), lambda b, d, i: (slab(b, d, i), 0))
    state = pl.BlockSpec((1, 1, HG_HEADS, HG_DK, HG_DV), lambda b, d, i: (b, d, 0, 0, 0))
    ins = [fits, q, v, f2]
    in_specs = [flag, row, row, pl.BlockSpec((1, rt, n), lambda b, d, i: (d, slab(b, d, i), 0))]
    if has_init:
        ins.append(s0)
        in_specs.append(state)
    return pl.pallas_call(
        functools.partial(_hgrn_scan_kernel, rows=rt, has_init=has_init),
        grid=(batch, 2, nt),
        in_specs=in_specs,
        out_specs=[pl.BlockSpec((1, rt, n), lambda b, d, i: (d, local(b, d, i), 0)), state],
        out_shape=[jax.ShapeDtypeStruct((2, batch * seq, n), F32),
                   jax.ShapeDtypeStruct((batch, 2, HG_HEADS, HG_DK, HG_DV), F32)],
        scratch_shapes=[pltpu.VMEM((HG_HEADS, HG_DV, HG_DK), F32)],
        compiler_params=_params("arbitrary", "arbitrary", "arbitrary"),
        name="hgrn_scan_latent" if has_init else "hgrn_scan_prompt",
    )(*ins)


def _post_kernel(*refs, hgrn, final, pair_in, pair_out, prompt_tiles):
    refs = list(refs)
    n_mix = 4 if hgrn else 2
    mix_refs, refs = refs[:n_mix], refs[n_mix:]
    x, is_prompt, refs = _take_rows(refs, pair_in, prompt_tiles)
    mod_ref, wo_ref, gn_ref, wg_ref, wu_ref, wd_ref = refs[:6]
    refs = list(refs[6:])
    fn_ref = refs.pop(0) if final else None
    if hgrn:
        odp_ref, odl_ref, g_ref, onorm_ref = mix_refs
        o2 = jnp.where(is_prompt, odp_ref[0] + odp_ref[1], odl_ref[0] + odl_ref[1])
        gate = g_ref[...]
        onorm = onorm_ref[...]
        parts = []
        for h in range(HG_HEADS):
            sl = slice(h * HG_DV, (h + 1) * HG_DV)
            parts.append(_rms(o2[:, sl], onorm) * gate[:, sl])
        o = jnp.concatenate(parts, axis=1).astype(BF16)
    else:
        o = jnp.where(is_prompt, mix_refs[0][...], mix_refs[1][...])
    m = mod_ref[0]
    x1 = x + m[2:3] * _dot(o, wo_ref[...])
    h2 = _modulate(x1, gn_ref[...], m[3:4], m[4:5]).astype(BF16)
    a = (_silu(_dot(h2, wg_ref[...])) * _dot(h2, wu_ref[...])).astype(BF16)
    x2 = x1 + m[5:6] * _dot(a, wd_ref[...])
    if final:
        x2 = _rms(x2, fn_ref[...])
    if pair_out:
        @pl.when(is_prompt)
        def _():
            refs[0][...] = x2

        @pl.when(jnp.logical_not(is_prompt))
        def _():
            refs[1][...] = x2
    else:
        refs[0][...] = x2


def _post(mix, x, mods, wo, gn, wg, wu, wd, rows, final_norm=None, hgrn=False, pair_out=False):
    xs, x_specs, pair_in, t, d = _rows_inputs(x, rows)
    tm, pt = rows.tm, rows.prompt_tiles
    if hgrn:
        od_p, od_l, gate, onorm = mix
        head = [od_p, od_l, gate, onorm]
        head_specs = [pl.BlockSpec((2, tm, d), lambda i: (0, jnp.minimum(i, pt - 1), 0)),
                      pl.BlockSpec((2, tm, d), lambda i: (0, jnp.maximum(i - pt, 0), 0)),
                      rows.row_spec(d), _whole(onorm.shape)]
    else:
        head = list(mix)
        head_specs = [rows.prompt_spec(mix[0].shape[1]), rows.latent_spec(mix[1].shape[1])]
    ins = head + xs + [mods, wo, gn, wg, wu, wd]
    in_specs = head_specs + x_specs + [rows.mod_spec(d)] + [_whole(a.shape) for a in (wo, gn, wg, wu, wd)]
    if final_norm is not None:
        ins.append(final_norm)
        in_specs.append(_whole(final_norm.shape))
    if pair_out:
        out_specs = rows.x_specs(d)
        out_shape = [jax.ShapeDtypeStruct((pt * tm, d), F32), jax.ShapeDtypeStruct((t - pt * tm, d), F32)]
    else:
        out_specs = rows.row_spec(d)
        out_shape = jax.ShapeDtypeStruct((t, d), F32)
    out = pl.pallas_call(
        functools.partial(_post_kernel, hgrn=hgrn, final=final_norm is not None, pair_in=pair_in, pair_out=pair_out,
                          prompt_tiles=pt),
        grid=(rows.n_tiles,),
        in_specs=in_specs,
        out_specs=out_specs,
        out_shape=out_shape,
        compiler_params=_params("arbitrary"),
        name="post",
    )(*ins)
    return tuple(out) if pair_out else out


def _pick_tile(n_prompt_rows, dec_seq, want):
    tm = want
    while n_prompt_rows % tm or dec_seq % tm:
        tm //= 2
    return tm


def kernel(x_prompt, x_sample, cache_mla_ckv, cache_mla_krope, state_hgrn, cache_swa_k, cache_swa_v, c, c_ctx, ada_w, ada_b, norm_mix, norm_ffn, ffn_w_gate, ffn_w_up, ffn_w_down, final_norm, mla_w_dq, mla_q_norm, mla_w_uq, mla_w_dkv, mla_kv_norm, mla_w_uk, mla_w_uv, mla_w_o, hg_w_q, hg_w_f, hg_w_i, hg_w_g, hg_o_norm, hg_w_o, hg_lb_logits, swa_w_q, swa_w_k, swa_w_v, swa_w_o, swa_sink):
    batch, seq, d = x_prompt.shape
    dec_batch, dec_seq, _ = x_sample.shape
    past = cache_mla_ckv.shape[2]
    depth = ada_w.shape[0]
    n_prompt = batch * seq
    n_rows = n_prompt + dec_batch * dec_seq
    assert dec_batch + 1 <= COND_ROWS and seq % SWA_WINDOW == 0 and dec_seq % (2 * SWA_WINDOW) == 0
    assert n_prompt % dec_seq == 0

    pre_rows = _Rows(n_prompt, dec_seq, n_rows, _pick_tile(n_prompt, dec_seq, 512))
    post_rows = _Rows(n_prompt, dec_seq, n_rows, _pick_tile(n_prompt, dec_seq, 512))
    hg_pre_rows = _Rows(n_prompt, dec_seq, n_rows, _pick_tile(n_prompt, dec_seq, 256))
    hg_post_rows = _Rows(n_prompt, dec_seq, n_rows, _pick_tile(n_prompt, dec_seq, 256))
    tq = _pick_tile(n_prompt, dec_seq, 256)
    cos64, sin64 = _rope_tables(dec_seq, pre_rows.tm)
    n_tab = cos64.shape[0]
    cos_mla = jnp.asarray(np.concatenate([cos64, np.ones((n_tab, ROPE_PERIOD), np.float32)], axis=1))
    sin_mla = jnp.asarray(np.concatenate([sin64, np.zeros((n_tab, ROPE_PERIOD), np.float32)], axis=1))
    cos_swa, sin_swa = jnp.asarray(np.tile(cos64, (1, 2))), jnp.asarray(np.tile(sin64, (1, 2)))

    cond = jnp.concatenate([c_ctx[None, :], c, jnp.zeros((COND_ROWS - 1 - dec_batch, d), F32)], axis=0)
    mods = _adaln(cond, ada_w, ada_b).reshape(depth, COND_ROWS, 6, d)

    x = (x_prompt.reshape(n_prompt, d), x_sample.reshape(dec_batch * dec_seq, d))
    ffn_stacks = (ffn_w_gate, ffn_w_up, ffn_w_down)
    row1 = lambda a: a.reshape(1, -1)
    new_ckv, new_krope, new_hg, new_k, new_v = [], [], [], [], []
    for i in range(depth):
        kind, j = i % N_MIXERS, i // N_MIXERS
        gn = row1(norm_mix[i])
        rows_i = hg_pre_rows if kind == 1 else pre_rows
        cast = _Cast(ffn_stacks, i, rows_i.n_tiles)
        if kind == 0:
            uq = mla_w_uq[j].reshape(-1, MLA_HEADS, MLA_NOPE_DIM + MLA_ROPE_DIM)
            uq_rope = jnp.pad(uq[:, :, MLA_NOPE_DIM:], ((0, 0), (0, 0), (0, LANES - MLA_ROPE_DIM)))
            uq = jnp.concatenate([uq[:, :, :MLA_NOPE_DIM].reshape(uq.shape[0], -1),
                                  uq_rope.reshape(uq.shape[0], -1)], axis=1)
            w = {
                "dq": mla_w_dq[j].astype(BF16), "q_norm": row1(mla_q_norm[j]), "uq": uq.astype(BF16),
                "dkv": jnp.pad(mla_w_dkv[j], ((0, 0), (0, LANES - MLA_ROPE_DIM))).astype(BF16),
                "kv_norm": row1(mla_kv_norm[j]),
                "uk": mla_w_uk[j].astype(BF16), "uv_t": mla_w_uv[j].T.astype(BF16),
            }
            (q, ckv, kr_raw, kcat, vt), ffn = _mla_pre(x, mods[i], gn, cos_mla, sin_mla, w, rows_i, cast)
            kr_ctx = jnp.pad(cache_mla_krope[:, j].reshape(dec_batch * past, -1), ((0, 0), (0, LANES - MLA_ROPE_DIM)))
            kcat_ctx, vt_ctx = _mla_expand(cache_mla_ckv[:, j].reshape(dec_batch * past, -1), kr_ctx, w["uk"], w["uv_t"])
            mix = (_mla_attn_prompt(q, kcat, vt, batch, seq),
                   _mla_attn_latent(q, kcat, vt, kcat_ctx, vt_ctx, n_prompt, dec_batch, dec_seq, past, tq))
            wo = mla_w_o[j].astype(BF16)
            new_ckv.append(ckv.reshape(batch, seq, -1))
            new_krope.append(kr_raw.reshape(batch, seq, -1))
        elif kind == 1:
            w5 = jnp.concatenate([hg_w_q[j], hg_w_i[j], hg_w_g[j], hg_w_f[j, 0], hg_w_f[j, 1]], axis=1).astype(BF16)
            (q, v, gate, f2, q_max), ffn = _hgrn_pre(x, mods[i], gn, w5, hg_lb_logits, i, rows_i, cast)
            rt = _pick_tile(seq, dec_seq, 256)
            sm = jax.nn.softmax(hg_lb_logits.astype(F32), axis=1)
            lb_min = jnp.min(jnp.cumsum(sm, axis=1)[:, i] - sm[:, 0])
            worst = (HG_DIAG_BLOCK // 2) * -jnp.log(lb_min) + jnp.log(jnp.maximum(jnp.max(q_max), 1.0))
            fits = (worst < HG_MAX_EXPONENT).astype(jnp.int32).reshape(1)
            od_p, s_prompt = _hgrn_scan(fits, q, v, f2, batch, seq, 0, rt)
            od_l, _ = _hgrn_scan(fits, q, v, f2, dec_batch, dec_seq, n_prompt, rt, s0=state_hgrn[:, j])
            mix = (od_p, od_l, gate, row1(hg_o_norm[j]))
            wo = hg_w_o[j].astype(BF16)
            new_hg.append(s_prompt)
        else:
            wqkv = jnp.concatenate([swa_w_q[j], swa_w_k[j], swa_w_v[j]], axis=1).astype(BF16)
            (q, kx, vt, k_raw, v_raw), ffn = _swa_pre(x, mods[i], gn, cos_swa, sin_swa, wqkv, rows_i, cast)
            sink = swa_sink[j]
            k_ctx = cache_swa_k[:, j].reshape(dec_batch * past, SWA_KV_HEADS, 1, SWA_HEAD_DIM).transpose(1, 0, 2, 3)
            kx_ctx = jnp.broadcast_to(k_ctx, (SWA_KV_HEADS, dec_batch * past, SWA_GROUP, SWA_HEAD_DIM))
            kx_ctx = kx_ctx.reshape(SWA_KV_HEADS, dec_batch * past, SWA_GW).astype(BF16)
            vt_ctx = cache_swa_v[:, j].reshape(dec_batch * past, SWA_KV_HEADS, SWA_HEAD_DIM).transpose(1, 2, 0).astype(BF16)
            mix = (_swa_attn_prompt(q, kx, vt, sink, batch, seq),
                   _swa_attn_latent(q, kx, vt, kx_ctx, vt_ctx, sink, n_prompt, dec_batch, dec_seq, past, tq))
            wo = swa_w_o[j].astype(BF16)
            new_k.append(k_raw.reshape(batch, seq, SWA_KV_HEADS, SWA_HEAD_DIM))
            new_v.append(v_raw.reshape(batch, seq, SWA_KV_HEADS, SWA_HEAD_DIM))
        if ffn is None:
            ffn = [a[i].astype(BF16) for a in ffn_stacks]
        x = _post(mix, x, mods[i], wo, row1(norm_ffn[i]), *ffn,
                  hg_post_rows if kind == 1 else post_rows,
                  final_norm=row1(final_norm) if i == depth - 1 else None, hgrn=kind == 1,
                  pair_out=i == depth - 1)
    y_prompt = x[0].reshape(batch, seq, d)
    y_sample = x[1].reshape(dec_batch, dec_seq, d)
    return (y_prompt, y_sample, jnp.stack(new_ckv, axis=1), jnp.stack(new_krope, axis=1),
            jnp.stack(new_hg, axis=1), jnp.stack(new_k, axis=1), jnp.stack(new_v, axis=1))
```

```python
import functools

import numpy as np
import jax
import jax.numpy as jnp
from jax import lax
from jax.experimental import pallas as pl
from jax.experimental.pallas import tpu as pltpu

F32 = jnp.float32
BF16 = jnp.bfloat16

GRID_W = 64
N_MIXERS = 3

MLA_HEADS = 8
MLA_KV_LORA = 256
MLA_NOPE_DIM = 128
MLA_ROPE_DIM = 64
MLA_V_DIM = 128
MLA_QK_PAD = 256
MLA_SCALE = (MLA_NOPE_DIM + MLA_ROPE_DIM) ** -0.5

HG_HEADS = 8
HG_DK = 128
HG_DV = 128
HG_DIAG_BLOCK = 32
HG_MAX_EXPONENT = 80.0

SWA_HEADS = 16
SWA_KV_HEADS = 4
SWA_GROUP = SWA_HEADS // SWA_KV_HEADS
SWA_HEAD_DIM = 64
SWA_GW = SWA_GROUP * SWA_HEAD_DIM
SWA_WINDOW = 128
SWA_SCALE = SWA_HEAD_DIM ** -0.5

ROPE_BASE = 10000.0
ROPE_PERIOD = 64
ROPE_QUARTER = 16
NORM_EPS = 1e-6
NEG_INF = -1e30
LOG2_E = 1.4426950408889634

LANES = 128
COND_ROWS = 8
VMEM_LIMIT = 56 * 1024 * 1024
ROW_TILE = 512
ROW_TILE_F32_HEAVY = 256
ATTN_TILE = 256


def _sigmoid(x):
    return jax.nn.sigmoid(x)


def _silu(x):
    return x * jax.nn.sigmoid(x)


def _rms(x, g):
    return x * lax.rsqrt(jnp.mean(x * x, axis=-1, keepdims=True) + NORM_EPS) * g


def _modulate(x, g, shift, scale):
    return _rms(x, g) * (1.0 + scale) + shift


def _dot(a, b):
    return jnp.dot(a, b, preferred_element_type=F32)


def _dot_nt(a, b):
    return lax.dot_general(a, b, (((1,), (1,)), ((), ())), preferred_element_type=F32)


def _dot_tn(a, b):
    return lax.dot_general(a, b, (((0,), (0,)), ((), ())), preferred_element_type=F32)


def _swap_pairs(x):
    n = x.shape[1]
    lane = lax.broadcasted_iota(jnp.int32, x.shape, 1)
    ahead = pltpu.roll(x, n - ROPE_QUARTER, 1)
    behind = pltpu.roll(x, ROPE_QUARTER, 1)
    return jnp.where((lane & (2 * ROPE_QUARTER - 1)) < ROPE_QUARTER, ahead, behind)


def _rope(x, cos, sin):
    reps = x.shape[1] // cos.shape[1]
    if reps > 1:
        cos = jnp.concatenate([cos] * reps, axis=1)
        sin = jnp.concatenate([sin] * reps, axis=1)
    return x * cos + _swap_pairs(x) * sin


def _whole(shape):
    zeros = (0,) * len(shape)
    return pl.BlockSpec(shape, lambda *_: zeros, pipeline_mode=pl.Buffered(1))


def _params(*sem):
    return pltpu.CompilerParams(dimension_semantics=sem, vmem_limit_bytes=VMEM_LIMIT)


class _Rows:
    def __init__(self, n_prompt_rows, dec_seq, n_rows, tm):
        assert n_prompt_rows % tm == 0 and dec_seq % tm == 0
        self.tm = tm
        self.n_tiles = n_rows // tm
        self.prompt_tiles = n_prompt_rows // tm
        self.seq_tiles = dec_seq // tm

    def cond(self, i):
        return jnp.where(i < self.prompt_tiles, 0, 1 + jnp.maximum(i - self.prompt_tiles, 0) // self.seq_tiles)

    def rope_block(self, i):
        return jnp.where(i < self.prompt_tiles, 0, 1 + jnp.maximum(i - self.prompt_tiles, 0) % self.seq_tiles)

    def row_spec(self, width):
        return pl.BlockSpec((self.tm, width), lambda i: (i, 0))

    def mod_spec(self, d):
        return pl.BlockSpec((1, 6, d), lambda i: (self.cond(i), 0, 0))

    def rope_spec(self, width):
        return pl.BlockSpec((self.tm, width), lambda i: (self.rope_block(i), 0))

    def x_specs(self, width):
        return [self.prompt_spec(width), self.latent_spec(width)]

    def prompt_spec(self, width):
        return pl.BlockSpec((self.tm, width), lambda i: (jnp.minimum(i, self.prompt_tiles - 1), 0))

    def latent_spec(self, width):
        return pl.BlockSpec((self.tm, width), lambda i: (jnp.maximum(i - self.prompt_tiles, 0), 0))


def _rope_tables(dec_seq, tm):
    pos = np.arange(dec_seq)
    row = (pos // GRID_W).astype(np.float32)
    col = (pos % GRID_W).astype(np.float32)
    inv_freq = (ROPE_BASE ** (-np.arange(ROPE_QUARTER, dtype=np.float32) / ROPE_QUARTER)).astype(np.float32)
    ang_r = row[:, None] * inv_freq[None, :]
    ang_c = col[:, None] * inv_freq[None, :]
    cos = np.concatenate([np.cos(ang_r), np.cos(ang_r), np.cos(ang_c), np.cos(ang_c)], axis=1)
    sin = np.concatenate([-np.sin(ang_r), np.sin(ang_r), -np.sin(ang_c), np.sin(ang_c)], axis=1)
    cos = np.concatenate([np.ones((tm, ROPE_PERIOD), np.float32), cos], axis=0)
    sin = np.concatenate([np.zeros((tm, ROPE_PERIOD), np.float32), sin], axis=0)
    return cos.astype(np.float32), sin.astype(np.float32)


def _adaln_kernel(c_ref, w_ref, b_ref, o_ref):
    o_ref[0] = _dot(_silu(c_ref[...]), w_ref[0]) + b_ref[0]


def _adaln(cond, ada_w, ada_b, tn=1536):
    depth, d, n = ada_w.shape
    return pl.pallas_call(
        _adaln_kernel,
        grid=(depth, n // tn),
        in_specs=[pl.BlockSpec((COND_ROWS, d), lambda l, j: (0, 0)),
                  pl.BlockSpec((1, d, tn), lambda l, j: (l, 0, j)),
                  pl.BlockSpec((1, 1, tn), lambda l, j: (l, 0, j))],
        out_specs=pl.BlockSpec((1, COND_ROWS, tn), lambda l, j: (l, 0, j)),
        out_shape=jax.ShapeDtypeStruct((depth, COND_ROWS, n), F32),
        compiler_params=_params("arbitrary", "arbitrary"),
        name="adaln",
    )(cond, ada_w, ada_b.reshape(depth, 1, n))


def _mla_store_heads(q_ref, kcat_ref, vt_ref, qn, qr_pad, kn, kr_pad, vt):
    for h in range(MLA_HEADS):
        nope = slice(h * MLA_NOPE_DIM, (h + 1) * MLA_NOPE_DIM)
        if q_ref is not None:
            q_ref[h] = jnp.concatenate([qn[:, nope], qr_pad[:, h * LANES:(h + 1) * LANES]], axis=1)
        kcat_ref[h] = jnp.concatenate([kn[:, nope], kr_pad], axis=1)
        vt_ref[h] = vt[h * MLA_V_DIM:(h + 1) * MLA_V_DIM, :]


def _read_rows(is_prompt, p_ref, l_ref):
    return jnp.where(is_prompt, p_ref[...], l_ref[...])


def _take_rows(refs, pair, prompt_tiles):
    is_prompt = pl.program_id(0) < prompt_tiles
    if pair:
        return _read_rows(is_prompt, refs[0], refs[1]), is_prompt, refs[2:]
    return refs[0][...], is_prompt, refs[1:]


def _rows_inputs(x, rows):
    if isinstance(x, tuple):
        d = x[0].shape[1]
        return list(x), rows.x_specs(d), True, x[0].shape[0] + x[1].shape[0], d
    return [x], [rows.row_spec(x.shape[1])], False, x.shape[0], x.shape[1]


class _Cast:
    def __init__(self, stacks, layer, n_steps):
        self.stacks, self.layer = stacks, layer
        self.chunks = n_steps // len(stacks)
        self.ok = (n_steps % len(stacks) == 0
                   and all(a.shape[1] % self.chunks == 0 and (a.shape[1] // self.chunks) % 16 == 0 for a in stacks))

    def _spec(self, k, a, with_layer):
        rows = a.shape[1] // self.chunks
        chunk = lambda i: jnp.clip(i - k * self.chunks, 0, self.chunks - 1)
        if with_layer:
            return pl.BlockSpec((1, rows, a.shape[2]), lambda i: (self.layer, chunk(i), 0))
        return pl.BlockSpec((rows, a.shape[2]), lambda i: (chunk(i), 0))

    def in_specs(self):
        return [self._spec(k, a, True) for k, a in enumerate(self.stacks)]

    def out_specs(self):
        return [self._spec(k, a, False) for k, a in enumerate(self.stacks)]

    def out_shape(self):
        return [jax.ShapeDtypeStruct(a.shape[1:], BF16) for a in self.stacks]

    def run(self, src_refs, dst_refs):
        turn = pl.program_id(0) // self.chunks
        for k, (src, dst) in enumerate(zip(src_refs, dst_refs)):
            @pl.when(turn == k)
            def _():
                dst[...] = src[0].astype(BF16)


def _split_cast(refs, n_in, cast):
    if cast is None:
        return refs, (), ()
    n = len(cast.stacks)
    return refs[:n_in] + refs[n_in + n:len(refs) - n], refs[n_in:n_in + n], refs[len(refs) - n:]


def _mla_pre_kernel(*refs, pair, prompt_tiles, cast):
    x, is_prompt, refs = _take_rows(refs, pair, prompt_tiles)
    refs, cast_src, cast_dst = _split_cast(refs, 11, cast)
    (mod_ref, gn_ref, cos_ref, sin_ref, wdq_ref, qnorm_ref, wuq_ref, wdkv_ref, kvnorm_ref, wuk_ref, wuvt_ref,
     q_ref, ckv_ref, krraw_ref, kcat_ref, vt_ref) = refs
    if cast is not None:
        cast.run(cast_src, cast_dst)
    m = mod_ref[0]
    h = _modulate(x, gn_ref[...], m[0:1], m[1:2]).astype(BF16)
    q_lat = _rms(_dot(h, wdq_ref[...]), qnorm_ref[...]).astype(BF16)
    cos, sin = cos_ref[...], sin_ref[...]
    nn = MLA_HEADS * MLA_NOPE_DIM
    qn = _dot(q_lat, wuq_ref[:, :nn]).astype(BF16)
    qr_pad = _rope(_dot(q_lat, wuq_ref[:, nn:]), cos, sin).astype(BF16)
    kv = _dot(h, wdkv_ref[...])
    ckv = _rms(kv[:, :MLA_KV_LORA], kvnorm_ref[...])
    kr = kv[:, MLA_KV_LORA:]

    @pl.when(is_prompt)
    def _():
        ckv_ref[...] = ckv
        krraw_ref[...] = kr[:, :MLA_ROPE_DIM]

    kr_pad = _rope(kr, cos, sin).astype(BF16)
    cb = ckv.astype(BF16)
    _mla_store_heads(q_ref, kcat_ref, vt_ref, qn, qr_pad, _dot(cb, wuk_ref[...]).astype(BF16), kr_pad,
                     _dot_nt(wuvt_ref[...], cb).astype(BF16))


def _pre_call(kernel, cast, rows, ins, in_specs, out_specs, out_shape, name):
    if cast is not None and cast.ok:
        ins, in_specs = ins + list(cast.stacks), in_specs + cast.in_specs()
        out_specs, out_shape = out_specs + cast.out_specs(), out_shape + cast.out_shape()
    else:
        cast = None
    outs = pl.pallas_call(
        functools.partial(kernel, cast=cast),
        grid=(rows.n_tiles,),
        in_specs=in_specs,
        out_specs=out_specs,
        out_shape=out_shape,
        compiler_params=_params("arbitrary"),
        name=name,
    )(*ins)
    if cast is None:
        return outs, None
    n = len(cast.stacks)
    return outs[:-n], outs[-n:]


def _mla_pre(x, mods, gn, cos, sin, w, rows, cast):
    xs, x_specs, pair, t, d = _rows_inputs(x, rows)
    tm = rows.tm
    n_prompt = rows.prompt_tiles * tm
    weights = [w["dq"], w["q_norm"], w["uq"], w["dkv"], w["kv_norm"], w["uk"], w["uv_t"]]
    ins = xs + [mods, gn, cos, sin] + weights
    in_specs = x_specs + [rows.mod_spec(d), _whole(gn.shape),
                          rows.rope_spec(cos.shape[1]), rows.rope_spec(sin.shape[1])]
    in_specs += [_whole(a.shape) for a in weights]
    heads_rows = pl.BlockSpec((MLA_HEADS, tm, MLA_QK_PAD), lambda i: (0, i, 0))
    return _pre_call(
        functools.partial(_mla_pre_kernel, pair=pair, prompt_tiles=rows.prompt_tiles), cast, rows, ins, in_specs,
        [heads_rows, rows.prompt_spec(MLA_KV_LORA), rows.prompt_spec(MLA_ROPE_DIM), heads_rows,
         pl.BlockSpec((MLA_HEADS, MLA_V_DIM, tm), lambda i: (0, 0, i))],
        [jax.ShapeDtypeStruct((MLA_HEADS, t, MLA_QK_PAD), BF16),
         jax.ShapeDtypeStruct((n_prompt, MLA_KV_LORA), F32),
         jax.ShapeDtypeStruct((n_prompt, MLA_ROPE_DIM), F32),
         jax.ShapeDtypeStruct((MLA_HEADS, t, MLA_QK_PAD), BF16),
         jax.ShapeDtypeStruct((MLA_HEADS, MLA_V_DIM, t), BF16)],
        "mla_pre")


def _mla_expand_kernel(c_ref, kr_ref, wuk_ref, wuvt_ref, kcat_ref, vt_ref):
    cb = c_ref[...].astype(BF16)
    _mla_store_heads(None, kcat_ref, vt_ref, None, None, _dot(cb, wuk_ref[...]).astype(BF16),
                     kr_ref[...].astype(BF16), _dot_nt(wuvt_ref[...], cb).astype(BF16))


def _mla_expand(ckv, kr_pad, wuk, wuvt):
    n = ckv.shape[0]
    return pl.pallas_call(
        _mla_expand_kernel,
        out_shape=[jax.ShapeDtypeStruct((MLA_HEADS, n, MLA_QK_PAD), BF16),
                   jax.ShapeDtypeStruct((MLA_HEADS, MLA_V_DIM, n), BF16)],
        compiler_params=pltpu.CompilerParams(vmem_limit_bytes=VMEM_LIMIT),
        name="mla_expand",
    )(ckv, kr_pad, wuk, wuvt)


def _mla_attn_kernel(*refs, n_src):
    q_ref = refs[0]
    srcs = [(refs[1 + 2 * i], refs[2 + 2 * i]) for i in range(n_src)]
    o_ref, o_buf = refs[1 + 2 * n_src:3 + 2 * n_src]
    s_bufs = refs[3 + 2 * n_src:]
    n_keys = [k_ref.shape[1] for k_ref, _ in srcs]
    starts = [sum(n_keys[:i]) for i in range(n_src)]

    def put_scores(h, s_buf):
        q = q_ref[h]
        for (k_ref, _), first, n in zip(srcs, starts, n_keys):
            s_buf[first:first + n, :] = _dot_nt(k_ref[h], q)

    def finish(h, s_buf):
        s = s_buf[...]
        mx = jnp.max(s, axis=0, keepdims=True)
        p = jnp.exp2((s - mx) * (MLA_SCALE * LOG2_E))
        den = jnp.sum(p, axis=0, keepdims=True)
        p = p.astype(BF16)
        acc = None
        for (_, vt_ref), first, n in zip(srcs, starts, n_keys):
            a = _dot(vt_ref[h], p[first:first + n, :])
            acc = a if acc is None else acc + a
        o_buf[h] = (acc / den).T.astype(BF16)

    _pipelined_heads(MLA_HEADS, put_scores, finish, s_bufs)
    for h in range(MLA_HEADS):
        o_ref[:, h * MLA_V_DIM:(h + 1) * MLA_V_DIM] = o_buf[h]


def _mla_attn_scratch(n_keys, tq):
    return [pltpu.VMEM((MLA_HEADS, tq, MLA_V_DIM), BF16)] + [pltpu.VMEM((n_keys, tq), F32)] * 4


def _pipelined_heads(n_heads, put_scores, finish, bufs):
    a, b, c, d = bufs
    assert n_heads % 4 == 0
    put_scores(0, a)
    put_scores(1, b)

    def quad(j, carry):
        h = 4 * j
        put_scores(h + 2, c)
        put_scores(h + 3, d)
        finish(h, a)
        finish(h + 1, b)
        put_scores(h + 4, a)
        put_scores(h + 5, b)
        finish(h + 2, c)
        finish(h + 3, d)
        return carry

    lax.fori_loop(0, n_heads // 4 - 1, quad, 0)
    h = n_heads - 4
    put_scores(h + 2, c)
    put_scores(h + 3, d)
    finish(h, a)
    finish(h + 1, b)
    finish(h + 2, c)
    finish(h + 3, d)


def _staged_attention(scores, values_t, scale, sinks=None):
    c = scale * LOG2_E
    mx = [jnp.max(s, axis=0, keepdims=True) for s in scores]
    if sinks is not None:
        mx = [jnp.maximum(m, z) for m, z in zip(mx, sinks)]
    p = [jnp.exp2((s - m) * c) for s, m in zip(scores, mx)]
    den = [jnp.sum(x, axis=0, keepdims=True) for x in p]
    if sinks is not None:
        den = [d + jnp.exp2((z - m) * c) for d, z, m in zip(den, sinks, mx)]
    acc = [_dot(v, x.astype(BF16)) for v, x in zip(values_t, p)]
    return [a / d for a, d in zip(acc, den)]


def _mla_attn_prompt_kernel(q_ref, k_ref, vt_ref, o_ref, *, seq):
    per_step = q_ref.shape[1] // seq
    spans = [slice(s * seq, (s + 1) * seq) for s in range(per_step)]
    scores = [_dot_nt(k_ref[h, sp, :], q_ref[h, sp, :]) for sp in spans for h in range(MLA_HEADS)]
    values = [vt_ref[h, :, sp] for sp in spans for h in range(MLA_HEADS)]
    outs = _staged_attention(scores, values, MLA_SCALE)
    for s, sp in enumerate(spans):
        for h in range(MLA_HEADS):
            o_ref[sp, h * MLA_V_DIM:(h + 1) * MLA_V_DIM] = outs[s * MLA_HEADS + h].T.astype(BF16)


def _mla_attn_prompt(q, kcat, vt, batch, seq):
    hv = MLA_HEADS * MLA_V_DIM
    per_step = next(n for n in (4, 2, 1) if batch % n == 0)
    seq_rows = seq * per_step
    rows = pl.BlockSpec((MLA_HEADS, seq_rows, MLA_QK_PAD), lambda b: (0, b, 0))
    return pl.pallas_call(
        functools.partial(_mla_attn_prompt_kernel, seq=seq),
        grid=(batch // per_step,),
        in_specs=[rows, rows, pl.BlockSpec((MLA_HEADS, MLA_V_DIM, seq_rows), lambda b: (0, 0, b))],
        out_specs=pl.BlockSpec((seq_rows, hv), lambda b: (b, 0)),
        out_shape=jax.ShapeDtypeStruct((batch * seq, hv), BF16),
        compiler_params=_params("arbitrary"),
        name="mla_attn_prompt",
    )(q, kcat, vt)


def _mla_attn_latent(q, kcat, vt, kcat_ctx, vt_ctx, n_prompt, dec_batch, dec_seq, past, tq):
    hv = MLA_HEADS * MLA_V_DIM
    nq = dec_seq // tq
    lat_blk = n_prompt // dec_seq
    return pl.pallas_call(
        functools.partial(_mla_attn_kernel, n_src=2),
        grid=(dec_batch, nq),
        in_specs=[pl.BlockSpec((MLA_HEADS, tq, MLA_QK_PAD), lambda b, i: (0, n_prompt // tq + b * nq + i, 0)),
                  pl.BlockSpec((MLA_HEADS, past, MLA_QK_PAD), lambda b, i: (0, b, 0)),
                  pl.BlockSpec((MLA_HEADS, MLA_V_DIM, past), lambda b, i: (0, 0, b)),
                  pl.BlockSpec((MLA_HEADS, dec_seq, MLA_QK_PAD), lambda b, i: (0, lat_blk + b, 0)),
                  pl.BlockSpec((MLA_HEADS, MLA_V_DIM, dec_seq), lambda b, i: (0, 0, lat_blk + b))],
        out_specs=pl.BlockSpec((tq, hv), lambda b, i: (b * nq + i, 0)),
        out_shape=jax.ShapeDtypeStruct((dec_batch * dec_seq, hv), BF16),
        scratch_shapes=_mla_attn_scratch(past + dec_seq, tq),
        compiler_params=_params("arbitrary", "arbitrary"),
        name="mla_attn_latent",
    )(q, kcat_ctx, vt_ctx, kcat, vt)


def _tile_heads(x):
    n = x.shape[1]
    block = lax.broadcasted_iota(jnp.int32, x.shape, 1) // SWA_HEAD_DIM
    rolled = [x] + [pltpu.roll(x, s * SWA_HEAD_DIM, 1) for s in range(1, SWA_KV_HEADS)]
    out = []
    for kvh in range(SWA_KV_HEADS):
        blk = rolled[(0 - kvh) % SWA_KV_HEADS]
        for g in range(1, n // SWA_HEAD_DIM):
            blk = jnp.where(block == g, rolled[(g - kvh) % SWA_KV_HEADS], blk)
        out.append(blk)
    return jnp.concatenate(out, axis=1)


def _swa_pre_kernel(*refs, pair, prompt_tiles, cast):
    x, is_prompt, refs = _take_rows(refs, pair, prompt_tiles)
    refs, cast_src, cast_dst = _split_cast(refs, 5, cast)
    mod_ref, gn_ref, cos_ref, sin_ref, wqkv_ref, q_ref, kx_ref, vt_ref, kraw_ref, vraw_ref = refs
    if cast is not None:
        cast.run(cast_src, cast_dst)
    m = mod_ref[0]
    h = _modulate(x, gn_ref[...], m[0:1], m[1:2]).astype(BF16)
    qkv = _dot(h, wqkv_ref[...])
    nq, nk = SWA_HEADS * SWA_HEAD_DIM, SWA_KV_HEADS * SWA_HEAD_DIM
    cos, sin = cos_ref[...], sin_ref[...]
    q = _rope(qkv[:, :nq], cos, sin).astype(BF16)
    k = qkv[:, nq:nq + nk]
    v = qkv[:, nq + nk:]

    @pl.when(is_prompt)
    def _():
        kraw_ref[...] = k.reshape(kraw_ref.shape)
        vraw_ref[...] = v.reshape(vraw_ref.shape)

    kx = _tile_heads(_rope(k, cos, sin)).astype(BF16)
    vt = v.T.astype(BF16)
    for kvh in range(SWA_KV_HEADS):
        q_ref[kvh] = q[:, kvh * SWA_GW:(kvh + 1) * SWA_GW]
        kx_ref[kvh] = kx[:, kvh * SWA_GW:(kvh + 1) * SWA_GW]
        vt_ref[kvh] = vt[kvh * SWA_HEAD_DIM:(kvh + 1) * SWA_HEAD_DIM, :]


def _swa_pre(x, mods, gn, cos, sin, wqkv, rows, cast):
    xs, x_specs, pair, t, d = _rows_inputs(x, rows)
    tm = rows.tm
    n_prompt = rows.prompt_tiles * tm
    nk = SWA_KV_HEADS * SWA_HEAD_DIM
    heads_rows = pl.BlockSpec((SWA_KV_HEADS, tm, SWA_GW), lambda i: (0, i, 0))
    raw_spec = pl.BlockSpec((tm, SWA_KV_HEADS, SWA_HEAD_DIM), lambda i: (jnp.minimum(i, rows.prompt_tiles - 1), 0, 0))
    return _pre_call(
        functools.partial(_swa_pre_kernel, pair=pair, prompt_tiles=rows.prompt_tiles), cast, rows,
        xs + [mods, gn, cos, sin, wqkv],
        x_specs + [rows.mod_spec(d), _whole(gn.shape),
                   rows.rope_spec(cos.shape[1]), rows.rope_spec(sin.shape[1]), _whole(wqkv.shape)],
        [heads_rows, heads_rows, pl.BlockSpec((SWA_KV_HEADS, SWA_HEAD_DIM, tm), lambda i: (0, 0, i)),
         raw_spec, raw_spec],
        [jax.ShapeDtypeStruct((SWA_KV_HEADS, t, SWA_GW), BF16),
         jax.ShapeDtypeStruct((SWA_KV_HEADS, t, SWA_GW), BF16),
         jax.ShapeDtypeStruct((SWA_KV_HEADS, SWA_HEAD_DIM, t), BF16),
         jax.ShapeDtypeStruct((n_prompt, SWA_KV_HEADS, SWA_HEAD_DIM), F32),
         jax.ShapeDtypeStruct((n_prompt, SWA_KV_HEADS, SWA_HEAD_DIM), F32)],
        "swa_pre")


def _swa_heads(q_ref, sink_ref, o_ref, srcs, bias_ref, ot_buf, s_bufs):
    tq = q_ref.shape[1]
    n_keys = [kx_ref.shape[1] for kx_ref, _ in srcs]
    starts = [sum(n_keys[:i]) for i in range(len(srcs))]
    group = lax.broadcasted_iota(jnp.int32, (tq, SWA_GW), 1) // SWA_HEAD_DIM

    def put_scores(hq, s_buf):
        kvh, g = hq // SWA_GROUP, hq % SWA_GROUP
        q = jnp.where(group == g, q_ref[kvh].astype(F32), 0.0).astype(BF16)
        for (kx_ref, _), first, n in zip(srcs, starts, n_keys):
            s_buf[first:first + n, :] = _dot_nt(kx_ref[kvh], q)

    def finish(hq, s_buf):
        kvh, g = hq // SWA_GROUP, hq % SWA_GROUP
        s = s_buf[...]
        if bias_ref is not None:
            s = s + bias_ref[...]
        sink = jnp.full((1, 1), sink_ref[hq] * (1.0 / SWA_SCALE), F32)
        mx = jnp.maximum(jnp.max(s, axis=0, keepdims=True), sink)
        p = jnp.exp2((s - mx) * (SWA_SCALE * LOG2_E))
        den = jnp.sum(p, axis=0, keepdims=True) + jnp.exp2((sink - mx) * (SWA_SCALE * LOG2_E))
        p = p.astype(BF16)
        acc = None
        for (_, vt_ref), first, n in zip(srcs, starts, n_keys):
            a = _dot(vt_ref[kvh], p[first:first + n, :])
            acc = a if acc is None else acc + a
        ot_buf[kvh, pl.ds(pl.multiple_of(g * SWA_HEAD_DIM, SWA_HEAD_DIM), SWA_HEAD_DIM), :] = acc / den

    _pipelined_heads(SWA_HEADS, put_scores, finish, s_bufs)
    for kvh in range(SWA_KV_HEADS):
        o_ref[:, kvh * SWA_GW:(kvh + 1) * SWA_GW] = ot_buf[kvh].T.astype(BF16)


def _swa_attn_scratch(n_keys, tq):
    return [pltpu.VMEM((SWA_KV_HEADS, SWA_GW, tq), F32)] + [pltpu.VMEM((n_keys, tq), F32)] * 5


def _swa_attn_prompt_kernel(q_ref, kx_ref, vt_ref, sink_ref, o_ref, *, seq):
    per_step = q_ref.shape[1] // seq
    spans = [slice(s * seq, (s + 1) * seq) for s in range(per_step)]
    group = lax.broadcasted_iota(jnp.int32, (seq, SWA_GW), 1) // SWA_HEAD_DIM
    scores, values_t, sinks = [], [], []
    for sp in spans:
        for kvh in range(SWA_KV_HEADS):
            q_all = q_ref[kvh, sp, :].astype(F32)
            for g in range(SWA_GROUP):
                q = jnp.where(group == g, q_all, 0.0).astype(BF16)
                scores.append(_dot_nt(kx_ref[kvh, sp, :], q))
                values_t.append(vt_ref[kvh, :, sp])
                sinks.append(jnp.full((1, 1), sink_ref[kvh * SWA_GROUP + g] * (1.0 / SWA_SCALE), F32))
    outs = _staged_attention(scores, values_t, SWA_SCALE, sinks)
    for s, sp in enumerate(spans):
        for kvh in range(SWA_KV_HEADS):
            first = s * SWA_HEADS + kvh * SWA_GROUP
            ot = jnp.concatenate(outs[first:first + SWA_GROUP], axis=0)
            o_ref[sp, kvh * SWA_GW:(kvh + 1) * SWA_GW] = ot.T.astype(BF16)


def _swa_attn_prompt(q, kx, vt, sink, batch, seq):
    wq = SWA_HEADS * SWA_HEAD_DIM
    per_step = next(n for n in (4, 2, 1) if batch % n == 0)
    seq_rows = seq * per_step
    rows = pl.BlockSpec((SWA_KV_HEADS, seq_rows, SWA_GW), lambda b: (0, b, 0))
    return pl.pallas_call(
        functools.partial(_swa_attn_prompt_kernel, seq=seq),
        grid=(batch // per_step,),
        in_specs=[rows, rows, pl.BlockSpec((SWA_KV_HEADS, SWA_HEAD_DIM, seq_rows), lambda b: (0, 0, b)),
                  pl.BlockSpec(memory_space=pltpu.SMEM)],
        out_specs=pl.BlockSpec((seq_rows, wq), lambda b: (b, 0)),
        out_shape=jax.ShapeDtypeStruct((batch * seq, wq), BF16),
        compiler_params=_params("arbitrary"),
        name="swa_attn_prompt",
    )(q, kx, vt, sink)


def _swa_attn_latent_kernel(q_ref, kc_ref, vc_ref, kp_ref, vp_ref, km_ref, vm_ref, kn_ref, vn_ref,
                            sink_ref, o_ref, ot_buf, s_a, s_b, s_c, s_d, bias_ref, kxc_buf, vtc_buf, *,
                            tq, dec_seq, past):
    i = pl.program_id(1)
    qpos = i * tq + lax.broadcasted_iota(jnp.int32, (1, tq), 1)

    @pl.when(i == 0)
    def _():
        kx = _tile_heads(kc_ref[...]).astype(BF16)
        vt = vc_ref[...].T.astype(BF16)
        for kvh in range(SWA_KV_HEADS):
            kxc_buf[kvh] = kx[:, kvh * SWA_GW:(kvh + 1) * SWA_GW]
            vtc_buf[kvh] = vt[kvh * SWA_HEAD_DIM:(kvh + 1) * SWA_HEAD_DIM, :]

    def band(first, n):
        kpos = first + lax.broadcasted_iota(jnp.int32, (n, 1), 0)
        valid = (jnp.abs(qpos - kpos) <= SWA_WINDOW) & (kpos >= 0) & (kpos < dec_seq)
        return jnp.where(valid, 0.0, NEG_INF)

    bias_ref[...] = jnp.concatenate(
        [jnp.zeros((past, tq), F32), band(i * tq - SWA_WINDOW, SWA_WINDOW), band(i * tq, tq),
         band((i + 1) * tq, SWA_WINDOW)], axis=0)
    srcs = [(kxc_buf, vtc_buf), (kp_ref, vp_ref), (km_ref, vm_ref), (kn_ref, vn_ref)]
    _swa_heads(q_ref, sink_ref, o_ref, srcs, bias_ref, ot_buf, (s_a, s_b, s_c, s_d))


def _swa_attn_latent(q, kx, vt, k_ctx, v_ctx, sink, n_prompt, dec_batch, dec_seq, past, tq):
    t = q.shape[1]
    wq = SWA_HEADS * SWA_HEAD_DIM
    nq = dec_seq // tq
    w = SWA_WINDOW
    n_keys = past + tq + 2 * w
    first = lambda b, i: n_prompt + b * dec_seq + i * tq
    prev = lambda b, i: first(b, i) // w - 1
    nxt = lambda b, i: jnp.minimum((first(b, i) + tq) // w, t // w - 1)
    rows = lambda n, blk: pl.BlockSpec((SWA_KV_HEADS, n, SWA_GW), lambda b, i: (0, blk(b, i), 0))
    cols = lambda n, blk: pl.BlockSpec((SWA_KV_HEADS, SWA_HEAD_DIM, n), lambda b, i: (0, 0, blk(b, i)))
    main = lambda b, i: first(b, i) // tq
    ctx = pl.BlockSpec((past, k_ctx.shape[1]), lambda b, i: (b, 0))
    return pl.pallas_call(
        functools.partial(_swa_attn_latent_kernel, tq=tq, dec_seq=dec_seq, past=past),
        grid=(dec_batch, nq),
        in_specs=[rows(tq, main), ctx, ctx, rows(w, prev), cols(w, prev),
                  rows(tq, main), cols(tq, main), rows(w, nxt), cols(w, nxt),
                  pl.BlockSpec(memory_space=pltpu.SMEM)],
        out_specs=pl.BlockSpec((tq, wq), lambda b, i: (b * nq + i, 0)),
        out_shape=jax.ShapeDtypeStruct((dec_batch * dec_seq, wq), BF16),
        scratch_shapes=_swa_attn_scratch(n_keys, tq) + [pltpu.VMEM((SWA_KV_HEADS, past, SWA_GW), BF16),
                                                        pltpu.VMEM((SWA_KV_HEADS, SWA_HEAD_DIM, past), BF16)],
        compiler_params=_params("arbitrary", "arbitrary"),
        name="swa_attn_latent",
    )(q, k_ctx, v_ctx, kx, vt, kx, vt, kx, vt, sink)


def _hgrn_pre_kernel(*refs, layer, pair, prompt_tiles, cast):
    x, _, refs = _take_rows(refs, pair, prompt_tiles)
    refs, cast_src, cast_dst = _split_cast(refs, 4, cast)
    mod_ref, gn_ref, w_ref, lbl_ref, q_ref, v_ref, g_ref, f_ref, qmax_ref = refs
    if cast is not None:
        cast.run(cast_src, cast_dst)
    m = mod_ref[0]
    h = _modulate(x, gn_ref[...], m[0:1], m[1:2]).astype(BF16)
    y = _dot(h, w_ref[...])
    n = HG_HEADS * HG_DK
    q = _silu(y[:, :n])
    q_ref[...] = q
    qmax_ref[0] = jnp.max(jnp.abs(q), axis=0, keepdims=True)
    v_ref[...] = y[:, n:2 * n].astype(BF16)
    g_ref[...] = _silu(y[:, 2 * n:3 * n])
    for d in range(2):
        logits = lbl_ref[d]
        e = jnp.exp(logits - jnp.max(logits, axis=0, keepdims=True))
        s = e / jnp.sum(e, axis=0, keepdims=True)
        cs = s[0:1]
        for r in range(1, layer + 1):
            cs = cs + s[r:r + 1]
        lb = cs - s[0:1]
        f_ref[d] = lb + (1.0 - lb) * _sigmoid(y[:, (3 + d) * n:(4 + d) * n])


def _hgrn_pre(x, mods, gn, w5, lb_logits, layer, rows, cast):
    xs, x_specs, pair, t, d = _rows_inputs(x, rows)
    n = HG_HEADS * HG_DK
    return _pre_call(
        functools.partial(_hgrn_pre_kernel, layer=layer, pair=pair, prompt_tiles=rows.prompt_tiles), cast, rows,
        xs + [mods, gn, w5, lb_logits],
        x_specs + [rows.mod_spec(d), _whole(gn.shape), _whole(w5.shape), _whole(lb_logits.shape)],
        [rows.row_spec(n), rows.row_spec(n), rows.row_spec(n),
         pl.BlockSpec((2, rows.tm, n), lambda i: (0, i, 0)), pl.BlockSpec((1, 1, n), lambda i: (i, 0, 0))],
        [jax.ShapeDtypeStruct((t, n), F32), jax.ShapeDtypeStruct((t, n), BF16),
         jax.ShapeDtypeStruct((t, n), F32), jax.ShapeDtypeStruct((2, t, n), F32),
         jax.ShapeDtypeStruct((rows.n_tiles, 1, n), F32)],
        "hgrn_pre")


def _tri_cumsum(tri, x):
    hi = x.astype(BF16)
    r1 = x - hi.astype(F32)
    mid = r1.astype(BF16)
    lo = (r1 - mid.astype(F32)).astype(BF16)
    return _dot(tri, hi) + _dot(tri, mid) + _dot(tri, lo)


def _hgrn_tile(q_ref, v_ref, f_ref, o_ref, st_ref, *, rows, reverse, bounded, slot=0, sfin_ref=None):
    r = rows
    diag = HG_DIAG_BLOCK if bounded else 1
    a = lax.broadcasted_iota(jnp.int32, (r, r), 0)
    b = lax.broadcasted_iota(jnp.int32, (r, r), 1)
    seen = (b >= a) if reverse else (b <= a)
    tri = jnp.where(seen, 1.0, 0.0).astype(BF16)

    q = q_ref[...]
    f = f_ref[slot]
    vb = v_ref[...].astype(BF16)
    lf = jnp.log(f)
    cum = _tri_cumsum(tri, lf)
    tot = cum[0:1, :] if reverse else cum[r - 1:r, :]
    kk = 1.0 - f
    q_in = (q * jnp.exp(cum)).astype(BF16)
    k_d = (kk * jnp.exp(tot - cum)).astype(BF16)
    e_tot = jnp.exp(tot)

    levels = []
    c = r // 2
    while c >= diag:
        q_half = (a % (2 * c) < c) if reverse else (a % (2 * c) >= c)
        k_half = (b % (2 * c) >= c) if reverse else (b % (2 * c) < c)
        levels.append((2 * c, c if reverse else c - 1, ((a // (2 * c)) == (b // (2 * c))) & q_half & k_half))
        c //= 2
    levels.append((diag, diag // 2, ((a // diag) == (b // diag)) & seen))

    def spread(rows_of_block, size):
        return jnp.concatenate([jnp.broadcast_to(row, (size, row.shape[1])) for row in rows_of_block], axis=0)

    factors = []
    for size, ref_row, own in levels:
        if size == 1:
            factors.append((q.astype(BF16), kk.astype(BF16), own))
            continue
        if size >= 8:
            ref = spread([cum[j * size + ref_row:j * size + ref_row + 1, :] for j in range(r // size)], size)
        else:
            ref = _tri_cumsum(jnp.where(b == (a // size) * size + ref_row, 1.0, 0.0).astype(BF16), cum)
        factors.append(((q * jnp.exp(cum - ref)).astype(BF16), (kk * jnp.exp(ref - cum)).astype(BF16), own))

    outs = []
    for h in range(HG_HEADS):
        sl = slice(h * HG_DK, (h + 1) * HG_DK)
        att = jnp.zeros((r, r), F32)
        for q_l, k_l, own in factors:
            att = jnp.where(own, _dot_nt(q_l[:, sl], k_l[:, sl]), att)
        intra = _dot(att.astype(BF16), vb[:, sl])
        if sfin_ref is not None:
            outs.append(intra)
            sfin_ref[0, slot, h] = _dot_tn(k_d[:, sl], vb[:, sl])
            continue
        st = st_ref[h]
        outs.append(_dot_nt(q_in[:, sl], st.astype(BF16)) + intra)
        st_ref[h] = st * e_tot[:, sl] + _dot_tn(vb[:, sl], k_d[:, sl])
    o_ref[slot] = jnp.concatenate(outs, axis=1)


def _hgrn_whole_kernel(fits_ref, q_ref, v_ref, f_ref, o_ref, sfin_ref, *, rows):
    for bounded in (True, False):
        @pl.when((fits_ref[0] != 0) == bounded)
        def _():
            for slot, reverse in enumerate((False, True)):
                _hgrn_tile(q_ref, v_ref, f_ref, o_ref, None, rows=rows, reverse=reverse, bounded=bounded,
                           slot=slot, sfin_ref=sfin_ref)


def _hgrn_scan_kernel(*refs, rows, has_init):
    if has_init:
        fits_ref, q_ref, v_ref, f_ref, s0_ref, o_ref, sfin_ref, st_ref = refs
    else:
        fits_ref, q_ref, v_ref, f_ref, o_ref, sfin_ref, st_ref = refs
    d = pl.program_id(1)
    t = pl.program_id(2)

    @pl.when(t == 0)
    def _():
        for h in range(HG_HEADS):
            st_ref[h] = s0_ref[0, 0, h].T if has_init else jnp.zeros((HG_DV, HG_DK), F32)

    for reverse in (False, True):
        for bounded in (True, False):
            @pl.when((d == int(reverse)) & ((fits_ref[0] != 0) == bounded))
            def _():
                _hgrn_tile(q_ref, v_ref, f_ref, o_ref, st_ref, rows=rows, reverse=reverse, bounded=bounded)

    @pl.when(t == pl.num_programs(2) - 1)
    def _():
        for h in range(HG_HEADS):
            sfin_ref[0, 0, h] = st_ref[h].T


def _hgrn_scan(fits, q, v, f2, batch, seq, first_row, rt, s0=None):
    n = q.shape[1]
    nt = seq // rt
    has_init = s0 is not None
    flag = pl.BlockSpec(memory_space=pltpu.SMEM)
    out_shape = [jax.ShapeDtypeStruct((2, batch * seq, n), F32),
                 jax.ShapeDtypeStruct((batch, 2, HG_HEADS, HG_DK, HG_DV), F32)]
    if nt == 1 and not has_init:
        rows_of = lambda b: (first_row // rt + b, 0)
        return pl.pallas_call(
            functools.partial(_hgrn_whole_kernel, rows=rt),
            grid=(batch,),
            in_specs=[flag, pl.BlockSpec((rt, n), rows_of), pl.BlockSpec((rt, n), rows_of),
                      pl.BlockSpec((2, rt, n), lambda b: (0, first_row // rt + b, 0))],
            out_specs=[pl.BlockSpec((2, rt, n), lambda b: (0, b, 0)),
                       pl.BlockSpec((1, 2, HG_HEADS, HG_DK, HG_DV), lambda b: (b, 0, 0, 0, 0))],
            out_shape=out_shape,
            compiler_params=_params("arbitrary"),
            name="hgrn_scan_whole",
        )(fits, q, v, f2)

    def local(b, d, i):
        return b * nt + jnp.where(d == 0, i, nt - 1 - i)

    def slab(b, d, i):
        return first_row // rt + local(b, d, i)

    row = pl.BlockSpec((rt, n), lambda b, d, i: (slab(b, d, i), 0))
    state = pl.BlockSpec((1, 1, HG_HEADS, HG_DK, HG_DV), lambda b, d, i: (b, d, 0, 0, 0))
    ins = [fits, q, v, f2]
    in_specs = [flag, row, row, pl.BlockSpec((1, rt, n), lambda b, d, i: (d, slab(b, d, i), 0))]
    if has_init:
        ins.append(s0)
        in_specs.append(state)
    return pl.pallas_call(
        functools.partial(_hgrn_scan_kernel, rows=rt, has_init=has_init),
        grid=(batch, 2, nt),
        in_specs=in_specs,
        out_specs=[pl.BlockSpec((1, rt, n), lambda b, d, i: (d, local(b, d, i), 0)), state],
        out_shape=[jax.ShapeDtypeStruct((2, batch * seq, n), F32),
                   jax.ShapeDtypeStruct((batch, 2, HG_HEADS, HG_DK, HG_DV), F32)],
        scratch_shapes=[pltpu.VMEM((HG_HEADS, HG_DV, HG_DK), F32)],
        compiler_params=_params("arbitrary", "arbitrary", "arbitrary"),
        name="hgrn_scan_latent" if has_init else "hgrn_scan_prompt",
    )(*ins)


def _post_kernel(*refs, hgrn, final, pair_in, pair_out, prompt_tiles):
    refs = list(refs)
    n_mix = 4 if hgrn else 2
    mix_refs, refs = refs[:n_mix], refs[n_mix:]
    x, is_prompt, refs = _take_rows(refs, pair_in, prompt_tiles)
    mod_ref, wo_ref, gn_ref, wg_ref, wu_ref, wd_ref = refs[:6]
    refs = list(refs[6:])
    fn_ref = refs.pop(0) if final else None
    if hgrn:
        odp_ref, odl_ref, g_ref, onorm_ref = mix_refs
        o2 = jnp.where(is_prompt, odp_ref[0] + odp_ref[1], odl_ref[0] + odl_ref[1])
        gate = g_ref[...]
        onorm = onorm_ref[...]
        parts = []
        for h in range(HG_HEADS):
            sl = slice(h * HG_DV, (h + 1) * HG_DV)
            parts.append(_rms(o2[:, sl], onorm) * gate[:, sl])
        o = jnp.concatenate(parts, axis=1).astype(BF16)
    else:
        o = jnp.where(is_prompt, mix_refs[0][...], mix_refs[1][...])
    m = mod_ref[0]
    x1 = x + m[2:3] * _dot(o, wo_ref[...])
    h2 = _modulate(x1, gn_ref[...], m[3:4], m[4:5]).astype(BF16)
    a = (_silu(_dot(h2, wg_ref[...])) * _dot(h2, wu_ref[...])).astype(BF16)
    x2 = x1 + m[5:6] * _dot(a, wd_ref[...])
    if final:
        x2 = _rms(x2, fn_ref[...])
    if pair_out:
        @pl.when(is_prompt)
        def _():
            refs[0][...] = x2

        @pl.when(jnp.logical_not(is_prompt))
        def _():
            refs[1][...] = x2
    else:
        refs[0][...] = x2


def _post(mix, x, mods, wo, gn, wg, wu, wd, rows, final_norm=None, hgrn=False, pair_out=False):
    xs, x_specs, pair_in, t, d = _rows_inputs(x, rows)
    tm, pt = rows.tm, rows.prompt_tiles
    if hgrn:
        od_p, od_l, gate, onorm = mix
        head = [od_p, od_l, gate, onorm]
        head_specs = [pl.BlockSpec((2, tm, d), lambda i: (0, jnp.minimum(i, pt - 1), 0)),
                      pl.BlockSpec((2, tm, d), lambda i: (0, jnp.maximum(i - pt, 0), 0)),
                      rows.row_spec(d), _whole(onorm.shape)]
    else:
        head = list(mix)
        head_specs = [rows.prompt_spec(mix[0].shape[1]), rows.latent_spec(mix[1].shape[1])]
    ins = head + xs + [mods, wo, gn, wg, wu, wd]
    in_specs = head_specs + x_specs + [rows.mod_spec(d)] + [_whole(a.shape) for a in (wo, gn, wg, wu, wd)]
    if final_norm is not None:
        ins.append(final_norm)
        in_specs.append(_whole(final_norm.shape))
    if pair_out:
        out_specs = rows.x_specs(d)
        out_shape = [jax.ShapeDtypeStruct((pt * tm, d), F32), jax.ShapeDtypeStruct((t - pt * tm, d), F32)]
    else:
        out_specs = rows.row_spec(d)
        out_shape = jax.ShapeDtypeStruct((t, d), F32)
    out = pl.pallas_call(
        functools.partial(_post_kernel, hgrn=hgrn, final=final_norm is not None, pair_in=pair_in, pair_out=pair_out,
                          prompt_tiles=pt),
        grid=(rows.n_tiles,),
        in_specs=in_specs,
        out_specs=out_specs,
        out_shape=out_shape,
        compiler_params=_params("arbitrary"),
        name="post",
    )(*ins)
    return tuple(out) if pair_out else out


def _pick_tile(n_prompt_rows, dec_seq, want):
    tm = want
    while n_prompt_rows % tm or dec_seq % tm:
        tm //= 2
    return tm


def kernel(x_prompt, x_sample, cache_mla_ckv, cache_mla_krope, state_hgrn, cache_swa_k, cache_swa_v, c, c_ctx, ada_w, ada_b, norm_mix, norm_ffn, ffn_w_gate, ffn_w_up, ffn_w_down, final_norm, mla_w_dq, mla_q_norm, mla_w_uq, mla_w_dkv, mla_kv_norm, mla_w_uk, mla_w_uv, mla_w_o, hg_w_q, hg_w_f, hg_w_i, hg_w_g, hg_o_norm, hg_w_o, hg_lb_logits, swa_w_q, swa_w_k, swa_w_v, swa_w_o, swa_sink):
    batch, seq, d = x_prompt.shape
    dec_batch, dec_seq, _ = x_sample.shape
    past = cache_mla_ckv.shape[2]
    depth = ada_w.shape[0]
    n_prompt = batch * seq
    n_rows = n_prompt + dec_batch * dec_seq
    assert dec_batch + 1 <= COND_ROWS and seq % SWA_WINDOW == 0 and dec_seq % (2 * SWA_WINDOW) == 0
    assert n_prompt % dec_seq == 0

    wide = _Rows(n_prompt, dec_seq, n_rows, _pick_tile(n_prompt, dec_seq, ROW_TILE))
    narrow = _Rows(n_prompt, dec_seq, n_rows, _pick_tile(n_prompt, dec_seq, ROW_TILE_F32_HEAVY))
    pre_rows, post_rows = wide, wide
    hg_pre_rows, hg_post_rows = narrow, narrow
    tq = _pick_tile(n_prompt, dec_seq, ATTN_TILE)
    cos64, sin64 = _rope_tables(dec_seq, pre_rows.tm)
    n_tab = cos64.shape[0]
    cos_mla = jnp.asarray(np.concatenate([cos64, np.ones((n_tab, ROPE_PERIOD), np.float32)], axis=1))
    sin_mla = jnp.asarray(np.concatenate([sin64, np.zeros((n_tab, ROPE_PERIOD), np.float32)], axis=1))
    cos_swa, sin_swa = jnp.asarray(np.tile(cos64, (1, 2))), jnp.asarray(np.tile(sin64, (1, 2)))

    cond = jnp.concatenate([c_ctx[None, :], c, jnp.zeros((COND_ROWS - 1 - dec_batch, d), F32)], axis=0)
    mods = _adaln(cond, ada_w, ada_b).reshape(depth, COND_ROWS, 6, d)

    x = (x_prompt.reshape(n_prompt, d), x_sample.reshape(dec_batch * dec_seq, d))
    ffn_stacks = (ffn_w_gate, ffn_w_up, ffn_w_down)
    row1 = lambda a: a.reshape(1, -1)
    new_ckv, new_krope, new_hg, new_k, new_v = [], [], [], [], []
    for i in range(depth):
        kind, j = i % N_MIXERS, i // N_MIXERS
        gn = row1(norm_mix[i])
        rows_i = hg_pre_rows if kind == 1 else pre_rows
        cast = _Cast(ffn_stacks, i, rows_i.n_tiles)
        if kind == 0:
            uq = mla_w_uq[j].reshape(-1, MLA_HEADS, MLA_NOPE_DIM + MLA_ROPE_DIM)
            uq_rope = jnp.pad(uq[:, :, MLA_NOPE_DIM:], ((0, 0), (0, 0), (0, LANES - MLA_ROPE_DIM)))
            uq = jnp.concatenate([uq[:, :, :MLA_NOPE_DIM].reshape(uq.shape[0], -1),
                                  uq_rope.reshape(uq.shape[0], -1)], axis=1)
            w = {
                "dq": mla_w_dq[j].astype(BF16), "q_norm": row1(mla_q_norm[j]), "uq": uq.astype(BF16),
                "dkv": jnp.pad(mla_w_dkv[j], ((0, 0), (0, LANES - MLA_ROPE_DIM))).astype(BF16),
                "kv_norm": row1(mla_kv_norm[j]),
                "uk": mla_w_uk[j].astype(BF16), "uv_t": mla_w_uv[j].T.astype(BF16),
            }
            (q, ckv, kr_raw, kcat, vt), ffn = _mla_pre(x, mods[i], gn, cos_mla, sin_mla, w, rows_i, cast)
            kr_ctx = jnp.pad(cache_mla_krope[:, j].reshape(dec_batch * past, -1), ((0, 0), (0, LANES - MLA_ROPE_DIM)))
            kcat_ctx, vt_ctx = _mla_expand(cache_mla_ckv[:, j].reshape(dec_batch * past, -1), kr_ctx, w["uk"], w["uv_t"])
            mix = (_mla_attn_prompt(q, kcat, vt, batch, seq),
                   _mla_attn_latent(q, kcat, vt, kcat_ctx, vt_ctx, n_prompt, dec_batch, dec_seq, past, tq))
            wo = mla_w_o[j].astype(BF16)
            new_ckv.append(ckv.reshape(batch, seq, -1))
            new_krope.append(kr_raw.reshape(batch, seq, -1))
        elif kind == 1:
            w5 = jnp.concatenate([hg_w_q[j], hg_w_i[j], hg_w_g[j], hg_w_f[j, 0], hg_w_f[j, 1]], axis=1).astype(BF16)
            (q, v, gate, f2, q_max), ffn = _hgrn_pre(x, mods[i], gn, w5, hg_lb_logits, i, rows_i, cast)
            rt = _pick_tile(seq, dec_seq, ATTN_TILE)
            sm = jax.nn.softmax(hg_lb_logits.astype(F32), axis=1)
            lb_min = jnp.min(jnp.cumsum(sm, axis=1)[:, i] - sm[:, 0])
            worst = (HG_DIAG_BLOCK // 2) * -jnp.log(lb_min) + jnp.log(jnp.maximum(jnp.max(q_max), 1.0))
            fits = (worst < HG_MAX_EXPONENT).astype(jnp.int32).reshape(1)
            od_p, s_prompt = _hgrn_scan(fits, q, v, f2, batch, seq, 0, rt)
            od_l, _ = _hgrn_scan(fits, q, v, f2, dec_batch, dec_seq, n_prompt, rt, s0=state_hgrn[:, j])
            mix = (od_p, od_l, gate, row1(hg_o_norm[j]))
            wo = hg_w_o[j].astype(BF16)
            new_hg.append(s_prompt)
        else:
            wqkv = jnp.concatenate([swa_w_q[j], swa_w_k[j], swa_w_v[j]], axis=1).astype(BF16)
            (q, kx, vt, k_raw, v_raw), ffn = _swa_pre(x, mods[i], gn, cos_swa, sin_swa, wqkv, rows_i, cast)
            sink = swa_sink[j]
            k_ctx = cache_swa_k[:, j].reshape(dec_batch * past, -1)
            v_ctx = cache_swa_v[:, j].reshape(dec_batch * past, -1)
            mix = (_swa_attn_prompt(q, kx, vt, sink, batch, seq),
                   _swa_attn_latent(q, kx, vt, k_ctx, v_ctx, sink, n_prompt, dec_batch, dec_seq, past, tq))
            wo = swa_w_o[j].astype(BF16)
            new_k.append(k_raw.reshape(batch, seq, SWA_KV_HEADS, SWA_HEAD_DIM))
            new_v.append(v_raw.reshape(batch, seq, SWA_KV_HEADS, SWA_HEAD_DIM))
        if ffn is None:
            ffn = [a[i].astype(BF16) for a in ffn_stacks]
        x = _post(mix, x, mods[i], wo, row1(norm_ffn[i]), *ffn,
                  hg_post_rows if kind == 1 else post_rows,
                  final_norm=row1(final_norm) if i == depth - 1 else None, hgrn=kind == 1,
                  pair_out=i == depth - 1)
    y_prompt = x[0].reshape(batch, seq, d)
    y_sample = x[1].reshape(dec_batch, dec_seq, d)
    return (y_prompt, y_sample, jnp.stack(new_ckv, axis=1), jnp.stack(new_krope, axis=1),
            jnp.stack(new_hg, axis=1), jnp.stack(new_k, axis=1), jnp.stack(new_v, axis=1))
```

```python
import functools

import numpy as np
import jax
import jax.numpy as jnp
from jax import lax
from jax.experimental import pallas as pl
from jax.experimental.pallas import tpu as pltpu

F32 = jnp.float32
BF16 = jnp.bfloat16

GRID_W = 64
N_MIXERS = 3

MLA_HEADS = 8
MLA_KV_LORA = 256
MLA_NOPE_DIM = 128
MLA_ROPE_DIM = 64
MLA_V_DIM = 128
MLA_QK_PAD = 256
MLA_SCALE = (MLA_NOPE_DIM + MLA_ROPE_DIM) ** -0.5

HG_HEADS = 8
HG_DK = 128
HG_DV = 128
HG_DIAG_BLOCK = 32
HG_MAX_EXPONENT = 80.0

SWA_HEADS = 16
SWA_KV_HEADS = 4
SWA_GROUP = SWA_HEADS // SWA_KV_HEADS
SWA_HEAD_DIM = 64
SWA_GW = SWA_GROUP * SWA_HEAD_DIM
SWA_WINDOW = 128
SWA_SCALE = SWA_HEAD_DIM ** -0.5

ROPE_BASE = 10000.0
ROPE_PERIOD = 64
ROPE_QUARTER = 16
NORM_EPS = 1e-6
NEG_INF = -1e30
LOG2_E = 1.4426950408889634

LANES = 128
COND_ROWS = 8
VMEM_LIMIT = 56 * 1024 * 1024
ROW_TILE = 512
ROW_TILE_F32_HEAVY = 256
ATTN_TILE = 256


def _sigmoid(x):
    return jax.nn.sigmoid(x)


def _silu(x):
    return x * jax.nn.sigmoid(x)


def _rms(x, g):
    return x * lax.rsqrt(jnp.mean(x * x, axis=-1, keepdims=True) + NORM_EPS) * g


def _modulate(x, g, shift, scale):
    return _rms(x, g) * (1.0 + scale) + shift


def _dot(a, b):
    return jnp.dot(a, b, preferred_element_type=F32)


def _dot_nt(a, b):
    return lax.dot_general(a, b, (((1,), (1,)), ((), ())), preferred_element_type=F32)


def _dot_tn(a, b):
    return lax.dot_general(a, b, (((0,), (0,)), ((), ())), preferred_element_type=F32)


def _swap_pairs(x):
    n = x.shape[1]
    lane = lax.broadcasted_iota(jnp.int32, x.shape, 1)
    ahead = pltpu.roll(x, n - ROPE_QUARTER, 1)
    behind = pltpu.roll(x, ROPE_QUARTER, 1)
    return jnp.where((lane & (2 * ROPE_QUARTER - 1)) < ROPE_QUARTER, ahead, behind)


def _rope(x, cos, sin):
    reps = x.shape[1] // cos.shape[1]
    if reps > 1:
        cos = jnp.concatenate([cos] * reps, axis=1)
        sin = jnp.concatenate([sin] * reps, axis=1)
    return x * cos + _swap_pairs(x) * sin


def _whole(shape):
    zeros = (0,) * len(shape)
    return pl.BlockSpec(shape, lambda *_: zeros, pipeline_mode=pl.Buffered(1))


def _params(*sem):
    return pltpu.CompilerParams(dimension_semantics=sem, vmem_limit_bytes=VMEM_LIMIT)


class _Rows:
    def __init__(self, n_prompt_rows, dec_seq, n_rows, tm):
        assert n_prompt_rows % tm == 0 and dec_seq % tm == 0
        self.tm = tm
        self.n_tiles = n_rows // tm
        self.prompt_tiles = n_prompt_rows // tm
        self.seq_tiles = dec_seq // tm

    def cond(self, i):
        return jnp.where(i < self.prompt_tiles, 0, 1 + jnp.maximum(i - self.prompt_tiles, 0) // self.seq_tiles)

    def rope_block(self, i):
        return jnp.where(i < self.prompt_tiles, 0, 1 + jnp.maximum(i - self.prompt_tiles, 0) % self.seq_tiles)

    def row_spec(self, width):
        return pl.BlockSpec((self.tm, width), lambda i: (i, 0))

    def mod_spec(self, d):
        return pl.BlockSpec((1, 6, d), lambda i: (self.cond(i), 0, 0))

    def rope_spec(self, width):
        return pl.BlockSpec((self.tm, width), lambda i: (self.rope_block(i), 0))

    def x_specs(self, width):
        return [self.prompt_spec(width), self.latent_spec(width)]

    def prompt_spec(self, width):
        return pl.BlockSpec((self.tm, width), lambda i: (jnp.minimum(i, self.prompt_tiles - 1), 0))

    def latent_spec(self, width):
        return pl.BlockSpec((self.tm, width), lambda i: (jnp.maximum(i - self.prompt_tiles, 0), 0))


def _rope_tables(dec_seq, tm):
    pos = np.arange(dec_seq)
    row = (pos // GRID_W).astype(np.float32)
    col = (pos % GRID_W).astype(np.float32)
    inv_freq = (ROPE_BASE ** (-np.arange(ROPE_QUARTER, dtype=np.float32) / ROPE_QUARTER)).astype(np.float32)
    ang_r = row[:, None] * inv_freq[None, :]
    ang_c = col[:, None] * inv_freq[None, :]
    cos = np.concatenate([np.cos(ang_r), np.cos(ang_r), np.cos(ang_c), np.cos(ang_c)], axis=1)
    sin = np.concatenate([-np.sin(ang_r), np.sin(ang_r), -np.sin(ang_c), np.sin(ang_c)], axis=1)
    cos = np.concatenate([np.ones((tm, ROPE_PERIOD), np.float32), cos], axis=0)
    sin = np.concatenate([np.zeros((tm, ROPE_PERIOD), np.float32), sin], axis=0)
    return cos.astype(np.float32), sin.astype(np.float32)


def _adaln_kernel(c_ref, w_ref, b_ref, o_ref):
    o_ref[0] = _dot(_silu(c_ref[...]), w_ref[0]) + b_ref[0]


def _adaln(cond, ada_w, ada_b, tn=1536):
    depth, d, n = ada_w.shape
    return pl.pallas_call(
        _adaln_kernel,
        grid=(depth, n // tn),
        in_specs=[pl.BlockSpec((COND_ROWS, d), lambda l, j: (0, 0)),
                  pl.BlockSpec((1, d, tn), lambda l, j: (l, 0, j)),
                  pl.BlockSpec((1, 1, tn), lambda l, j: (l, 0, j))],
        out_specs=pl.BlockSpec((1, COND_ROWS, tn), lambda l, j: (l, 0, j)),
        out_shape=jax.ShapeDtypeStruct((depth, COND_ROWS, n), F32),
        compiler_params=_params("arbitrary", "arbitrary"),
        name="adaln",
    )(cond, ada_w, ada_b.reshape(depth, 1, n))


def _mla_store_heads(q_ref, kcat_ref, vt_ref, qn, qr_pad, kn, kr_pad, vt):
    for h in range(MLA_HEADS):
        nope = slice(h * MLA_NOPE_DIM, (h + 1) * MLA_NOPE_DIM)
        if q_ref is not None:
            q_ref[h] = jnp.concatenate([qn[:, nope], qr_pad[:, h * LANES:(h + 1) * LANES]], axis=1)
        kcat_ref[h] = jnp.concatenate([kn[:, nope], kr_pad], axis=1)
        vt_ref[h] = vt[h * MLA_V_DIM:(h + 1) * MLA_V_DIM, :]


def _read_rows(is_prompt, p_ref, l_ref):
    return jnp.where(is_prompt, p_ref[...], l_ref[...])


def _take_rows(refs, pair, prompt_tiles):
    is_prompt = pl.program_id(0) < prompt_tiles
    if pair:
        return _read_rows(is_prompt, refs[0], refs[1]), is_prompt, refs[2:]
    return refs[0][...], is_prompt, refs[1:]


def _rows_inputs(x, rows):
    if isinstance(x, tuple):
        d = x[0].shape[1]
        return list(x), rows.x_specs(d), True, x[0].shape[0] + x[1].shape[0], d
    return [x], [rows.row_spec(x.shape[1])], False, x.shape[0], x.shape[1]


class _Cast:
    def __init__(self, stacks, layer, n_steps):
        self.stacks, self.layer = stacks, layer
        self.chunks = n_steps // len(stacks)
        self.ok = (n_steps % len(stacks) == 0
                   and all(a.shape[1] % self.chunks == 0 and (a.shape[1] // self.chunks) % 16 == 0 for a in stacks))

    def _spec(self, k, a, with_layer):
        rows = a.shape[1] // self.chunks
        chunk = lambda i: jnp.clip(i - k * self.chunks, 0, self.chunks - 1)
        if with_layer:
            return pl.BlockSpec((1, rows, a.shape[2]), lambda i: (self.layer, chunk(i), 0))
        return pl.BlockSpec((rows, a.shape[2]), lambda i: (chunk(i), 0))

    def in_specs(self):
        return [self._spec(k, a, True) for k, a in enumerate(self.stacks)]

    def out_specs(self):
        return [self._spec(k, a, False) for k, a in enumerate(self.stacks)]

    def out_shape(self):
        return [jax.ShapeDtypeStruct(a.shape[1:], BF16) for a in self.stacks]

    def run(self, src_refs, dst_refs):
        turn = pl.program_id(0) // self.chunks
        for k, (src, dst) in enumerate(zip(src_refs, dst_refs)):
            @pl.when(turn == k)
            def _():
                dst[...] = src[0].astype(BF16)


def _split_cast(refs, n_in, cast):
    if cast is None:
        return refs, (), ()
    n = len(cast.stacks)
    return refs[:n_in] + refs[n_in + n:len(refs) - n], refs[n_in:n_in + n], refs[len(refs) - n:]


def _mla_pre_kernel(*refs, pair, prompt_tiles, cast):
    x, is_prompt, refs = _take_rows(refs, pair, prompt_tiles)
    refs, cast_src, cast_dst = _split_cast(refs, 11, cast)
    (mod_ref, gn_ref, cos_ref, sin_ref, wdq_ref, qnorm_ref, wuq_ref, wdkv_ref, kvnorm_ref, wuk_ref, wuvt_ref,
     q_ref, ckv_ref, krraw_ref, kcat_ref, vt_ref) = refs
    if cast is not None:
        cast.run(cast_src, cast_dst)
    m = mod_ref[0]
    h = _modulate(x, gn_ref[...], m[0:1], m[1:2]).astype(BF16)
    q_lat = _rms(_dot(h, wdq_ref[...]), qnorm_ref[...]).astype(BF16)
    cos, sin = cos_ref[...], sin_ref[...]
    nn = MLA_HEADS * MLA_NOPE_DIM
    qn = _dot(q_lat, wuq_ref[:, :nn]).astype(BF16)
    qr_pad = _rope(_dot(q_lat, wuq_ref[:, nn:]), cos, sin).astype(BF16)
    kv = _dot(h, wdkv_ref[...])
    ckv = _rms(kv[:, :MLA_KV_LORA], kvnorm_ref[...])
    kr = kv[:, MLA_KV_LORA:]

    @pl.when(is_prompt)
    def _():
        ckv_ref[...] = ckv
        krraw_ref[...] = kr[:, :MLA_ROPE_DIM]

    kr_pad = _rope(kr, cos, sin).astype(BF16)
    cb = ckv.astype(BF16)
    _mla_store_heads(q_ref, kcat_ref, vt_ref, qn, qr_pad, _dot(cb, wuk_ref[...]).astype(BF16), kr_pad,
                     _dot_nt(wuvt_ref[...], cb).astype(BF16))


def _pre_call(kernel, cast, rows, ins, in_specs, out_specs, out_shape, name):
    if cast is not None and cast.ok:
        ins, in_specs = ins + list(cast.stacks), in_specs + cast.in_specs()
        out_specs, out_shape = out_specs + cast.out_specs(), out_shape + cast.out_shape()
    else:
        cast = None
    outs = pl.pallas_call(
        functools.partial(kernel, cast=cast),
        grid=(rows.n_tiles,),
        in_specs=in_specs,
        out_specs=out_specs,
        out_shape=out_shape,
        compiler_params=_params("arbitrary"),
        name=name,
    )(*ins)
    if cast is None:
        return outs, None
    n = len(cast.stacks)
    return outs[:-n], outs[-n:]


def _mla_pre(x, mods, gn, cos, sin, w, rows, cast):
    xs, x_specs, pair, t, d = _rows_inputs(x, rows)
    tm = rows.tm
    n_prompt = rows.prompt_tiles * tm
    weights = [w["dq"], w["q_norm"], w["uq"], w["dkv"], w["kv_norm"], w["uk"], w["uv_t"]]
    ins = xs + [mods, gn, cos, sin] + weights
    in_specs = x_specs + [rows.mod_spec(d), _whole(gn.shape),
                          rows.rope_spec(cos.shape[1]), rows.rope_spec(sin.shape[1])]
    in_specs += [_whole(a.shape) for a in weights]
    heads_rows = pl.BlockSpec((MLA_HEADS, tm, MLA_QK_PAD), lambda i: (0, i, 0))
    return _pre_call(
        functools.partial(_mla_pre_kernel, pair=pair, prompt_tiles=rows.prompt_tiles), cast, rows, ins, in_specs,
        [heads_rows, rows.prompt_spec(MLA_KV_LORA), rows.prompt_spec(MLA_ROPE_DIM), heads_rows,
         pl.BlockSpec((MLA_HEADS, MLA_V_DIM, tm), lambda i: (0, 0, i))],
        [jax.ShapeDtypeStruct((MLA_HEADS, t, MLA_QK_PAD), BF16),
         jax.ShapeDtypeStruct((n_prompt, MLA_KV_LORA), F32),
         jax.ShapeDtypeStruct((n_prompt, MLA_ROPE_DIM), F32),
         jax.ShapeDtypeStruct((MLA_HEADS, t, MLA_QK_PAD), BF16),
         jax.ShapeDtypeStruct((MLA_HEADS, MLA_V_DIM, t), BF16)],
        "mla_pre")


def _mla_expand_kernel(c_ref, kr_ref, wuk_ref, wuvt_ref, kcat_ref, vt_ref):
    cb = c_ref[...].astype(BF16)
    _mla_store_heads(None, kcat_ref, vt_ref, None, None, _dot(cb, wuk_ref[...]).astype(BF16),
                     kr_ref[...].astype(BF16), _dot_nt(wuvt_ref[...], cb).astype(BF16))


def _mla_expand(ckv, kr_pad, wuk, wuvt):
    n = ckv.shape[0]
    return pl.pallas_call(
        _mla_expand_kernel,
        out_shape=[jax.ShapeDtypeStruct((MLA_HEADS, n, MLA_QK_PAD), BF16),
                   jax.ShapeDtypeStruct((MLA_HEADS, MLA_V_DIM, n), BF16)],
        compiler_params=pltpu.CompilerParams(vmem_limit_bytes=VMEM_LIMIT),
        name="mla_expand",
    )(ckv, kr_pad, wuk, wuvt)


def _mla_attn_kernel(*refs, n_src):
    q_ref = refs[0]
    srcs = [(refs[1 + 2 * i], refs[2 + 2 * i]) for i in range(n_src)]
    o_ref, o_buf = refs[1 + 2 * n_src:3 + 2 * n_src]
    s_bufs = refs[3 + 2 * n_src:]
    n_keys = [k_ref.shape[1] for k_ref, _ in srcs]
    starts = [sum(n_keys[:i]) for i in range(n_src)]

    def put_scores(h, s_buf):
        q = q_ref[h]
        for (k_ref, _), first, n in zip(srcs, starts, n_keys):
            s_buf[first:first + n, :] = _dot_nt(k_ref[h], q)

    def finish(h, s_buf):
        s = s_buf[...]
        mx = jnp.max(s, axis=0, keepdims=True)
        p = jnp.exp2((s - mx) * (MLA_SCALE * LOG2_E))
        den = jnp.sum(p, axis=0, keepdims=True)
        p = p.astype(BF16)
        acc = None
        for (_, vt_ref), first, n in zip(srcs, starts, n_keys):
            a = _dot(vt_ref[h], p[first:first + n, :])
            acc = a if acc is None else acc + a
        o_buf[h] = (acc / den).T.astype(BF16)

    _pipelined_heads(MLA_HEADS, put_scores, finish, s_bufs)
    for h in range(MLA_HEADS):
        o_ref[:, h * MLA_V_DIM:(h + 1) * MLA_V_DIM] = o_buf[h]


def _mla_attn_scratch(n_keys, tq):
    return [pltpu.VMEM((MLA_HEADS, tq, MLA_V_DIM), BF16)] + [pltpu.VMEM((n_keys, tq), F32)] * 4


def _pipelined_heads(n_heads, put_scores, finish, bufs):
    a, b, c, d = bufs
    assert n_heads % 4 == 0
    put_scores(0, a)
    put_scores(1, b)

    def quad(j, carry):
        h = 4 * j
        put_scores(h + 2, c)
        put_scores(h + 3, d)
        finish(h, a)
        finish(h + 1, b)
        put_scores(h + 4, a)
        put_scores(h + 5, b)
        finish(h + 2, c)
        finish(h + 3, d)
        return carry

    lax.fori_loop(0, n_heads // 4 - 1, quad, 0)
    h = n_heads - 4
    put_scores(h + 2, c)
    put_scores(h + 3, d)
    finish(h, a)
    finish(h + 1, b)
    finish(h + 2, c)
    finish(h + 3, d)


def _staged_attention(scores, values_t, scale, sinks=None):
    c = scale * LOG2_E
    mx = [jnp.max(s, axis=0, keepdims=True) for s in scores]
    if sinks is not None:
        mx = [jnp.maximum(m, z) for m, z in zip(mx, sinks)]
    p = [jnp.exp2((s - m) * c) for s, m in zip(scores, mx)]
    den = [jnp.sum(x, axis=0, keepdims=True) for x in p]
    if sinks is not None:
        den = [d + jnp.exp2((z - m) * c) for d, z, m in zip(den, sinks, mx)]
    acc = [_dot(v, x.astype(BF16)) for v, x in zip(values_t, p)]
    return [a / d for a, d in zip(acc, den)]


def _mla_attn_prompt_kernel(q_ref, k_ref, vt_ref, o_ref, *, seq):
    per_step = q_ref.shape[1] // seq
    spans = [slice(s * seq, (s + 1) * seq) for s in range(per_step)]
    scores = [_dot_nt(k_ref[h, sp, :], q_ref[h, sp, :]) for sp in spans for h in range(MLA_HEADS)]
    values = [vt_ref[h, :, sp] for sp in spans for h in range(MLA_HEADS)]
    outs = _staged_attention(scores, values, MLA_SCALE)
    for s, sp in enumerate(spans):
        for h in range(MLA_HEADS):
            o_ref[sp, h * MLA_V_DIM:(h + 1) * MLA_V_DIM] = outs[s * MLA_HEADS + h].T.astype(BF16)


def _mla_attn_prompt(q, kcat, vt, batch, seq):
    hv = MLA_HEADS * MLA_V_DIM
    per_step = next(n for n in (4, 2, 1) if batch % n == 0)
    seq_rows = seq * per_step
    rows = pl.BlockSpec((MLA_HEADS, seq_rows, MLA_QK_PAD), lambda b: (0, b, 0))
    return pl.pallas_call(
        functools.partial(_mla_attn_prompt_kernel, seq=seq),
        grid=(batch // per_step,),
        in_specs=[rows, rows, pl.BlockSpec((MLA_HEADS, MLA_V_DIM, seq_rows), lambda b: (0, 0, b))],
        out_specs=pl.BlockSpec((seq_rows, hv), lambda b: (b, 0)),
        out_shape=jax.ShapeDtypeStruct((batch * seq, hv), BF16),
        compiler_params=_params("arbitrary"),
        name="mla_attn_prompt",
    )(q, kcat, vt)


def _mla_attn_latent(q, kcat, vt, kcat_ctx, vt_ctx, n_prompt, dec_batch, dec_seq, past, tq):
    hv = MLA_HEADS * MLA_V_DIM
    nq = dec_seq // tq
    lat_blk = n_prompt // dec_seq
    return pl.pallas_call(
        functools.partial(_mla_attn_kernel, n_src=2),
        grid=(dec_batch, nq),
        in_specs=[pl.BlockSpec((MLA_HEADS, tq, MLA_QK_PAD), lambda b, i: (0, n_prompt // tq + b * nq + i, 0)),
                  pl.BlockSpec((MLA_HEADS, past, MLA_QK_PAD), lambda b, i: (0, b, 0)),
                  pl.BlockSpec((MLA_HEADS, MLA_V_DIM, past), lambda b, i: (0, 0, b)),
                  pl.BlockSpec((MLA_HEADS, dec_seq, MLA_QK_PAD), lambda b, i: (0, lat_blk + b, 0)),
                  pl.BlockSpec((MLA_HEADS, MLA_V_DIM, dec_seq), lambda b, i: (0, 0, lat_blk + b))],
        out_specs=pl.BlockSpec((tq, hv), lambda b, i: (b * nq + i, 0)),
        out_shape=jax.ShapeDtypeStruct((dec_batch * dec_seq, hv), BF16),
        scratch_shapes=_mla_attn_scratch(past + dec_seq, tq),
        compiler_params=_params("arbitrary", "arbitrary"),
        name="mla_attn_latent",
    )(q, kcat_ctx, vt_ctx, kcat, vt)


def _tile_heads(x):
    n = x.shape[1]
    block = lax.broadcasted_iota(jnp.int32, x.shape, 1) // SWA_HEAD_DIM
    rolled = [x] + [pltpu.roll(x, s * SWA_HEAD_DIM, 1) for s in range(1, SWA_KV_HEADS)]
    out = []
    for kvh in range(SWA_KV_HEADS):
        blk = rolled[(0 - kvh) % SWA_KV_HEADS]
        for g in range(1, n // SWA_HEAD_DIM):
            blk = jnp.where(block == g, rolled[(g - kvh) % SWA_KV_HEADS], blk)
        out.append(blk)
    return jnp.concatenate(out, axis=1)


def _swa_pre_kernel(*refs, pair, prompt_tiles, cast):
    x, is_prompt, refs = _take_rows(refs, pair, prompt_tiles)
    refs, cast_src, cast_dst = _split_cast(refs, 5, cast)
    mod_ref, gn_ref, cos_ref, sin_ref, wqkv_ref, q_ref, kx_ref, vt_ref, kraw_ref, vraw_ref = refs
    if cast is not None:
        cast.run(cast_src, cast_dst)
    m = mod_ref[0]
    h = _modulate(x, gn_ref[...], m[0:1], m[1:2]).astype(BF16)
    qkv = _dot(h, wqkv_ref[...])
    nq, nk = SWA_HEADS * SWA_HEAD_DIM, SWA_KV_HEADS * SWA_HEAD_DIM
    cos, sin = cos_ref[...], sin_ref[...]
    q = _rope(qkv[:, :nq], cos, sin).astype(BF16)
    k = qkv[:, nq:nq + nk]
    v = qkv[:, nq + nk:]

    @pl.when(is_prompt)
    def _():
        kraw_ref[...] = k.reshape(kraw_ref.shape)
        vraw_ref[...] = v.reshape(vraw_ref.shape)

    kx = _tile_heads(_rope(k, cos, sin)).astype(BF16)
    vt = v.T.astype(BF16)
    for kvh in range(SWA_KV_HEADS):
        q_ref[kvh] = q[:, kvh * SWA_GW:(kvh + 1) * SWA_GW]
        kx_ref[kvh] = kx[:, kvh * SWA_GW:(kvh + 1) * SWA_GW]
        vt_ref[kvh] = vt[kvh * SWA_HEAD_DIM:(kvh + 1) * SWA_HEAD_DIM, :]


def _swa_pre(x, mods, gn, cos, sin, wqkv, rows, cast):
    xs, x_specs, pair, t, d = _rows_inputs(x, rows)
    tm = rows.tm
    n_prompt = rows.prompt_tiles * tm
    nk = SWA_KV_HEADS * SWA_HEAD_DIM
    heads_rows = pl.BlockSpec((SWA_KV_HEADS, tm, SWA_GW), lambda i: (0, i, 0))
    raw_spec = pl.BlockSpec((tm, SWA_KV_HEADS, SWA_HEAD_DIM), lambda i: (jnp.minimum(i, rows.prompt_tiles - 1), 0, 0))
    return _pre_call(
        functools.partial(_swa_pre_kernel, pair=pair, prompt_tiles=rows.prompt_tiles), cast, rows,
        xs + [mods, gn, cos, sin, wqkv],
        x_specs + [rows.mod_spec(d), _whole(gn.shape),
                   rows.rope_spec(cos.shape[1]), rows.rope_spec(sin.shape[1]), _whole(wqkv.shape)],
        [heads_rows, heads_rows, pl.BlockSpec((SWA_KV_HEADS, SWA_HEAD_DIM, tm), lambda i: (0, 0, i)),
         raw_spec, raw_spec],
        [jax.ShapeDtypeStruct((SWA_KV_HEADS, t, SWA_GW), BF16),
         jax.ShapeDtypeStruct((SWA_KV_HEADS, t, SWA_GW), BF16),
         jax.ShapeDtypeStruct((SWA_KV_HEADS, SWA_HEAD_DIM, t), BF16),
         jax.ShapeDtypeStruct((n_prompt, SWA_KV_HEADS, SWA_HEAD_DIM), F32),
         jax.ShapeDtypeStruct((n_prompt, SWA_KV_HEADS, SWA_HEAD_DIM), F32)],
        "swa_pre")


def _swa_heads(q_ref, sink_ref, o_ref, srcs, bias_ref, ot_buf, s_bufs):
    tq = q_ref.shape[1]
    n_keys = [kx_ref.shape[1] for kx_ref, _ in srcs]
    starts = [sum(n_keys[:i]) for i in range(len(srcs))]
    group = lax.broadcasted_iota(jnp.int32, (tq, SWA_GW), 1) // SWA_HEAD_DIM

    def put_scores(hq, s_buf):
        kvh, g = hq // SWA_GROUP, hq % SWA_GROUP
        q = jnp.where(group == g, q_ref[kvh].astype(F32), 0.0).astype(BF16)
        for (kx_ref, _), first, n in zip(srcs, starts, n_keys):
            s_buf[first:first + n, :] = _dot_nt(kx_ref[kvh], q)

    def finish(hq, s_buf):
        kvh, g = hq // SWA_GROUP, hq % SWA_GROUP
        s = s_buf[...]
        if bias_ref is not None:
            s = s + bias_ref[...]
        sink = jnp.full((1, 1), sink_ref[hq] * (1.0 / SWA_SCALE), F32)
        mx = jnp.maximum(jnp.max(s, axis=0, keepdims=True), sink)
        p = jnp.exp2((s - mx) * (SWA_SCALE * LOG2_E))
        den = jnp.sum(p, axis=0, keepdims=True) + jnp.exp2((sink - mx) * (SWA_SCALE * LOG2_E))
        p = p.astype(BF16)
        acc = None
        for (_, vt_ref), first, n in zip(srcs, starts, n_keys):
            a = _dot(vt_ref[kvh], p[first:first + n, :])
            acc = a if acc is None else acc + a
        ot_buf[kvh, pl.ds(pl.multiple_of(g * SWA_HEAD_DIM, SWA_HEAD_DIM), SWA_HEAD_DIM), :] = acc / den

    _pipelined_heads(SWA_HEADS, put_scores, finish, s_bufs)
    for kvh in range(SWA_KV_HEADS):
        o_ref[:, kvh * SWA_GW:(kvh + 1) * SWA_GW] = ot_buf[kvh].T.astype(BF16)


def _swa_attn_scratch(n_keys, tq):
    return [pltpu.VMEM((SWA_KV_HEADS, SWA_GW, tq), F32)] + [pltpu.VMEM((n_keys, tq), F32)] * 5


def _swa_attn_prompt_kernel(q_ref, kx_ref, vt_ref, sink_ref, o_ref, *, seq):
    per_step = q_ref.shape[1] // seq
    spans = [slice(s * seq, (s + 1) * seq) for s in range(per_step)]
    group = lax.broadcasted_iota(jnp.int32, (seq, SWA_GW), 1) // SWA_HEAD_DIM
    scores, values_t, sinks = [], [], []
    for sp in spans:
        for kvh in range(SWA_KV_HEADS):
            q_all = q_ref[kvh, sp, :].astype(F32)
            for g in range(SWA_GROUP):
                q = jnp.where(group == g, q_all, 0.0).astype(BF16)
                scores.append(_dot_nt(kx_ref[kvh, sp, :], q))
                values_t.append(vt_ref[kvh, :, sp])
                sinks.append(jnp.full((1, 1), sink_ref[kvh * SWA_GROUP + g] * (1.0 / SWA_SCALE), F32))
    outs = _staged_attention(scores, values_t, SWA_SCALE, sinks)
    for s, sp in enumerate(spans):
        for kvh in range(SWA_KV_HEADS):
            first = s * SWA_HEADS + kvh * SWA_GROUP
            ot = jnp.concatenate(outs[first:first + SWA_GROUP], axis=0)
            o_ref[sp, kvh * SWA_GW:(kvh + 1) * SWA_GW] = ot.T.astype(BF16)


def _swa_attn_prompt(q, kx, vt, sink, batch, seq):
    wq = SWA_HEADS * SWA_HEAD_DIM
    per_step = next(n for n in (4, 2, 1) if batch % n == 0)
    seq_rows = seq * per_step
    rows = pl.BlockSpec((SWA_KV_HEADS, seq_rows, SWA_GW), lambda b: (0, b, 0))
    return pl.pallas_call(
        functools.partial(_swa_attn_prompt_kernel, seq=seq),
        grid=(batch // per_step,),
        in_specs=[rows, rows, pl.BlockSpec((SWA_KV_HEADS, SWA_HEAD_DIM, seq_rows), lambda b: (0, 0, b)),
                  pl.BlockSpec(memory_space=pltpu.SMEM)],
        out_specs=pl.BlockSpec((seq_rows, wq), lambda b: (b, 0)),
        out_shape=jax.ShapeDtypeStruct((batch * seq, wq), BF16),
        compiler_params=_params("arbitrary"),
        name="swa_attn_prompt",
    )(q, kx, vt, sink)


def _swa_attn_latent_kernel(q_ref, kc_ref, vc_ref, kp_ref, vp_ref, km_ref, vm_ref, kn_ref, vn_ref,
                            sink_ref, o_ref, ot_buf, s_a, s_b, s_c, s_d, bias_ref, kxc_buf, vtc_buf, *,
                            tq, dec_seq, past):
    i = pl.program_id(1)
    qpos = i * tq + lax.broadcasted_iota(jnp.int32, (1, tq), 1)

    @pl.when(i == 0)
    def _():
        kx = _tile_heads(kc_ref[...]).astype(BF16)
        vt = vc_ref[...].T.astype(BF16)
        for kvh in range(SWA_KV_HEADS):
            kxc_buf[kvh] = kx[:, kvh * SWA_GW:(kvh + 1) * SWA_GW]
            vtc_buf[kvh] = vt[kvh * SWA_HEAD_DIM:(kvh + 1) * SWA_HEAD_DIM, :]

    def band(first, n):
        kpos = first + lax.broadcasted_iota(jnp.int32, (n, 1), 0)
        valid = (jnp.abs(qpos - kpos) <= SWA_WINDOW) & (kpos >= 0) & (kpos < dec_seq)
        return jnp.where(valid, 0.0, NEG_INF)

    bias_ref[...] = jnp.concatenate(
        [jnp.zeros((past, tq), F32), band(i * tq - SWA_WINDOW, SWA_WINDOW), band(i * tq, tq),
         band((i + 1) * tq, SWA_WINDOW)], axis=0)
    srcs = [(kxc_buf, vtc_buf), (kp_ref, vp_ref), (km_ref, vm_ref), (kn_ref, vn_ref)]
    _swa_heads(q_ref, sink_ref, o_ref, srcs, bias_ref, ot_buf, (s_a, s_b, s_c, s_d))


def _swa_attn_latent(q, kx, vt, k_ctx, v_ctx, sink, n_prompt, dec_batch, dec_seq, past, tq):
    t = q.shape[1]
    wq = SWA_HEADS * SWA_HEAD_DIM
    nq = dec_seq // tq
    w = SWA_WINDOW
    n_keys = past + tq + 2 * w
    first = lambda b, i: n_prompt + b * dec_seq + i * tq
    prev = lambda b, i: first(b, i) // w - 1
    nxt = lambda b, i: jnp.minimum((first(b, i) + tq) // w, t // w - 1)
    rows = lambda n, blk: pl.BlockSpec((SWA_KV_HEADS, n, SWA_GW), lambda b, i: (0, blk(b, i), 0))
    cols = lambda n, blk: pl.BlockSpec((SWA_KV_HEADS, SWA_HEAD_DIM, n), lambda b, i: (0, 0, blk(b, i)))
    main = lambda b, i: first(b, i) // tq
    ctx = pl.BlockSpec((past, k_ctx.shape[1]), lambda b, i: (b, 0))
    return pl.pallas_call(
        functools.partial(_swa_attn_latent_kernel, tq=tq, dec_seq=dec_seq, past=past),
        grid=(dec_batch, nq),
        in_specs=[rows(tq, main), ctx, ctx, rows(w, prev), cols(w, prev),
                  rows(tq, main), cols(tq, main), rows(w, nxt), cols(w, nxt),
                  pl.BlockSpec(memory_space=pltpu.SMEM)],
        out_specs=pl.BlockSpec((tq, wq), lambda b, i: (b * nq + i, 0)),
        out_shape=jax.ShapeDtypeStruct((dec_batch * dec_seq, wq), BF16),
        scratch_shapes=_swa_attn_scratch(n_keys, tq) + [pltpu.VMEM((SWA_KV_HEADS, past, SWA_GW), BF16),
                                                        pltpu.VMEM((SWA_KV_HEADS, SWA_HEAD_DIM, past), BF16)],
        compiler_params=_params("arbitrary", "arbitrary"),
        name="swa_attn_latent",
    )(q, k_ctx, v_ctx, kx, vt, kx, vt, kx, vt, sink)


def _hgrn_pre_kernel(*refs, layer, pair, prompt_tiles, cast):
    x, _, refs = _take_rows(refs, pair, prompt_tiles)
    refs, cast_src, cast_dst = _split_cast(refs, 4, cast)
    mod_ref, gn_ref, w_ref, lbl_ref, q_ref, v_ref, g_ref, f_ref, qmax_ref = refs
    if cast is not None:
        cast.run(cast_src, cast_dst)
    m = mod_ref[0]
    h = _modulate(x, gn_ref[...], m[0:1], m[1:2]).astype(BF16)
    y = _dot(h, w_ref[...])
    n = HG_HEADS * HG_DK
    q = _silu(y[:, :n])
    q_ref[...] = q
    qmax_ref[0] = jnp.max(jnp.abs(q), axis=0, keepdims=True)
    v_ref[...] = y[:, n:2 * n].astype(BF16)
    g_ref[...] = _silu(y[:, 2 * n:3 * n])
    for d in range(2):
        logits = lbl_ref[d]
        e = jnp.exp(logits - jnp.max(logits, axis=0, keepdims=True))
        s = e / jnp.sum(e, axis=0, keepdims=True)
        cs = s[0:1]
        for r in range(1, layer + 1):
            cs = cs + s[r:r + 1]
        lb = cs - s[0:1]
        f_ref[d] = lb + (1.0 - lb) * _sigmoid(y[:, (3 + d) * n:(4 + d) * n])


def _hgrn_pre(x, mods, gn, w5, lb_logits, layer, rows, cast):
    xs, x_specs, pair, t, d = _rows_inputs(x, rows)
    n = HG_HEADS * HG_DK
    return _pre_call(
        functools.partial(_hgrn_pre_kernel, layer=layer, pair=pair, prompt_tiles=rows.prompt_tiles), cast, rows,
        xs + [mods, gn, w5, lb_logits],
        x_specs + [rows.mod_spec(d), _whole(gn.shape), _whole(w5.shape), _whole(lb_logits.shape)],
        [rows.row_spec(n), rows.row_spec(n), rows.row_spec(n),
         pl.BlockSpec((2, rows.tm, n), lambda i: (0, i, 0)), pl.BlockSpec((1, 1, n), lambda i: (i, 0, 0))],
        [jax.ShapeDtypeStruct((t, n), F32), jax.ShapeDtypeStruct((t, n), BF16),
         jax.ShapeDtypeStruct((t, n), F32), jax.ShapeDtypeStruct((2, t, n), F32),
         jax.ShapeDtypeStruct((rows.n_tiles, 1, n), F32)],
        "hgrn_pre")


def _tri_cumsum(tri, x):
    hi = x.astype(BF16)
    r1 = x - hi.astype(F32)
    mid = r1.astype(BF16)
    lo = (r1 - mid.astype(F32)).astype(BF16)
    return _dot(tri, hi) + _dot(tri, mid) + _dot(tri, lo)


def _hgrn_tile(q_ref, v_ref, f_ref, o_ref, st_ref, *, rows, reverse, bounded, slot=0, sfin_ref=None):
    r = rows
    diag = HG_DIAG_BLOCK if bounded else 1
    a = lax.broadcasted_iota(jnp.int32, (r, r), 0)
    b = lax.broadcasted_iota(jnp.int32, (r, r), 1)
    seen = (b >= a) if reverse else (b <= a)
    tri = jnp.where(seen, 1.0, 0.0).astype(BF16)

    q = q_ref[...]
    f = f_ref[slot]
    vb = v_ref[...].astype(BF16)
    lf = jnp.log(f)
    cum = _tri_cumsum(tri, lf)
    tot = cum[0:1, :] if reverse else cum[r - 1:r, :]
    kk = 1.0 - f
    q_in = (q * jnp.exp(cum)).astype(BF16)
    k_d = (kk * jnp.exp(tot - cum)).astype(BF16)
    e_tot = jnp.exp(tot)

    levels = []
    c = r // 2
    while c >= diag:
        q_half = (a % (2 * c) < c) if reverse else (a % (2 * c) >= c)
        k_half = (b % (2 * c) >= c) if reverse else (b % (2 * c) < c)
        levels.append((2 * c, c if reverse else c - 1, ((a // (2 * c)) == (b // (2 * c))) & q_half & k_half))
        c //= 2
    levels.append((diag, diag // 2, ((a // diag) == (b // diag)) & seen))

    def spread(rows_of_block, size):
        return jnp.concatenate([jnp.broadcast_to(row, (size, row.shape[1])) for row in rows_of_block], axis=0)

    factors = []
    for size, ref_row, own in levels:
        if size == 1:
            factors.append((q.astype(BF16), kk.astype(BF16), own))
            continue
        if size >= 8:
            ref = spread([cum[j * size + ref_row:j * size + ref_row + 1, :] for j in range(r // size)], size)
        else:
            ref = _tri_cumsum(jnp.where(b == (a // size) * size + ref_row, 1.0, 0.0).astype(BF16), cum)
        factors.append(((q * jnp.exp(cum - ref)).astype(BF16), (kk * jnp.exp(ref - cum)).astype(BF16), own))

    outs = []
    for h in range(HG_HEADS):
        sl = slice(h * HG_DK, (h + 1) * HG_DK)
        att = jnp.zeros((r, r), F32)
        for q_l, k_l, own in factors:
            att = jnp.where(own, _dot_nt(q_l[:, sl], k_l[:, sl]), att)
        intra = _dot(att.astype(BF16), vb[:, sl])
        if sfin_ref is not None:
            outs.append(intra)
            sfin_ref[0, slot, h] = _dot_tn(k_d[:, sl], vb[:, sl])
            continue
        st = st_ref[h]
        outs.append(_dot_nt(q_in[:, sl], st.astype(BF16)) + intra)
        st_ref[h] = st * e_tot[:, sl] + _dot_tn(vb[:, sl], k_d[:, sl])
    o_ref[slot] = jnp.concatenate(outs, axis=1)


def _hgrn_whole_kernel(fits_ref, q_ref, v_ref, f_ref, o_ref, sfin_ref, *, rows):
    for bounded in (True, False):
        @pl.when((fits_ref[0] != 0) == bounded)
        def _():
            for slot, reverse in enumerate((False, True)):
                _hgrn_tile(q_ref, v_ref, f_ref, o_ref, None, rows=rows, reverse=reverse, bounded=bounded,
                           slot=slot, sfin_ref=sfin_ref)


def _hgrn_scan_kernel(*refs, rows, has_init, bounded):
    if has_init:
        q_ref, v_ref, f_ref, s0_ref, o_ref, sfin_ref, st_ref = refs
    else:
        q_ref, v_ref, f_ref, o_ref, sfin_ref, st_ref = refs
    d = pl.program_id(1)
    t = pl.program_id(2)

    @pl.when(t == 0)
    def _():
        for h in range(HG_HEADS):
            st_ref[h] = s0_ref[0, 0, h].T if has_init else jnp.zeros((HG_DV, HG_DK), F32)

    for reverse in (False, True):
        @pl.when(d == int(reverse))
        def _():
            _hgrn_tile(q_ref, v_ref, f_ref, o_ref, st_ref, rows=rows, reverse=reverse, bounded=bounded)

    @pl.when(t == pl.num_programs(2) - 1)
    def _():
        for h in range(HG_HEADS):
            sfin_ref[0, 0, h] = st_ref[h].T


def _hgrn_scan(fits, q, v, f2, batch, seq, first_row, rt, s0=None):
    n = q.shape[1]
    nt = seq // rt
    has_init = s0 is not None
    flag = pl.BlockSpec(memory_space=pltpu.SMEM)
    out_shape = [jax.ShapeDtypeStruct((2, batch * seq, n), F32),
                 jax.ShapeDtypeStruct((batch, 2, HG_HEADS, HG_DK, HG_DV), F32)]
    if nt == 1 and not has_init:
        rows_of = lambda b: (first_row // rt + b, 0)
        return pl.pallas_call(
            functools.partial(_hgrn_whole_kernel, rows=rt),
            grid=(batch,),
            in_specs=[flag, pl.BlockSpec((rt, n), rows_of), pl.BlockSpec((rt, n), rows_of),
                      pl.BlockSpec((2, rt, n), lambda b: (0, first_row // rt + b, 0))],
            out_specs=[pl.BlockSpec((2, rt, n), lambda b: (0, b, 0)),
                       pl.BlockSpec((1, 2, HG_HEADS, HG_DK, HG_DV), lambda b: (b, 0, 0, 0, 0))],
            out_shape=out_shape,
            compiler_params=_params("arbitrary"),
            name="hgrn_scan_whole",
        )(fits, q, v, f2)

    def local(b, d, i):
        return b * nt + jnp.where(d == 0, i, nt - 1 - i)

    def slab(b, d, i):
        return first_row // rt + local(b, d, i)

    row = pl.BlockSpec((rt, n), lambda b, d, i: (slab(b, d, i), 0))
    state = pl.BlockSpec((1, 1, HG_HEADS, HG_DK, HG_DV), lambda b, d, i: (b, d, 0, 0, 0))
    ins = [q, v, f2]
    in_specs = [row, row, pl.BlockSpec((1, rt, n), lambda b, d, i: (d, slab(b, d, i), 0))]
    if has_init:
        ins.append(s0)
        in_specs.append(state)

    def call(bounded):
        return pl.pallas_call(
            functools.partial(_hgrn_scan_kernel, rows=rt, has_init=has_init, bounded=bounded),
            grid=(batch, 2, nt),
            in_specs=in_specs,
            out_specs=[pl.BlockSpec((1, rt, n), lambda b, d, i: (d, local(b, d, i), 0)), state],
            out_shape=out_shape,
            scratch_shapes=[pltpu.VMEM((HG_HEADS, HG_DV, HG_DK), F32)],
            compiler_params=_params("arbitrary", "arbitrary", "arbitrary"),
            name="hgrn_scan_tiles",
        )(*ins)

    return lax.cond(fits[0] != 0, lambda: call(True), lambda: call(False))


def _post_kernel(*refs, hgrn, final, pair_in, pair_out, prompt_tiles):
    refs = list(refs)
    n_mix = 4 if hgrn else 2
    mix_refs, refs = refs[:n_mix], refs[n_mix:]
    x, is_prompt, refs = _take_rows(refs, pair_in, prompt_tiles)
    mod_ref, wo_ref, gn_ref, wg_ref, wu_ref, wd_ref = refs[:6]
    refs = list(refs[6:])
    fn_ref = refs.pop(0) if final else None
    if hgrn:
        odp_ref, odl_ref, g_ref, onorm_ref = mix_refs
        o2 = jnp.where(is_prompt, odp_ref[0] + odp_ref[1], odl_ref[0] + odl_ref[1])
        gate = g_ref[...]
        onorm = onorm_ref[...]
        parts = []
        for h in range(HG_HEADS):
            sl = slice(h * HG_DV, (h + 1) * HG_DV)
            parts.append(_rms(o2[:, sl], onorm) * gate[:, sl])
        o = jnp.concatenate(parts, axis=1).astype(BF16)
    else:
        o = jnp.where(is_prompt, mix_refs[0][...], mix_refs[1][...])
    m = mod_ref[0]
    x1 = x + m[2:3] * _dot(o, wo_ref[...])
    h2 = _modulate(x1, gn_ref[...], m[3:4], m[4:5]).astype(BF16)
    a = (_silu(_dot(h2, wg_ref[...])) * _dot(h2, wu_ref[...])).astype(BF16)
    x2 = x1 + m[5:6] * _dot(a, wd_ref[...])
    if final:
        x2 = _rms(x2, fn_ref[...])
    if pair_out:
        @pl.when(is_prompt)
        def _():
            refs[0][...] = x2

        @pl.when(jnp.logical_not(is_prompt))
        def _():
            refs[1][...] = x2
    else:
        refs[0][...] = x2


def _post(mix, x, mods, wo, gn, wg, wu, wd, rows, final_norm=None, hgrn=False, pair_out=False):
    xs, x_specs, pair_in, t, d = _rows_inputs(x, rows)
    tm, pt = rows.tm, rows.prompt_tiles
    if hgrn:
        od_p, od_l, gate, onorm = mix
        head = [od_p, od_l, gate, onorm]
        head_specs = [pl.BlockSpec((2, tm, d), lambda i: (0, jnp.minimum(i, pt - 1), 0)),
                      pl.BlockSpec((2, tm, d), lambda i: (0, jnp.maximum(i - pt, 0), 0)),
                      rows.row_spec(d), _whole(onorm.shape)]
    else:
        head = list(mix)
        head_specs = [rows.prompt_spec(mix[0].shape[1]), rows.latent_spec(mix[1].shape[1])]
    ins = head + xs + [mods, wo, gn, wg, wu, wd]
    in_specs = head_specs + x_specs + [rows.mod_spec(d)] + [_whole(a.shape) for a in (wo, gn, wg, wu, wd)]
    if final_norm is not None:
        ins.append(final_norm)
        in_specs.append(_whole(final_norm.shape))
    if pair_out:
        out_specs = rows.x_specs(d)
        out_shape = [jax.ShapeDtypeStruct((pt * tm, d), F32), jax.ShapeDtypeStruct((t - pt * tm, d), F32)]
    else:
        out_specs = rows.row_spec(d)
        out_shape = jax.ShapeDtypeStruct((t, d), F32)
    out = pl.pallas_call(
        functools.partial(_post_kernel, hgrn=hgrn, final=final_norm is not None, pair_in=pair_in, pair_out=pair_out,
                          prompt_tiles=pt),
        grid=(rows.n_tiles,),
        in_specs=in_specs,
        out_specs=out_specs,
        out_shape=out_shape,
        compiler_params=_params("arbitrary"),
        name="post",
    )(*ins)
    return tuple(out) if pair_out else out


def _pick_tile(n_prompt_rows, dec_seq, want):
    tm = want
    while n_prompt_rows % tm or dec_seq % tm:
        tm //= 2
    return tm


def kernel(x_prompt, x_sample, cache_mla_ckv, cache_mla_krope, state_hgrn, cache_swa_k, cache_swa_v, c, c_ctx, ada_w, ada_b, norm_mix, norm_ffn, ffn_w_gate, ffn_w_up, ffn_w_down, final_norm, mla_w_dq, mla_q_norm, mla_w_uq, mla_w_dkv, mla_kv_norm, mla_w_uk, mla_w_uv, mla_w_o, hg_w_q, hg_w_f, hg_w_i, hg_w_g, hg_o_norm, hg_w_o, hg_lb_logits, swa_w_q, swa_w_k, swa_w_v, swa_w_o, swa_sink):
    batch, seq, d = x_prompt.shape
    dec_batch, dec_seq, _ = x_sample.shape
    past = cache_mla_ckv.shape[2]
    depth = ada_w.shape[0]
    n_prompt = batch * seq
    n_rows = n_prompt + dec_batch * dec_seq
    assert dec_batch + 1 <= COND_ROWS and seq % SWA_WINDOW == 0 and dec_seq % (2 * SWA_WINDOW) == 0
    assert n_prompt % dec_seq == 0

    wide = _Rows(n_prompt, dec_seq, n_rows, _pick_tile(n_prompt, dec_seq, ROW_TILE))
    narrow = _Rows(n_prompt, dec_seq, n_rows, _pick_tile(n_prompt, dec_seq, ROW_TILE_F32_HEAVY))
    pre_rows, post_rows = wide, wide
    hg_pre_rows, hg_post_rows = narrow, narrow
    tq = _pick_tile(n_prompt, dec_seq, ATTN_TILE)
    cos64, sin64 = _rope_tables(dec_seq, pre_rows.tm)
    n_tab = cos64.shape[0]
    cos_mla = jnp.asarray(np.concatenate([cos64, np.ones((n_tab, ROPE_PERIOD), np.float32)], axis=1))
    sin_mla = jnp.asarray(np.concatenate([sin64, np.zeros((n_tab, ROPE_PERIOD), np.float32)], axis=1))
    cos_swa, sin_swa = jnp.asarray(np.tile(cos64, (1, 2))), jnp.asarray(np.tile(sin64, (1, 2)))

    cond = jnp.concatenate([c_ctx[None, :], c, jnp.zeros((COND_ROWS - 1 - dec_batch, d), F32)], axis=0)
    mods = _adaln(cond, ada_w, ada_b).reshape(depth, COND_ROWS, 6, d)

    x = (x_prompt.reshape(n_prompt, d), x_sample.reshape(dec_batch * dec_seq, d))
    ffn_stacks = (ffn_w_gate, ffn_w_up, ffn_w_down)
    row1 = lambda a: a.reshape(1, -1)
    new_ckv, new_krope, new_hg, new_k, new_v = [], [], [], [], []
    for i in range(depth):
        kind, j = i % N_MIXERS, i // N_MIXERS
        gn = row1(norm_mix[i])
        rows_i = hg_pre_rows if kind == 1 else pre_rows
        cast = _Cast(ffn_stacks, i, rows_i.n_tiles)
        if kind == 0:
            uq = mla_w_uq[j].reshape(-1, MLA_HEADS, MLA_NOPE_DIM + MLA_ROPE_DIM)
            uq_rope = jnp.pad(uq[:, :, MLA_NOPE_DIM:], ((0, 0), (0, 0), (0, LANES - MLA_ROPE_DIM)))
            uq = jnp.concatenate([uq[:, :, :MLA_NOPE_DIM].reshape(uq.shape[0], -1),
                                  uq_rope.reshape(uq.shape[0], -1)], axis=1)
            w = {
                "dq": mla_w_dq[j].astype(BF16), "q_norm": row1(mla_q_norm[j]), "uq": uq.astype(BF16),
                "dkv": jnp.pad(mla_w_dkv[j], ((0, 0), (0, LANES - MLA_ROPE_DIM))).astype(BF16),
                "kv_norm": row1(mla_kv_norm[j]),
                "uk": mla_w_uk[j].astype(BF16), "uv_t": mla_w_uv[j].T.astype(BF16),
            }
            (q, ckv, kr_raw, kcat, vt), ffn = _mla_pre(x, mods[i], gn, cos_mla, sin_mla, w, rows_i, cast)
            kr_ctx = jnp.pad(cache_mla_krope[:, j].reshape(dec_batch * past, -1), ((0, 0), (0, LANES - MLA_ROPE_DIM)))
            kcat_ctx, vt_ctx = _mla_expand(cache_mla_ckv[:, j].reshape(dec_batch * past, -1), kr_ctx, w["uk"], w["uv_t"])
            mix = (_mla_attn_prompt(q, kcat, vt, batch, seq),
                   _mla_attn_latent(q, kcat, vt, kcat_ctx, vt_ctx, n_prompt, dec_batch, dec_seq, past, tq))
            wo = mla_w_o[j].astype(BF16)
            new_ckv.append(ckv.reshape(batch, seq, -1))
            new_krope.append(kr_raw.reshape(batch, seq, -1))
        elif kind == 1:
            w5 = jnp.concatenate([hg_w_q[j], hg_w_i[j], hg_w_g[j], hg_w_f[j, 0], hg_w_f[j, 1]], axis=1).astype(BF16)
            (q, v, gate, f2, q_max), ffn = _hgrn_pre(x, mods[i], gn, w5, hg_lb_logits, i, rows_i, cast)
            rt = _pick_tile(seq, dec_seq, ATTN_TILE)
            sm = jax.nn.softmax(hg_lb_logits.astype(F32), axis=1)
            lb_min = jnp.min(jnp.cumsum(sm, axis=1)[:, i] - sm[:, 0])
            worst = (HG_DIAG_BLOCK // 2) * -jnp.log(lb_min) + jnp.log(jnp.maximum(jnp.max(q_max), 1.0))
            fits = (worst < HG_MAX_EXPONENT).astype(jnp.int32).reshape(1)
            od_p, s_prompt = _hgrn_scan(fits, q, v, f2, batch, seq, 0, rt)
            od_l, _ = _hgrn_scan(fits, q, v, f2, dec_batch, dec_seq, n_prompt, rt, s0=state_hgrn[:, j])
            mix = (od_p, od_l, gate, row1(hg_o_norm[j]))
            wo = hg_w_o[j].astype(BF16)
            new_hg.append(s_prompt)
        else:
            wqkv = jnp.concatenate([swa_w_q[j], swa_w_k[j], swa_w_v[j]], axis=1).astype(BF16)
            (q, kx, vt, k_raw, v_raw), ffn = _swa_pre(x, mods[i], gn, cos_swa, sin_swa, wqkv, rows_i, cast)
            sink = swa_sink[j]
            k_ctx = cache_swa_k[:, j].reshape(dec_batch * past, -1)
            v_ctx = cache_swa_v[:, j].reshape(dec_batch * past, -1)
            mix = (_swa_attn_prompt(q, kx, vt, sink, batch, seq),
                   _swa_attn_latent(q, kx, vt, k_ctx, v_ctx, sink, n_prompt, dec_batch, dec_seq, past, tq))
            wo = swa_w_o[j].astype(BF16)
            new_k.append(k_raw.reshape(batch, seq, SWA_KV_HEADS, SWA_HEAD_DIM))
            new_v.append(v_raw.reshape(batch, seq, SWA_KV_HEADS, SWA_HEAD_DIM))
        if ffn is None:
            ffn = [a[i].astype(BF16) for a in ffn_stacks]
        x = _post(mix, x, mods[i], wo, row1(norm_ffn[i]), *ffn,
                  hg_post_rows if kind == 1 else post_rows,
                  final_norm=row1(final_norm) if i == depth - 1 else None, hgrn=kind == 1,
                  pair_out=i == depth - 1)
    y_prompt = x[0].reshape(batch, seq, d)
    y_sample = x[1].reshape(dec_batch, dec_seq, d)
    return (y_prompt, y_sample, jnp.stack(new_ckv, axis=1), jnp.stack(new_krope, axis=1),
            jnp.stack(new_hg, axis=1), jnp.stack(new_k, axis=1), jnp.stack(new_v, axis=1))
```

```python
import functools

import numpy as np
import jax
import jax.numpy as jnp
from jax import lax
from jax.experimental import pallas as pl
from jax.experimental.pallas import tpu as pltpu

F32 = jnp.float32
BF16 = jnp.bfloat16

GRID_W = 64
N_MIXERS = 3

MLA_HEADS = 8
MLA_KV_LORA = 256
MLA_NOPE_DIM = 128
MLA_ROPE_DIM = 64
MLA_V_DIM = 128
MLA_QK_PAD = 256
MLA_SCALE = (MLA_NOPE_DIM + MLA_ROPE_DIM) ** -0.5

HG_HEADS = 8
HG_DK = 128
HG_DV = 128
HG_DIAG_BLOCK = 32
HG_MAX_EXPONENT = 80.0

SWA_HEADS = 16
SWA_KV_HEADS = 4
SWA_GROUP = SWA_HEADS // SWA_KV_HEADS
SWA_HEAD_DIM = 64
SWA_GW = SWA_GROUP * SWA_HEAD_DIM
SWA_WINDOW = 128
SWA_SCALE = SWA_HEAD_DIM ** -0.5

ROPE_BASE = 10000.0
ROPE_PERIOD = 64
ROPE_QUARTER = 16
NORM_EPS = 1e-6
NEG_INF = -1e30
LOG2_E = 1.4426950408889634

LANES = 128
COND_ROWS = 8
VMEM_LIMIT = 56 * 1024 * 1024
ROW_TILE = 512
ROW_TILE_F32_HEAVY = 256
ATTN_TILE = 256


def _sigmoid(x):
    return jax.nn.sigmoid(x)


def _silu(x):
    return x * jax.nn.sigmoid(x)


def _rms(x, g):
    return x * lax.rsqrt(jnp.mean(x * x, axis=-1, keepdims=True) + NORM_EPS) * g


def _modulate(x, g, shift, scale):
    return _rms(x, g) * (1.0 + scale) + shift


def _dot(a, b):
    return jnp.dot(a, b, preferred_element_type=F32)


def _dot_nt(a, b):
    return lax.dot_general(a, b, (((1,), (1,)), ((), ())), preferred_element_type=F32)


def _dot_tn(a, b):
    return lax.dot_general(a, b, (((0,), (0,)), ((), ())), preferred_element_type=F32)


def _swap_pairs(x):
    n = x.shape[1]
    lane = lax.broadcasted_iota(jnp.int32, x.shape, 1)
    ahead = pltpu.roll(x, n - ROPE_QUARTER, 1)
    behind = pltpu.roll(x, ROPE_QUARTER, 1)
    return jnp.where((lane & (2 * ROPE_QUARTER - 1)) < ROPE_QUARTER, ahead, behind)


def _rope(x, cos, sin):
    reps = x.shape[1] // cos.shape[1]
    if reps > 1:
        cos = jnp.concatenate([cos] * reps, axis=1)
        sin = jnp.concatenate([sin] * reps, axis=1)
    return x * cos + _swap_pairs(x) * sin


def _whole(shape):
    zeros = (0,) * len(shape)
    return pl.BlockSpec(shape, lambda *_: zeros, pipeline_mode=pl.Buffered(1))


def _params(*sem):
    return pltpu.CompilerParams(dimension_semantics=sem, vmem_limit_bytes=VMEM_LIMIT)


class _Rows:
    def __init__(self, n_prompt_rows, dec_seq, n_rows, tm):
        assert n_prompt_rows % tm == 0 and dec_seq % tm == 0
        self.tm = tm
        self.n_tiles = n_rows // tm
        self.prompt_tiles = n_prompt_rows // tm
        self.seq_tiles = dec_seq // tm

    def cond(self, i):
        return jnp.where(i < self.prompt_tiles, 0, 1 + jnp.maximum(i - self.prompt_tiles, 0) // self.seq_tiles)

    def rope_block(self, i):
        return jnp.where(i < self.prompt_tiles, 0, 1 + jnp.maximum(i - self.prompt_tiles, 0) % self.seq_tiles)

    def row_spec(self, width):
        return pl.BlockSpec((self.tm, width), lambda i: (i, 0))

    def mod_spec(self, d):
        return pl.BlockSpec((1, 6, d), lambda i: (self.cond(i), 0, 0))

    def rope_spec(self, width):
        return pl.BlockSpec((self.tm, width), lambda i: (self.rope_block(i), 0))

    def x_specs(self, width):
        return [self.prompt_spec(width), self.latent_spec(width)]

    def prompt_spec(self, width):
        return pl.BlockSpec((self.tm, width), lambda i: (jnp.minimum(i, self.prompt_tiles - 1), 0))

    def latent_spec(self, width):
        return pl.BlockSpec((self.tm, width), lambda i: (jnp.maximum(i - self.prompt_tiles, 0), 0))


def _rope_tables(dec_seq, tm):
    pos = np.arange(dec_seq)
    row = (pos // GRID_W).astype(np.float32)
    col = (pos % GRID_W).astype(np.float32)
    inv_freq = (ROPE_BASE ** (-np.arange(ROPE_QUARTER, dtype=np.float32) / ROPE_QUARTER)).astype(np.float32)
    ang_r = row[:, None] * inv_freq[None, :]
    ang_c = col[:, None] * inv_freq[None, :]
    cos = np.concatenate([np.cos(ang_r), np.cos(ang_r), np.cos(ang_c), np.cos(ang_c)], axis=1)
    sin = np.concatenate([-np.sin(ang_r), np.sin(ang_r), -np.sin(ang_c), np.sin(ang_c)], axis=1)
    cos = np.concatenate([np.ones((tm, ROPE_PERIOD), np.float32), cos], axis=0)
    sin = np.concatenate([np.zeros((tm, ROPE_PERIOD), np.float32), sin], axis=0)
    return cos.astype(np.float32), sin.astype(np.float32)


def _adaln_kernel(c_ref, w_ref, b_ref, o_ref):
    o_ref[0] = _dot(_silu(c_ref[...]), w_ref[0]) + b_ref[0]


def _adaln(cond, ada_w, ada_b, tn=1536):
    depth, d, n = ada_w.shape
    return pl.pallas_call(
        _adaln_kernel,
        grid=(depth, n // tn),
        in_specs=[pl.BlockSpec((COND_ROWS, d), lambda l, j: (0, 0)),
                  pl.BlockSpec((1, d, tn), lambda l, j: (l, 0, j)),
                  pl.BlockSpec((1, 1, tn), lambda l, j: (l, 0, j))],
        out_specs=pl.BlockSpec((1, COND_ROWS, tn), lambda l, j: (l, 0, j)),
        out_shape=jax.ShapeDtypeStruct((depth, COND_ROWS, n), F32),
        compiler_params=_params("arbitrary", "arbitrary"),
        name="adaln",
    )(cond, ada_w, ada_b.reshape(depth, 1, n))


def _mla_store_heads(q_ref, kcat_ref, vt_ref, qn, qr_pad, kn, kr_pad, vt):
    for h in range(MLA_HEADS):
        nope = slice(h * MLA_NOPE_DIM, (h + 1) * MLA_NOPE_DIM)
        if q_ref is not None:
            q_ref[h] = jnp.concatenate([qn[:, nope], qr_pad[:, h * LANES:(h + 1) * LANES]], axis=1)
        kcat_ref[h] = jnp.concatenate([kn[:, nope], kr_pad], axis=1)
        vt_ref[h] = vt[h * MLA_V_DIM:(h + 1) * MLA_V_DIM, :]


def _read_rows(is_prompt, p_ref, l_ref):
    return jnp.where(is_prompt, p_ref[...], l_ref[...])


def _take_rows(refs, pair, prompt_tiles):
    is_prompt = pl.program_id(0) < prompt_tiles
    if pair:
        return _read_rows(is_prompt, refs[0], refs[1]), is_prompt, refs[2:]
    return refs[0][...], is_prompt, refs[1:]


def _rows_inputs(x, rows):
    if isinstance(x, tuple):
        d = x[0].shape[1]
        return list(x), rows.x_specs(d), True, x[0].shape[0] + x[1].shape[0], d
    return [x], [rows.row_spec(x.shape[1])], False, x.shape[0], x.shape[1]


class _Cast:
    def __init__(self, stacks, layer, n_steps):
        self.stacks, self.layer = stacks, layer
        self.chunks = n_steps // len(stacks)
        self.ok = (n_steps % len(stacks) == 0
                   and all(a.shape[1] % self.chunks == 0 and (a.shape[1] // self.chunks) % 16 == 0 for a in stacks))

    def _spec(self, k, a, with_layer):
        rows = a.shape[1] // self.chunks
        chunk = lambda i: jnp.clip(i - k * self.chunks, 0, self.chunks - 1)
        if with_layer:
            return pl.BlockSpec((1, rows, a.shape[2]), lambda i: (self.layer, chunk(i), 0))
        return pl.BlockSpec((rows, a.shape[2]), lambda i: (chunk(i), 0))

    def in_specs(self):
        return [self._spec(k, a, True) for k, a in enumerate(self.stacks)]

    def out_specs(self):
        return [self._spec(k, a, False) for k, a in enumerate(self.stacks)]

    def out_shape(self):
        return [jax.ShapeDtypeStruct(a.shape[1:], BF16) for a in self.stacks]

    def run(self, src_refs, dst_refs):
        turn = pl.program_id(0) // self.chunks
        for k, (src, dst) in enumerate(zip(src_refs, dst_refs)):
            @pl.when(turn == k)
            def _():
                dst[...] = src[0].astype(BF16)


def _split_cast(refs, n_in, cast):
    if cast is None:
        return refs, (), ()
    n = len(cast.stacks)
    return refs[:n_in] + refs[n_in + n:len(refs) - n], refs[n_in:n_in + n], refs[len(refs) - n:]


def _mla_pre_kernel(*refs, pair, prompt_tiles, cast):
    x, is_prompt, refs = _take_rows(refs, pair, prompt_tiles)
    refs, cast_src, cast_dst = _split_cast(refs, 11, cast)
    (mod_ref, gn_ref, cos_ref, sin_ref, wdq_ref, qnorm_ref, wuq_ref, wdkv_ref, kvnorm_ref, wuk_ref, wuvt_ref,
     q_ref, ckv_ref, krraw_ref, kcat_ref, vt_ref) = refs
    if cast is not None:
        cast.run(cast_src, cast_dst)
    m = mod_ref[0]
    h = _modulate(x, gn_ref[...], m[0:1], m[1:2]).astype(BF16)
    q_lat = _rms(_dot(h, wdq_ref[...]), qnorm_ref[...]).astype(BF16)
    cos, sin = cos_ref[...], sin_ref[...]
    nn = MLA_HEADS * MLA_NOPE_DIM
    qn = _dot(q_lat, wuq_ref[:, :nn]).astype(BF16)
    qr_pad = _rope(_dot(q_lat, wuq_ref[:, nn:]), cos, sin).astype(BF16)
    kv = _dot(h, wdkv_ref[...])
    ckv = _rms(kv[:, :MLA_KV_LORA], kvnorm_ref[...])
    kr = kv[:, MLA_KV_LORA:]

    @pl.when(is_prompt)
    def _():
        ckv_ref[...] = ckv
        krraw_ref[...] = kr[:, :MLA_ROPE_DIM]

    kr_pad = _rope(kr, cos, sin).astype(BF16)
    cb = ckv.astype(BF16)
    _mla_store_heads(q_ref, kcat_ref, vt_ref, qn, qr_pad, _dot(cb, wuk_ref[...]).astype(BF16), kr_pad,
                     _dot_nt(wuvt_ref[...], cb).astype(BF16))


def _pre_call(kernel, cast, rows, ins, in_specs, out_specs, out_shape, name):
    if cast is not None and cast.ok:
        ins, in_specs = ins + list(cast.stacks), in_specs + cast.in_specs()
        out_specs, out_shape = out_specs + cast.out_specs(), out_shape + cast.out_shape()
    else:
        cast = None
    outs = pl.pallas_call(
        functools.partial(kernel, cast=cast),
        grid=(rows.n_tiles,),
        in_specs=in_specs,
        out_specs=out_specs,
        out_shape=out_shape,
        compiler_params=_params("arbitrary"),
        name=name,
    )(*ins)
    if cast is None:
        return outs, None
    n = len(cast.stacks)
    return outs[:-n], outs[-n:]


def _mla_pre(x, mods, gn, cos, sin, w, rows, cast):
    xs, x_specs, pair, t, d = _rows_inputs(x, rows)
    tm = rows.tm
    n_prompt = rows.prompt_tiles * tm
    weights = [w["dq"], w["q_norm"], w["uq"], w["dkv"], w["kv_norm"], w["uk"], w["uv_t"]]
    ins = xs + [mods, gn, cos, sin] + weights
    in_specs = x_specs + [rows.mod_spec(d), _whole(gn.shape),
                          rows.rope_spec(cos.shape[1]), rows.rope_spec(sin.shape[1])]
    in_specs += [_whole(a.shape) for a in weights]
    heads_rows = pl.BlockSpec((MLA_HEADS, tm, MLA_QK_PAD), lambda i: (0, i, 0))
    return _pre_call(
        functools.partial(_mla_pre_kernel, pair=pair, prompt_tiles=rows.prompt_tiles), cast, rows, ins, in_specs,
        [heads_rows, rows.prompt_spec(MLA_KV_LORA), rows.prompt_spec(MLA_ROPE_DIM), heads_rows,
         pl.BlockSpec((MLA_HEADS, MLA_V_DIM, tm), lambda i: (0, 0, i))],
        [jax.ShapeDtypeStruct((MLA_HEADS, t, MLA_QK_PAD), BF16),
         jax.ShapeDtypeStruct((n_prompt, MLA_KV_LORA), F32),
         jax.ShapeDtypeStruct((n_prompt, MLA_ROPE_DIM), F32),
         jax.ShapeDtypeStruct((MLA_HEADS, t, MLA_QK_PAD), BF16),
         jax.ShapeDtypeStruct((MLA_HEADS, MLA_V_DIM, t), BF16)],
        "mla_pre")


def _mla_expand_kernel(c_ref, kr_ref, wuk_ref, wuvt_ref, kcat_ref, vt_ref):
    cb = c_ref[...].astype(BF16)
    _mla_store_heads(None, kcat_ref, vt_ref, None, None, _dot(cb, wuk_ref[...]).astype(BF16),
                     kr_ref[...].astype(BF16), _dot_nt(wuvt_ref[...], cb).astype(BF16))


def _mla_expand(ckv, kr_pad, wuk, wuvt):
    n = ckv.shape[0]
    return pl.pallas_call(
        _mla_expand_kernel,
        out_shape=[jax.ShapeDtypeStruct((MLA_HEADS, n, MLA_QK_PAD), BF16),
                   jax.ShapeDtypeStruct((MLA_HEADS, MLA_V_DIM, n), BF16)],
        compiler_params=pltpu.CompilerParams(vmem_limit_bytes=VMEM_LIMIT),
        name="mla_expand",
    )(ckv, kr_pad, wuk, wuvt)


def _mla_attn_kernel(*refs, n_src):
    q_ref = refs[0]
    srcs = [(refs[1 + 2 * i], refs[2 + 2 * i]) for i in range(n_src)]
    o_ref, o_buf = refs[1 + 2 * n_src:3 + 2 * n_src]
    s_bufs = refs[3 + 2 * n_src:]
    n_keys = [k_ref.shape[1] for k_ref, _ in srcs]
    starts = [sum(n_keys[:i]) for i in range(n_src)]

    def put_scores(h, s_buf):
        q = q_ref[h]
        for (k_ref, _), first, n in zip(srcs, starts, n_keys):
            s_buf[first:first + n, :] = _dot_nt(k_ref[h], q)

    def finish(h, s_buf):
        s = s_buf[...]
        mx = jnp.max(s, axis=0, keepdims=True)
        p = jnp.exp2((s - mx) * (MLA_SCALE * LOG2_E))
        den = jnp.sum(p, axis=0, keepdims=True)
        p = p.astype(BF16)
        acc = None
        for (_, vt_ref), first, n in zip(srcs, starts, n_keys):
            a = _dot(vt_ref[h], p[first:first + n, :])
            acc = a if acc is None else acc + a
        o_buf[h] = (acc / den).T.astype(BF16)

    _pipelined_heads(MLA_HEADS, put_scores, finish, s_bufs)
    for h in range(MLA_HEADS):
        o_ref[:, h * MLA_V_DIM:(h + 1) * MLA_V_DIM] = o_buf[h]


def _mla_attn_scratch(n_keys, tq):
    return [pltpu.VMEM((MLA_HEADS, tq, MLA_V_DIM), BF16)] + [pltpu.VMEM((n_keys, tq), F32)] * 4


def _pipelined_heads(n_heads, put_scores, finish, bufs):
    a, b, c, d = bufs
    assert n_heads % 4 == 0
    put_scores(0, a)
    put_scores(1, b)

    def quad(j, carry):
        h = 4 * j
        put_scores(h + 2, c)
        put_scores(h + 3, d)
        finish(h, a)
        finish(h + 1, b)
        put_scores(h + 4, a)
        put_scores(h + 5, b)
        finish(h + 2, c)
        finish(h + 3, d)
        return carry

    lax.fori_loop(0, n_heads // 4 - 1, quad, 0)
    h = n_heads - 4
    put_scores(h + 2, c)
    put_scores(h + 3, d)
    finish(h, a)
    finish(h + 1, b)
    finish(h + 2, c)
    finish(h + 3, d)


def _staged_attention(scores, values_t, scale, sinks=None):
    c = scale * LOG2_E
    mx = [jnp.max(s, axis=0, keepdims=True) for s in scores]
    if sinks is not None:
        mx = [jnp.maximum(m, z) for m, z in zip(mx, sinks)]
    p = [jnp.exp2((s - m) * c) for s, m in zip(scores, mx)]
    den = [jnp.sum(x, axis=0, keepdims=True) for x in p]
    if sinks is not None:
        den = [d + jnp.exp2((z - m) * c) for d, z, m in zip(den, sinks, mx)]
    acc = [_dot(v, x.astype(BF16)) for v, x in zip(values_t, p)]
    return [a / d for a, d in zip(acc, den)]


def _mla_attn_prompt_kernel(q_ref, k_ref, vt_ref, o_ref, *, seq):
    per_step = q_ref.shape[1] // seq
    spans = [slice(s * seq, (s + 1) * seq) for s in range(per_step)]
    scores = [_dot_nt(k_ref[h, sp, :], q_ref[h, sp, :]) for sp in spans for h in range(MLA_HEADS)]
    values = [vt_ref[h, :, sp] for sp in spans for h in range(MLA_HEADS)]
    outs = _staged_attention(scores, values, MLA_SCALE)
    for s, sp in enumerate(spans):
        for h in range(MLA_HEADS):
            o_ref[sp, h * MLA_V_DIM:(h + 1) * MLA_V_DIM] = outs[s * MLA_HEADS + h].T.astype(BF16)


def _mla_attn_prompt(q, kcat, vt, batch, seq):
    hv = MLA_HEADS * MLA_V_DIM
    per_step = next(n for n in (4, 2, 1) if batch % n == 0)
    seq_rows = seq * per_step
    rows = pl.BlockSpec((MLA_HEADS, seq_rows, MLA_QK_PAD), lambda b: (0, b, 0))
    return pl.pallas_call(
        functools.partial(_mla_attn_prompt_kernel, seq=seq),
        grid=(batch // per_step,),
        in_specs=[rows, rows, pl.BlockSpec((MLA_HEADS, MLA_V_DIM, seq_rows), lambda b: (0, 0, b))],
        out_specs=pl.BlockSpec((seq_rows, hv), lambda b: (b, 0)),
        out_shape=jax.ShapeDtypeStruct((batch * seq, hv), BF16),
        compiler_params=_params("arbitrary"),
        name="mla_attn_prompt",
    )(q, kcat, vt)


def _mla_attn_latent(q, kcat, vt, kcat_ctx, vt_ctx, n_prompt, dec_batch, dec_seq, past, tq):
    hv = MLA_HEADS * MLA_V_DIM
    nq = dec_seq // tq
    lat_blk = n_prompt // dec_seq
    return pl.pallas_call(
        functools.partial(_mla_attn_kernel, n_src=2),
        grid=(dec_batch, nq),
        in_specs=[pl.BlockSpec((MLA_HEADS, tq, MLA_QK_PAD), lambda b, i: (0, n_prompt // tq + b * nq + i, 0)),
                  pl.BlockSpec((MLA_HEADS, past, MLA_QK_PAD), lambda b, i: (0, b, 0)),
                  pl.BlockSpec((MLA_HEADS, MLA_V_DIM, past), lambda b, i: (0, 0, b)),
                  pl.BlockSpec((MLA_HEADS, dec_seq, MLA_QK_PAD), lambda b, i: (0, lat_blk + b, 0)),
                  pl.BlockSpec((MLA_HEADS, MLA_V_DIM, dec_seq), lambda b, i: (0, 0, lat_blk + b))],
        out_specs=pl.BlockSpec((tq, hv), lambda b, i: (b * nq + i, 0)),
        out_shape=jax.ShapeDtypeStruct((dec_batch * dec_seq, hv), BF16),
        scratch_shapes=_mla_attn_scratch(past + dec_seq, tq),
        compiler_params=_params("arbitrary", "arbitrary"),
        name="mla_attn_latent",
    )(q, kcat_ctx, vt_ctx, kcat, vt)


def _tile_heads(x):
    n = x.shape[1]
    block = lax.broadcasted_iota(jnp.int32, x.shape, 1) // SWA_HEAD_DIM
    rolled = [x] + [pltpu.roll(x, s * SWA_HEAD_DIM, 1) for s in range(1, SWA_KV_HEADS)]
    out = []
    for kvh in range(SWA_KV_HEADS):
        blk = rolled[(0 - kvh) % SWA_KV_HEADS]
        for g in range(1, n // SWA_HEAD_DIM):
            blk = jnp.where(block == g, rolled[(g - kvh) % SWA_KV_HEADS], blk)
        out.append(blk)
    return jnp.concatenate(out, axis=1)


def _swa_pre_kernel(*refs, pair, prompt_tiles, cast):
    x, is_prompt, refs = _take_rows(refs, pair, prompt_tiles)
    refs, cast_src, cast_dst = _split_cast(refs, 5, cast)
    mod_ref, gn_ref, cos_ref, sin_ref, wqkv_ref, q_ref, kx_ref, vt_ref, kraw_ref, vraw_ref = refs
    if cast is not None:
        cast.run(cast_src, cast_dst)
    m = mod_ref[0]
    h = _modulate(x, gn_ref[...], m[0:1], m[1:2]).astype(BF16)
    qkv = _dot(h, wqkv_ref[...])
    nq, nk = SWA_HEADS * SWA_HEAD_DIM, SWA_KV_HEADS * SWA_HEAD_DIM
    cos, sin = cos_ref[...], sin_ref[...]
    q = _rope(qkv[:, :nq], cos, sin).astype(BF16)
    k = qkv[:, nq:nq + nk]
    v = qkv[:, nq + nk:]

    @pl.when(is_prompt)
    def _():
        kraw_ref[...] = k.reshape(kraw_ref.shape)
        vraw_ref[...] = v.reshape(vraw_ref.shape)

    kx = _tile_heads(_rope(k, cos, sin)).astype(BF16)
    vt = v.T.astype(BF16)
    for kvh in range(SWA_KV_HEADS):
        q_ref[kvh] = q[:, kvh * SWA_GW:(kvh + 1) * SWA_GW]
        kx_ref[kvh] = kx[:, kvh * SWA_GW:(kvh + 1) * SWA_GW]
        vt_ref[kvh] = vt[kvh * SWA_HEAD_DIM:(kvh + 1) * SWA_HEAD_DIM, :]


def _swa_pre(x, mods, gn, cos, sin, wqkv, rows, cast):
    xs, x_specs, pair, t, d = _rows_inputs(x, rows)
    tm = rows.tm
    n_prompt = rows.prompt_tiles * tm
    nk = SWA_KV_HEADS * SWA_HEAD_DIM
    heads_rows = pl.BlockSpec((SWA_KV_HEADS, tm, SWA_GW), lambda i: (0, i, 0))
    raw_spec = pl.BlockSpec((tm, SWA_KV_HEADS, SWA_HEAD_DIM), lambda i: (jnp.minimum(i, rows.prompt_tiles - 1), 0, 0))
    return _pre_call(
        functools.partial(_swa_pre_kernel, pair=pair, prompt_tiles=rows.prompt_tiles), cast, rows,
        xs + [mods, gn, cos, sin, wqkv],
        x_specs + [rows.mod_spec(d), _whole(gn.shape),
                   rows.rope_spec(cos.shape[1]), rows.rope_spec(sin.shape[1]), _whole(wqkv.shape)],
        [heads_rows, heads_rows, pl.BlockSpec((SWA_KV_HEADS, SWA_HEAD_DIM, tm), lambda i: (0, 0, i)),
         raw_spec, raw_spec],
        [jax.ShapeDtypeStruct((SWA_KV_HEADS, t, SWA_GW), BF16),
         jax.ShapeDtypeStruct((SWA_KV_HEADS, t, SWA_GW), BF16),
         jax.ShapeDtypeStruct((SWA_KV_HEADS, SWA_HEAD_DIM, t), BF16),
         jax.ShapeDtypeStruct((n_prompt, SWA_KV_HEADS, SWA_HEAD_DIM), F32),
         jax.ShapeDtypeStruct((n_prompt, SWA_KV_HEADS, SWA_HEAD_DIM), F32)],
        "swa_pre")


def _swa_heads(q_ref, sink_ref, o_ref, srcs, bias_ref, ot_buf, s_bufs):
    tq = q_ref.shape[1]
    n_keys = [kx_ref.shape[1] for kx_ref, _ in srcs]
    starts = [sum(n_keys[:i]) for i in range(len(srcs))]
    group = lax.broadcasted_iota(jnp.int32, (tq, SWA_GW), 1) // SWA_HEAD_DIM

    def put_scores(hq, s_buf):
        kvh, g = hq // SWA_GROUP, hq % SWA_GROUP
        q = jnp.where(group == g, q_ref[kvh].astype(F32), 0.0).astype(BF16)
        for (kx_ref, _), first, n in zip(srcs, starts, n_keys):
            s_buf[first:first + n, :] = _dot_nt(kx_ref[kvh], q)

    def finish(hq, s_buf):
        kvh, g = hq // SWA_GROUP, hq % SWA_GROUP
        s = s_buf[...]
        if bias_ref is not None:
            s = s + bias_ref[...]
        sink = jnp.full((1, 1), sink_ref[hq] * (1.0 / SWA_SCALE), F32)
        mx = jnp.maximum(jnp.max(s, axis=0, keepdims=True), sink)
        p = jnp.exp2((s - mx) * (SWA_SCALE * LOG2_E))
        den = jnp.sum(p, axis=0, keepdims=True) + jnp.exp2((sink - mx) * (SWA_SCALE * LOG2_E))
        p = p.astype(BF16)
        acc = None
        for (_, vt_ref), first, n in zip(srcs, starts, n_keys):
            a = _dot(vt_ref[kvh], p[first:first + n, :])
            acc = a if acc is None else acc + a
        ot_buf[kvh, pl.ds(pl.multiple_of(g * SWA_HEAD_DIM, SWA_HEAD_DIM), SWA_HEAD_DIM), :] = acc / den

    _pipelined_heads(SWA_HEADS, put_scores, finish, s_bufs)
    for kvh in range(SWA_KV_HEADS):
        o_ref[:, kvh * SWA_GW:(kvh + 1) * SWA_GW] = ot_buf[kvh].T.astype(BF16)


def _swa_attn_scratch(n_keys, tq):
    return [pltpu.VMEM((SWA_KV_HEADS, SWA_GW, tq), F32)] + [pltpu.VMEM((n_keys, tq), F32)] * 5


def _swa_attn_prompt_kernel(q_ref, kx_ref, vt_ref, sink_ref, o_ref, *, seq):
    per_step = q_ref.shape[1] // seq
    spans = [slice(s * seq, (s + 1) * seq) for s in range(per_step)]
    group = lax.broadcasted_iota(jnp.int32, (seq, SWA_GW), 1) // SWA_HEAD_DIM
    scores, values_t, sinks = [], [], []
    for sp in spans:
        for kvh in range(SWA_KV_HEADS):
            q_all = q_ref[kvh, sp, :].astype(F32)
            for g in range(SWA_GROUP):
                q = jnp.where(group == g, q_all, 0.0).astype(BF16)
                scores.append(_dot_nt(kx_ref[kvh, sp, :], q))
                values_t.append(vt_ref[kvh, :, sp])
                sinks.append(jnp.full((1, 1), sink_ref[kvh * SWA_GROUP + g] * (1.0 / SWA_SCALE), F32))
    outs = _staged_attention(scores, values_t, SWA_SCALE, sinks)
    for s, sp in enumerate(spans):
        for kvh in range(SWA_KV_HEADS):
            first = s * SWA_HEADS + kvh * SWA_GROUP
            ot = jnp.concatenate(outs[first:first + SWA_GROUP], axis=0)
            o_ref[sp, kvh * SWA_GW:(kvh + 1) * SWA_GW] = ot.T.astype(BF16)


def _swa_attn_prompt(q, kx, vt, sink, batch, seq):
    wq = SWA_HEADS * SWA_HEAD_DIM
    per_step = next(n for n in (4, 2, 1) if batch % n == 0)
    seq_rows = seq * per_step
    rows = pl.BlockSpec((SWA_KV_HEADS, seq_rows, SWA_GW), lambda b: (0, b, 0))
    return pl.pallas_call(
        functools.partial(_swa_attn_prompt_kernel, seq=seq),
        grid=(batch // per_step,),
        in_specs=[rows, rows, pl.BlockSpec((SWA_KV_HEADS, SWA_HEAD_DIM, seq_rows), lambda b: (0, 0, b)),
                  pl.BlockSpec(memory_space=pltpu.SMEM)],
        out_specs=pl.BlockSpec((seq_rows, wq), lambda b: (b, 0)),
        out_shape=jax.ShapeDtypeStruct((batch * seq, wq), BF16),
        compiler_params=_params("arbitrary"),
        name="swa_attn_prompt",
    )(q, kx, vt, sink)


def _swa_attn_latent_kernel(q_ref, kc_ref, vc_ref, kp_ref, vp_ref, km_ref, vm_ref, kn_ref, vn_ref,
                            sink_ref, o_ref, ot_buf, s_a, s_b, s_c, s_d, bias_ref, kxc_buf, vtc_buf, *,
                            tq, dec_seq, past):
    i = pl.program_id(1)
    qpos = i * tq + lax.broadcasted_iota(jnp.int32, (1, tq), 1)

    @pl.when(i == 0)
    def _():
        kx = _tile_heads(kc_ref[...]).astype(BF16)
        vt = vc_ref[...].T.astype(BF16)
        for kvh in range(SWA_KV_HEADS):
            kxc_buf[kvh] = kx[:, kvh * SWA_GW:(kvh + 1) * SWA_GW]
            vtc_buf[kvh] = vt[kvh * SWA_HEAD_DIM:(kvh + 1) * SWA_HEAD_DIM, :]

    def band(first, n):
        kpos = first + lax.broadcasted_iota(jnp.int32, (n, 1), 0)
        valid = (jnp.abs(qpos - kpos) <= SWA_WINDOW) & (kpos >= 0) & (kpos < dec_seq)
        return jnp.where(valid, 0.0, NEG_INF)

    bias_ref[...] = jnp.concatenate(
        [jnp.zeros((past, tq), F32), band(i * tq - SWA_WINDOW, SWA_WINDOW), band(i * tq, tq),
         band((i + 1) * tq, SWA_WINDOW)], axis=0)
    srcs = [(kxc_buf, vtc_buf), (kp_ref, vp_ref), (km_ref, vm_ref), (kn_ref, vn_ref)]
    _swa_heads(q_ref, sink_ref, o_ref, srcs, bias_ref, ot_buf, (s_a, s_b, s_c, s_d))


def _swa_attn_latent(q, kx, vt, k_ctx, v_ctx, sink, n_prompt, dec_batch, dec_seq, past, tq):
    t = q.shape[1]
    wq = SWA_HEADS * SWA_HEAD_DIM
    nq = dec_seq // tq
    w = SWA_WINDOW
    n_keys = past + tq + 2 * w
    first = lambda b, i: n_prompt + b * dec_seq + i * tq
    prev = lambda b, i: first(b, i) // w - 1
    nxt = lambda b, i: jnp.minimum((first(b, i) + tq) // w, t // w - 1)
    rows = lambda n, blk: pl.BlockSpec((SWA_KV_HEADS, n, SWA_GW), lambda b, i: (0, blk(b, i), 0))
    cols = lambda n, blk: pl.BlockSpec((SWA_KV_HEADS, SWA_HEAD_DIM, n), lambda b, i: (0, 0, blk(b, i)))
    main = lambda b, i: first(b, i) // tq
    ctx = pl.BlockSpec((past, k_ctx.shape[1]), lambda b, i: (b, 0))
    return pl.pallas_call(
        functools.partial(_swa_attn_latent_kernel, tq=tq, dec_seq=dec_seq, past=past),
        grid=(dec_batch, nq),
        in_specs=[rows(tq, main), ctx, ctx, rows(w, prev), cols(w, prev),
                  rows(tq, main), cols(tq, main), rows(w, nxt), cols(w, nxt),
                  pl.BlockSpec(memory_space=pltpu.SMEM)],
        out_specs=pl.BlockSpec((tq, wq), lambda b, i: (b * nq + i, 0)),
        out_shape=jax.ShapeDtypeStruct((dec_batch * dec_seq, wq), BF16),
        scratch_shapes=_swa_attn_scratch(n_keys, tq) + [pltpu.VMEM((SWA_KV_HEADS, past, SWA_GW), BF16),
                                                        pltpu.VMEM((SWA_KV_HEADS, SWA_HEAD_DIM, past), BF16)],
        compiler_params=_params("arbitrary", "arbitrary"),
        name="swa_attn_latent",
    )(q, k_ctx, v_ctx, kx, vt, kx, vt, kx, vt, sink)


def _hgrn_pre_kernel(*refs, layer, pair, prompt_tiles, cast):
    x, _, refs = _take_rows(refs, pair, prompt_tiles)
    refs, cast_src, cast_dst = _split_cast(refs, 4, cast)
    mod_ref, gn_ref, w_ref, lbl_ref, q_ref, v_ref, g_ref, f_ref, qmax_ref = refs
    if cast is not None:
        cast.run(cast_src, cast_dst)
    m = mod_ref[0]
    h = _modulate(x, gn_ref[...], m[0:1], m[1:2]).astype(BF16)
    y = _dot(h, w_ref[...])
    n = HG_HEADS * HG_DK
    q = _silu(y[:, :n])
    q_ref[...] = q
    qmax_ref[0] = jnp.max(jnp.abs(q), axis=0, keepdims=True)
    v_ref[...] = y[:, n:2 * n].astype(BF16)
    g_ref[...] = _silu(y[:, 2 * n:3 * n])
    for d in range(2):
        logits = lbl_ref[d]
        e = jnp.exp(logits - jnp.max(logits, axis=0, keepdims=True))
        s = e / jnp.sum(e, axis=0, keepdims=True)
        cs = s[0:1]
        for r in range(1, layer + 1):
            cs = cs + s[r:r + 1]
        lb = cs - s[0:1]
        f_ref[d] = lb + (1.0 - lb) * _sigmoid(y[:, (3 + d) * n:(4 + d) * n])


def _hgrn_pre(x, mods, gn, w5, lb_logits, layer, rows, cast):
    xs, x_specs, pair, t, d = _rows_inputs(x, rows)
    n = HG_HEADS * HG_DK
    return _pre_call(
        functools.partial(_hgrn_pre_kernel, layer=layer, pair=pair, prompt_tiles=rows.prompt_tiles), cast, rows,
        xs + [mods, gn, w5, lb_logits],
        x_specs + [rows.mod_spec(d), _whole(gn.shape), _whole(w5.shape), _whole(lb_logits.shape)],
        [rows.row_spec(n), rows.row_spec(n), rows.row_spec(n),
         pl.BlockSpec((2, rows.tm, n), lambda i: (0, i, 0)), pl.BlockSpec((1, 1, n), lambda i: (i, 0, 0))],
        [jax.ShapeDtypeStruct((t, n), F32), jax.ShapeDtypeStruct((t, n), BF16),
         jax.ShapeDtypeStruct((t, n), F32), jax.ShapeDtypeStruct((2, t, n), F32),
         jax.ShapeDtypeStruct((rows.n_tiles, 1, n), F32)],
        "hgrn_pre")


def _tri_cumsum(tri, x):
    hi = x.astype(BF16)
    r1 = x - hi.astype(F32)
    mid = r1.astype(BF16)
    lo = (r1 - mid.astype(F32)).astype(BF16)
    return _dot(tri, hi) + _dot(tri, mid) + _dot(tri, lo)


def _hgrn_tile(q_ref, v_ref, f_ref, o_ref, st_ref, *, rows, reverse, bounded, slot=0, sfin_ref=None):
    r = rows
    diag = HG_DIAG_BLOCK if bounded else 1
    a = lax.broadcasted_iota(jnp.int32, (r, r), 0)
    b = lax.broadcasted_iota(jnp.int32, (r, r), 1)
    seen = (b >= a) if reverse else (b <= a)
    tri = jnp.where(seen, 1.0, 0.0).astype(BF16)

    q = q_ref[...]
    f = f_ref[slot]
    vb = v_ref[...].astype(BF16)
    lf = jnp.log(f)
    cum = _tri_cumsum(tri, lf)
    tot = cum[0:1, :] if reverse else cum[r - 1:r, :]
    kk = 1.0 - f
    q_in = (q * jnp.exp(cum)).astype(BF16)
    k_d = (kk * jnp.exp(tot - cum)).astype(BF16)
    e_tot = jnp.exp(tot)

    levels = []
    c = r // 2
    while c >= diag:
        q_half = (a % (2 * c) < c) if reverse else (a % (2 * c) >= c)
        k_half = (b % (2 * c) >= c) if reverse else (b % (2 * c) < c)
        levels.append((2 * c, c if reverse else c - 1, ((a // (2 * c)) == (b // (2 * c))) & q_half & k_half))
        c //= 2
    levels.append((diag, diag // 2, ((a // diag) == (b // diag)) & seen))

    def spread(rows_of_block, size):
        return jnp.concatenate([jnp.broadcast_to(row, (size, row.shape[1])) for row in rows_of_block], axis=0)

    half = r // 2
    top_ref = cum[levels[0][1]:levels[0][1] + 1, :]
    top_q = slice(0, half) if reverse else slice(half, r)
    top_k = slice(half, r) if reverse else slice(0, half)
    top_qf = (q[top_q] * jnp.exp(cum[top_q] - top_ref)).astype(BF16)
    top_kf = (kk[top_k] * jnp.exp(top_ref - cum[top_k])).astype(BF16)
    blank = jnp.zeros((half, half), F32)

    factors = []
    for size, ref_row, own in levels[1:]:
        if size == 1:
            factors.append((q.astype(BF16), kk.astype(BF16), own))
            continue
        if size >= 8:
            ref = spread([cum[j * size + ref_row:j * size + ref_row + 1, :] for j in range(r // size)], size)
        else:
            ref = _tri_cumsum(jnp.where(b == (a // size) * size + ref_row, 1.0, 0.0).astype(BF16), cum)
        factors.append(((q * jnp.exp(cum - ref)).astype(BF16), (kk * jnp.exp(ref - cum)).astype(BF16), own))

    outs = []
    for h in range(HG_HEADS):
        sl = slice(h * HG_DK, (h + 1) * HG_DK)
        top = _dot_nt(top_qf[:, sl], top_kf[:, sl])
        if reverse:
            att = jnp.concatenate([jnp.concatenate([blank, top], axis=1),
                                   jnp.concatenate([blank, blank], axis=1)], axis=0)
        else:
            att = jnp.concatenate([jnp.concatenate([blank, blank], axis=1),
                                   jnp.concatenate([top, blank], axis=1)], axis=0)
        for q_l, k_l, own in factors:
            att = jnp.where(own, _dot_nt(q_l[:, sl], k_l[:, sl]), att)
        intra = _dot(att.astype(BF16), vb[:, sl])
        if sfin_ref is not None:
            outs.append(intra)
            sfin_ref[0, slot, h] = _dot_tn(k_d[:, sl], vb[:, sl])
            continue
        st = st_ref[h]
        outs.append(_dot_nt(q_in[:, sl], st.astype(BF16)) + intra)
        st_ref[h] = st * e_tot[:, sl] + _dot_tn(vb[:, sl], k_d[:, sl])
    o_ref[slot] = jnp.concatenate(outs, axis=1)


def _hgrn_whole_kernel(fits_ref, q_ref, v_ref, f_ref, o_ref, sfin_ref, *, rows):
    for bounded in (True, False):
        @pl.when((fits_ref[0] != 0) == bounded)
        def _():
            for slot, reverse in enumerate((False, True)):
                _hgrn_tile(q_ref, v_ref, f_ref, o_ref, None, rows=rows, reverse=reverse, bounded=bounded,
                           slot=slot, sfin_ref=sfin_ref)


def _hgrn_scan_kernel(*refs, rows, has_init, bounded):
    if has_init:
        q_ref, v_ref, f_ref, s0_ref, o_ref, sfin_ref, st_ref = refs
    else:
        q_ref, v_ref, f_ref, o_ref, sfin_ref, st_ref = refs
    d = pl.program_id(1)
    t = pl.program_id(2)

    @pl.when(t == 0)
    def _():
        for h in range(HG_HEADS):
            st_ref[h] = s0_ref[0, 0, h].T if has_init else jnp.zeros((HG_DV, HG_DK), F32)

    for reverse in (False, True):
        @pl.when(d == int(reverse))
        def _():
            _hgrn_tile(q_ref, v_ref, f_ref, o_ref, st_ref, rows=rows, reverse=reverse, bounded=bounded)

    @pl.when(t == pl.num_programs(2) - 1)
    def _():
        for h in range(HG_HEADS):
            sfin_ref[0, 0, h] = st_ref[h].T


def _hgrn_scan(fits, q, v, f2, batch, seq, first_row, rt, s0=None):
    n = q.shape[1]
    nt = seq // rt
    has_init = s0 is not None
    flag = pl.BlockSpec(memory_space=pltpu.SMEM)
    out_shape = [jax.ShapeDtypeStruct((2, batch * seq, n), F32),
                 jax.ShapeDtypeStruct((batch, 2, HG_HEADS, HG_DK, HG_DV), F32)]
    if nt == 1 and not has_init:
        rows_of = lambda b: (first_row // rt + b, 0)
        return pl.pallas_call(
            functools.partial(_hgrn_whole_kernel, rows=rt),
            grid=(batch,),
            in_specs=[flag, pl.BlockSpec((rt, n), rows_of), pl.BlockSpec((rt, n), rows_of),
                      pl.BlockSpec((2, rt, n), lambda b: (0, first_row // rt + b, 0))],
            out_specs=[pl.BlockSpec((2, rt, n), lambda b: (0, b, 0)),
                       pl.BlockSpec((1, 2, HG_HEADS, HG_DK, HG_DV), lambda b: (b, 0, 0, 0, 0))],
            out_shape=out_shape,
            compiler_params=_params("arbitrary"),
            name="hgrn_scan_whole",
        )(fits, q, v, f2)

    def local(b, d, i):
        return b * nt + jnp.where(d == 0, i, nt - 1 - i)

    def slab(b, d, i):
        return first_row // rt + local(b, d, i)

    row = pl.BlockSpec((rt, n), lambda b, d, i: (slab(b, d, i), 0))
    state = pl.BlockSpec((1, 1, HG_HEADS, HG_DK, HG_DV), lambda b, d, i: (b, d, 0, 0, 0))
    ins = [q, v, f2]
    in_specs = [row, row, pl.BlockSpec((1, rt, n), lambda b, d, i: (d, slab(b, d, i), 0))]
    if has_init:
        ins.append(s0)
        in_specs.append(state)

    def call(bounded):
        return pl.pallas_call(
            functools.partial(_hgrn_scan_kernel, rows=rt, has_init=has_init, bounded=bounded),
            grid=(batch, 2, nt),
            in_specs=in_specs,
            out_specs=[pl.BlockSpec((1, rt, n), lambda b, d, i: (d, local(b, d, i), 0)), state],
            out_shape=out_shape,
            scratch_shapes=[pltpu.VMEM((HG_HEADS, HG_DV, HG_DK), F32)],
            compiler_params=_params("arbitrary", "arbitrary", "arbitrary"),
            name="hgrn_scan_tiles",
        )(*ins)

    return lax.cond(fits[0] != 0, lambda: call(True), lambda: call(False))


def _post_kernel(*refs, hgrn, final, pair_in, pair_out, prompt_tiles):
    refs = list(refs)
    n_mix = 4 if hgrn else 2
    mix_refs, refs = refs[:n_mix], refs[n_mix:]
    x, is_prompt, refs = _take_rows(refs, pair_in, prompt_tiles)
    mod_ref, wo_ref, gn_ref, wg_ref, wu_ref, wd_ref = refs[:6]
    refs = list(refs[6:])
    fn_ref = refs.pop(0) if final else None
    if hgrn:
        odp_ref, odl_ref, g_ref, onorm_ref = mix_refs
        o2 = jnp.where(is_prompt, odp_ref[0] + odp_ref[1], odl_ref[0] + odl_ref[1])
        gate = g_ref[...]
        onorm = onorm_ref[...]
        parts = []
        for h in range(HG_HEADS):
            sl = slice(h * HG_DV, (h + 1) * HG_DV)
            parts.append(_rms(o2[:, sl], onorm) * gate[:, sl])
        o = jnp.concatenate(parts, axis=1).astype(BF16)
    else:
        o = jnp.where(is_prompt, mix_refs[0][...], mix_refs[1][...])
    m = mod_ref[0]
    x1 = x + m[2:3] * _dot(o, wo_ref[...])
    h2 = _modulate(x1, gn_ref[...], m[3:4], m[4:5]).astype(BF16)
    a = (_silu(_dot(h2, wg_ref[...])) * _dot(h2, wu_ref[...])).astype(BF16)
    x2 = x1 + m[5:6] * _dot(a, wd_ref[...])
    if final:
        x2 = _rms(x2, fn_ref[...])
    if pair_out:
        @pl.when(is_prompt)
        def _():
            refs[0][...] = x2

        @pl.when(jnp.logical_not(is_prompt))
        def _():
            refs[1][...] = x2
    else:
        refs[0][...] = x2


def _post(mix, x, mods, wo, gn, wg, wu, wd, rows, final_norm=None, hgrn=False, pair_out=False):
    xs, x_specs, pair_in, t, d = _rows_inputs(x, rows)
    tm, pt = rows.tm, rows.prompt_tiles
    if hgrn:
        od_p, od_l, gate, onorm = mix
        head = [od_p, od_l, gate, onorm]
        head_specs = [pl.BlockSpec((2, tm, d), lambda i: (0, jnp.minimum(i, pt - 1), 0)),
                      pl.BlockSpec((2, tm, d), lambda i: (0, jnp.maximum(i - pt, 0), 0)),
                      rows.row_spec(d), _whole(onorm.shape)]
    else:
        head = list(mix)
        head_specs = [rows.prompt_spec(mix[0].shape[1]), rows.latent_spec(mix[1].shape[1])]
    ins = head + xs + [mods, wo, gn, wg, wu, wd]
    in_specs = head_specs + x_specs + [rows.mod_spec(d)] + [_whole(a.shape) for a in (wo, gn, wg, wu, wd)]
    if final_norm is not None:
        ins.append(final_norm)
        in_specs.append(_whole(final_norm.shape))
    if pair_out:
        out_specs = rows.x_specs(d)
        out_shape = [jax.ShapeDtypeStruct((pt * tm, d), F32), jax.ShapeDtypeStruct((t - pt * tm, d), F32)]
    else:
        out_specs = rows.row_spec(d)
        out_shape = jax.ShapeDtypeStruct((t, d), F32)
    out = pl.pallas_call(
        functools.partial(_post_kernel, hgrn=hgrn, final=final_norm is not None, pair_in=pair_in, pair_out=pair_out,
                          prompt_tiles=pt),
        grid=(rows.n_tiles,),
        in_specs=in_specs,
        out_specs=out_specs,
        out_shape=out_shape,
        compiler_params=_params("arbitrary"),
        name="post",
    )(*ins)
    return tuple(out) if pair_out else out


def _pick_tile(n_prompt_rows, dec_seq, want):
    tm = want
    while n_prompt_rows % tm or dec_seq % tm:
        tm //= 2
    return tm


def kernel(x_prompt, x_sample, cache_mla_ckv, cache_mla_krope, state_hgrn, cache_swa_k, cache_swa_v, c, c_ctx, ada_w, ada_b, norm_mix, norm_ffn, ffn_w_gate, ffn_w_up, ffn_w_down, final_norm, mla_w_dq, mla_q_norm, mla_w_uq, mla_w_dkv, mla_kv_norm, mla_w_uk, mla_w_uv, mla_w_o, hg_w_q, hg_w_f, hg_w_i, hg_w_g, hg_o_norm, hg_w_o, hg_lb_logits, swa_w_q, swa_w_k, swa_w_v, swa_w_o, swa_sink):
    batch, seq, d = x_prompt.shape
    dec_batch, dec_seq, _ = x_sample.shape
    past = cache_mla_ckv.shape[2]
    depth = ada_w.shape[0]
    n_prompt = batch * seq
    n_rows = n_prompt + dec_batch * dec_seq
    assert dec_batch + 1 <= COND_ROWS and seq % SWA_WINDOW == 0 and dec_seq % (2 * SWA_WINDOW) == 0
    assert n_prompt % dec_seq == 0

    wide = _Rows(n_prompt, dec_seq, n_rows, _pick_tile(n_prompt, dec_seq, ROW_TILE))
    narrow = _Rows(n_prompt, dec_seq, n_rows, _pick_tile(n_prompt, dec_seq, ROW_TILE_F32_HEAVY))
    pre_rows, post_rows = wide, wide
    hg_pre_rows, hg_post_rows = narrow, narrow
    tq = _pick_tile(n_prompt, dec_seq, ATTN_TILE)
    cos64, sin64 = _rope_tables(dec_seq, pre_rows.tm)
    n_tab = cos64.shape[0]
    cos_mla = jnp.asarray(np.concatenate([cos64, np.ones((n_tab, ROPE_PERIOD), np.float32)], axis=1))
    sin_mla = jnp.asarray(np.concatenate([sin64, np.zeros((n_tab, ROPE_PERIOD), np.float32)], axis=1))
    cos_swa, sin_swa = jnp.asarray(np.tile(cos64, (1, 2))), jnp.asarray(np.tile(sin64, (1, 2)))

    cond = jnp.concatenate([c_ctx[None, :], c, jnp.zeros((COND_ROWS - 1 - dec_batch, d), F32)], axis=0)
    mods = _adaln(cond, ada_w, ada_b).reshape(depth, COND_ROWS, 6, d)

    x = (x_prompt.reshape(n_prompt, d), x_sample.reshape(dec_batch * dec_seq, d))
    ffn_stacks = (ffn_w_gate, ffn_w_up, ffn_w_down)
    row1 = lambda a: a.reshape(1, -1)
    new_ckv, new_krope, new_hg, new_k, new_v = [], [], [], [], []
    for i in range(depth):
        kind, j = i % N_MIXERS, i // N_MIXERS
        gn = row1(norm_mix[i])
        rows_i = hg_pre_rows if kind == 1 else pre_rows
        cast = _Cast(ffn_stacks, i, rows_i.n_tiles)
        if kind == 0:
            uq = mla_w_uq[j].reshape(-1, MLA_HEADS, MLA_NOPE_DIM + MLA_ROPE_DIM)
            uq_rope = jnp.pad(uq[:, :, MLA_NOPE_DIM:], ((0, 0), (0, 0), (0, LANES - MLA_ROPE_DIM)))
            uq = jnp.concatenate([uq[:, :, :MLA_NOPE_DIM].reshape(uq.shape[0], -1),
                                  uq_rope.reshape(uq.shape[0], -1)], axis=1)
            w = {
                "dq": mla_w_dq[j].astype(BF16), "q_norm": row1(mla_q_norm[j]), "uq": uq.astype(BF16),
                "dkv": jnp.pad(mla_w_dkv[j], ((0, 0), (0, LANES - MLA_ROPE_DIM))).astype(BF16),
                "kv_norm": row1(mla_kv_norm[j]),
                "uk": mla_w_uk[j].astype(BF16), "uv_t": mla_w_uv[j].T.astype(BF16),
            }
            (q, ckv, kr_raw, kcat, vt), ffn = _mla_pre(x, mods[i], gn, cos_mla, sin_mla, w, rows_i, cast)
            kr_ctx = jnp.pad(cache_mla_krope[:, j].reshape(dec_batch * past, -1), ((0, 0), (0, LANES - MLA_ROPE_DIM)))
            kcat_ctx, vt_ctx = _mla_expand(cache_mla_ckv[:, j].reshape(dec_batch * past, -1), kr_ctx, w["uk"], w["uv_t"])
            mix = (_mla_attn_prompt(q, kcat, vt, batch, seq),
                   _mla_attn_latent(q, kcat, vt, kcat_ctx, vt_ctx, n_prompt, dec_batch, dec_seq, past, tq))
            wo = mla_w_o[j].astype(BF16)
            new_ckv.append(ckv.reshape(batch, seq, -1))
            new_krope.append(kr_raw.reshape(batch, seq, -1))
        elif kind == 1:
            w5 = jnp.concatenate([hg_w_q[j], hg_w_i[j], hg_w_g[j], hg_w_f[j, 0], hg_w_f[j, 1]], axis=1).astype(BF16)
            (q, v, gate, f2, q_max), ffn = _hgrn_pre(x, mods[i], gn, w5, hg_lb_logits, i, rows_i, cast)
            rt = _pick_tile(seq, dec_seq, ATTN_TILE)
            sm = jax.nn.softmax(hg_lb_logits.astype(F32), axis=1)
            lb_min = jnp.min(jnp.cumsum(sm, axis=1)[:, i] - sm[:, 0])
            worst = (HG_DIAG_BLOCK // 2) * -jnp.log(lb_min) + jnp.log(jnp.maximum(jnp.max(q_max), 1.0))
            fits = (worst < HG_MAX_EXPONENT).astype(jnp.int32).reshape(1)
            od_p, s_prompt = _hgrn_scan(fits, q, v, f2, batch, seq, 0, rt)
            od_l, _ = _hgrn_scan(fits, q, v, f2, dec_batch, dec_seq, n_prompt, rt, s0=state_hgrn[:, j])
            mix = (od_p, od_l, gate, row1(hg_o_norm[j]))
            wo = hg_w_o[j].astype(BF16)
            new_hg.append(s_prompt)
        else:
            wqkv = jnp.concatenate([swa_w_q[j], swa_w_k[j], swa_w_v[j]], axis=1).astype(BF16)
            (q, kx, vt, k_raw, v_raw), ffn = _swa_pre(x, mods[i], gn, cos_swa, sin_swa, wqkv, rows_i, cast)
            sink = swa_sink[j]
            k_ctx = cache_swa_k[:, j].reshape(dec_batch * past, -1)
            v_ctx = cache_swa_v[:, j].reshape(dec_batch * past, -1)
            mix = (_swa_attn_prompt(q, kx, vt, sink, batch, seq),
                   _swa_attn_latent(q, kx, vt, k_ctx, v_ctx, sink, n_prompt, dec_batch, dec_seq, past, tq))
            wo = swa_w_o[j].astype(BF16)
            new_k.append(k_raw.reshape(batch, seq, SWA_KV_HEADS, SWA_HEAD_DIM))
            new_v.append(v_raw.reshape(batch, seq, SWA_KV_HEADS, SWA_HEAD_DIM))
        if ffn is None:
            ffn = [a[i].astype(BF16) for a in ffn_stacks]
        x = _post(mix, x, mods[i], wo, row1(norm_ffn[i]), *ffn,
                  hg_post_rows if kind == 1 else post_rows,
                  final_norm=row1(final_norm) if i == depth - 1 else None, hgrn=kind == 1,
                  pair_out=i == depth - 1)
    y_prompt = x[0].reshape(batch, seq, d)
    y_sample = x[1].reshape(dec_batch, dec_seq, d)
    return (y_prompt, y_sample, jnp.stack(new_ckv, axis=1), jnp.stack(new_krope, axis=1),
            jnp.stack(new_hg, axis=1), jnp.stack(new_k, axis=1), jnp.stack(new_v, axis=1))
```
